```python
import math
import jax, jax.numpy as jnp
from jax import lax
import numpy as np

D_MODEL = 1024
BATCH = 16
SEQ = 2048
DEPTH = 4

GRID_W = 64
CTX_LEN = 256
N_MIXERS = 4
D_INNER = 2 * D_MODEL
RMS_EPS = 1e-6
ROPE_THETA = 10000.0
Q_BLOCK = 128

MLSTM_HEADS = 8
MLSTM_HEAD_DIM = D_INNER // MLSTM_HEADS
MLSTM_QKV_BLOCK = 4
MLSTM_CONV = 5
MLSTM_CHUNK = 128
MLSTM_F_BIAS_LO = 3.0
MLSTM_F_BIAS_HI = 6.0

DIFF_HEADS = 16
DIFF_HEAD_DIM = D_INNER // (2 * DIFF_HEADS)

NA_HEADS = 32
NA_HEAD_DIM = D_INNER // NA_HEADS
NA_WIN_ROWS = 8
NA_WIN_COLS = 16
NA_QCOL_BLOCK = 16
NA_KCOL_BLOCK = NA_QCOL_BLOCK + NA_WIN_COLS

SWA_Q_HEADS = 32
SWA_KV_HEADS = 4
SWA_HEAD_DIM = D_INNER // SWA_Q_HEADS
SWA_WINDOW = 128

kernel_name = 'hybrid_mlstm_diffattn_natten_swa_dit'

F32 = jnp.float32


def _rmsnorm(x, g):
    xf = x.astype(F32)
    y = xf * lax.rsqrt(jnp.mean(xf * xf, axis=-1, keepdims=True) + RMS_EPS)
    return (y * g.astype(F32)).astype(x.dtype)


def _rope_1d(x, ang):
    cos = jnp.cos(ang)[None, :, None, :].astype(x.dtype)
    sin = jnp.sin(ang)[None, :, None, :].astype(x.dtype)
    x1, x2 = jnp.split(x, 2, axis=-1)
    return jnp.concatenate([x1 * cos - x2 * sin, x2 * cos + x1 * sin], axis=-1)


def _rope_2d(x):
    T, dh = x.shape[1], x.shape[-1]
    half = dh // 2
    t = jnp.arange(T)
    inv = jnp.power(ROPE_THETA, -jnp.arange(0, half, 2, dtype=F32) / half)
    ang_r = (t // GRID_W).astype(F32)[:, None] * inv
    ang_c = (t % GRID_W).astype(F32)[:, None] * inv
    xr, xc = jnp.split(x, 2, axis=-1)
    return jnp.concatenate([_rope_1d(xr, ang_r), _rope_1d(xc, ang_c)], axis=-1)


def _dwconv(u, w, b):
    K, C = w.shape
    out = lax.conv_general_dilated(u, w[:, None, :].astype(u.dtype), window_strides=(1,),
                                   padding=[(K // 2, K // 2)], dimension_numbers=('NWC', 'WIO', 'NWC'),
                                   feature_group_count=C)
    return out + b


def _blockdiag(u, w):
    B, T, C = u.shape
    ub = u.reshape(B, T, w.shape[0], w.shape[1])
    return jnp.einsum('btgi,gij->btgj', ub, w).reshape(B, T, C)


def _sink_softmax(s, sink):
    sk = jnp.broadcast_to(sink[:, :, None, None], s.shape[:-1] + (1,))
    return jax.nn.softmax(jnp.concatenate([s, sk], axis=-1), axis=-1)[..., :-1]


def _mlstm_scan(q, k, v, log_i, log_f, state):
    L = MLSTM_CHUNK
    nc = q.shape[2] // L

    def chunks(a):
        return jnp.moveaxis(a.reshape(a.shape[:2] + (nc, L) + a.shape[3:]), 2, 0)

    tril = jnp.tril(jnp.ones((L, L), dtype=bool))

    def step(carry, inp):
        C, n, m = carry
        qc, kc, vc, ic, fc = inp
        b = jnp.cumsum(fc, axis=-1)
        d = jnp.where(tril, b[..., :, None] - b[..., None, :] + ic[..., None, :], -jnp.inf)
        inter = b + m[..., None]
        m_t = jnp.maximum(jnp.max(d, axis=-1), inter)
        w_intra = jnp.exp(d - m_t[..., None])
        w_state = jnp.exp(inter - m_t)
        s = jnp.einsum('bhtd,bhsd->bhts', qc, kc) * w_intra
        num = w_state[..., None] * jnp.einsum('bhtd,bhde->bhte', qc, C) + jnp.einsum('bhts,bhse->bhte', s, vc)
        den = w_state * jnp.einsum('bhtd,bhd->bht', qc, n) + jnp.sum(s, axis=-1)
        h = num / jnp.maximum(jnp.abs(den), jnp.exp(-m_t))[..., None]
        b_end = b[..., -1]
        g = b_end[..., None] - b + ic
        m_new = jnp.maximum(b_end + m, jnp.max(g, axis=-1))
        w_s = jnp.exp(g - m_new[..., None])
        decay = jnp.exp(b_end + m - m_new)
        C = decay[..., None, None] * C + jnp.einsum('bhs,bhsd,bhse->bhde', w_s, kc, vc)
        n = decay[..., None] * n + jnp.einsum('bhs,bhsd->bhd', w_s, kc)
        return (C, n, m_new), h

    state, hs = lax.scan(step, state, tuple(chunks(a) for a in (q, k, v, log_i, log_f)))
    return jnp.moveaxis(hs, 0, 2).reshape(q.shape[:3] + (v.shape[-1],)), state


def _mlstm_project(h, w_in, conv_w, conv_b, wq, wk, wv, w_gate, b_gate):
    B, T, _ = h.shape
    xm, z = jnp.split(h @ w_in, 2, axis=-1)
    xc = jax.nn.silu(_dwconv(xm, conv_w, conv_b))
    q, k, v = _blockdiag(xc, wq), _blockdiag(xc, wk), _blockdiag(xm, wv)
    gates = (q @ w_gate[:D_INNER] + k @ w_gate[D_INNER:2 * D_INNER] + v @ w_gate[2 * D_INNER:] + b_gate)
    gates = jnp.moveaxis(gates.reshape(B, T, 4, MLSTM_HEADS), 1, 3).astype(F32)

    def heads(a):
        return jnp.moveaxis(a.reshape(B, T, MLSTM_HEADS, MLSTM_HEAD_DIM), 2, 1).astype(F32)

    return xc, z, heads(q), heads(k) * (MLSTM_HEAD_DIM ** -0.5), heads(v), gates


def _mlstm_out(h, xc, z, norm_g, skip, w_out):
    B, H, T, dv = h.shape
    mu = jnp.mean(h, axis=-1, keepdims=True)
    hn = (h - mu) * lax.rsqrt(jnp.mean(jnp.square(h - mu), axis=-1, keepdims=True) + RMS_EPS)
    hn = jnp.moveaxis(hn, 1, 2).reshape(B, T, H * dv).astype(xc.dtype) * norm_g
    return ((hn + skip * xc) * jax.nn.silu(z)) @ w_out


def _mlstm_mixer(h_lat, h_ctx, w_in, conv_w, conv_b, wq, wk, wv, w_gate, b_gate, norm_g, skip, w_out, need_ctx):
    prm = (w_in, conv_w, conv_b, wq, wk, wv, w_gate, b_gate)
    xc_l, z_l, q_l, k_l, v_l, g_l = _mlstm_project(h_lat, *prm)
    xc_c, z_c, q_c, k_c, v_c, g_c = _mlstm_project(h_ctx, *prm)
    B = h_lat.shape[0]
    dh = MLSTM_HEAD_DIM
    zero = (jnp.zeros((B, MLSTM_HEADS, dh, dh), F32), jnp.zeros((B, MLSTM_HEADS, dh), F32),
            jnp.zeros((B, MLSTM_HEADS), F32))
    lat_dirs, ctx_dirs = [], []
    for d, rev in ((0, False), (1, True)):
        tr = (lambda a: jnp.flip(a, axis=2)) if rev else (lambda a: a)
        hc, st = _mlstm_scan(tr(q_c), tr(k_c), tr(v_c), tr(g_c[:, 2 * d]),
                             jax.nn.log_sigmoid(tr(g_c[:, 2 * d + 1])), zero)
        hl, _ = _mlstm_scan(tr(q_l), tr(k_l), tr(v_l), tr(g_l[:, 2 * d]),
                            jax.nn.log_sigmoid(tr(g_l[:, 2 * d + 1])), st)
        lat_dirs.append(tr(hl))
        ctx_dirs.append(tr(hc))
    y_lat = _mlstm_out(lat_dirs[0] + lat_dirs[1], xc_l, z_l, norm_g, skip, w_out)
    y_ctx = _mlstm_out(ctx_dirs[0] + ctx_dirs[1], xc_c, z_c, norm_g, skip, w_out) if need_ctx else None
    return y_lat, y_ctx


def _diff_mixer(h_lat, h_ctx, w_in, lq1, lk1, lq2, lk2, subln_g, w_out, layer_idx, need_ctx):
    lam_init = 0.8 - 0.6 * math.exp(-0.3 * layer_idx)
    lam = (jnp.exp(jnp.sum(lq1 * lk1)) - jnp.exp(jnp.sum(lq2 * lk2))).astype(F32) + lam_init
    H, d = DIFF_HEADS, DIFF_HEAD_DIM

    def proj(h, rope):
        B, T, _ = h.shape
        q, k, v, z = jnp.split(h @ w_in, 4, axis=-1)
        q = q.reshape(B, T, 2 * H, d)
        k = k.reshape(B, T, 2 * H, d)
        if rope:
            q, k = _rope_2d(q), _rope_2d(k)
        q = q.reshape(B, T, H, 2, d) * (d ** -0.5)
        k = k.reshape(B, T, H, 2, d)
        return q, k, v.reshape(B, T, H, 2 * d), z

    def attend(q, k, v):
        p = jax.nn.softmax(jnp.einsum('bqhcd,bkhcd->bhcqk', q, k).astype(F32), axis=-1)
        a = (p[:, :, 0] - lam * p[:, :, 1]).astype(v.dtype)
        return jnp.einsum('bhqk,bkhe->bqhe', a, v)

    def finish(o, z):
        B, T = o.shape[:2]
        o = _rmsnorm(o, subln_g) * (1.0 - lam_init)
        return (o.reshape(B, T, D_INNER) * jax.nn.silu(z)) @ w_out

    q_l, k_l, v_l, z_l = proj(h_lat, True)
    q_c, k_c, v_c, z_c = proj(h_ctx, False)
    B, S = h_lat.shape[:2]
    nb = S // Q_BLOCK
    k_all = jnp.concatenate([k_l, k_c], axis=1)
    v_all = jnp.concatenate([v_l, v_c], axis=1)
    q_blocks = jnp.moveaxis(q_l.reshape(B, nb, Q_BLOCK, H, 2, d), 1, 0)
    o = lax.map(lambda qb: attend(qb, k_all, v_all), q_blocks)
    o = jnp.moveaxis(o, 0, 1).reshape(B, S, H, 2 * d)
    y_lat = finish(o, z_l)
    y_ctx = finish(attend(q_c, k_c, v_c), z_c) if need_ctx else None
    return y_lat, y_ctx


def _na_mixer(h_lat, h_ctx, w_in, rpb, w_out, need_ctx):
    H, dh = NA_HEADS, NA_HEAD_DIM

    def proj(h):
        B, T, _ = h.shape
        q, k, v, z = jnp.split(h @ w_in, 4, axis=-1)
        return (q.reshape(B, T, H, dh) * (dh ** -0.5), k.reshape(B, T, H, dh), v.reshape(B, T, H, dh), z)

    q_l, k_l, v_l, z_l = proj(h_lat)
    q_c, k_c, v_c, z_c = proj(h_ctx)
    B, S = h_lat.shape[:2]
    rows = S // GRID_W
    wr = min(NA_WIN_ROWS, rows)
    n_cb = GRID_W // NA_QCOL_BLOCK
    jb = np.arange(n_cb)
    kc0 = np.clip(jb * NA_QCOL_BLOCK - NA_WIN_COLS // 2, 0, GRID_W - NA_KCOL_BLOCK)
    key_cols = kc0[:, None] + np.arange(NA_KCOL_BLOCK)
    q_cols = jb[:, None] * NA_QCOL_BLOCK + np.arange(NA_QCOL_BLOCK)
    cs = np.clip(q_cols - NA_WIN_COLS // 2, 0, GRID_W - NA_WIN_COLS)
    col_ok = (key_cols[:, None, :] >= cs[:, :, None]) & (key_cols[:, None, :] < cs[:, :, None] + NA_WIN_COLS)
    dc_idx = np.clip(key_cols[:, None, :] - q_cols[:, :, None], -(NA_WIN_COLS - 1), NA_WIN_COLS - 1) + NA_WIN_COLS - 1
    k_grid = k_l.reshape(B, rows, GRID_W, H, dh)
    v_grid = v_l.reshape(B, rows, GRID_W, H, dh)
    q_rows = jnp.moveaxis(q_l.reshape(B, rows, n_cb, NA_QCOL_BLOCK, H, dh), 1, 0)
    nk = wr * NA_KCOL_BLOCK

    def row_block(args):
        r, qr = args
        rs = jnp.clip(r - wr // 2, 0, rows - wr)
        kb = lax.dynamic_slice_in_dim(k_grid, rs, wr, axis=1)[:, :, key_cols]
        vb = lax.dynamic_slice_in_dim(v_grid, rs, wr, axis=1)[:, :, key_cols]
        dr_idx = rs + jnp.arange(wr) - r + NA_WIN_ROWS - 1
        bias = jnp.moveaxis(rpb[:, dr_idx][:, :, dc_idx], 1, 3).astype(F32)
        s_lat = jnp.einsum('bjuhd,bajwhd->bhjuaw', qr, kb).astype(F32) + bias
        s_lat = jnp.where(col_ok[:, :, None, :], s_lat, -jnp.inf)
        s_ctx = jnp.einsum('bjuhd,bkhd->bhjuk', qr, k_c).astype(F32)
        p = jax.nn.softmax(jnp.concatenate([s_lat.reshape(B, H, n_cb, NA_QCOL_BLOCK, nk), s_ctx], axis=-1), axis=-1)
        p = p.astype(vb.dtype)
        p_lat = p[..., :nk].reshape(B, H, n_cb, NA_QCOL_BLOCK, wr, NA_KCOL_BLOCK)
        return (jnp.einsum('bhjuaw,bajwhe->bjuhe', p_lat, vb)
                + jnp.einsum('bhjuk,bkhe->bjuhe', p[..., nk:], v_c))

    o = lax.map(row_block, (jnp.arange(rows), q_rows))
    o = jnp.moveaxis(o, 0, 1).reshape(B, S, D_INNER)
    y_lat = (o * jax.nn.silu(z_l)) @ w_out
    y_ctx = None
    if need_ctx:
        pc = jax.nn.softmax(jnp.einsum('bqhd,bkhd->bhqk', q_c, k_c).astype(F32), axis=-1).astype(v_c.dtype)
        oc = jnp.einsum('bhqk,bkhd->bqhd', pc, v_c).reshape(B, h_ctx.shape[1], D_INNER)
        y_ctx = (oc * jax.nn.silu(z_c)) @ w_out
    return y_lat, y_ctx


def _swa_mixer(h_lat, h_ctx, w_in, sink, w_out, need_ctx):
    Hkv, G, dh = SWA_KV_HEADS, SWA_Q_HEADS // SWA_KV_HEADS, SWA_HEAD_DIM
    kvw = Hkv * dh
    sink_hg = sink.reshape(Hkv, G).astype(F32)

    def proj(h, rope):
        B, T, _ = h.shape
        q, k, v, z = jnp.split(h @ w_in, [D_INNER, D_INNER + kvw, D_INNER + 2 * kvw], axis=-1)
        q = q.reshape(B, T, SWA_Q_HEADS, dh)
        k = k.reshape(B, T, Hkv, dh)
        if rope:
            q, k = _rope_2d(q), _rope_2d(k)
        return (q * (dh ** -0.5)).reshape(B, T, Hkv, G, dh), k, v.reshape(B, T, Hkv, dh), z

    q_l, k_l, v_l, z_l = proj(h_lat, True)
    q_c, k_c, v_c, z_c = proj(h_ctx, False)
    B, S = h_lat.shape[:2]
    nb = S // Q_BLOCK
    band = Q_BLOCK + 2 * SWA_WINDOW
    pad = ((0, 0), (SWA_WINDOW, SWA_WINDOW), (0, 0), (0, 0))
    kp, vp = jnp.pad(k_l, pad), jnp.pad(v_l, pad)
    u = np.arange(Q_BLOCK)[:, None]
    w = np.arange(band)[None, :]
    in_win = np.abs(u + SWA_WINDOW - w) <= SWA_WINDOW
    q_blocks = jnp.moveaxis(q_l.reshape(B, nb, Q_BLOCK, Hkv, G, dh), 1, 0)

    def block(args):
        bi, qb = args
        kb = lax.dynamic_slice_in_dim(kp, bi * Q_BLOCK, band, axis=1)
        vb = lax.dynamic_slice_in_dim(vp, bi * Q_BLOCK, band, axis=1)
        s_pos = bi * Q_BLOCK - SWA_WINDOW + jnp.arange(band)
        ok = in_win & ((s_pos >= 0) & (s_pos < S))[None, :]
        s_lat = jnp.where(ok, jnp.einsum('bqhgd,bkhd->bhgqk', qb, kb).astype(F32), -jnp.inf)
        s_ctx = jnp.einsum('bqhgd,bkhd->bhgqk', qb, k_c).astype(F32)
        p = _sink_softmax(jnp.concatenate([s_lat, s_ctx], axis=-1), sink_hg).astype(vb.dtype)
        return (jnp.einsum('bhgqk,bkhd->bqhgd', p[..., :band], vb)
                + jnp.einsum('bhgqk,bkhd->bqhgd', p[..., band:], v_c))

    o = lax.map(block, (jnp.arange(nb), q_blocks))
    o = jnp.moveaxis(o, 0, 1).reshape(B, S, D_INNER)
    y_lat = (o * jax.nn.silu(z_l)) @ w_out
    y_ctx = None
    if need_ctx:
        pc = _sink_softmax(jnp.einsum('bqhgd,bkhd->bhgqk', q_c, k_c).astype(F32), sink_hg).astype(v_c.dtype)
        oc = jnp.einsum('bhgqk,bkhd->bqhgd', pc, v_c).reshape(B, h_ctx.shape[1], D_INNER)
        y_ctx = (oc * jax.nn.silu(z_c)) @ w_out
    return y_lat, y_ctx


def setup_inputs(seed: int = 0) -> dict:
    key = jax.random.key(seed)
    ks = iter(jax.random.split(key, 48))

    def nrm(shape, scale):
        return jax.random.normal(next(ks), shape, F32) * scale

    nA, nB, nC, nD = (len(range(m, DEPTH, N_MIXERS)) for m in range(N_MIXERS))
    H = MLSTM_HEADS
    gate_base = jnp.concatenate([jnp.zeros((H,), F32), jnp.linspace(MLSTM_F_BIAS_LO, MLSTM_F_BIAS_HI, H),
                                 jnp.zeros((H,), F32), jnp.linspace(MLSTM_F_BIAS_LO, MLSTM_F_BIAS_HI, H)])
    nblk = D_INNER // MLSTM_QKV_BLOCK
    bs = MLSTM_QKV_BLOCK
    kvw = SWA_KV_HEADS * SWA_HEAD_DIM
    return {
        'x': nrm((BATCH, SEQ, D_MODEL), 1.0),
        'c': nrm((BATCH, D_MODEL), 1.0),
        'ctx': nrm((BATCH, CTX_LEN, D_MODEL), 1.0),
        'c_ctx': nrm((D_MODEL,), 1.0),
        'norm_g': 1.0 + nrm((DEPTH, D_MODEL), 0.02),
        'w_ada': nrm((DEPTH, D_MODEL, 3 * D_MODEL), 0.5 * D_MODEL ** -0.5),
        'b_ada': nrm((DEPTH, 3 * D_MODEL), 0.02),
        'final_g': 1.0 + nrm((D_MODEL,), 0.02),
        'mlstm_w_in': nrm((nA, D_MODEL, 2 * D_INNER), D_MODEL ** -0.5),
        'mlstm_conv_w': nrm((nA, MLSTM_CONV, D_INNER), MLSTM_CONV ** -0.5),
        'mlstm_conv_b': nrm((nA, D_INNER), 0.02),
        'mlstm_wq': nrm((nA, nblk, bs, bs), bs ** -0.5),
        'mlstm_wk': nrm((nA, nblk, bs, bs), bs ** -0.5),
        'mlstm_wv': nrm((nA, nblk, bs, bs), bs ** -0.5),
        'mlstm_w_gate': nrm((nA, 3 * D_INNER, 4 * H), (3 * D_INNER) ** -0.5),
        'mlstm_b_gate': gate_base[None, :] + nrm((nA, 4 * H), 0.1),
        'mlstm_norm_g': 1.0 + nrm((nA, D_INNER), 0.02),
        'mlstm_skip': 1.0 + nrm((nA, D_INNER), 0.02),
        'mlstm_w_out': nrm((nA, D_INNER, D_MODEL), D_INNER ** -0.5),
        'diff_w_in': nrm((nB, D_MODEL, 4 * D_INNER), D_MODEL ** -0.5),
        'diff_lq1': nrm((nB, DIFF_HEAD_DIM), 0.1),
        'diff_lk1': nrm((nB, DIFF_HEAD_DIM), 0.1),
        'diff_lq2': nrm((nB, DIFF_HEAD_DIM), 0.1),
        'diff_lk2': nrm((nB, DIFF_HEAD_DIM), 0.1),
        'diff_subln_g': 1.0 + nrm((nB, 2 * DIFF_HEAD_DIM), 0.02),
        'diff_w_out': nrm((nB, D_INNER, D_MODEL), D_INNER ** -0.5),
        'na_w_in': nrm((nC, D_MODEL, 4 * D_INNER), D_MODEL ** -0.5),
        'na_rpb': nrm((nC, NA_HEADS, 2 * NA_WIN_ROWS - 1, 2 * NA_WIN_COLS - 1), 0.1),
        'na_w_out': nrm((nC, D_INNER, D_MODEL), D_INNER ** -0.5),
        'swa_w_in': nrm((nD, D_MODEL, 2 * D_INNER + 2 * kvw), D_MODEL ** -0.5),
        'swa_sink': nrm((nD, SWA_Q_HEADS), 1.0),
        'swa_w_out': nrm((nD, D_INNER, D_MODEL), D_INNER ** -0.5),
    }


def reference(x, c, ctx, c_ctx, norm_g, w_ada, b_ada, final_g,
              mlstm_w_in, mlstm_conv_w, mlstm_conv_b, mlstm_wq, mlstm_wk, mlstm_wv, mlstm_w_gate, mlstm_b_gate,
              mlstm_norm_g, mlstm_skip, mlstm_w_out,
              diff_w_in, diff_lq1, diff_lk1, diff_lq2, diff_lk2, diff_subln_g, diff_w_out,
              na_w_in, na_rpb, na_w_out,
              swa_w_in, swa_sink, swa_w_out):
    s_c = jax.nn.silu(c)
    s_ctx = jax.nn.silu(c_ctx)
    ctx_s = ctx
    for i in range(DEPTH):
        mixer, j = i % N_MIXERS, i // N_MIXERS
        need_ctx = i < DEPTH - 1
        sh, scl, gt = jnp.split((s_c @ w_ada[i] + b_ada[i])[:, None, :], 3, axis=-1)
        sh_c, scl_c, gt_c = jnp.split(s_ctx @ w_ada[i] + b_ada[i], 3)
        h_lat = _rmsnorm(x, norm_g[i]) * (1.0 + scl) + sh
        h_ctx = _rmsnorm(ctx_s, norm_g[i]) * (1.0 + scl_c) + sh_c
        if mixer == 0:
            y_lat, y_ctx = _mlstm_mixer(h_lat, h_ctx, mlstm_w_in[j], mlstm_conv_w[j], mlstm_conv_b[j], mlstm_wq[j],
                                        mlstm_wk[j], mlstm_wv[j], mlstm_w_gate[j], mlstm_b_gate[j],
                                        mlstm_norm_g[j], mlstm_skip[j], mlstm_w_out[j], need_ctx)
        elif mixer == 1:
            y_lat, y_ctx = _diff_mixer(h_lat, h_ctx, diff_w_in[j], diff_lq1[j], diff_lk1[j], diff_lq2[j],
                                       diff_lk2[j], diff_subln_g[j], diff_w_out[j], i, need_ctx)
        elif mixer == 2:
            y_lat, y_ctx = _na_mixer(h_lat, h_ctx, na_w_in[j], na_rpb[j], na_w_out[j], need_ctx)
        else:
            y_lat, y_ctx = _swa_mixer(h_lat, h_ctx, swa_w_in[j], swa_sink[j], swa_w_out[j], need_ctx)
        x = x + gt * y_lat
        if need_ctx:
            ctx_s = ctx_s + gt_c * y_ctx
    return _rmsnorm(x, final_g)
```

```python
import functools
import math

import numpy as np
import jax
import jax.numpy as jnp
from jax import lax
from jax.experimental import pallas as pl
from jax.experimental.pallas import tpu as pltpu

F32 = jnp.float32
BF16 = jnp.bfloat16

D_MODEL = 1024
SEQ = 2048
CTX_LEN = 256
DEPTH = 4
D_INNER = 2 * D_MODEL
GRID_W = 64
RMS_EPS = 1e-6
ROPE_THETA = 10000.0

MLSTM_HEADS = 8
MLSTM_HEAD_DIM = D_INNER // MLSTM_HEADS
MLSTM_QKV_BLOCK = 4
MLSTM_CONV = 5
MLSTM_CHUNK = 128

DIFF_HEADS = 16
HEAD_DIM = 64
NA_HEADS = 32
NA_WIN_ROWS = 8
NA_WIN_COLS = 16
SWA_Q_HEADS = 32
SWA_KV_HEADS = 4
SWA_WINDOW = 128

LANES = 128
MIB = 1024 * 1024
NEG_BIG = -1e30


def _cparams(semantics, vmem_mib):
    return pltpu.CompilerParams(dimension_semantics=semantics, vmem_limit_bytes=vmem_mib * MIB)


def _silu(v):
    return v * jax.nn.sigmoid(v)


def _adaln_kernel(c_ref, w_ref, b_ref, o_ref):
    s = _silu(c_ref[...])
    o_ref[0] = jnp.dot(s, w_ref[0], preferred_element_type=F32,
                       precision=lax.Precision.HIGHEST) + b_ref[0]


def _adaln(cvec, w_ada, b_ada):
    rows = cvec.shape[0]
    return pl.pallas_call(
        _adaln_kernel,
        grid=(DEPTH, 3),
        in_specs=[
            pl.BlockSpec((rows, D_MODEL), lambda i, j: (0, 0)),
            pl.BlockSpec((1, D_MODEL, D_MODEL), lambda i, j: (i, 0, j)),
            pl.BlockSpec((1, 1, D_MODEL), lambda i, j: (i, 0, j)),
        ],
        out_specs=pl.BlockSpec((1, rows, D_MODEL), lambda i, j: (i, 0, j)),
        out_shape=jax.ShapeDtypeStruct((DEPTH, rows, 3 * D_MODEL), F32),
        compiler_params=_cparams(("parallel", "parallel"), 32),
        name="adaln",
    )(cvec, w_ada, b_ada.reshape(DEPTH, 1, 3 * D_MODEL))


def _rope_tables():
    half = HEAD_DIM // 2
    inv = np.power(ROPE_THETA, -np.arange(0, half, 2, dtype=np.float64) / half)
    t = np.arange(SEQ)
    ang_r = (t // GRID_W).astype(np.float64)[:, None] * inv
    ang_c = (t % GRID_W).astype(np.float64)[:, None] * inv
    cos = np.concatenate([np.cos(ang_r), np.cos(ang_r), np.cos(ang_c), np.cos(ang_c)], axis=1)
    sin = np.concatenate([-np.sin(ang_r), np.sin(ang_r), -np.sin(ang_c), np.sin(ang_c)], axis=1)
    reps = LANES // HEAD_DIM
    return (jnp.asarray(np.tile(cos, (1, reps)), F32), jnp.asarray(np.tile(sin, (1, reps)), F32))


def _rope_chunk(x, cos, sin, lane_lo):
    fwd = pltpu.roll(x, LANES - 16, axis=1)
    bwd = pltpu.roll(x, 16, axis=1)
    return x * cos + jnp.where(lane_lo, fwd, bwd) * sin


def _inproj_kernel(*refs, tn, rope_tiles):
    has_rope = bool(rope_tiles)
    if has_rope:
        x_ref, g_ref, sh_ref, sc_ref, w_ref, cos_ref, sin_ref, o_ref, h_scr = refs
    else:
        x_ref, g_ref, sh_ref, sc_ref, w_ref, o_ref, h_scr = refs
    j = pl.program_id(1)

    @pl.when(j == 0)
    def _():
        x = x_ref[...]
        ms = jnp.mean(x * x, axis=-1, keepdims=True)
        y = x * lax.rsqrt(ms + RMS_EPS) * g_ref[...]
        h_scr[...] = (y * (1.0 + sc_ref[0]) + sh_ref[0]).astype(BF16)

    acc = jnp.dot(h_scr[...], w_ref[...], preferred_element_type=F32)

    def store_plain():
        o_ref[...] = acc.astype(BF16)

    def store_rope():
        cos = cos_ref[...]
        sin = sin_ref[...]
        lane = lax.broadcasted_iota(jnp.int32, (1, LANES), 1)
        lane_lo = (lane % 32) < 16
        for c in range(tn // LANES):
            blk = _rope_chunk(acc[:, c * LANES:(c + 1) * LANES], cos, sin, lane_lo)
            o_ref[:, c * LANES:(c + 1) * LANES] = blk.astype(BF16)

    if not has_rope:
        store_plain()
    else:
        is_rope = functools.reduce(jnp.logical_or, [(j >= a) & (j < b) for a, b in rope_tiles])
        pl.when(is_rope)(store_rope)
        pl.when(jnp.logical_not(is_rope))(store_plain)


def _inproj(x2d, norm_g, mods, w, *, rows_per_mod, mod_row0, tm, tn, rope_cols=(), rope_tabs=None):
    rows, _ = x2d.shape
    n = w.shape[1]
    assert rows % tm == 0 and n % tn == 0 and rows_per_mod % tm == 0
    tiles_per_mod = rows_per_mod // tm
    assert all(lo % tn == 0 and hi % tn == 0 for lo, hi in rope_cols)
    rope_tiles = tuple((lo // tn, hi // tn) for lo, hi in rope_cols)
    has_rope = bool(rope_tiles)
    in_specs = [
        pl.BlockSpec((tm, D_MODEL), lambda i, j: (i, 0)),
        pl.BlockSpec((1, D_MODEL), lambda i, j: (0, 0)),
        pl.BlockSpec((1, 1, D_MODEL), lambda i, j: (mod_row0 + i // tiles_per_mod, 0, 0)),
        pl.BlockSpec((1, 1, D_MODEL), lambda i, j: (mod_row0 + i // tiles_per_mod, 0, 1)),
        pl.BlockSpec((D_MODEL, tn), lambda i, j: (0, j)),
    ]
    args = [x2d, norm_g.reshape(1, D_MODEL), mods, mods, w]
    if has_rope:
        assert SEQ % tm == 0
        seq_tiles = SEQ // tm
        in_specs += [pl.BlockSpec((tm, LANES), lambda i, j: (i % seq_tiles, 0))] * 2
        args += list(rope_tabs)
    vmem = (2 * tm * D_MODEL * 4 + tm * D_MODEL * 2 + 2 * D_MODEL * tn * 2 + 2 * tm * tn * 2
            + 2 * tm * tn * 4 + 4 * tm * LANES * 4) // MIB + 8
    return pl.pallas_call(
        functools.partial(_inproj_kernel, tn=tn, rope_tiles=rope_tiles),
        grid=(rows // tm, n // tn),
        in_specs=in_specs,
        out_specs=pl.BlockSpec((tm, tn), lambda i, j: (i, j)),
        out_shape=jax.ShapeDtypeStruct((rows, n), BF16),
        scratch_shapes=[pltpu.VMEM((tm, D_MODEL), BF16)],
        compiler_params=_cparams(("parallel", "arbitrary"), vmem),
        name="inproj",
    )(*args)


def _outproj_kernel(o_ref, z_ref, w_ref, x_ref, gt_ref, *rest, final):
    z = z_ref[...].astype(F32)
    a = (o_ref[...].astype(F32) * _silu(z)).astype(BF16)
    y = jnp.dot(a, w_ref[...], preferred_element_type=F32)
    xn = x_ref[...] + gt_ref[0] * y
    if final:
        fg_ref, out_ref = rest
        ms = jnp.mean(xn * xn, axis=-1, keepdims=True)
        xn = xn * lax.rsqrt(ms + RMS_EPS) * fg_ref[...]
    else:
        (out_ref,) = rest
    out_ref[...] = xn


def _outproj(o2d, proj2d, z_col, w_out, x2d, mods, *, rows_per_mod, mod_row0, tm, final_g=None):
    rows = x2d.shape[0]
    assert rows % tm == 0 and rows_per_mod % tm == 0 and z_col % D_INNER == 0
    tiles_per_mod = rows_per_mod // tm
    zb = z_col // D_INNER
    final = final_g is not None
    in_specs = [
        pl.BlockSpec((tm, D_INNER), lambda i: (i, 0)),
        pl.BlockSpec((tm, D_INNER), lambda i: (i, zb)),
        pl.BlockSpec((D_INNER, D_MODEL), lambda i: (0, 0)),
        pl.BlockSpec((tm, D_MODEL), lambda i: (i, 0)),
        pl.BlockSpec((1, 1, D_MODEL), lambda i: (mod_row0 + i // tiles_per_mod, 0, 2)),
    ]
    args = [o2d, proj2d, w_out, x2d, mods]
    if final:
        in_specs.append(pl.BlockSpec((1, D_MODEL), lambda i: (0, 0)))
        args.append(final_g.reshape(1, D_MODEL))
    vmem = (4 * tm * D_INNER * 2 + 2 * D_INNER * D_MODEL * 2 + 4 * tm * D_MODEL * 4
            + 3 * tm * D_INNER * 4) // MIB + 8
    return pl.pallas_call(
        functools.partial(_outproj_kernel, final=final),
        grid=(rows // tm,),
        in_specs=in_specs,
        out_specs=pl.BlockSpec((tm, D_MODEL), lambda i: (i, 0)),
        out_shape=jax.ShapeDtypeStruct((rows, D_MODEL), F32),
        input_output_aliases={3: 0},
        compiler_params=_cparams(("parallel",), vmem),
        name="outproj",
    )(*args)


def _half_masks():
    lane = lax.broadcasted_iota(jnp.int32, (1, LANES), 1)
    return lane < HEAD_DIM, lane >= HEAD_DIM


def _nt(a, b):
    return lax.dot_general(a, b, (((1,), (1,)), ((), ())), preferred_element_type=F32)


def _split_heads(q):
    lo, hi = _half_masks()
    zero = jnp.zeros_like(q)
    scale = jnp.asarray(HEAD_DIM ** -0.5, q.dtype)
    return jnp.concatenate([jnp.where(lo, q, zero), jnp.where(hi, q, zero)], axis=0) * scale


def _diff_attn_kernel(*refs, has_lat, lam_init):
    if has_lat:
        q_ref, kl_ref, vl_ref, kc_ref, vc_ref, lqk_ref, g_ref, o_ref = refs
    else:
        q_ref, kc_ref, vc_ref, lqk_ref, g_ref, o_ref = refs
    tq = q_ref.shape[1]
    lqk = lqk_ref[...]
    s1 = jnp.sum(lqk[0:1] * lqk[1:2], axis=-1, keepdims=True)
    s2 = jnp.sum(lqk[2:3] * lqk[3:4], axis=-1, keepdims=True)
    lam = jnp.exp(s1) - jnp.exp(s2) + lam_init

    qq = _split_heads(q_ref[0])
    s_ctx = _nt(qq, kc_ref[0])
    m = jnp.max(s_ctx, axis=-1, keepdims=True)
    if has_lat:
        s_lat = _nt(qq, kl_ref[0])
        m = jnp.maximum(m, jnp.max(s_lat, axis=-1, keepdims=True))
    p_ctx = jnp.exp(s_ctx - m)
    den = jnp.sum(p_ctx, axis=-1, keepdims=True)
    if has_lat:
        p_lat = jnp.exp(s_lat - m)
        den = den + jnp.sum(p_lat, axis=-1, keepdims=True)
    r = 1.0 / den
    coef = jnp.concatenate([r[:tq], -lam * r[tq:]], axis=0)

    def combine(p):
        pc = p * coef
        return (pc[:tq] + pc[tq:]).astype(BF16)

    o = jnp.dot(combine(p_ctx), vc_ref[0], preferred_element_type=F32)
    if has_lat:
        o = o + jnp.dot(combine(p_lat), vl_ref[0], preferred_element_type=F32)
    ms = jnp.mean(o * o, axis=-1, keepdims=True)
    o = o * lax.rsqrt(ms + RMS_EPS) * g_ref[...] * (1.0 - lam_init)
    o_ref[0] = o.astype(BF16)


def _diff_attn(proj_q, proj_lat, proj_ctx, lqk, subln_g, *, tq, lam_init):
    bsz, t, _ = proj_q.shape
    has_lat = proj_lat is not None
    kb, vb = D_INNER // LANES, 2 * D_INNER // LANES
    in_specs = [pl.BlockSpec((1, tq, LANES), lambda b, h, i: (b, i, h))]
    args = [proj_q]
    if has_lat:
        in_specs += [pl.BlockSpec((1, SEQ, LANES), lambda b, h, i: (b, 0, kb + h)),
                     pl.BlockSpec((1, SEQ, LANES), lambda b, h, i: (b, 0, vb + h))]
        args += [proj_lat, proj_lat]
    in_specs += [pl.BlockSpec((1, CTX_LEN, LANES), lambda b, h, i: (b, 0, kb + h)),
                 pl.BlockSpec((1, CTX_LEN, LANES), lambda b, h, i: (b, 0, vb + h)),
                 pl.BlockSpec((4, HEAD_DIM), lambda b, h, i: (0, 0)),
                 pl.BlockSpec((1, LANES), lambda b, h, i: (0, 0))]
    args += [proj_ctx, proj_ctx, lqk, subln_g.reshape(1, LANES)]
    nk = CTX_LEN + (SEQ if has_lat else 0)
    vmem = (4 * 2 * tq * nk * 4) // MIB + 16
    return pl.pallas_call(
        functools.partial(_diff_attn_kernel, has_lat=has_lat, lam_init=lam_init),
        grid=(bsz, DIFF_HEADS, t // tq),
        in_specs=in_specs,
        out_specs=pl.BlockSpec((1, tq, LANES), lambda b, h, i: (b, i, h)),
        out_shape=jax.ShapeDtypeStruct((bsz, t, D_INNER), BF16),
        compiler_params=_cparams(("parallel", "parallel", "arbitrary"), vmem),
        name="diff_attn",
    )(*args)


def _ctx_attn_kernel(q_ref, k_ref, v_ref, o_ref):
    t = q_ref.shape[1]
    qq = _split_heads(q_ref[0])
    s = _nt(qq, k_ref[0])
    m = jnp.max(s, axis=-1, keepdims=True)
    p = jnp.exp(s - m)
    den = jnp.sum(p, axis=-1, keepdims=True)
    o = jnp.dot(p.astype(BF16), v_ref[0], preferred_element_type=F32) / den
    lo, _ = _half_masks()
    o_ref[0] = jnp.where(lo, o[:t], o[t:]).astype(BF16)


def _ctx_attn(proj_ctx, n_heads):
    bsz = proj_ctx.shape[0]
    pairs = n_heads // 2
    kb, vb = D_INNER // LANES, 2 * D_INNER // LANES
    return pl.pallas_call(
        _ctx_attn_kernel,
        grid=(bsz, pairs),
        in_specs=[pl.BlockSpec((1, CTX_LEN, LANES), lambda b, h: (b, 0, h)),
                  pl.BlockSpec((1, CTX_LEN, LANES), lambda b, h: (b, 0, kb + h)),
                  pl.BlockSpec((1, CTX_LEN, LANES), lambda b, h: (b, 0, vb + h))],
        out_specs=pl.BlockSpec((1, CTX_LEN, LANES), lambda b, h: (b, 0, h)),
        out_shape=jax.ShapeDtypeStruct((bsz, CTX_LEN, D_INNER), BF16),
        compiler_params=_cparams(("parallel", "parallel"), 32),
        name="ctx_attn",
    )(proj_ctx, proj_ctx, proj_ctx)


NA_ROWS = SEQ // GRID_W
NA_KWIN = NA_WIN_ROWS * GRID_W
NA_NOFF = 8


def _na_bias_table(rpb):
    c = np.arange(GRID_W)
    cs = np.clip(c - NA_WIN_COLS // 2, 0, GRID_W - NA_WIN_COLS)
    kc = np.arange(GRID_W)
    col_ok = (kc[None, :] >= cs[:, None]) & (kc[None, :] < cs[:, None] + NA_WIN_COLS)
    dc_idx = np.clip(kc[None, :] - c[:, None], -(NA_WIN_COLS - 1), NA_WIN_COLS - 1) + NA_WIN_COLS - 1
    off = np.arange(NA_NOFF)
    a = np.arange(NA_WIN_ROWS)
    dr_idx = off[:, None] + a[None, :]
    tab = rpb[:, dr_idx][:, :, :, dc_idx]
    tab = jnp.where(col_ok[None, None, None], tab, NEG_BIG)
    tab = jnp.transpose(tab, (0, 1, 3, 2, 4))
    return tab.reshape(rpb.shape[0], NA_NOFF, GRID_W, NA_KWIN).astype(F32)


def _na_attn_kernel(q_ref, k_ref, v_ref, kc_ref, vc_ref, bias_ref, o_ref):
    lo, hi = _half_masks()
    kctx = kc_ref[0]
    vctx = vc_ref[0]
    scale = jnp.asarray(HEAD_DIM ** -0.5, BF16)

    def row(r, carry):
        rs = jnp.clip(r - NA_WIN_ROWS // 2, 0, NA_ROWS - NA_WIN_ROWS)
        off = rs - r + (NA_WIN_ROWS - 1)
        q = q_ref[0, pl.ds(pl.multiple_of(r * GRID_W, GRID_W), GRID_W), :] * scale
        kw = k_ref[0, pl.ds(pl.multiple_of(rs * GRID_W, GRID_W), NA_KWIN), :]
        vw = v_ref[0, pl.ds(pl.multiple_of(rs * GRID_W, GRID_W), NA_KWIN), :]
        zero = jnp.zeros_like(q)
        outs = []
        for hl, hm in enumerate((lo, hi)):
            qm = jnp.where(hm, q, zero)
            s_lat = _nt(qm, kw) + bias_ref[hl, off]
            s_ctx = _nt(qm, kctx)
            m = jnp.maximum(jnp.max(s_lat, axis=-1, keepdims=True), jnp.max(s_ctx, axis=-1, keepdims=True))
            p_lat = jnp.exp(s_lat - m)
            p_ctx = jnp.exp(s_ctx - m)
            den = jnp.sum(p_lat, axis=-1, keepdims=True) + jnp.sum(p_ctx, axis=-1, keepdims=True)
            o = (jnp.dot(p_lat.astype(BF16), vw, preferred_element_type=F32)
                 + jnp.dot(p_ctx.astype(BF16), vctx, preferred_element_type=F32))
            outs.append(o / den)
        o_ref[0, pl.ds(pl.multiple_of(r * GRID_W, GRID_W), GRID_W), :] = (
            jnp.where(lo, outs[0], outs[1]).astype(BF16))
        return carry

    lax.fori_loop(0, NA_ROWS, row, 0)


def _na_attn(proj_lat, proj_ctx, bias_tab):
    bsz = proj_lat.shape[0]
    pairs = NA_HEADS // 2
    kb, vb = D_INNER // LANES, 2 * D_INNER // LANES
    return pl.pallas_call(
        _na_attn_kernel,
        grid=(pairs, bsz),
        in_specs=[pl.BlockSpec((1, SEQ, LANES), lambda h, b: (b, 0, h)),
                  pl.BlockSpec((1, SEQ, LANES), lambda h, b: (b, 0, kb + h)),
                  pl.BlockSpec((1, SEQ, LANES), lambda h, b: (b, 0, vb + h)),
                  pl.BlockSpec((1, CTX_LEN, LANES), lambda h, b: (b, 0, kb + h)),
                  pl.BlockSpec((1, CTX_LEN, LANES), lambda h, b: (b, 0, vb + h)),
                  pl.BlockSpec((2, NA_NOFF, GRID_W, NA_KWIN), lambda h, b: (h, 0, 0, 0))],
        out_specs=pl.BlockSpec((1, SEQ, LANES), lambda h, b: (b, 0, h)),
        out_shape=jax.ShapeDtypeStruct((bsz, SEQ, D_INNER), BF16),
        compiler_params=_cparams(("parallel", "arbitrary"), 32),
        name="na_attn",
    )(proj_lat, proj_lat, proj_lat, proj_ctx, proj_ctx, bias_tab)


SWA_TQ = 256
SWA_BAND = SWA_TQ + 2 * SWA_WINDOW
SWA_K_COL = 2 * D_INNER
SWA_V_COL = SWA_K_COL + SWA_KV_HEADS * LANES
SWA_N = SWA_V_COL + SWA_KV_HEADS * LANES


def _swa_attn_kernel(q_ref, k_ref, v_ref, kc_ref, vc_ref, sink_ref, o_ref):
    lo, _ = _half_masks()
    pair = pl.program_id(1)
    kctx = kc_ref[0]
    vctx = vc_ref[0]
    sink2 = sink_ref[pl.ds(pl.multiple_of(pair * 2, 2), 2), :][:, 0:1]
    sink = jnp.concatenate([jnp.broadcast_to(sink2[0:1], (SWA_TQ, 1)),
                            jnp.broadcast_to(sink2[1:2], (SWA_TQ, 1))], axis=0)

    def block(i, carry):
        t0 = pl.multiple_of(i * SWA_TQ, SWA_TQ)
        s0 = pl.multiple_of(jnp.clip(i * SWA_TQ - SWA_WINDOW, 0, SEQ - SWA_BAND), SWA_WINDOW)
        qq = _split_heads(q_ref[0, pl.ds(t0, SWA_TQ), :])
        kw = k_ref[0, pl.ds(s0, SWA_BAND), :]
        vw = v_ref[0, pl.ds(s0, SWA_BAND), :]
        tpos = t0 + lax.broadcasted_iota(jnp.int32, (SWA_TQ, SWA_BAND), 0)
        spos = s0 + lax.broadcasted_iota(jnp.int32, (SWA_TQ, SWA_BAND), 1)
        ok = jnp.abs(tpos - spos) <= SWA_WINDOW
        ok = jnp.concatenate([ok, ok], axis=0)
        s_lat = jnp.where(ok, _nt(qq, kw), NEG_BIG)
        s_ctx = _nt(qq, kctx)
        m = jnp.maximum(jnp.maximum(jnp.max(s_lat, axis=-1, keepdims=True),
                                    jnp.max(s_ctx, axis=-1, keepdims=True)), sink)
        p_lat = jnp.exp(s_lat - m)
        p_ctx = jnp.exp(s_ctx - m)
        den = (jnp.sum(p_lat, axis=-1, keepdims=True) + jnp.sum(p_ctx, axis=-1, keepdims=True)
               + jnp.exp(sink - m))
        o = (jnp.dot(p_lat.astype(BF16), vw, preferred_element_type=F32)
             + jnp.dot(p_ctx.astype(BF16), vctx, preferred_element_type=F32)) / den
        o_ref[0, pl.ds(t0, SWA_TQ), :] = jnp.where(lo, o[:SWA_TQ], o[SWA_TQ:]).astype(BF16)
        return carry

    lax.fori_loop(0, SEQ // SWA_TQ, block, 0)


def _swa_attn(proj_lat, proj_ctx, sink_rows):
    bsz = proj_lat.shape[0]
    pairs = SWA_Q_HEADS // 2
    group = SWA_Q_HEADS // SWA_KV_HEADS
    kb, vb = SWA_K_COL // LANES, SWA_V_COL // LANES

    def kv_of(h):
        return (2 * h) // group

    return pl.pallas_call(
        _swa_attn_kernel,
        grid=(bsz, pairs),
        in_specs=[pl.BlockSpec((1, SEQ, LANES), lambda b, h: (b, 0, h)),
                  pl.BlockSpec((1, SEQ, LANES), lambda b, h: (b, 0, kb + kv_of(h))),
                  pl.BlockSpec((1, SEQ, LANES), lambda b, h: (b, 0, vb + kv_of(h))),
                  pl.BlockSpec((1, CTX_LEN, LANES), lambda b, h: (b, 0, kb + kv_of(h))),
                  pl.BlockSpec((1, CTX_LEN, LANES), lambda b, h: (b, 0, vb + kv_of(h))),
                  pl.BlockSpec((SWA_Q_HEADS, LANES), lambda b, h: (0, 0))],
        out_specs=pl.BlockSpec((1, SEQ, LANES), lambda b, h: (b, 0, h)),
        out_shape=jax.ShapeDtypeStruct((bsz, SEQ, D_INNER), BF16),
        compiler_params=_cparams(("parallel", "arbitrary"), 32),
        name="swa_attn",
    )(proj_lat, proj_lat, proj_lat, proj_ctx, proj_ctx, sink_rows)


GATE_LANES = 128


def _mlstm_prep_kernel(xm_ref, cw_ref, cb_ref, bdq_ref, bdk_ref, bdv_ref, wg_ref, bg_ref,
                       q_ref, kt_ref, v_ref, xc_ref, g_ref):
    h = pl.program_id(1)
    t = xm_ref.shape[1]
    xm_b = xm_ref[0]
    xm = xm_b.astype(F32)
    tpos = lax.broadcasted_iota(jnp.int32, (t, 1), 0)
    conv = jnp.zeros_like(xm) + cb_ref[...]
    for j in range(MLSTM_CONV):
        d = j - MLSTM_CONV // 2
        sh = pltpu.roll(xm, (-d) % t, axis=0) if d else xm
        ok = (tpos + d >= 0) & (tpos + d < t)
        conv = conv + jnp.where(ok, sh, 0.0) * cw_ref[j:j + 1, :]
    xc = _silu(conv)
    xc_b = xc.astype(BF16)
    q = jnp.dot(xc_b, bdq_ref[0], preferred_element_type=F32)
    k = jnp.dot(xc_b, bdk_ref[0], preferred_element_type=F32)
    v = jnp.dot(xm_b, bdv_ref[0], preferred_element_type=F32)
    q_b, k_b, v_b = q.astype(BF16), k.astype(BF16), v.astype(BF16)
    q_ref[0] = q_b
    v_ref[0] = v_b
    xc_ref[0] = xc_b
    ks = k * (MLSTM_HEAD_DIM ** -0.5)
    for c in range(t // MLSTM_CHUNK):
        kt_ref[0, 0, c] = ks[c * MLSTM_CHUNK:(c + 1) * MLSTM_CHUNK, :].T.astype(BF16)
    part = (jnp.dot(q_b, wg_ref[0, 0], preferred_element_type=F32)
            + jnp.dot(k_b, wg_ref[1, 0], preferred_element_type=F32)
            + jnp.dot(v_b, wg_ref[2, 0], preferred_element_type=F32))

    @pl.when(h == 0)
    def _():
        g_ref[0] = part + bg_ref[...]

    @pl.when(h != 0)
    def _():
        g_ref[0] = g_ref[0] + part


def _mlstm_prep(proj3, conv_w, conv_b, bdq, bdk, bdv, wg, bg):
    bsz, t, _ = proj3.shape
    hd = MLSTM_HEAD_DIM
    nc = t // MLSTM_CHUNK
    seq_spec = pl.BlockSpec((1, t, hd), lambda b, h: (b, 0, h))
    return pl.pallas_call(
        _mlstm_prep_kernel,
        grid=(bsz, MLSTM_HEADS),
        in_specs=[seq_spec,
                  pl.BlockSpec((MLSTM_CONV, hd), lambda b, h: (0, h)),
                  pl.BlockSpec((1, hd), lambda b, h: (0, h)),
                  pl.BlockSpec((1, hd, hd), lambda b, h: (h, 0, 0)),
                  pl.BlockSpec((1, hd, hd), lambda b, h: (h, 0, 0)),
                  pl.BlockSpec((1, hd, hd), lambda b, h: (h, 0, 0)),
                  pl.BlockSpec((3, 1, hd, GATE_LANES), lambda b, h: (0, h, 0, 0)),
                  pl.BlockSpec((1, GATE_LANES), lambda b, h: (0, 0))],
        out_specs=[seq_spec,
                   pl.BlockSpec((1, 1, nc, hd, MLSTM_CHUNK), lambda b, h: (b, h, 0, 0, 0)),
                   seq_spec, seq_spec,
                   pl.BlockSpec((1, t, GATE_LANES), lambda b, h: (b, 0, 0))],
        out_shape=[jax.ShapeDtypeStruct((bsz, t, D_INNER), BF16),
                   jax.ShapeDtypeStruct((bsz, MLSTM_HEADS, nc, hd, MLSTM_CHUNK), BF16),
                   jax.ShapeDtypeStruct((bsz, t, D_INNER), BF16),
                   jax.ShapeDtypeStruct((bsz, t, D_INNER), BF16),
                   jax.ShapeDtypeStruct((bsz, t, GATE_LANES), F32)],
        compiler_params=_cparams(("parallel", "arbitrary"), 48),
        name="mlstm_prep",
    )(proj3, conv_w, conv_b.reshape(1, D_INNER), bdq, bdk, bdv, wg, bg)


def _lane_scan(x, op, fill, reverse):
    lane = lax.broadcasted_iota(jnp.int32, x.shape, 1)
    sh = 1
    while sh < LANES:
        if reverse:
            y = jnp.where(lane < LANES - sh, pltpu.roll(x, LANES - sh, axis=1), fill)
        else:
            y = jnp.where(lane >= sh, pltpu.roll(x, sh, axis=1), fill)
        x = op(x, y)
        sh *= 2
    return x


GP_ROWS = 8


def _mlstm_gates_kernel(gc_ref, gl_ref, oc_ref, ol_ref):
    nh = MLSTM_HEADS
    gct = gc_ref[0].T
    glt = gl_ref[0].T
    n_c = gc_ref.shape[1] // MLSTM_CHUNK
    n_l = gl_ref.shape[1] // MLSTM_CHUNK
    zeros = jnp.zeros((nh, LANES), F32)
    for d in range(2):
        rev = d == 1
        m = jnp.zeros((nh, 1), F32)
        order = [(gct, oc_ref, c) for c in range(n_c)] + [(glt, ol_ref, c) for c in range(n_l)]
        if rev:
            order = ([(gct, oc_ref, c) for c in reversed(range(n_c))]
                     + [(glt, ol_ref, c) for c in reversed(range(n_l))])
        for src, dst, c in order:
            sl = slice(c * MLSTM_CHUNK, (c + 1) * MLSTM_CHUNK)
            ig = src[2 * d * nh:(2 * d + 1) * nh, sl]
            fg = src[(2 * d + 1) * nh:(2 * d + 2) * nh, sl]
            lf = jnp.minimum(fg, 0.0) - jnp.log1p(jnp.exp(-jnp.abs(fg)))
            b = _lane_scan(lf, jnp.add, 0.0, rev)
            cm = _lane_scan(ig - b, jnp.maximum, -jnp.inf, rev)
            mx = jnp.maximum(cm, m)
            m_t = b + mx
            b_end = b[:, 0:1] if rev else b[:, LANES - 1:LANES]
            g = b_end - b + ig
            m_new = jnp.maximum(b_end + m, jnp.max(g, axis=-1, keepdims=True))
            dst[0, d, c, 0] = -mx
            dst[0, d, c, 1] = b - ig
            dst[0, d, c, 2] = jnp.exp(m - mx)
            dst[0, d, c, 3] = jnp.exp(-m_t)
            dst[0, d, c, 4] = jnp.exp(g - m_new)
            dst[0, d, c, 5] = jnp.broadcast_to(jnp.exp(b_end + m - m_new), (nh, LANES))
            dst[0, d, c, 6] = zeros
            dst[0, d, c, 7] = zeros
            m = m_new


def _mlstm_gates(gates_ctx, gates_lat):
    bsz = gates_lat.shape[0]
    n_c, n_l = CTX_LEN // MLSTM_CHUNK, SEQ // MLSTM_CHUNK

    def out(n):
        shape = (bsz, 2, n, GP_ROWS, MLSTM_HEADS, LANES)
        return (pl.BlockSpec((1,) + shape[1:], lambda b: (b, 0, 0, 0, 0, 0)), jax.ShapeDtypeStruct(shape, F32))

    (spec_c, shape_c), (spec_l, shape_l) = out(n_c), out(n_l)
    return pl.pallas_call(
        _mlstm_gates_kernel,
        grid=(bsz,),
        in_specs=[pl.BlockSpec((1, CTX_LEN, GATE_LANES), lambda b: (b, 0, 0)),
                  pl.BlockSpec((1, SEQ, GATE_LANES), lambda b: (b, 0, 0))],
        out_specs=[spec_c, spec_l],
        out_shape=[shape_c, shape_l],
        compiler_params=_cparams(("parallel",), 32),
        name="mlstm_gates",
    )(gates_ctx, gates_lat)


def _mlstm_scan_kernel(qc_ref, ktc_ref, vc_ref, xcc_ref, gpc_ref,
                       ql_ref, ktl_ref, vl_ref, xcl_ref, gpl_ref, ng_ref, sk_ref,
                       oc_ref, ol_ref, hc_scr, hl_scr, st_scr):
    L = MLSTM_CHUNK
    hd = MLSTM_HEAD_DIM
    n_c = qc_ref.shape[1] // L
    n_l = ql_ref.shape[1] // L
    lane = lax.broadcasted_iota(jnp.int32, (L, LANES), 1)
    ones_col = jnp.where(lane == 0, 1.0, 0.0).astype(BF16)
    tt = lax.broadcasted_iota(jnp.int32, (L, L), 0)
    ss = lax.broadcasted_iota(jnp.int32, (L, L), 1)
    masks = (ss <= tt, ss >= tt)

    hc_scr[...] = jnp.zeros_like(hc_scr)
    hl_scr[...] = jnp.zeros_like(hl_scr)
    st_scr[...] = jnp.zeros_like(st_scr)

    def chunk(d, q_ref, kt_ref, v_ref, gp_ref, h_scr, c):
        rows = gp_ref[0, d, 0, c]
        cols = rows.T
        a_col, wst, einv = cols[:, 0:1], cols[:, 2:3], cols[:, 3:4]
        c_row, ws_row, decay = rows[1:2, :], rows[4:5, :], rows[5:6, 0:1]
        r0 = pl.multiple_of(c * L, L)
        q = q_ref[0, pl.ds(r0, L), :]
        kt = kt_ref[0, 0, c]
        vx = jnp.concatenate([v_ref[0, pl.ds(r0, L), :], ones_col], axis=1)
        w = jnp.exp(jnp.where(masks[d], a_col - c_row, -jnp.inf))
        s = (jnp.dot(q, kt, preferred_element_type=F32) * w).astype(BF16)
        state = st_scr[d]
        tot = (wst * jnp.dot(q, state.astype(BF16), preferred_element_type=F32)
               + jnp.dot(s, vx, preferred_element_type=F32))
        den = tot[:, hd:hd + 1]
        h = tot[:, :hd] / jnp.maximum(jnp.abs(den), einv)
        h_scr[pl.ds(r0, L), :] = h_scr[pl.ds(r0, L), :] + h
        kw = (kt.astype(F32) * ws_row).astype(BF16)
        st_scr[d] = decay * state + jnp.dot(kw, vx, preferred_element_type=F32)

    for j in range(n_c):
        chunk(0, qc_ref, ktc_ref, vc_ref, gpc_ref, hc_scr, j)
        chunk(1, qc_ref, ktc_ref, vc_ref, gpc_ref, hc_scr, n_c - 1 - j)

    def lat_step(j, carry):
        chunk(0, ql_ref, ktl_ref, vl_ref, gpl_ref, hl_scr, j)
        chunk(1, ql_ref, ktl_ref, vl_ref, gpl_ref, hl_scr, n_l - 1 - j)
        return carry

    lax.fori_loop(0, n_l, lat_step, 0)

    ng = ng_ref[...]
    sk = sk_ref[...]

    def finish(h_scr, xc_ref, o_ref, r0, n):
        h = h_scr[pl.ds(r0, n), :]
        mu = jnp.mean(h, axis=-1, keepdims=True)
        hcen = h - mu
        hn = hcen * lax.rsqrt(jnp.mean(hcen * hcen, axis=-1, keepdims=True) + RMS_EPS)
        o_ref[0, pl.ds(r0, n), :] = (hn * ng + sk * xc_ref[0, pl.ds(r0, n), :].astype(F32)).astype(BF16)

    finish(hc_scr, xcc_ref, oc_ref, 0, CTX_LEN)

    def fin_step(i, carry):
        finish(hl_scr, xcl_ref, ol_ref, pl.multiple_of(i * 256, 256), 256)
        return carry

    lax.fori_loop(0, SEQ // 256, fin_step, 0)


def _mlstm_scan(prep_c, gp_c, prep_l, gp_l, norm_g, skip):
    bsz = prep_l[0].shape[0]
    hd = MLSTM_HEAD_DIM

    def seq_specs(t):
        nc = t // MLSTM_CHUNK
        seq = pl.BlockSpec((1, t, hd), lambda b, h: (b, 0, h))
        return [seq,
                pl.BlockSpec((1, 1, nc, hd, MLSTM_CHUNK), lambda b, h: (b, h, 0, 0, 0)),
                seq, seq,
                pl.BlockSpec((1, 2, 1, nc, GP_ROWS, LANES), lambda b, h: (b, 0, h, 0, 0, 0))]

    vec = pl.BlockSpec((1, hd), lambda b, h: (0, h))
    return pl.pallas_call(
        _mlstm_scan_kernel,
        grid=(bsz, MLSTM_HEADS),
        in_specs=seq_specs(CTX_LEN) + seq_specs(SEQ) + [vec, vec],
        out_specs=[pl.BlockSpec((1, CTX_LEN, hd), lambda b, h: (b, 0, h)),
                   pl.BlockSpec((1, SEQ, hd), lambda b, h: (b, 0, h))],
        out_shape=[jax.ShapeDtypeStruct((bsz, CTX_LEN, D_INNER), BF16),
                   jax.ShapeDtypeStruct((bsz, SEQ, D_INNER), BF16)],
        scratch_shapes=[pltpu.VMEM((CTX_LEN, hd), F32), pltpu.VMEM((SEQ, hd), F32),
                        pltpu.VMEM((2, hd, hd + LANES), F32)],
        compiler_params=_cparams(("parallel", "parallel"), 40),
        name="mlstm_scan",
    )(*prep_c, gp_c, *prep_l, gp_l, norm_g.reshape(1, D_INNER), skip.reshape(1, D_INNER))


def _blockdiag_dense(w):
    per_head = MLSTM_HEAD_DIM // MLSTM_QKV_BLOCK
    wh = w.reshape(MLSTM_HEADS, per_head, MLSTM_QKV_BLOCK, MLSTM_QKV_BLOCK)
    eye = jnp.eye(per_head, dtype=w.dtype)
    dense = wh[:, :, :, None, :] * eye[None, :, None, :, None]
    return dense.reshape(MLSTM_HEADS, MLSTM_HEAD_DIM, MLSTM_HEAD_DIM).astype(BF16)


def _mlstm_mixer(proj_lat, proj_ctx, conv_w, conv_b, wq, wk, wv, w_gate, b_gate, norm_g, skip):
    n_gate = 4 * MLSTM_HEADS
    wg = jnp.pad(w_gate, ((0, 0), (0, GATE_LANES - n_gate))).astype(BF16)
    wg = wg.reshape(3, MLSTM_HEADS, MLSTM_HEAD_DIM, GATE_LANES)
    bg = jnp.pad(b_gate, (0, GATE_LANES - n_gate)).reshape(1, GATE_LANES)
    bds = (_blockdiag_dense(wq), _blockdiag_dense(wk), _blockdiag_dense(wv))
    *prep_c, gates_c = _mlstm_prep(proj_ctx, conv_w, conv_b, *bds, wg, bg)
    *prep_l, gates_l = _mlstm_prep(proj_lat, conv_w, conv_b, *bds, wg, bg)
    gp_c, gp_l = _mlstm_gates(gates_c, gates_l)
    gp_c = jnp.transpose(gp_c, (0, 1, 4, 2, 3, 5))
    gp_l = jnp.transpose(gp_l, (0, 1, 4, 2, 3, 5))
    return _mlstm_scan(prep_c, gp_c, prep_l, gp_l, norm_g, skip)


def kernel(x, c, ctx, c_ctx, norm_g, w_ada, b_ada, final_g, mlstm_w_in, mlstm_conv_w, mlstm_conv_b, mlstm_wq, mlstm_wk, mlstm_wv, mlstm_w_gate, mlstm_b_gate, mlstm_norm_g, mlstm_skip, mlstm_w_out, diff_w_in, diff_lq1, diff_lk1, diff_lq2, diff_lk2, diff_subln_g, diff_w_out, na_w_in, na_rpb, na_w_out, swa_w_in, swa_sink, swa_w_out):
    bsz = x.shape[0]
    ctx_row = bsz
    mod_rows = -(-(bsz + 1) // 8) * 8
    cvec = jnp.zeros((mod_rows, D_MODEL), F32).at[:bsz].set(c).at[ctx_row].set(c_ctx)
    mods_all = _adaln(cvec, w_ada, b_ada)
    rope_tabs = _rope_tables()

    xl = x.reshape(bsz * SEQ, D_MODEL)
    xc = ctx.reshape(bsz * CTX_LEN, D_MODEL)
    lat = dict(rows_per_mod=SEQ, mod_row0=0)
    con = dict(rows_per_mod=bsz * CTX_LEN, mod_row0=ctx_row)
    tm_ctx = min(1024, bsz * CTX_LEN)

    def in_both(i, w, tn, rope_cols=()):
        mods = mods_all[i].reshape(mod_rows, 1, 3 * D_MODEL)
        wb = w.astype(BF16)
        pl_ = _inproj(xl, norm_g[i], mods, wb, tm=1024, tn=tn, rope_cols=rope_cols, rope_tabs=rope_tabs, **lat)
        pc_ = _inproj(xc, norm_g[i], mods, wb, tm=tm_ctx, tn=tn, **con)
        return mods, pl_, pc_

    def out_both(mods, o_lat, o_ctx, p_lat, p_ctx, z_col, w_out, need_ctx, final=None):
        wb = w_out.astype(BF16)
        new_l = _outproj(o_lat.reshape(bsz * SEQ, D_INNER), p_lat, z_col, wb, xl, mods, tm=512,
                         final_g=final, **lat)
        new_c = xc
        if need_ctx:
            new_c = _outproj(o_ctx.reshape(bsz * CTX_LEN, D_INNER), p_ctx, z_col, wb, xc, mods, tm=512, **con)
        return new_l, new_c

    mods, p_l, p_c = in_both(0, mlstm_w_in[0], 1024)
    o_c, o_l = _mlstm_mixer(p_l.reshape(bsz, SEQ, -1), p_c.reshape(bsz, CTX_LEN, -1), mlstm_conv_w[0],
                            mlstm_conv_b[0], mlstm_wq[0], mlstm_wk[0], mlstm_wv[0], mlstm_w_gate[0],
                            mlstm_b_gate[0], mlstm_norm_g[0], mlstm_skip[0])
    xl, xc = out_both(mods, o_l, o_c, p_l, p_c, D_INNER, mlstm_w_out[0], True)

    mods, p_l, p_c = in_both(1, diff_w_in[0], 1024, rope_cols=((0, 2 * D_INNER),))
    p_l3, p_c3 = p_l.reshape(bsz, SEQ, -1), p_c.reshape(bsz, CTX_LEN, -1)
    lam_init = 0.8 - 0.6 * math.exp(-0.3 * 1)
    lqk = jnp.stack([diff_lq1[0], diff_lk1[0], diff_lq2[0], diff_lk2[0]])
    o_l = _diff_attn(p_l3, p_l3, p_c3, lqk, diff_subln_g[0], tq=512, lam_init=lam_init)
    o_c = _diff_attn(p_c3, None, p_c3, lqk, diff_subln_g[0], tq=CTX_LEN, lam_init=lam_init)
    xl, xc = out_both(mods, o_l, o_c, p_l, p_c, 3 * D_INNER, diff_w_out[0], True)

    mods, p_l, p_c = in_both(2, na_w_in[0], 1024)
    p_l3, p_c3 = p_l.reshape(bsz, SEQ, -1), p_c.reshape(bsz, CTX_LEN, -1)
    o_l = _na_attn(p_l3, p_c3, _na_bias_table(na_rpb[0]))
    o_c = _ctx_attn(p_c3, NA_HEADS)
    xl, xc = out_both(mods, o_l, o_c, p_l, p_c, 3 * D_INNER, na_w_out[0], True)

    kvw = SWA_KV_HEADS * HEAD_DIM
    w = swa_w_in[0]
    wq, wk, wv, wz = (w[:, :D_INNER], w[:, D_INNER:D_INNER + kvw], w[:, D_INNER + kvw:D_INNER + 2 * kvw],
                      w[:, D_INNER + 2 * kvw:])

    def twice(m):
        m = m.reshape(D_MODEL, SWA_KV_HEADS, 1, HEAD_DIM)
        return jnp.broadcast_to(m, (D_MODEL, SWA_KV_HEADS, 2, HEAD_DIM)).reshape(D_MODEL, SWA_KV_HEADS * LANES)

    w3 = jnp.concatenate([wq, wz, twice(wk), twice(wv)], axis=1)
    mods, p_l, p_c = in_both(3, w3, 512, rope_cols=((0, D_INNER), (SWA_K_COL, SWA_V_COL)))
    sink_rows = jnp.broadcast_to(swa_sink[0][:, None], (SWA_Q_HEADS, LANES))
    o_l = _swa_attn(p_l.reshape(bsz, SEQ, -1), p_c.reshape(bsz, CTX_LEN, -1), sink_rows)
    xl, _ = out_both(mods, o_l, None, p_l, p_c, D_INNER, swa_w_out[0], False, final=final_g)
    return xl.reshape(bsz, SEQ, D_MODEL)
```

```python
import functools
import math

import numpy as np
import jax
import jax.numpy as jnp
from jax import lax
from jax.experimental import pallas as pl
from jax.experimental.pallas import tpu as pltpu

F32 = jnp.float32
BF16 = jnp.bfloat16

D_MODEL = 1024
SEQ = 2048
CTX_LEN = 256
DEPTH = 4
D_INNER = 2 * D_MODEL
GRID_W = 64
RMS_EPS = 1e-6
ROPE_THETA = 10000.0

MLSTM_HEADS = 8
MLSTM_HEAD_DIM = D_INNER // MLSTM_HEADS
MLSTM_QKV_BLOCK = 4
MLSTM_CONV = 5
MLSTM_CHUNK = 128

DIFF_HEADS = 16
HEAD_DIM = 64
NA_HEADS = 32
NA_WIN_ROWS = 8
NA_WIN_COLS = 16
SWA_Q_HEADS = 32
SWA_KV_HEADS = 4
SWA_WINDOW = 128

LANES = 128
MIB = 1024 * 1024
NEG_BIG = -1e30


def _cparams(semantics, vmem_mib):
    return pltpu.CompilerParams(dimension_semantics=semantics, vmem_limit_bytes=vmem_mib * MIB)


def _silu(v):
    return v * jax.nn.sigmoid(v)


def _adaln_kernel(c_ref, w_ref, b_ref, o_ref):
    s = _silu(c_ref[...])
    o_ref[0] = jnp.dot(s, w_ref[0], preferred_element_type=F32,
                       precision=lax.Precision.HIGHEST) + b_ref[0]


def _adaln(cvec, w_ada, b_ada):
    rows = cvec.shape[0]
    return pl.pallas_call(
        _adaln_kernel,
        grid=(DEPTH, 3),
        in_specs=[
            pl.BlockSpec((rows, D_MODEL), lambda i, j: (0, 0)),
            pl.BlockSpec((1, D_MODEL, D_MODEL), lambda i, j: (i, 0, j)),
            pl.BlockSpec((1, 1, D_MODEL), lambda i, j: (i, 0, j)),
        ],
        out_specs=pl.BlockSpec((1, rows, D_MODEL), lambda i, j: (i, 0, j)),
        out_shape=jax.ShapeDtypeStruct((DEPTH, rows, 3 * D_MODEL), F32),
        compiler_params=_cparams(("parallel", "parallel"), 32),
        name="adaln",
    )(cvec, w_ada, b_ada.reshape(DEPTH, 1, 3 * D_MODEL))


def _rope_tables():
    half = HEAD_DIM // 2
    inv = np.power(ROPE_THETA, -np.arange(0, half, 2, dtype=np.float64) / half)
    t = np.arange(SEQ)
    ang_r = (t // GRID_W).astype(np.float64)[:, None] * inv
    ang_c = (t % GRID_W).astype(np.float64)[:, None] * inv
    cos = np.concatenate([np.cos(ang_r), np.cos(ang_r), np.cos(ang_c), np.cos(ang_c)], axis=1)
    sin = np.concatenate([-np.sin(ang_r), np.sin(ang_r), -np.sin(ang_c), np.sin(ang_c)], axis=1)
    reps = LANES // HEAD_DIM
    return (jnp.asarray(np.tile(cos, (1, reps)), F32), jnp.asarray(np.tile(sin, (1, reps)), F32))


def _rope_chunk(x, cos, sin, lane_lo):
    fwd = pltpu.roll(x, LANES - 16, axis=1)
    bwd = pltpu.roll(x, 16, axis=1)
    return x * cos + jnp.where(lane_lo, fwd, bwd) * sin


def _inproj_kernel(*refs, tn, rope_tiles):
    has_rope = bool(rope_tiles)
    if has_rope:
        x_ref, g_ref, sh_ref, sc_ref, w_ref, cos_ref, sin_ref, o_ref, h_scr = refs
    else:
        x_ref, g_ref, sh_ref, sc_ref, w_ref, o_ref, h_scr = refs
    j = pl.program_id(1)

    @pl.when(j == 0)
    def _():
        x = x_ref[...]
        ms = jnp.mean(x * x, axis=-1, keepdims=True)
        y = x * lax.rsqrt(ms + RMS_EPS) * g_ref[...]
        h_scr[...] = (y * (1.0 + sc_ref[0]) + sh_ref[0]).astype(BF16)

    acc = jnp.dot(h_scr[...], w_ref[...], preferred_element_type=F32)

    def store_plain():
        o_ref[...] = acc.astype(BF16)

    def store_rope():
        cos = cos_ref[...]
        sin = sin_ref[...]
        lane = lax.broadcasted_iota(jnp.int32, (1, LANES), 1)
        lane_lo = (lane % 32) < 16
        for c in range(tn // LANES):
            blk = _rope_chunk(acc[:, c * LANES:(c + 1) * LANES], cos, sin, lane_lo)
            o_ref[:, c * LANES:(c + 1) * LANES] = blk.astype(BF16)

    if not has_rope:
        store_plain()
    else:
        is_rope = functools.reduce(jnp.logical_or, [(j >= a) & (j < b) for a, b in rope_tiles])
        pl.when(is_rope)(store_rope)
        pl.when(jnp.logical_not(is_rope))(store_plain)


def _inproj(x2d, norm_g, mods, w, *, rows_per_mod, mod_row0, tm, tn, rope_cols=(), rope_tabs=None):
    rows, _ = x2d.shape
    n = w.shape[1]
    assert rows % tm == 0 and n % tn == 0 and rows_per_mod % tm == 0
    tiles_per_mod = rows_per_mod // tm
    assert all(lo % tn == 0 and hi % tn == 0 for lo, hi in rope_cols)
    rope_tiles = tuple((lo // tn, hi // tn) for lo, hi in rope_cols)
    has_rope = bool(rope_tiles)
    in_specs = [
        pl.BlockSpec((tm, D_MODEL), lambda i, j: (i, 0)),
        pl.BlockSpec((1, D_MODEL), lambda i, j: (0, 0)),
        pl.BlockSpec((1, 1, D_MODEL), lambda i, j: (mod_row0 + i // tiles_per_mod, 0, 0)),
        pl.BlockSpec((1, 1, D_MODEL), lambda i, j: (mod_row0 + i // tiles_per_mod, 0, 1)),
        pl.BlockSpec((D_MODEL, tn), lambda i, j: (0, j)),
    ]
    args = [x2d, norm_g.reshape(1, D_MODEL), mods, mods, w]
    if has_rope:
        assert SEQ % tm == 0
        seq_tiles = SEQ // tm
        in_specs += [pl.BlockSpec((tm, LANES), lambda i, j: (i % seq_tiles, 0))] * 2
        args += list(rope_tabs)
    vmem = (2 * tm * D_MODEL * 4 + tm * D_MODEL * 2 + 2 * D_MODEL * tn * 2 + 2 * tm * tn * 2
            + 2 * tm * tn * 4 + 4 * tm * LANES * 4) // MIB + 8
    return pl.pallas_call(
        functools.partial(_inproj_kernel, tn=tn, rope_tiles=rope_tiles),
        grid=(rows // tm, n // tn),
        in_specs=in_specs,
        out_specs=pl.BlockSpec((tm, tn), lambda i, j: (i, j)),
        out_shape=jax.ShapeDtypeStruct((rows, n), BF16),
        scratch_shapes=[pltpu.VMEM((tm, D_MODEL), BF16)],
        compiler_params=_cparams(("parallel", "arbitrary"), vmem),
        name="inproj",
    )(*args)


def _outproj_kernel(o_ref, z_ref, w_ref, x_ref, gt_ref, *rest, final):
    z = z_ref[...].astype(F32)
    a = (o_ref[...].astype(F32) * _silu(z)).astype(BF16)
    y = jnp.dot(a, w_ref[...], preferred_element_type=F32)
    xn = x_ref[...] + gt_ref[0] * y
    if final:
        fg_ref, out_ref = rest
        ms = jnp.mean(xn * xn, axis=-1, keepdims=True)
        xn = xn * lax.rsqrt(ms + RMS_EPS) * fg_ref[...]
    else:
        (out_ref,) = rest
    out_ref[...] = xn


def _outproj(o2d, proj2d, z_col, w_out, x2d, mods, *, rows_per_mod, mod_row0, tm, final_g=None):
    rows = x2d.shape[0]
    assert rows % tm == 0 and rows_per_mod % tm == 0 and z_col % D_INNER == 0
    tiles_per_mod = rows_per_mod // tm
    zb = z_col // D_INNER
    final = final_g is not None
    in_specs = [
        pl.BlockSpec((tm, D_INNER), lambda i: (i, 0)),
        pl.BlockSpec((tm, D_INNER), lambda i: (i, zb)),
        pl.BlockSpec((D_INNER, D_MODEL), lambda i: (0, 0)),
        pl.BlockSpec((tm, D_MODEL), lambda i: (i, 0)),
        pl.BlockSpec((1, 1, D_MODEL), lambda i: (mod_row0 + i // tiles_per_mod, 0, 2)),
    ]
    args = [o2d, proj2d, w_out, x2d, mods]
    if final:
        in_specs.append(pl.BlockSpec((1, D_MODEL), lambda i: (0, 0)))
        args.append(final_g.reshape(1, D_MODEL))
    vmem = (4 * tm * D_INNER * 2 + 2 * D_INNER * D_MODEL * 2 + 4 * tm * D_MODEL * 4
            + 3 * tm * D_INNER * 4) // MIB + 8
    return pl.pallas_call(
        functools.partial(_outproj_kernel, final=final),
        grid=(rows // tm,),
        in_specs=in_specs,
        out_specs=pl.BlockSpec((tm, D_MODEL), lambda i: (i, 0)),
        out_shape=jax.ShapeDtypeStruct((rows, D_MODEL), F32),
        input_output_aliases={3: 0},
        compiler_params=_cparams(("parallel",), vmem),
        name="outproj",
    )(*args)


def _half_masks():
    lane = lax.broadcasted_iota(jnp.int32, (1, LANES), 1)
    return lane < HEAD_DIM, lane >= HEAD_DIM


def _nt(a, b):
    return lax.dot_general(a, b, (((1,), (1,)), ((), ())), preferred_element_type=F32)


LOG2E = math.log2(math.e)
QK_SCALE = HEAD_DIM ** -0.5 * LOG2E


def _split_heads(q):
    lo, hi = _half_masks()
    qs = q.astype(F32) * QK_SCALE
    return jnp.concatenate([jnp.where(lo, qs, 0.0), jnp.where(hi, qs, 0.0)], axis=0).astype(BF16)


def _ones_col(rows):
    lane = lax.broadcasted_iota(jnp.int32, (rows, LANES), 1)
    return jnp.where(lane == 0, 1.0, 0.0).astype(BF16)


def _diff_attn_kernel(*refs, has_lat, lam_init, tq):
    if has_lat:
        q_ref, kl_ref, vl_ref, kc_ref, vc_ref, lqk_ref, g_ref, o_ref, sa_scr, sb_scr = refs
    else:
        q_ref, kc_ref, vc_ref, lqk_ref, g_ref, o_ref, sa_scr, sb_scr = refs
    nq = q_ref.shape[1] // tq
    lqk = lqk_ref[...]
    s1 = jnp.sum(lqk[0:1] * lqk[1:2], axis=-1, keepdims=True)
    s2 = jnp.sum(lqk[2:3] * lqk[3:4], axis=-1, keepdims=True)
    lam = jnp.exp(s1) - jnp.exp(s2) + lam_init
    bufs = (sa_scr, sb_scr)

    def logits(i):
        s_scr = bufs[i % 2]
        qq = _split_heads(q_ref[0, i * tq:(i + 1) * tq, :])
        s_scr[:, :CTX_LEN] = _nt(qq, kc_ref[0])
        if has_lat:
            s_scr[:, CTX_LEN:] = _nt(qq, kl_ref[0])

    def finish(i):
        s = bufs[i % 2][...]
        m = jnp.max(s, axis=-1, keepdims=True)
        p = jnp.exp2(s - m)
        r = 1.0 / jnp.sum(p, axis=-1, keepdims=True)
        pc = p * jnp.concatenate([r[:tq], -lam * r[tq:]], axis=0)
        a = (pc[:tq] + pc[tq:]).astype(BF16)
        o = jnp.dot(a[:, :CTX_LEN], vc_ref[0], preferred_element_type=F32)
        if has_lat:
            o = o + jnp.dot(a[:, CTX_LEN:], vl_ref[0], preferred_element_type=F32)
        ms = jnp.mean(o * o, axis=-1, keepdims=True)
        o = o * lax.rsqrt(ms + RMS_EPS) * g_ref[...] * (1.0 - lam_init)
        o_ref[0, i * tq:(i + 1) * tq, :] = o.astype(BF16)

    logits(0)
    for i in range(nq):
        if i + 1 < nq:
            logits(i + 1)
        finish(i)


def _diff_attn(proj_q, proj_lat, proj_ctx, lqk, subln_g, *, tq, lam_init):
    bsz, t, _ = proj_q.shape
    has_lat = proj_lat is not None
    kb, vb = D_INNER // LANES, 2 * D_INNER // LANES
    in_specs = [pl.BlockSpec((1, t, LANES), lambda b, h: (b, 0, h))]
    args = [proj_q]
    if has_lat:
        in_specs += [pl.BlockSpec((1, SEQ, LANES), lambda b, h: (b, 0, kb + h)),
                     pl.BlockSpec((1, SEQ, LANES), lambda b, h: (b, 0, vb + h))]
        args += [proj_lat, proj_lat]
    in_specs += [pl.BlockSpec((1, CTX_LEN, LANES), lambda b, h: (b, 0, kb + h)),
                 pl.BlockSpec((1, CTX_LEN, LANES), lambda b, h: (b, 0, vb + h)),
                 pl.BlockSpec((4, HEAD_DIM), lambda b, h: (0, 0)),
                 pl.BlockSpec((1, LANES), lambda b, h: (0, 0))]
    args += [proj_ctx, proj_ctx, lqk, subln_g.reshape(1, LANES)]
    nk = CTX_LEN + (SEQ if has_lat else 0)
    score_bytes = 2 * tq * nk * 4
    vmem = (5 * score_bytes) // MIB + 16
    return pl.pallas_call(
        functools.partial(_diff_attn_kernel, has_lat=has_lat, lam_init=lam_init, tq=tq),
        grid=(bsz, DIFF_HEADS),
        in_specs=in_specs,
        out_specs=pl.BlockSpec((1, t, LANES), lambda b, h: (b, 0, h)),
        out_shape=jax.ShapeDtypeStruct((bsz, t, D_INNER), BF16),
        scratch_shapes=[pltpu.VMEM((2 * tq, nk), F32), pltpu.VMEM((2 * tq, nk), F32)],
        compiler_params=_cparams(("parallel", "parallel"), vmem),
        name="diff_attn",
    )(*args)


def _ctx_attn_kernel(q_ref, k_ref, v_ref, o_ref):
    t = q_ref.shape[1]
    qq = _split_heads(q_ref[0])
    s = _nt(qq, k_ref[0])
    m = jnp.max(s, axis=-1, keepdims=True)
    p = jnp.exp2(s - m)
    den = jnp.sum(p, axis=-1, keepdims=True)
    o = jnp.dot(p.astype(BF16), v_ref[0], preferred_element_type=F32) / den
    lo, _ = _half_masks()
    o_ref[0] = jnp.where(lo, o[:t], o[t:]).astype(BF16)


def _ctx_attn(proj_ctx, n_heads):
    bsz = proj_ctx.shape[0]
    pairs = n_heads // 2
    kb, vb = D_INNER // LANES, 2 * D_INNER // LANES
    return pl.pallas_call(
        _ctx_attn_kernel,
        grid=(bsz, pairs),
        in_specs=[pl.BlockSpec((1, CTX_LEN, LANES), lambda b, h: (b, 0, h)),
                  pl.BlockSpec((1, CTX_LEN, LANES), lambda b, h: (b, 0, kb + h)),
                  pl.BlockSpec((1, CTX_LEN, LANES), lambda b, h: (b, 0, vb + h))],
        out_specs=pl.BlockSpec((1, CTX_LEN, LANES), lambda b, h: (b, 0, h)),
        out_shape=jax.ShapeDtypeStruct((bsz, CTX_LEN, D_INNER), BF16),
        compiler_params=_cparams(("parallel", "parallel"), 32),
        name="ctx_attn",
    )(proj_ctx, proj_ctx, proj_ctx)


NA_ROWS = SEQ // GRID_W
NA_BLK_ROWS = 4
NA_BLK_Q = NA_BLK_ROWS * GRID_W
NA_NBLK = NA_ROWS // NA_BLK_ROWS
NA_WIN_R = NA_BLK_ROWS + NA_WIN_ROWS
NA_KWIN = NA_WIN_R * GRID_W
NA_CTX_CHUNK = 512
NA_MAX_LANE = LANES + 1


def _na_win_start(i):
    lo, hi = 0, NA_ROWS - NA_WIN_R
    start = i * NA_BLK_ROWS - NA_WIN_ROWS // 2
    return np.clip(start, lo, hi) if isinstance(i, (int, np.integer)) else jnp.clip(start, lo, hi)


def _na_bias_table(rpb):
    c = np.arange(GRID_W)
    cs = np.clip(c - NA_WIN_COLS // 2, 0, GRID_W - NA_WIN_COLS)
    kc = np.arange(GRID_W)
    col_ok = (kc[None, :] >= cs[:, None]) & (kc[None, :] < cs[:, None] + NA_WIN_COLS)
    dc_idx = np.clip(kc[None, :] - c[:, None], -(NA_WIN_COLS - 1), NA_WIN_COLS - 1) + NA_WIN_COLS - 1
    dr_idx = np.zeros((3, NA_BLK_ROWS, NA_WIN_R), np.int32)
    row_ok = np.zeros((3, NA_BLK_ROWS, NA_WIN_R), bool)
    for pat, i in enumerate((0, 1, NA_NBLK - 1)):
        ws = int(_na_win_start(i))
        for qr in range(NA_BLK_ROWS):
            r = i * NA_BLK_ROWS + qr
            rs = min(max(r - NA_WIN_ROWS // 2, 0), NA_ROWS - NA_WIN_ROWS)
            for a in range(NA_WIN_R):
                row_ok[pat, qr, a] = rs <= ws + a < rs + NA_WIN_ROWS
                dr_idx[pat, qr, a] = min(max(ws + a - r, -(NA_WIN_ROWS - 1)), NA_WIN_ROWS - 1) + NA_WIN_ROWS - 1
    t1 = jnp.where(col_ok[None, None], rpb[:, :, dc_idx] * LOG2E, NEG_BIG)
    tab = t1[:, dr_idx]
    tab = jnp.where(row_ok[None, :, :, :, None, None], tab, NEG_BIG)
    tab = jnp.transpose(tab, (0, 1, 2, 4, 3, 5))
    return tab.reshape(rpb.shape[0], 3, NA_BLK_Q, NA_KWIN).astype(F32)


def _na_attn_kernel(q_ref, k_ref, v_ref, kc_ref, vc_ref, bias_ref, o_ref, oc_scr):
    lo, _ = _half_masks()
    kctx = kc_ref[0]
    vctx = jnp.concatenate([vc_ref[0], _ones_col(CTX_LEN)], axis=1)
    lane2 = lax.broadcasted_iota(jnp.int32, (1, 2 * LANES), 1)

    def ctx_chunk(j, carry):
        r0 = pl.multiple_of(j * NA_CTX_CHUNK, NA_CTX_CHUNK)
        qq = _split_heads(q_ref[0, pl.ds(r0, NA_CTX_CHUNK), :])
        s = _nt(qq, kctx)
        m = jnp.max(s, axis=-1, keepdims=True)
        oc = jnp.dot(jnp.exp2(s - m).astype(BF16), vctx, preferred_element_type=F32)
        oc = jnp.where(lane2 == NA_MAX_LANE, m, oc)
        oc_scr[0, pl.ds(r0, NA_CTX_CHUNK), :] = oc[:NA_CTX_CHUNK]
        oc_scr[1, pl.ds(r0, NA_CTX_CHUNK), :] = oc[NA_CTX_CHUNK:]
        return carry

    lax.fori_loop(0, SEQ // NA_CTX_CHUNK, ctx_chunk, 0)

    def block(i, carry):
        pat = (i > 0).astype(jnp.int32) + (i == NA_NBLK - 1).astype(jnp.int32)
        t0 = pl.multiple_of(i * NA_BLK_Q, NA_BLK_Q)
        s0 = pl.multiple_of(_na_win_start(i) * GRID_W, NA_BLK_Q)
        qq = _split_heads(q_ref[0, pl.ds(t0, NA_BLK_Q), :])
        kw = k_ref[0, pl.ds(s0, NA_KWIN), :]
        vw = jnp.concatenate([v_ref[0, pl.ds(s0, NA_KWIN), :], _ones_col(NA_KWIN)], axis=1)
        s = _nt(qq, kw) + jnp.concatenate([bias_ref[0, pat], bias_ref[1, pat]], axis=0)
        oc = jnp.concatenate([oc_scr[0, pl.ds(t0, NA_BLK_Q), :], oc_scr[1, pl.ds(t0, NA_BLK_Q), :]], axis=0)
        mc = oc[:, NA_MAX_LANE:NA_MAX_LANE + 1]
        m = jnp.maximum(jnp.max(s, axis=-1, keepdims=True), mc)
        tot = (jnp.dot(jnp.exp2(s - m).astype(BF16), vw, preferred_element_type=F32)
               + jnp.exp2(mc - m) * oc)
        out = tot[:, :LANES] / tot[:, LANES:LANES + 1]
        o_ref[0, pl.ds(t0, NA_BLK_Q), :] = jnp.where(lo, out[:NA_BLK_Q], out[NA_BLK_Q:]).astype(BF16)
        return carry

    lax.fori_loop(0, NA_NBLK, block, 0, unroll=True)


def _na_attn(proj_lat, proj_ctx, bias_tab):
    bsz = proj_lat.shape[0]
    pairs = NA_HEADS // 2
    kb, vb = D_INNER // LANES, 2 * D_INNER // LANES
    return pl.pallas_call(
        _na_attn_kernel,
        grid=(pairs, bsz),
        in_specs=[pl.BlockSpec((1, SEQ, LANES), lambda h, b: (b, 0, h)),
                  pl.BlockSpec((1, SEQ, LANES), lambda h, b: (b, 0, kb + h)),
                  pl.BlockSpec((1, SEQ, LANES), lambda h, b: (b, 0, vb + h)),
                  pl.BlockSpec((1, CTX_LEN, LANES), lambda h, b: (b, 0, kb + h)),
                  pl.BlockSpec((1, CTX_LEN, LANES), lambda h, b: (b, 0, vb + h)),
                  pl.BlockSpec((2, 3, NA_BLK_Q, NA_KWIN), lambda h, b: (h, 0, 0, 0))],
        out_specs=pl.BlockSpec((1, SEQ, LANES), lambda h, b: (b, 0, h)),
        out_shape=jax.ShapeDtypeStruct((bsz, SEQ, D_INNER), BF16),
        scratch_shapes=[pltpu.VMEM((2, SEQ, 2 * LANES), F32)],
        compiler_params=_cparams(("parallel", "arbitrary"), 48),
        name="na_attn",
    )(proj_lat, proj_lat, proj_lat, proj_ctx, proj_ctx, bias_tab)


SWA_TQ = 256
SWA_BAND = SWA_TQ + 2 * SWA_WINDOW
SWA_K_COL = 2 * D_INNER
SWA_V_COL = SWA_K_COL + SWA_KV_HEADS * LANES
SWA_N = SWA_V_COL + SWA_KV_HEADS * LANES


SWA_GROUP = SWA_Q_HEADS // SWA_KV_HEADS
SWA_GW = SWA_GROUP * HEAD_DIM


def _swa_attn_kernel(q_ref, k_ref, v_ref, kc_ref, vc_ref, sink_ref, o_ref):
    lo, _ = _half_masks()
    kvh = pl.program_id(1)
    n_pair = SWA_GROUP // 2
    kctx = kc_ref[0]
    vctx = jnp.concatenate([vc_ref[0], _ones_col(CTX_LEN)], axis=1)
    sinks = sink_ref[pl.ds(pl.multiple_of(kvh * SWA_GROUP, SWA_GROUP), SWA_GROUP), :][:, 0:1] * LOG2E
    sink = jnp.concatenate([jnp.broadcast_to(sinks[g:g + 1], (SWA_TQ, 1)) for g in range(SWA_GROUP)], axis=0)

    def block(i, carry):
        t0 = pl.multiple_of(i * SWA_TQ, SWA_TQ)
        s0 = pl.multiple_of(jnp.clip(i * SWA_TQ - SWA_WINDOW, 0, SEQ - SWA_BAND), SWA_WINDOW)
        q = q_ref[0, pl.ds(t0, SWA_TQ), :]
        qq = jnp.concatenate([_split_heads(q[:, p * LANES:(p + 1) * LANES]) for p in range(n_pair)], axis=0)
        kw = k_ref[0, pl.ds(s0, SWA_BAND), :]
        vw = jnp.concatenate([v_ref[0, pl.ds(s0, SWA_BAND), :], _ones_col(SWA_BAND)], axis=1)
        tpos = t0 + lax.broadcasted_iota(jnp.int32, (SWA_TQ, SWA_BAND), 0)
        spos = s0 + lax.broadcasted_iota(jnp.int32, (SWA_TQ, SWA_BAND), 1)
        bias = jnp.where(jnp.abs(tpos - spos) <= SWA_WINDOW, 0.0, NEG_BIG)
        s_lat = _nt(qq, kw) + jnp.concatenate([bias] * SWA_GROUP, axis=0)
        s_ctx = _nt(qq, kctx)
        m = jnp.maximum(jnp.maximum(jnp.max(s_lat, axis=-1, keepdims=True),
                                    jnp.max(s_ctx, axis=-1, keepdims=True)), sink)
        tot = (jnp.dot(jnp.exp2(s_lat - m).astype(BF16), vw, preferred_element_type=F32)
               + jnp.dot(jnp.exp2(s_ctx - m).astype(BF16), vctx, preferred_element_type=F32))
        out = tot[:, :LANES] / (tot[:, LANES:LANES + 1] + jnp.exp2(sink - m))
        pairs = [jnp.where(lo, out[(2 * p) * SWA_TQ:(2 * p + 1) * SWA_TQ],
                           out[(2 * p + 1) * SWA_TQ:(2 * p + 2) * SWA_TQ]) for p in range(n_pair)]
        o_ref[0, pl.ds(t0, SWA_TQ), :] = jnp.concatenate(pairs, axis=1).astype(BF16)
        return carry

    lax.fori_loop(0, SEQ // SWA_TQ, block, 0, unroll=4)


def _swa_attn(proj_lat, proj_ctx, sink_rows):
    bsz = proj_lat.shape[0]
    kb, vb = SWA_K_COL // LANES, SWA_V_COL // LANES
    return pl.pallas_call(
        _swa_attn_kernel,
        grid=(bsz, SWA_KV_HEADS),
        in_specs=[pl.BlockSpec((1, SEQ, SWA_GW), lambda b, h: (b, 0, h)),
                  pl.BlockSpec((1, SEQ, LANES), lambda b, h: (b, 0, kb + h)),
                  pl.BlockSpec((1, SEQ, LANES), lambda b, h: (b, 0, vb + h)),
                  pl.BlockSpec((1, CTX_LEN, LANES), lambda b, h: (b, 0, kb + h)),
                  pl.BlockSpec((1, CTX_LEN, LANES), lambda b, h: (b, 0, vb + h)),
                  pl.BlockSpec((SWA_Q_HEADS, LANES), lambda b, h: (0, 0))],
        out_specs=pl.BlockSpec((1, SEQ, SWA_GW), lambda b, h: (b, 0, h)),
        out_shape=jax.ShapeDtypeStruct((bsz, SEQ, D_INNER), BF16),
        compiler_params=_cparams(("parallel", "parallel"), 48),
        name="swa_attn",
    )(proj_lat, proj_lat, proj_lat, proj_ctx, proj_ctx, sink_rows)


GATE_LANES = 128


def _mlstm_prep_kernel(xm_ref, cw_ref, cb_ref, bdq_ref, bdk_ref, bdv_ref, wg_ref, bg_ref,
                       q_ref, kt_ref, v_ref, xc_ref, g_ref):
    h = pl.program_id(1)
    t = xm_ref.shape[1]
    xm_b = xm_ref[0]
    xm = xm_b.astype(F32)
    tpos = lax.broadcasted_iota(jnp.int32, (t, 1), 0)
    conv = jnp.zeros_like(xm) + cb_ref[...]
    for j in range(MLSTM_CONV):
        d = j - MLSTM_CONV // 2
        sh = pltpu.roll(xm, (-d) % t, axis=0) if d else xm
        ok = (tpos + d >= 0) & (tpos + d < t)
        conv = conv + jnp.where(ok, sh, 0.0) * cw_ref[j:j + 1, :]
    xc = _silu(conv)
    xc_b = xc.astype(BF16)
    q = jnp.dot(xc_b, bdq_ref[0], preferred_element_type=F32)
    k = jnp.dot(xc_b, bdk_ref[0], preferred_element_type=F32)
    v = jnp.dot(xm_b, bdv_ref[0], preferred_element_type=F32)
    q_b, k_b, v_b = q.astype(BF16), k.astype(BF16), v.astype(BF16)
    q_ref[0] = q_b
    v_ref[0] = v_b
    xc_ref[0] = xc_b
    ks = k * (MLSTM_HEAD_DIM ** -0.5)
    for c in range(t // MLSTM_CHUNK):
        kt_ref[0, 0, c] = ks[c * MLSTM_CHUNK:(c + 1) * MLSTM_CHUNK, :].T.astype(BF16)
    part = (jnp.dot(q_b, wg_ref[0, 0], preferred_element_type=F32)
            + jnp.dot(k_b, wg_ref[1, 0], preferred_element_type=F32)
            + jnp.dot(v_b, wg_ref[2, 0], preferred_element_type=F32))

    @pl.when(h == 0)
    def _():
        g_ref[0] = part + bg_ref[...]

    @pl.when(h != 0)
    def _():
        g_ref[0] = g_ref[0] + part


def _mlstm_prep(proj3, conv_w, conv_b, bdq, bdk, bdv, wg, bg):
    bsz, t, _ = proj3.shape
    hd = MLSTM_HEAD_DIM
    nc = t // MLSTM_CHUNK
    seq_spec = pl.BlockSpec((1, t, hd), lambda b, h: (b, 0, h))
    return pl.pallas_call(
        _mlstm_prep_kernel,
        grid=(bsz, MLSTM_HEADS),
        in_specs=[seq_spec,
                  pl.BlockSpec((MLSTM_CONV, hd), lambda b, h: (0, h)),
                  pl.BlockSpec((1, hd), lambda b, h: (0, h)),
                  pl.BlockSpec((1, hd, hd), lambda b, h: (h, 0, 0)),
                  pl.BlockSpec((1, hd, hd), lambda b, h: (h, 0, 0)),
                  pl.BlockSpec((1, hd, hd), lambda b, h: (h, 0, 0)),
                  pl.BlockSpec((3, 1, hd, GATE_LANES), lambda b, h: (0, h, 0, 0)),
                  pl.BlockSpec((1, GATE_LANES), lambda b, h: (0, 0))],
        out_specs=[seq_spec,
                   pl.BlockSpec((1, 1, nc, hd, MLSTM_CHUNK), lambda b, h: (b, h, 0, 0, 0)),
                   seq_spec, seq_spec,
                   pl.BlockSpec((1, t, GATE_LANES), lambda b, h: (b, 0, 0))],
        out_shape=[jax.ShapeDtypeStruct((bsz, t, D_INNER), BF16),
                   jax.ShapeDtypeStruct((bsz, MLSTM_HEADS, nc, hd, MLSTM_CHUNK), BF16),
                   jax.ShapeDtypeStruct((bsz, t, D_INNER), BF16),
                   jax.ShapeDtypeStruct((bsz, t, D_INNER), BF16),
                   jax.ShapeDtypeStruct((bsz, t, GATE_LANES), F32)],
        compiler_params=_cparams(("parallel", "arbitrary"), 48),
        name="mlstm_prep",
    )(proj3, conv_w, conv_b.reshape(1, D_INNER), bdq, bdk, bdv, wg, bg)


def _lane_scan(x, op, fill, reverse):
    lane = lax.broadcasted_iota(jnp.int32, x.shape, 1)
    sh = 1
    while sh < LANES:
        if reverse:
            y = jnp.where(lane < LANES - sh, pltpu.roll(x, LANES - sh, axis=1), fill)
        else:
            y = jnp.where(lane >= sh, pltpu.roll(x, sh, axis=1), fill)
        x = op(x, y)
        sh *= 2
    return x


GP_ROWS = 8


def _mlstm_gates_kernel(gc_ref, gl_ref, oc_ref, ol_ref):
    nh = MLSTM_HEADS
    gct = gc_ref[0].T
    glt = gl_ref[0].T
    n_c = gc_ref.shape[1] // MLSTM_CHUNK
    n_l = gl_ref.shape[1] // MLSTM_CHUNK
    zeros = jnp.zeros((nh, LANES), F32)
    for d in range(2):
        rev = d == 1
        m = jnp.zeros((nh, 1), F32)
        order = [(gct, oc_ref, c) for c in range(n_c)] + [(glt, ol_ref, c) for c in range(n_l)]
        if rev:
            order = ([(gct, oc_ref, c) for c in reversed(range(n_c))]
                     + [(glt, ol_ref, c) for c in reversed(range(n_l))])
        for src, dst, c in order:
            sl = slice(c * MLSTM_CHUNK, (c + 1) * MLSTM_CHUNK)
            ig = src[2 * d * nh:(2 * d + 1) * nh, sl]
            fg = src[(2 * d + 1) * nh:(2 * d + 2) * nh, sl]
            lf = jnp.minimum(fg, 0.0) - jnp.log1p(jnp.exp(-jnp.abs(fg)))
            b = _lane_scan(lf, jnp.add, 0.0, rev)
            cm = _lane_scan(ig - b, jnp.maximum, -jnp.inf, rev)
            mx = jnp.maximum(cm, m)
            m_t = b + mx
            b_end = b[:, 0:1] if rev else b[:, LANES - 1:LANES]
            g = b_end - b + ig
            m_new = jnp.maximum(b_end + m, jnp.max(g, axis=-1, keepdims=True))
            dst[0, d, c, 0] = -mx
            dst[0, d, c, 1] = b - ig
            dst[0, d, c, 2] = jnp.exp(m - mx)
            dst[0, d, c, 3] = jnp.exp(-m_t)
            dst[0, d, c, 4] = jnp.exp(g - m_new)
            dst[0, d, c, 5] = jnp.broadcast_to(jnp.exp(b_end + m - m_new), (nh, LANES))
            dst[0, d, c, 6] = zeros
            dst[0, d, c, 7] = zeros
            m = m_new


def _mlstm_gates(gates_ctx, gates_lat):
    bsz = gates_lat.shape[0]
    n_c, n_l = CTX_LEN // MLSTM_CHUNK, SEQ // MLSTM_CHUNK

    def out(n):
        shape = (bsz, 2, n, GP_ROWS, MLSTM_HEADS, LANES)
        return (pl.BlockSpec((1,) + shape[1:], lambda b: (b, 0, 0, 0, 0, 0)), jax.ShapeDtypeStruct(shape, F32))

    (spec_c, shape_c), (spec_l, shape_l) = out(n_c), out(n_l)
    return pl.pallas_call(
        _mlstm_gates_kernel,
        grid=(bsz,),
        in_specs=[pl.BlockSpec((1, CTX_LEN, GATE_LANES), lambda b: (b, 0, 0)),
                  pl.BlockSpec((1, SEQ, GATE_LANES), lambda b: (b, 0, 0))],
        out_specs=[spec_c, spec_l],
        out_shape=[shape_c, shape_l],
        compiler_params=_cparams(("parallel",), 32),
        name="mlstm_gates",
    )(gates_ctx, gates_lat)


def _mlstm_scan_kernel(qc_ref, ktc_ref, vc_ref, xcc_ref, gpc_ref,
                       ql_ref, ktl_ref, vl_ref, xcl_ref, gpl_ref, ng_ref, sk_ref,
                       oc_ref, ol_ref, hc_scr, hl_scr, st_scr):
    L = MLSTM_CHUNK
    hd = MLSTM_HEAD_DIM
    n_c = qc_ref.shape[1] // L
    n_l = ql_ref.shape[1] // L
    lane = lax.broadcasted_iota(jnp.int32, (L, LANES), 1)
    ones_col = jnp.where(lane == 0, 1.0, 0.0).astype(BF16)
    tt = lax.broadcasted_iota(jnp.int32, (L, L), 0)
    ss = lax.broadcasted_iota(jnp.int32, (L, L), 1)
    masks = (ss <= tt, ss >= tt)

    hc_scr[...] = jnp.zeros_like(hc_scr)
    hl_scr[...] = jnp.zeros_like(hl_scr)
    st_scr[...] = jnp.zeros_like(st_scr)

    def chunk(d, q_ref, kt_ref, v_ref, gp_ref, h_scr, c):
        rows = gp_ref[0, d, 0, c]
        cols = rows.T
        a_col, wst, einv = cols[:, 0:1], cols[:, 2:3], cols[:, 3:4]
        c_row, ws_row, decay = rows[1:2, :], rows[4:5, :], rows[5:6, 0:1]
        r0 = pl.multiple_of(c * L, L)
        q = q_ref[0, pl.ds(r0, L), :]
        kt = kt_ref[0, 0, c]
        vx = jnp.concatenate([v_ref[0, pl.ds(r0, L), :], ones_col], axis=1)
        w = jnp.exp(jnp.where(masks[d], a_col - c_row, -jnp.inf))
        s = (jnp.dot(q, kt, preferred_element_type=F32) * w).astype(BF16)
        state = st_scr[d]
        tot = (wst * jnp.dot(q, state.astype(BF16), preferred_element_type=F32)
               + jnp.dot(s, vx, preferred_element_type=F32))
        den = tot[:, hd:hd + 1]
        h = tot[:, :hd] / jnp.maximum(jnp.abs(den), einv)
        h_scr[pl.ds(r0, L), :] = h_scr[pl.ds(r0, L), :] + h
        kw = (kt.astype(F32) * ws_row).astype(BF16)
        st_scr[d] = decay * state + jnp.dot(kw, vx, preferred_element_type=F32)

    for j in range(n_c):
        chunk(0, qc_ref, ktc_ref, vc_ref, gpc_ref, hc_scr, j)
        chunk(1, qc_ref, ktc_ref, vc_ref, gpc_ref, hc_scr, n_c - 1 - j)

    def lat_step(j, carry):
        chunk(0, ql_ref, ktl_ref, vl_ref, gpl_ref, hl_scr, j)
        chunk(1, ql_ref, ktl_ref, vl_ref, gpl_ref, hl_scr, n_l - 1 - j)
        return carry

    lax.fori_loop(0, n_l, lat_step, 0, unroll=4)

    ng = ng_ref[...]
    sk = sk_ref[...]

    def finish(h_scr, xc_ref, o_ref, r0, n):
        h = h_scr[pl.ds(r0, n), :]
        mu = jnp.mean(h, axis=-1, keepdims=True)
        hcen = h - mu
        hn = hcen * lax.rsqrt(jnp.mean(hcen * hcen, axis=-1, keepdims=True) + RMS_EPS)
        o_ref[0, pl.ds(r0, n), :] = (hn * ng + sk * xc_ref[0, pl.ds(r0, n), :].astype(F32)).astype(BF16)

    finish(hc_scr, xcc_ref, oc_ref, 0, CTX_LEN)

    def fin_step(i, carry):
        finish(hl_scr, xcl_ref, ol_ref, pl.multiple_of(i * 256, 256), 256)
        return carry

    lax.fori_loop(0, SEQ // 256, fin_step, 0)


def _mlstm_scan(prep_c, gp_c, prep_l, gp_l, norm_g, skip):
    bsz = prep_l[0].shape[0]
    hd = MLSTM_HEAD_DIM

    def seq_specs(t):
        nc = t // MLSTM_CHUNK
        seq = pl.BlockSpec((1, t, hd), lambda b, h: (b, 0, h))
        return [seq,
                pl.BlockSpec((1, 1, nc, hd, MLSTM_CHUNK), lambda b, h: (b, h, 0, 0, 0)),
                seq, seq,
                pl.BlockSpec((1, 2, 1, nc, GP_ROWS, LANES), lambda b, h: (b, 0, h, 0, 0, 0))]

    vec = pl.BlockSpec((1, hd), lambda b, h: (0, h))
    return pl.pallas_call(
        _mlstm_scan_kernel,
        grid=(bsz, MLSTM_HEADS),
        in_specs=seq_specs(CTX_LEN) + seq_specs(SEQ) + [vec, vec],
        out_specs=[pl.BlockSpec((1, CTX_LEN, hd), lambda b, h: (b, 0, h)),
                   pl.BlockSpec((1, SEQ, hd), lambda b, h: (b, 0, h))],
        out_shape=[jax.ShapeDtypeStruct((bsz, CTX_LEN, D_INNER), BF16),
                   jax.ShapeDtypeStruct((bsz, SEQ, D_INNER), BF16)],
        scratch_shapes=[pltpu.VMEM((CTX_LEN, hd), F32), pltpu.VMEM((SEQ, hd), F32),
                        pltpu.VMEM((2, hd, hd + LANES), F32)],
        compiler_params=_cparams(("parallel", "parallel"), 40),
        name="mlstm_scan",
    )(*prep_c, gp_c, *prep_l, gp_l, norm_g.reshape(1, D_INNER), skip.reshape(1, D_INNER))


def _blockdiag_dense(w):
    per_head = MLSTM_HEAD_DIM // MLSTM_QKV_BLOCK
    wh = w.reshape(MLSTM_HEADS, per_head, MLSTM_QKV_BLOCK, MLSTM_QKV_BLOCK)
    eye = jnp.eye(per_head, dtype=w.dtype)
    dense = wh[:, :, :, None, :] * eye[None, :, None, :, None]
    return dense.reshape(MLSTM_HEADS, MLSTM_HEAD_DIM, MLSTM_HEAD_DIM).astype(BF16)


def _mlstm_mixer(proj_lat, proj_ctx, conv_w, conv_b, wq, wk, wv, w_gate, b_gate, norm_g, skip):
    n_gate = 4 * MLSTM_HEADS
    wg = jnp.pad(w_gate, ((0, 0), (0, GATE_LANES - n_gate))).astype(BF16)
    wg = wg.reshape(3, MLSTM_HEADS, MLSTM_HEAD_DIM, GATE_LANES)
    bg = jnp.pad(b_gate, (0, GATE_LANES - n_gate)).reshape(1, GATE_LANES)
    bds = (_blockdiag_dense(wq), _blockdiag_dense(wk), _blockdiag_dense(wv))
    *prep_c, gates_c = _mlstm_prep(proj_ctx, conv_w, conv_b, *bds, wg, bg)
    *prep_l, gates_l = _mlstm_prep(proj_lat, conv_w, conv_b, *bds, wg, bg)
    gp_c, gp_l = _mlstm_gates(gates_c, gates_l)
    gp_c = jnp.transpose(gp_c, (0, 1, 4, 2, 3, 5))
    gp_l = jnp.transpose(gp_l, (0, 1, 4, 2, 3, 5))
    return _mlstm_scan(prep_c, gp_c, prep_l, gp_l, norm_g, skip)


def kernel(x, c, ctx, c_ctx, norm_g, w_ada, b_ada, final_g, mlstm_w_in, mlstm_conv_w, mlstm_conv_b, mlstm_wq, mlstm_wk, mlstm_wv, mlstm_w_gate, mlstm_b_gate, mlstm_norm_g, mlstm_skip, mlstm_w_out, diff_w_in, diff_lq1, diff_lk1, diff_lq2, diff_lk2, diff_subln_g, diff_w_out, na_w_in, na_rpb, na_w_out, swa_w_in, swa_sink, swa_w_out):
    bsz = x.shape[0]
    ctx_row = bsz
    mod_rows = -(-(bsz + 1) // 8) * 8
    cvec = jnp.zeros((mod_rows, D_MODEL), F32).at[:bsz].set(c).at[ctx_row].set(c_ctx)
    mods_all = _adaln(cvec, w_ada, b_ada)
    rope_tabs = _rope_tables()

    xl = x.reshape(bsz * SEQ, D_MODEL)
    xc = ctx.reshape(bsz * CTX_LEN, D_MODEL)
    lat = dict(rows_per_mod=SEQ, mod_row0=0)
    con = dict(rows_per_mod=bsz * CTX_LEN, mod_row0=ctx_row)
    tm_ctx = min(1024, bsz * CTX_LEN)

    def in_both(i, w, tn, rope_cols=()):
        mods = mods_all[i].reshape(mod_rows, 1, 3 * D_MODEL)
        wb = w.astype(BF16)
        pl_ = _inproj(xl, norm_g[i], mods, wb, tm=1024, tn=tn, rope_cols=rope_cols, rope_tabs=rope_tabs, **lat)
        pc_ = _inproj(xc, norm_g[i], mods, wb, tm=tm_ctx, tn=tn, **con)
        return mods, pl_, pc_

    def out_both(mods, o_lat, o_ctx, p_lat, p_ctx, z_col, w_out, need_ctx, final=None):
        wb = w_out.astype(BF16)
        new_l = _outproj(o_lat.reshape(bsz * SEQ, D_INNER), p_lat, z_col, wb, xl, mods, tm=512,
                         final_g=final, **lat)
        new_c = xc
        if need_ctx:
            new_c = _outproj(o_ctx.reshape(bsz * CTX_LEN, D_INNER), p_ctx, z_col, wb, xc, mods, tm=512, **con)
        return new_l, new_c

    mods, p_l, p_c = in_both(0, mlstm_w_in[0], 1024)
    o_c, o_l = _mlstm_mixer(p_l.reshape(bsz, SEQ, -1), p_c.reshape(bsz, CTX_LEN, -1), mlstm_conv_w[0],
                            mlstm_conv_b[0], mlstm_wq[0], mlstm_wk[0], mlstm_wv[0], mlstm_w_gate[0],
                            mlstm_b_gate[0], mlstm_norm_g[0], mlstm_skip[0])
    xl, xc = out_both(mods, o_l, o_c, p_l, p_c, D_INNER, mlstm_w_out[0], True)

    mods, p_l, p_c = in_both(1, diff_w_in[0], 1024, rope_cols=((0, 2 * D_INNER),))
    p_l3, p_c3 = p_l.reshape(bsz, SEQ, -1), p_c.reshape(bsz, CTX_LEN, -1)
    lam_init = 0.8 - 0.6 * math.exp(-0.3 * 1)
    lqk = jnp.stack([diff_lq1[0], diff_lk1[0], diff_lq2[0], diff_lk2[0]])
    o_l = _diff_attn(p_l3, p_l3, p_c3, lqk, diff_subln_g[0], tq=512, lam_init=lam_init)
    o_c = _diff_attn(p_c3, None, p_c3, lqk, diff_subln_g[0], tq=CTX_LEN, lam_init=lam_init)
    xl, xc = out_both(mods, o_l, o_c, p_l, p_c, 3 * D_INNER, diff_w_out[0], True)

    mods, p_l, p_c = in_both(2, na_w_in[0], 1024)
    p_l3, p_c3 = p_l.reshape(bsz, SEQ, -1), p_c.reshape(bsz, CTX_LEN, -1)
    o_l = _na_attn(p_l3, p_c3, _na_bias_table(na_rpb[0]))
    o_c = _ctx_attn(p_c3, NA_HEADS)
    xl, xc = out_both(mods, o_l, o_c, p_l, p_c, 3 * D_INNER, na_w_out[0], True)

    kvw = SWA_KV_HEADS * HEAD_DIM
    w = swa_w_in[0]
    wq, wk, wv, wz = (w[:, :D_INNER], w[:, D_INNER:D_INNER + kvw], w[:, D_INNER + kvw:D_INNER + 2 * kvw],
                      w[:, D_INNER + 2 * kvw:])

    def twice(m):
        m = m.reshape(D_MODEL, SWA_KV_HEADS, 1, HEAD_DIM)
        return jnp.broadcast_to(m, (D_MODEL, SWA_KV_HEADS, 2, HEAD_DIM)).reshape(D_MODEL, SWA_KV_HEADS * LANES)

    w3 = jnp.concatenate([wq, wz, twice(wk), twice(wv)], axis=1)
    mods, p_l, p_c = in_both(3, w3, 512, rope_cols=((0, D_INNER), (SWA_K_COL, SWA_V_COL)))
    sink_rows = jnp.broadcast_to(swa_sink[0][:, None], (SWA_Q_HEADS, LANES))
    o_l = _swa_attn(p_l.reshape(bsz, SEQ, -1), p_c.reshape(bsz, CTX_LEN, -1), sink_rows)
    xl, _ = out_both(mods, o_l, None, p_l, p_c, D_INNER, swa_w_out[0], False, final=final_g)
    return xl.reshape(bsz, SEQ, D_MODEL)
```

```python
import functools
import math

import numpy as np
import jax
import jax.numpy as jnp
from jax import lax
from jax.experimental import pallas as pl
from jax.experimental.pallas import tpu as pltpu

F32 = jnp.float32
BF16 = jnp.bfloat16

D_MODEL = 1024
SEQ = 2048
CTX_LEN = 256
DEPTH = 4
D_INNER = 2 * D_MODEL
GRID_W = 64
RMS_EPS = 1e-6
ROPE_THETA = 10000.0

MLSTM_HEADS = 8
MLSTM_HEAD_DIM = D_INNER // MLSTM_HEADS
MLSTM_QKV_BLOCK = 4
MLSTM_CONV = 5
MLSTM_CHUNK = 128

DIFF_HEADS = 16
HEAD_DIM = 64
NA_HEADS = 32
NA_WIN_ROWS = 8
NA_WIN_COLS = 16
SWA_Q_HEADS = 32
SWA_KV_HEADS = 4
SWA_WINDOW = 128

LANES = 128
MIB = 1024 * 1024
NEG_BIG = -1e30


def _cparams(semantics, vmem_mib):
    return pltpu.CompilerParams(dimension_semantics=semantics, vmem_limit_bytes=vmem_mib * MIB)


def _silu(v):
    return v * jax.nn.sigmoid(v)


def _adaln_kernel(c_ref, w_ref, b_ref, o_ref):
    s = _silu(c_ref[...])
    o_ref[0] = jnp.dot(s, w_ref[0], preferred_element_type=F32,
                       precision=lax.Precision.HIGHEST) + b_ref[0]


def _adaln(cvec, w_ada, b_ada):
    rows = cvec.shape[0]
    return pl.pallas_call(
        _adaln_kernel,
        grid=(DEPTH, 3),
        in_specs=[
            pl.BlockSpec((rows, D_MODEL), lambda i, j: (0, 0)),
            pl.BlockSpec((1, D_MODEL, D_MODEL), lambda i, j: (i, 0, j)),
            pl.BlockSpec((1, 1, D_MODEL), lambda i, j: (i, 0, j)),
        ],
        out_specs=pl.BlockSpec((1, rows, D_MODEL), lambda i, j: (i, 0, j)),
        out_shape=jax.ShapeDtypeStruct((DEPTH, rows, 3 * D_MODEL), F32),
        compiler_params=_cparams(("parallel", "parallel"), 32),
        name="adaln",
    )(cvec, w_ada, b_ada.reshape(DEPTH, 1, 3 * D_MODEL))


ROPE_Q = HEAD_DIM // 4


def _rope_tables():
    inv = np.power(ROPE_THETA, -np.arange(0, 2 * ROPE_Q, 2, dtype=np.float64) / (2 * ROPE_Q))
    t = np.arange(SEQ)
    ang = np.concatenate([(t // GRID_W).astype(np.float64)[:, None] * inv,
                          (t % GRID_W).astype(np.float64)[:, None] * inv], axis=1)
    cos = np.tile(np.cos(ang), (1, 4))
    sin = np.concatenate([-np.tile(np.sin(ang), (1, 2)), np.tile(np.sin(ang), (1, 2))], axis=1)
    return jnp.asarray(cos, F32), jnp.asarray(sin, F32)


def _pair_interleave(w, dup=False):
    d, n = w.shape
    h = n // HEAD_DIM
    w = w.reshape(d, h, 2, 2, ROPE_Q)
    if dup:
        w = jnp.broadcast_to(w[:, :, None], (d, h, 2, 2, 2, ROPE_Q))
    else:
        w = w.reshape(d, h // 2, 2, 2, 2, ROPE_Q)
    return jnp.transpose(w, (0, 1, 4, 2, 3, 5)).reshape(d, -1)


def _rope_chunk(x, cos, sin):
    return x * cos + pltpu.roll(x, LANES // 2, axis=1) * sin


def _inproj_kernel(*refs, tn, rope_plan):
    has_rope = bool(rope_plan)
    if has_rope:
        x_ref, g_ref, sh_ref, sc_ref, w_ref, cos_ref, sin_ref, o_ref, h_scr = refs
    else:
        x_ref, g_ref, sh_ref, sc_ref, w_ref, o_ref, h_scr = refs
    j = pl.program_id(1)

    @pl.when(j == 0)
    def _():
        x = x_ref[...]
        ms = jnp.mean(x * x, axis=-1, keepdims=True)
        y = x * lax.rsqrt(ms + RMS_EPS) * g_ref[...]
        h_scr[...] = (y * (1.0 + sc_ref[0]) + sh_ref[0]).astype(BF16)

    acc = jnp.dot(h_scr[...], w_ref[...], preferred_element_type=F32)

    def store_plain():
        o_ref[...] = acc.astype(BF16)

    def store_rope(flags):
        cos = cos_ref[...]
        sin = sin_ref[...]
        for c, roped in enumerate(flags):
            blk = acc[:, c * LANES:(c + 1) * LANES]
            if roped:
                blk = _rope_chunk(blk, cos, sin)
            o_ref[:, c * LANES:(c + 1) * LANES] = blk.astype(BF16)

    if not has_rope:
        store_plain()
    else:
        any_rope = None
        for flags, tiles in rope_plan:
            hit = functools.reduce(jnp.logical_or, [j == t for t in tiles])
            pl.when(hit)(functools.partial(store_rope, flags))
            any_rope = hit if any_rope is None else jnp.logical_or(any_rope, hit)
        pl.when(jnp.logical_not(any_rope))(store_plain)


def _inproj(x2d, norm_g, mods, w, *, rows_per_mod, mod_row0, tm, tn, rope_cols=(), rope_tabs=None):
    rows, _ = x2d.shape
    n = w.shape[1]
    assert rows % tm == 0 and n % tn == 0 and rows_per_mod % tm == 0
    tiles_per_mod = rows_per_mod // tm
    assert all(lo % LANES == 0 and hi % LANES == 0 for lo, hi in rope_cols)
    plan = {}
    for t in range(n // tn):
        flags = tuple(any(lo <= t * tn + c * LANES < hi for lo, hi in rope_cols) for c in range(tn // LANES))
        if any(flags):
            plan.setdefault(flags, []).append(t)
    rope_plan = tuple((flags, tuple(tiles)) for flags, tiles in plan.items())
    has_rope = bool(rope_plan)
    in_specs = [
        pl.BlockSpec((tm, D_MODEL), lambda i, j: (i, 0)),
        pl.BlockSpec((1, D_MODEL), lambda i, j: (0, 0)),
        pl.BlockSpec((1, 1, D_MODEL), lambda i, j: (mod_row0 + i // tiles_per_mod, 0, 0)),
        pl.BlockSpec((1, 1, D_MODEL), lambda i, j: (mod_row0 + i // tiles_per_mod, 0, 1)),
        pl.BlockSpec((D_MODEL, tn), lambda i, j: (0, j)),
    ]
    args = [x2d, norm_g.reshape(1, D_MODEL), mods, mods, w]
    if has_rope:
        assert SEQ % tm == 0
        seq_tiles = SEQ // tm
        in_specs += [pl.BlockSpec((tm, LANES), lambda i, j: (i % seq_tiles, 0))] * 2
        args += list(rope_tabs)
    vmem = (2 * tm * D_MODEL * 4 + tm * D_MODEL * 2 + 2 * D_MODEL * tn * 2 + 2 * tm * tn * 2
            + 2 * tm * tn * 4 + 4 * tm * LANES * 4) // MIB + 8
    return pl.pallas_call(
        functools.partial(_inproj_kernel, tn=tn, rope_plan=rope_plan),
        grid=(rows // tm, n // tn),
        in_specs=in_specs,
        out_specs=pl.BlockSpec((tm, tn), lambda i, j: (i, j)),
        out_shape=jax.ShapeDtypeStruct((rows, n), BF16),
        scratch_shapes=[pltpu.VMEM((tm, D_MODEL), BF16)],
        compiler_params=_cparams(("parallel", "arbitrary"), vmem),
        name="inproj",
    )(*args)


def _outproj_kernel(o_ref, z_ref, w_ref, x_ref, gt_ref, *rest, final):
    z = z_ref[...].astype(F32)
    a = (o_ref[...].astype(F32) * _silu(z)).astype(BF16)
    y = jnp.dot(a, w_ref[...], preferred_element_type=F32)
    xn = x_ref[...] + gt_ref[0] * y
    if final:
        fg_ref, out_ref = rest
        ms = jnp.mean(xn * xn, axis=-1, keepdims=True)
        xn = xn * lax.rsqrt(ms + RMS_EPS) * fg_ref[...]
    else:
        (out_ref,) = rest
    out_ref[...] = xn


def _outproj(o2d, proj2d, z_col, w_out, x2d, mods, *, rows_per_mod, mod_row0, tm, final_g=None, in_place=True):
    rows = x2d.shape[0]
    assert rows % tm == 0 and rows_per_mod % tm == 0 and z_col % D_INNER == 0
    tiles_per_mod = rows_per_mod // tm
    zb = z_col // D_INNER
    final = final_g is not None
    in_specs = [
        pl.BlockSpec((tm, D_INNER), lambda i: (i, 0)),
        pl.BlockSpec((tm, D_INNER), lambda i: (i, zb)),
        pl.BlockSpec((D_INNER, D_MODEL), lambda i: (0, 0)),
        pl.BlockSpec((tm, D_MODEL), lambda i: (i, 0)),
        pl.BlockSpec((1, 1, D_MODEL), lambda i: (mod_row0 + i // tiles_per_mod, 0, 2)),
    ]
    args = [o2d, proj2d, w_out, x2d, mods]
    if final:
        in_specs.append(pl.BlockSpec((1, D_MODEL), lambda i: (0, 0)))
        args.append(final_g.reshape(1, D_MODEL))
    vmem = (4 * tm * D_INNER * 2 + 2 * D_INNER * D_MODEL * 2 + 4 * tm * D_MODEL * 4
            + 3 * tm * D_INNER * 4) // MIB + 8
    return pl.pallas_call(
        functools.partial(_outproj_kernel, final=final),
        grid=(rows // tm,),
        in_specs=in_specs,
        out_specs=pl.BlockSpec((tm, D_MODEL), lambda i: (i, 0)),
        out_shape=jax.ShapeDtypeStruct((rows, D_MODEL), F32),
        input_output_aliases={3: 0} if in_place else {},
        compiler_params=_cparams(("parallel",), vmem),
        name="outproj",
    )(*args)


def _half_masks():
    lane = lax.broadcasted_iota(jnp.int32, (1, LANES), 1)
    return lane < HEAD_DIM, lane >= HEAD_DIM


def _nt(a, b):
    return lax.dot_general(a, b, (((1,), (1,)), ((), ())), preferred_element_type=F32)


LOG2E = math.log2(math.e)
QK_SCALE = HEAD_DIM ** -0.5 * LOG2E


def _split_heads(q, interleaved=False):
    lo, hi = _half_masks()
    if interleaved:
        lane = lax.broadcasted_iota(jnp.int32, (1, LANES), 1)
        lo = (lane % HEAD_DIM) < HEAD_DIM // 2
        hi = jnp.logical_not(lo)
    qs = q.astype(F32) * QK_SCALE
    return jnp.concatenate([jnp.where(lo, qs, 0.0), jnp.where(hi, qs, 0.0)], axis=0).astype(BF16)


def _ones_col(rows):
    lane = lax.broadcasted_iota(jnp.int32, (rows, LANES), 1)
    return jnp.where(lane == 0, 1.0, 0.0).astype(BF16)


def _diff_attn_kernel(*refs, has_lat, lam_init, tq):
    if has_lat:
        q_ref, kl_ref, vl_ref, kc_ref, vc_ref, lqk_ref, g_ref, o_ref, sa_scr, sb_scr = refs
    else:
        q_ref, kc_ref, vc_ref, lqk_ref, g_ref, o_ref, sa_scr, sb_scr = refs
    nq = q_ref.shape[1] // tq
    lqk = lqk_ref[...]
    s1 = jnp.sum(lqk[0:1] * lqk[1:2], axis=-1, keepdims=True)
    s2 = jnp.sum(lqk[2:3] * lqk[3:4], axis=-1, keepdims=True)
    lam = jnp.exp(s1) - jnp.exp(s2) + lam_init
    bufs = (sa_scr, sb_scr)

    def logits(i):
        s_scr = bufs[i % 2]
        qq = _split_heads(q_ref[0, i * tq:(i + 1) * tq, :], interleaved=True)
        s_scr[:, :CTX_LEN] = _nt(qq, kc_ref[0])
        if has_lat:
            s_scr[:, CTX_LEN:] = _nt(qq, kl_ref[0])

    def finish(i):
        s = bufs[i % 2][...]
        m = jnp.max(s, axis=-1, keepdims=True)
        p = jnp.exp2(s - m)
        den = jnp.sum(p, axis=-1, keepdims=True)
        a = (p[:tq] - (lam * den[:tq] / den[tq:]) * p[tq:]).astype(BF16)
        o = jnp.dot(a[:, :CTX_LEN], vc_ref[0], preferred_element_type=F32)
        if has_lat:
            o = o + jnp.dot(a[:, CTX_LEN:], vl_ref[0], preferred_element_type=F32)
        o = o / den[:tq]
        ms = jnp.mean(o * o, axis=-1, keepdims=True)
        o = o * lax.rsqrt(ms + RMS_EPS) * g_ref[...] * (1.0 - lam_init)
        o_ref[0, i * tq:(i + 1) * tq, :] = o.astype(BF16)

    logits(0)
    for i in range(nq):
        if i + 1 < nq:
            logits(i + 1)
        finish(i)


def _diff_attn(proj_q, proj_lat, proj_ctx, lqk, subln_g, *, tq, lam_init):
    bsz, t, _ = proj_q.shape
    has_lat = proj_lat is not None
    kb, vb = D_INNER // LANES, 2 * D_INNER // LANES
    in_specs = [pl.BlockSpec((1, t, LANES), lambda b, h: (b, 0, h))]
    args = [proj_q]
    if has_lat:
        in_specs += [pl.BlockSpec((1, SEQ, LANES), lambda b, h: (b, 0, kb + h)),
                     pl.BlockSpec((1, SEQ, LANES), lambda b, h: (b, 0, vb + h))]
        args += [proj_lat, proj_lat]
    in_specs += [pl.BlockSpec((1, CTX_LEN, LANES), lambda b, h: (b, 0, kb + h)),
                 pl.BlockSpec((1, CTX_LEN, LANES), lambda b, h: (b, 0, vb + h)),
                 pl.BlockSpec((4, HEAD_DIM), lambda b, h: (0, 0)),
                 pl.BlockSpec((1, LANES), lambda b, h: (0, 0))]
    args += [proj_ctx, proj_ctx, lqk, subln_g.reshape(1, LANES)]
    nk = CTX_LEN + (SEQ if has_lat else 0)
    score_bytes = 2 * tq * nk * 4
    vmem = (5 * score_bytes) // MIB + 16
    return pl.pallas_call(
        functools.partial(_diff_attn_kernel, has_lat=has_lat, lam_init=lam_init, tq=tq),
        grid=(bsz, DIFF_HEADS),
        in_specs=in_specs,
        out_specs=pl.BlockSpec((1, t, LANES), lambda b, h: (b, 0, h)),
        out_shape=jax.ShapeDtypeStruct((bsz, t, D_INNER), BF16),
        scratch_shapes=[pltpu.VMEM((2 * tq, nk), F32), pltpu.VMEM((2 * tq, nk), F32)],
        compiler_params=_cparams(("parallel", "parallel"), vmem),
        name="diff_attn",
    )(*args)


def _ctx_attn_kernel(q_ref, k_ref, v_ref, o_ref):
    t = q_ref.shape[1]
    qq = _split_heads(q_ref[0])
    s = _nt(qq, k_ref[0])
    m = jnp.max(s, axis=-1, keepdims=True)
    p = jnp.exp2(s - m)
    den = jnp.sum(p, axis=-1, keepdims=True)
    o = jnp.dot(p.astype(BF16), v_ref[0], preferred_element_type=F32) / den
    lo, _ = _half_masks()
    o_ref[0] = jnp.where(lo, o[:t], o[t:]).astype(BF16)


def _ctx_attn(proj_ctx, n_heads):
    bsz = proj_ctx.shape[0]
    pairs = n_heads // 2
    kb, vb = D_INNER // LANES, 2 * D_INNER // LANES
    return pl.pallas_call(
        _ctx_attn_kernel,
        grid=(bsz, pairs),
        in_specs=[pl.BlockSpec((1, CTX_LEN, LANES), lambda b, h: (b, 0, h)),
                  pl.BlockSpec((1, CTX_LEN, LANES), lambda b, h: (b, 0, kb + h)),
                  pl.BlockSpec((1, CTX_LEN, LANES), lambda b, h: (b, 0, vb + h))],
        out_specs=pl.BlockSpec((1, CTX_LEN, LANES), lambda b, h: (b, 0, h)),
        out_shape=jax.ShapeDtypeStruct((bsz, CTX_LEN, D_INNER), BF16),
        compiler_params=_cparams(("parallel", "parallel"), 32),
        name="ctx_attn",
    )(proj_ctx, proj_ctx, proj_ctx)


NA_ROWS = SEQ // GRID_W
NA_BLK_ROWS = 4
NA_BLK_Q = NA_BLK_ROWS * GRID_W
NA_NBLK = NA_ROWS // NA_BLK_ROWS
NA_WIN_R = NA_BLK_ROWS + NA_WIN_ROWS
NA_KWIN = NA_WIN_R * GRID_W
NA_CTX_CHUNK = 512
NA_MAX_LANE = LANES + 1


def _na_win_start(i):
    lo, hi = 0, NA_ROWS - NA_WIN_R
    start = i * NA_BLK_ROWS - NA_WIN_ROWS // 2
    return np.clip(start, lo, hi) if isinstance(i, (int, np.integer)) else jnp.clip(start, lo, hi)


def _na_bias_table(rpb):
    c = np.arange(GRID_W)
    cs = np.clip(c - NA_WIN_COLS // 2, 0, GRID_W - NA_WIN_COLS)
    kc = np.arange(GRID_W)
    col_ok = (kc[None, :] >= cs[:, None]) & (kc[None, :] < cs[:, None] + NA_WIN_COLS)
    dc_idx = np.clip(kc[None, :] - c[:, None], -(NA_WIN_COLS - 1), NA_WIN_COLS - 1) + NA_WIN_COLS - 1
    dr_idx = np.zeros((3, NA_BLK_ROWS, NA_WIN_R), np.int32)
    row_ok = np.zeros((3, NA_BLK_ROWS, NA_WIN_R), bool)
    for pat, i in enumerate((0, 1, NA_NBLK - 1)):
        ws = int(_na_win_start(i))
        for qr in range(NA_BLK_ROWS):
            r = i * NA_BLK_ROWS + qr
            rs = min(max(r - NA_WIN_ROWS // 2, 0), NA_ROWS - NA_WIN_ROWS)
            for a in range(NA_WIN_R):
                row_ok[pat, qr, a] = rs <= ws + a < rs + NA_WIN_ROWS
                dr_idx[pat, qr, a] = min(max(ws + a - r, -(NA_WIN_ROWS - 1)), NA_WIN_ROWS - 1) + NA_WIN_ROWS - 1
    t1 = jnp.where(col_ok[None, None], rpb[:, :, dc_idx] * LOG2E, NEG_BIG).astype(F32)
    masked = jnp.full((rpb.shape[0], GRID_W, GRID_W), NEG_BIG, F32)
    row_blocks = [jnp.concatenate([t1[:, dr_idx[pat, qr, a]] if row_ok[pat, qr, a] else masked
                                   for a in range(NA_WIN_R)], axis=-1)
                  for pat in range(3) for qr in range(NA_BLK_ROWS)]
    return jnp.stack(row_blocks, axis=1).reshape(rpb.shape[0], 3, NA_BLK_Q, NA_KWIN)


def _na_attn_kernel(q_ref, k_ref, v_ref, kc_ref, vc_ref, bias_ref, o_ref, oc_scr):
    lo, _ = _half_masks()
    kctx = kc_ref[0]
    vctx = jnp.concatenate([vc_ref[0], _ones_col(CTX_LEN)], axis=1)
    lane2 = lax.broadcasted_iota(jnp.int32, (1, 2 * LANES), 1)

    def ctx_chunk(j, carry):
        r0 = pl.multiple_of(jnp.asarray(j, jnp.int32) * NA_CTX_CHUNK, NA_CTX_CHUNK)
        qq = _split_heads(q_ref[0, pl.ds(r0, NA_CTX_CHUNK), :])
        s = _nt(qq, kctx)
        m = jnp.max(s, axis=-1, keepdims=True)
        oc = jnp.dot(jnp.exp2(s - m).astype(BF16), vctx, preferred_element_type=F32)
        oc = jnp.where(lane2 == NA_MAX_LANE, m, oc)
        oc_scr[0, pl.ds(r0, NA_CTX_CHUNK), :] = oc[:NA_CTX_CHUNK]
        oc_scr[1, pl.ds(r0, NA_CTX_CHUNK), :] = oc[NA_CTX_CHUNK:]
        return carry

    lax.fori_loop(0, SEQ // NA_CTX_CHUNK, ctx_chunk, 0, unroll=True)

    def block(i, carry):
        i = jnp.asarray(i, jnp.int32)
        pat = jnp.where(i == 0, 0, jnp.where(i == NA_NBLK - 1, 2, 1))
        t0 = pl.multiple_of(i * NA_BLK_Q, NA_BLK_Q)
        s0 = pl.multiple_of(_na_win_start(i) * GRID_W, NA_BLK_Q)
        qq = _split_heads(q_ref[0, pl.ds(t0, NA_BLK_Q), :])
        kw = k_ref[0, pl.ds(s0, NA_KWIN), :]
        vw = jnp.concatenate([v_ref[0, pl.ds(s0, NA_KWIN), :], _ones_col(NA_KWIN)], axis=1)
        s = _nt(qq, kw) + jnp.concatenate([bias_ref[0, pat], bias_ref[1, pat]], axis=0)
        oc = jnp.concatenate([oc_scr[0, pl.ds(t0, NA_BLK_Q), :], oc_scr[1, pl.ds(t0, NA_BLK_Q), :]], axis=0)
        mc = oc[:, NA_MAX_LANE:NA_MAX_LANE + 1]
        m = jnp.maximum(jnp.max(s, axis=-1, keepdims=True), mc)
        tot = (jnp.dot(jnp.exp2(s - m).astype(BF16), vw, preferred_element_type=F32)
               + jnp.exp2(mc - m) * oc)
        out = tot[:, :LANES] / tot[:, LANES:LANES + 1]
        o_ref[0, pl.ds(t0, NA_BLK_Q), :] = jnp.where(lo, out[:NA_BLK_Q], out[NA_BLK_Q:]).astype(BF16)
        return carry

    lax.fori_loop(0, NA_NBLK, block, 0, unroll=True)


def _na_attn(proj_lat, proj_ctx, bias_tab):
    bsz = proj_lat.shape[0]
    pairs = NA_HEADS // 2
    kb, vb = D_INNER // LANES, 2 * D_INNER // LANES
    return pl.pallas_call(
        _na_attn_kernel,
        grid=(pairs, bsz),
        in_specs=[pl.BlockSpec((1, SEQ, LANES), lambda h, b: (b, 0, h)),
                  pl.BlockSpec((1, SEQ, LANES), lambda h, b: (b, 0, kb + h)),
                  pl.BlockSpec((1, SEQ, LANES), lambda h, b: (b, 0, vb + h)),
                  pl.BlockSpec((1, CTX_LEN, LANES), lambda h, b: (b, 0, kb + h)),
                  pl.BlockSpec((1, CTX_LEN, LANES), lambda h, b: (b, 0, vb + h)),
                  pl.BlockSpec((2, 3, NA_BLK_Q, NA_KWIN), lambda h, b: (h, 0, 0, 0))],
        out_specs=pl.BlockSpec((1, SEQ, LANES), lambda h, b: (b, 0, h)),
        out_shape=jax.ShapeDtypeStruct((bsz, SEQ, D_INNER), BF16),
        scratch_shapes=[pltpu.VMEM((2, SEQ, 2 * LANES), F32)],
        compiler_params=_cparams(("parallel", "arbitrary"), 48),
        name="na_attn",
    )(proj_lat, proj_lat, proj_lat, proj_ctx, proj_ctx, bias_tab)


SWA_TQ = 256
SWA_BAND = SWA_TQ + 2 * SWA_WINDOW
SWA_K_COL = 2 * D_INNER
SWA_V_COL = SWA_K_COL + SWA_KV_HEADS * LANES
SWA_N = SWA_V_COL + SWA_KV_HEADS * LANES


SWA_GROUP = SWA_Q_HEADS // SWA_KV_HEADS
SWA_GW = SWA_GROUP * HEAD_DIM


def _swa_attn_kernel(q_ref, k_ref, v_ref, kc_ref, vc_ref, sink_ref, o_ref):
    lo, _ = _half_masks()
    kvh = pl.program_id(1)
    n_pair = SWA_GROUP // 2
    kctx = kc_ref[0]
    vctx = jnp.concatenate([vc_ref[0], _ones_col(CTX_LEN)], axis=1)
    sinks = sink_ref[pl.ds(pl.multiple_of(kvh * SWA_GROUP, SWA_GROUP), SWA_GROUP), :][:, 0:1] * LOG2E
    sink = jnp.concatenate([jnp.broadcast_to(sinks[g:g + 1], (SWA_TQ, 1)) for g in range(SWA_GROUP)], axis=0)

    def block(i, carry):
        i = jnp.asarray(i, jnp.int32)
        t0 = pl.multiple_of(i * SWA_TQ, SWA_TQ)
        s0 = pl.multiple_of(jnp.clip(i * SWA_TQ - SWA_WINDOW, 0, SEQ - SWA_BAND), SWA_WINDOW)
        q = q_ref[0, pl.ds(t0, SWA_TQ), :]
        qq = jnp.concatenate([_split_heads(q[:, p * LANES:(p + 1) * LANES], interleaved=True)
                              for p in range(n_pair)], axis=0)
        kw = k_ref[0, pl.ds(s0, SWA_BAND), :]
        vw = jnp.concatenate([v_ref[0, pl.ds(s0, SWA_BAND), :], _ones_col(SWA_BAND)], axis=1)
        tpos = t0 + lax.broadcasted_iota(jnp.int32, (SWA_TQ, SWA_BAND), 0)
        spos = s0 + lax.broadcasted_iota(jnp.int32, (SWA_TQ, SWA_BAND), 1)
        bias = jnp.where(jnp.abs(tpos - spos) <= SWA_WINDOW, 0.0, NEG_BIG)
        s_lat = _nt(qq, kw) + jnp.concatenate([bias] * SWA_GROUP, axis=0)
        s_ctx = _nt(qq, kctx)
        m = jnp.maximum(jnp.maximum(jnp.max(s_lat, axis=-1, keepdims=True),
                                    jnp.max(s_ctx, axis=-1, keepdims=True)), sink)
        tot = (jnp.dot(jnp.exp2(s_lat - m).astype(BF16), vw, preferred_element_type=F32)
               + jnp.dot(jnp.exp2(s_ctx - m).astype(BF16), vctx, preferred_element_type=F32))
        out = tot[:, :LANES] / (tot[:, LANES:LANES + 1] + jnp.exp2(sink - m))
        pairs = [jnp.where(lo, out[(2 * p) * SWA_TQ:(2 * p + 1) * SWA_TQ],
                           out[(2 * p + 1) * SWA_TQ:(2 * p + 2) * SWA_TQ]) for p in range(n_pair)]
        o_ref[0, pl.ds(t0, SWA_TQ), :] = jnp.concatenate(pairs, axis=1).astype(BF16)
        return carry

    lax.fori_loop(0, SEQ // SWA_TQ, block, 0, unroll=4)


def _swa_attn(proj_lat, proj_ctx, sink_rows):
    bsz = proj_lat.shape[0]
    kb, vb = SWA_K_COL // LANES, SWA_V_COL // LANES
    return pl.pallas_call(
        _swa_attn_kernel,
        grid=(bsz, SWA_KV_HEADS),
        in_specs=[pl.BlockSpec((1, SEQ, SWA_GW), lambda b, h: (b, 0, h)),
                  pl.BlockSpec((1, SEQ, LANES), lambda b, h: (b, 0, kb + h)),
                  pl.BlockSpec((1, SEQ, LANES), lambda b, h: (b, 0, vb + h)),
                  pl.BlockSpec((1, CTX_LEN, LANES), lambda b, h: (b, 0, kb + h)),
                  pl.BlockSpec((1, CTX_LEN, LANES), lambda b, h: (b, 0, vb + h)),
                  pl.BlockSpec((SWA_Q_HEADS, LANES), lambda b, h: (0, 0))],
        out_specs=pl.BlockSpec((1, SEQ, SWA_GW), lambda b, h: (b, 0, h)),
        out_shape=jax.ShapeDtypeStruct((bsz, SEQ, D_INNER), BF16),
        compiler_params=_cparams(("parallel", "parallel"), 48),
        name="swa_attn",
    )(proj_lat, proj_lat, proj_lat, proj_ctx, proj_ctx, sink_rows)


GATE_LANES = 128


def _mlstm_prep_kernel(xm_ref, cw_ref, cb_ref, bdq_ref, bdk_ref, bdv_ref, wg_ref, bg_ref,
                       q_ref, kt_ref, v_ref, xc_ref, g_ref):
    h = pl.program_id(1)
    t = xm_ref.shape[1]
    xm_b = xm_ref[0]
    xm = xm_b.astype(F32)
    tpos = lax.broadcasted_iota(jnp.int32, (t, 1), 0)
    conv = jnp.zeros_like(xm) + cb_ref[...]
    for j in range(MLSTM_CONV):
        d = j - MLSTM_CONV // 2
        sh = pltpu.roll(xm, (-d) % t, axis=0) if d else xm
        ok = (tpos + d >= 0) & (tpos + d < t)
        conv = conv + jnp.where(ok, sh, 0.0) * cw_ref[j:j + 1, :]
    xc = _silu(conv)
    xc_b = xc.astype(BF16)
    q = jnp.dot(xc_b, bdq_ref[0], preferred_element_type=F32)
    k = jnp.dot(xc_b, bdk_ref[0], preferred_element_type=F32)
    v = jnp.dot(xm_b, bdv_ref[0], preferred_element_type=F32)
    q_b, k_b, v_b = q.astype(BF16), k.astype(BF16), v.astype(BF16)
    q_ref[0] = q_b
    v_ref[0] = v_b
    xc_ref[0] = xc_b
    ks = k * (MLSTM_HEAD_DIM ** -0.5)
    for c in range(t // MLSTM_CHUNK):
        kt_ref[0, 0, c] = ks[c * MLSTM_CHUNK:(c + 1) * MLSTM_CHUNK, :].T.astype(BF16)
    part = (jnp.dot(q_b, wg_ref[0, 0], preferred_element_type=F32)
            + jnp.dot(k_b, wg_ref[1, 0], preferred_element_type=F32)
            + jnp.dot(v_b, wg_ref[2, 0], preferred_element_type=F32))

    @pl.when(h == 0)
    def _():
        g_ref[0] = part + bg_ref[...]

    @pl.when(h != 0)
    def _():
        g_ref[0] = g_ref[0] + part


def _mlstm_prep(proj3, conv_w, conv_b, bdq, bdk, bdv, wg, bg):
    bsz, t, _ = proj3.shape
    hd = MLSTM_HEAD_DIM
    nc = t // MLSTM_CHUNK
    seq_spec = pl.BlockSpec((1, t, hd), lambda b, h: (b, 0, h))
    return pl.pallas_call(
        _mlstm_prep_kernel,
        grid=(bsz, MLSTM_HEADS),
        in_specs=[seq_spec,
                  pl.BlockSpec((MLSTM_CONV, hd), lambda b, h: (0, h)),
                  pl.BlockSpec((1, hd), lambda b, h: (0, h)),
                  pl.BlockSpec((1, hd, hd), lambda b, h: (h, 0, 0)),
                  pl.BlockSpec((1, hd, hd), lambda b, h: (h, 0, 0)),
                  pl.BlockSpec((1, hd, hd), lambda b, h: (h, 0, 0)),
                  pl.BlockSpec((3, 1, hd, GATE_LANES), lambda b, h: (0, h, 0, 0)),
                  pl.BlockSpec((1, GATE_LANES), lambda b, h: (0, 0))],
        out_specs=[seq_spec,
                   pl.BlockSpec((1, 1, nc, hd, MLSTM_CHUNK), lambda b, h: (b, h, 0, 0, 0)),
                   seq_spec, seq_spec,
                   pl.BlockSpec((1, t, GATE_LANES), lambda b, h: (b, 0, 0))],
        out_shape=[jax.ShapeDtypeStruct((bsz, t, D_INNER), BF16),
                   jax.ShapeDtypeStruct((bsz, MLSTM_HEADS, nc, hd, MLSTM_CHUNK), BF16),
                   jax.ShapeDtypeStruct((bsz, t, D_INNER), BF16),
                   jax.ShapeDtypeStruct((bsz, t, D_INNER), BF16),
                   jax.ShapeDtypeStruct((bsz, t, GATE_LANES), F32)],
        compiler_params=_cparams(("parallel", "arbitrary"), 48),
        name="mlstm_prep",
    )(proj3, conv_w, conv_b.reshape(1, D_INNER), bdq, bdk, bdv, wg, bg)


def _lane_scan(x, op, fill, reverse):
    lane = lax.broadcasted_iota(jnp.int32, x.shape, 1)
    sh = 1
    while sh < LANES:
        if reverse:
            y = jnp.where(lane < LANES - sh, pltpu.roll(x, LANES - sh, axis=1), fill)
        else:
            y = jnp.where(lane >= sh, pltpu.roll(x, sh, axis=1), fill)
        x = op(x, y)
        sh *= 2
    return x


GP_ROWS = 8


def _mlstm_gates_kernel(gc_ref, gl_ref, oc_ref, ol_ref):
    nh = MLSTM_HEADS
    gct = gc_ref[0].T
    glt = gl_ref[0].T
    n_c = gc_ref.shape[1] // MLSTM_CHUNK
    n_l = gl_ref.shape[1] // MLSTM_CHUNK
    zeros = jnp.zeros((nh, LANES), F32)
    chunks = [(gct, oc_ref, c) for c in range(n_c)] + [(glt, ol_ref, c) for c in range(n_l)]
    n_all = len(chunks)

    def stack(row0):
        return jnp.concatenate([src[row0:row0 + nh, c * MLSTM_CHUNK:(c + 1) * MLSTM_CHUNK]
                                for src, _, c in chunks], axis=0)

    for d in range(2):
        rev = d == 1
        ig = stack(2 * d * nh)
        fg = stack((2 * d + 1) * nh)
        lf = jnp.minimum(fg, 0.0) - jnp.log1p(jnp.exp(-jnp.abs(fg)))
        b = _lane_scan(lf, jnp.add, 0.0, rev)
        cm = _lane_scan(ig - b, jnp.maximum, -jnp.inf, rev)
        b_end = b[:, 0:1] if rev else b[:, LANES - 1:LANES]
        g = b_end - b + ig
        gmax = jnp.max(g, axis=-1, keepdims=True)
        order = list(range(n_all))
        if rev:
            order = list(reversed(range(n_c))) + list(reversed(range(n_c, n_all)))
        m = jnp.zeros((nh, 1), F32)
        m_in, m_out = [None] * n_all, [None] * n_all
        for k in order:
            m_in[k] = m
            m = jnp.maximum(b_end[k * nh:(k + 1) * nh] + m, gmax[k * nh:(k + 1) * nh])
            m_out[k] = m
        m_in = jnp.concatenate(m_in, axis=0)
        m_out = jnp.concatenate(m_out, axis=0)
        mx = jnp.maximum(cm, m_in)
        rows = (-mx, b - ig, jnp.exp(m_in - mx), jnp.exp(-(b + mx)), jnp.exp(g - m_out),
                jnp.broadcast_to(jnp.exp(b_end + m_in - m_out), b.shape))
        for k, (_, dst, c) in enumerate(chunks):
            for r, val in enumerate(rows):
                dst[0, d, c, r] = val[k * nh:(k + 1) * nh]
            dst[0, d, c, 6] = zeros
            dst[0, d, c, 7] = zeros


def _mlstm_gates(gates_ctx, gates_lat):
    bsz = gates_lat.shape[0]
    n_c, n_l = CTX_LEN // MLSTM_CHUNK, SEQ // MLSTM_CHUNK

    def out(n):
        shape = (bsz, 2, n, GP_ROWS, MLSTM_HEADS, LANES)
        return (pl.BlockSpec((1,) + shape[1:], lambda b: (b, 0, 0, 0, 0, 0)), jax.ShapeDtypeStruct(shape, F32))

    (spec_c, shape_c), (spec_l, shape_l) = out(n_c), out(n_l)
    return pl.pallas_call(
        _mlstm_gates_kernel,
        grid=(bsz,),
        in_specs=[pl.BlockSpec((1, CTX_LEN, GATE_LANES), lambda b: (b, 0, 0)),
                  pl.BlockSpec((1, SEQ, GATE_LANES), lambda b: (b, 0, 0))],
        out_specs=[spec_c, spec_l],
        out_shape=[shape_c, shape_l],
        compiler_params=_cparams(("parallel",), 32),
        name="mlstm_gates",
    )(gates_ctx, gates_lat)


def _mlstm_scan_kernel(qc_ref, ktc_ref, vc_ref, xcc_ref, gpc_ref,
                       ql_ref, ktl_ref, vl_ref, xcl_ref, gpl_ref, ng_ref, sk_ref,
                       oc_ref, ol_ref, hc_scr, hl_scr, st_scr):
    L = MLSTM_CHUNK
    hd = MLSTM_HEAD_DIM
    n_c = qc_ref.shape[1] // L
    n_l = ql_ref.shape[1] // L
    lane = lax.broadcasted_iota(jnp.int32, (L, LANES), 1)
    ones_col = jnp.where(lane == 0, 1.0, 0.0).astype(BF16)
    tt = lax.broadcasted_iota(jnp.int32, (L, L), 0)
    ss = lax.broadcasted_iota(jnp.int32, (L, L), 1)
    masks = (ss <= tt, ss >= tt)

    hc_scr[...] = jnp.zeros_like(hc_scr)
    hl_scr[...] = jnp.zeros_like(hl_scr)
    st_scr[...] = jnp.zeros_like(st_scr)

    def chunk(d, q_ref, kt_ref, v_ref, gp_ref, h_scr, c):
        rows = gp_ref[0, d, 0, c]
        cols = rows.T
        a_col, wst, einv = cols[:, 0:1], cols[:, 2:3], cols[:, 3:4]
        c_row, ws_row, decay = rows[1:2, :], rows[4:5, :], rows[5:6, 0:1]
        r0 = pl.multiple_of(c * L, L)
        q = q_ref[0, pl.ds(r0, L), :]
        kt = kt_ref[0, 0, c]
        vx = jnp.concatenate([v_ref[0, pl.ds(r0, L), :], ones_col], axis=1)
        w = jnp.exp(jnp.where(masks[d], a_col - c_row, -jnp.inf))
        s = (jnp.dot(q, kt, preferred_element_type=F32) * w).astype(BF16)
        state = st_scr[d]
        tot = (wst * jnp.dot(q, state.astype(BF16), preferred_element_type=F32)
               + jnp.dot(s, vx, preferred_element_type=F32))
        den = tot[:, hd:hd + 1]
        h = tot[:, :hd] / jnp.maximum(jnp.abs(den), einv)
        h_scr[pl.ds(r0, L), :] = h_scr[pl.ds(r0, L), :] + h
        kw = (kt.astype(F32) * ws_row).astype(BF16)
        st_scr[d] = decay * state + jnp.dot(kw, vx, preferred_element_type=F32)

    for j in range(n_c):
        chunk(0, qc_ref, ktc_ref, vc_ref, gpc_ref, hc_scr, j)
        chunk(1, qc_ref, ktc_ref, vc_ref, gpc_ref, hc_scr, n_c - 1 - j)

    def lat_step(j, carry):
        chunk(0, ql_ref, ktl_ref, vl_ref, gpl_ref, hl_scr, j)
        chunk(1, ql_ref, ktl_ref, vl_ref, gpl_ref, hl_scr, n_l - 1 - j)
        return carry

    lax.fori_loop(0, n_l, lat_step, 0, unroll=4)

    ng = ng_ref[...]
    sk = sk_ref[...]

    def finish(h_scr, xc_ref, o_ref, r0, n):
        h = h_scr[pl.ds(r0, n), :]
        mu = jnp.mean(h, axis=-1, keepdims=True)
        hcen = h - mu
        hn = hcen * lax.rsqrt(jnp.mean(hcen * hcen, axis=-1, keepdims=True) + RMS_EPS)
        o_ref[0, pl.ds(r0, n), :] = (hn * ng + sk * xc_ref[0, pl.ds(r0, n), :].astype(F32)).astype(BF16)

    finish(hc_scr, xcc_ref, oc_ref, 0, CTX_LEN)

    def fin_step(i, carry):
        finish(hl_scr, xcl_ref, ol_ref, pl.multiple_of(i * 256, 256), 256)
        return carry

    lax.fori_loop(0, SEQ // 256, fin_step, 0)


def _mlstm_scan(prep_c, gp_c, prep_l, gp_l, norm_g, skip):
    bsz = prep_l[0].shape[0]
    hd = MLSTM_HEAD_DIM

    def seq_specs(t):
        nc = t // MLSTM_CHUNK
        seq = pl.BlockSpec((1, t, hd), lambda b, h: (b, 0, h))
        return [seq,
                pl.BlockSpec((1, 1, nc, hd, MLSTM_CHUNK), lambda b, h: (b, h, 0, 0, 0)),
                seq, seq,
                pl.BlockSpec((1, 2, 1, nc, GP_ROWS, LANES), lambda b, h: (b, 0, h, 0, 0, 0))]

    vec = pl.BlockSpec((1, hd), lambda b, h: (0, h))
    return pl.pallas_call(
        _mlstm_scan_kernel,
        grid=(bsz, MLSTM_HEADS),
        in_specs=seq_specs(CTX_LEN) + seq_specs(SEQ) + [vec, vec],
        out_specs=[pl.BlockSpec((1, CTX_LEN, hd), lambda b, h: (b, 0, h)),
                   pl.BlockSpec((1, SEQ, hd), lambda b, h: (b, 0, h))],
        out_shape=[jax.ShapeDtypeStruct((bsz, CTX_LEN, D_INNER), BF16),
                   jax.ShapeDtypeStruct((bsz, SEQ, D_INNER), BF16)],
        scratch_shapes=[pltpu.VMEM((CTX_LEN, hd), F32), pltpu.VMEM((SEQ, hd), F32),
                        pltpu.VMEM((2, hd, hd + LANES), F32)],
        compiler_params=_cparams(("parallel", "parallel"), 40),
        name="mlstm_scan",
    )(*prep_c, gp_c, *prep_l, gp_l, norm_g.reshape(1, D_INNER), skip.reshape(1, D_INNER))


def _blockdiag_dense(w):
    per_head = MLSTM_HEAD_DIM // MLSTM_QKV_BLOCK
    wh = w.reshape(MLSTM_HEADS, per_head, MLSTM_QKV_BLOCK, MLSTM_QKV_BLOCK)
    eye = jnp.eye(per_head, dtype=w.dtype)
    dense = wh[:, :, :, None, :] * eye[None, :, None, :, None]
    return dense.reshape(MLSTM_HEADS, MLSTM_HEAD_DIM, MLSTM_HEAD_DIM).astype(BF16)


def _mlstm_mixer(proj_lat, proj_ctx, conv_w, conv_b, wq, wk, wv, w_gate, b_gate, norm_g, skip):
    n_gate = 4 * MLSTM_HEADS
    wg = jnp.pad(w_gate, ((0, 0), (0, GATE_LANES - n_gate))).astype(BF16)
    wg = wg.reshape(3, MLSTM_HEADS, MLSTM_HEAD_DIM, GATE_LANES)
    bg = jnp.pad(b_gate, (0, GATE_LANES - n_gate)).reshape(1, GATE_LANES)
    bds = (_blockdiag_dense(wq), _blockdiag_dense(wk), _blockdiag_dense(wv))
    *prep_c, gates_c = _mlstm_prep(proj_ctx, conv_w, conv_b, *bds, wg, bg)
    *prep_l, gates_l = _mlstm_prep(proj_lat, conv_w, conv_b, *bds, wg, bg)
    gp_c, gp_l = _mlstm_gates(gates_c, gates_l)
    gp_c = jnp.transpose(gp_c, (0, 1, 4, 2, 3, 5))
    gp_l = jnp.transpose(gp_l, (0, 1, 4, 2, 3, 5))
    return _mlstm_scan(prep_c, gp_c, prep_l, gp_l, norm_g, skip)


def kernel(x, c, ctx, c_ctx, norm_g, w_ada, b_ada, final_g, mlstm_w_in, mlstm_conv_w, mlstm_conv_b, mlstm_wq, mlstm_wk, mlstm_wv, mlstm_w_gate, mlstm_b_gate, mlstm_norm_g, mlstm_skip, mlstm_w_out, diff_w_in, diff_lq1, diff_lk1, diff_lq2, diff_lk2, diff_subln_g, diff_w_out, na_w_in, na_rpb, na_w_out, swa_w_in, swa_sink, swa_w_out):
    bsz = x.shape[0]
    ctx_row = bsz
    mod_rows = -(-(bsz + 1) // 8) * 8
    cvec = jnp.zeros((mod_rows, D_MODEL), F32).at[:bsz].set(c).at[ctx_row].set(c_ctx)
    mods_all = _adaln(cvec, w_ada, b_ada)
    rope_tabs = _rope_tables()

    xl = x.reshape(bsz * SEQ, D_MODEL)
    xc = ctx.reshape(bsz * CTX_LEN, D_MODEL)
    lat = dict(rows_per_mod=SEQ, mod_row0=0)
    con = dict(rows_per_mod=bsz * CTX_LEN, mod_row0=ctx_row)
    tm_ctx = min(1024, bsz * CTX_LEN)

    def in_both(i, w, tn, rope_cols=()):
        mods = mods_all[i].reshape(mod_rows, 1, 3 * D_MODEL)
        wb = w.astype(BF16)
        pl_ = _inproj(xl, norm_g[i], mods, wb, tm=1024, tn=tn, rope_cols=rope_cols, rope_tabs=rope_tabs, **lat)
        pc_ = _inproj(xc, norm_g[i], mods, wb, tm=tm_ctx, tn=tn, **con)
        return mods, pl_, pc_

    def out_both(mods, o_lat, o_ctx, p_lat, p_ctx, z_col, w_out, need_ctx, final=None, in_place=True):
        wb = w_out.astype(BF16)
        new_l = _outproj(o_lat.reshape(bsz * SEQ, D_INNER), p_lat, z_col, wb, xl, mods, tm=512,
                         final_g=final, in_place=in_place, **lat)
        new_c = xc
        if need_ctx:
            new_c = _outproj(o_ctx.reshape(bsz * CTX_LEN, D_INNER), p_ctx, z_col, wb, xc, mods, tm=512, **con)
        return new_l, new_c

    mods, p_l, p_c = in_both(0, mlstm_w_in[0], 1024)
    o_c, o_l = _mlstm_mixer(p_l.reshape(bsz, SEQ, -1), p_c.reshape(bsz, CTX_LEN, -1), mlstm_conv_w[0],
                            mlstm_conv_b[0], mlstm_wq[0], mlstm_wk[0], mlstm_wv[0], mlstm_w_gate[0],
                            mlstm_b_gate[0], mlstm_norm_g[0], mlstm_skip[0])
    xl, xc = out_both(mods, o_l, o_c, p_l, p_c, D_INNER, mlstm_w_out[0], True, in_place=False)

    wd = diff_w_in[0]
    wd = jnp.concatenate([_pair_interleave(wd[:, :D_INNER]), _pair_interleave(wd[:, D_INNER:2 * D_INNER]),
                          wd[:, 2 * D_INNER:]], axis=1)
    mods, p_l, p_c = in_both(1, wd, 1024, rope_cols=((0, 2 * D_INNER),))
    p_l3, p_c3 = p_l.reshape(bsz, SEQ, -1), p_c.reshape(bsz, CTX_LEN, -1)
    lam_init = 0.8 - 0.6 * math.exp(-0.3 * 1)
    lqk = jnp.stack([diff_lq1[0], diff_lk1[0], diff_lq2[0], diff_lk2[0]])
    o_l = _diff_attn(p_l3, p_l3, p_c3, lqk, diff_subln_g[0], tq=512, lam_init=lam_init)
    o_c = _diff_attn(p_c3, None, p_c3, lqk, diff_subln_g[0], tq=CTX_LEN, lam_init=lam_init)
    xl, xc = out_both(mods, o_l, o_c, p_l, p_c, 3 * D_INNER, diff_w_out[0], True)

    mods, p_l, p_c = in_both(2, na_w_in[0], 1024)
    p_l3, p_c3 = p_l.reshape(bsz, SEQ, -1), p_c.reshape(bsz, CTX_LEN, -1)
    o_l = _na_attn(p_l3, p_c3, _na_bias_table(na_rpb[0]))
    o_c = _ctx_attn(p_c3, NA_HEADS)
    xl, xc = out_both(mods, o_l, o_c, p_l, p_c, 3 * D_INNER, na_w_out[0], True)

    kvw = SWA_KV_HEADS * HEAD_DIM
    w = swa_w_in[0]
    wq, wk, wv, wz = (w[:, :D_INNER], w[:, D_INNER:D_INNER + kvw], w[:, D_INNER + kvw:D_INNER + 2 * kvw],
                      w[:, D_INNER + 2 * kvw:])

    def twice(m):
        m = m.reshape(D_MODEL, SWA_KV_HEADS, 1, HEAD_DIM)
        return jnp.broadcast_to(m, (D_MODEL, SWA_KV_HEADS, 2, HEAD_DIM)).reshape(D_MODEL, SWA_KV_HEADS * LANES)

    w3 = jnp.concatenate([_pair_interleave(wq), wz, _pair_interleave(wk, dup=True), twice(wv)], axis=1)
    mods, p_l, p_c = in_both(3, w3, 1024, rope_cols=((0, D_INNER), (SWA_K_COL, SWA_V_COL)))
    sink_rows = jnp.broadcast_to(swa_sink[0][:, None], (SWA_Q_HEADS, LANES))
    o_l = _swa_attn(p_l.reshape(bsz, SEQ, -1), p_c.reshape(bsz, CTX_LEN, -1), sink_rows)
    xl, _ = out_both(mods, o_l, None, p_l, p_c, D_INNER, swa_w_out[0], False, final=final_g)
    return xl.reshape(bsz, SEQ, D_MODEL)
```

```python
import functools
import math

import numpy as np
import jax
import jax.numpy as jnp
from jax import lax
from jax.experimental import pallas as pl
from jax.experimental.pallas import tpu as pltpu

F32 = jnp.float32
BF16 = jnp.bfloat16

D_MODEL = 1024
SEQ = 2048
CTX_LEN = 256
DEPTH = 4
D_INNER = 2 * D_MODEL
GRID_W = 64
RMS_EPS = 1e-6
ROPE_THETA = 10000.0

MLSTM_HEADS = 8
MLSTM_HEAD_DIM = D_INNER // MLSTM_HEADS
MLSTM_QKV_BLOCK = 4
MLSTM_CONV = 5
SCAN_CHUNK = 256

DIFF_HEADS = 16
HEAD_DIM = 64
NA_HEADS = 32
NA_WIN_ROWS = 8
NA_WIN_COLS = 16
SWA_Q_HEADS = 32
SWA_KV_HEADS = 4
SWA_WINDOW = 128

LANES = 128
MIB = 1024 * 1024
NEG_BIG = -1e30


def _cparams(semantics, vmem_mib):
    return pltpu.CompilerParams(dimension_semantics=semantics, vmem_limit_bytes=vmem_mib * MIB)


def _silu(v):
    return v * jax.nn.sigmoid(v)


def _adaln_kernel(c_ref, w_ref, b_ref, o_ref):
    s = _silu(c_ref[...])
    o_ref[0] = jnp.dot(s, w_ref[0], preferred_element_type=F32,
                       precision=lax.Precision.HIGHEST) + b_ref[0]


def _adaln(cvec, w_ada, b_ada):
    rows = cvec.shape[0]
    return pl.pallas_call(
        _adaln_kernel,
        grid=(DEPTH, 3),
        in_specs=[
            pl.BlockSpec((rows, D_MODEL), lambda i, j: (0, 0)),
            pl.BlockSpec((1, D_MODEL, D_MODEL), lambda i, j: (i, 0, j)),
            pl.BlockSpec((1, 1, D_MODEL), lambda i, j: (i, 0, j)),
        ],
        out_specs=pl.BlockSpec((1, rows, D_MODEL), lambda i, j: (i, 0, j)),
        out_shape=jax.ShapeDtypeStruct((DEPTH, rows, 3 * D_MODEL), F32),
        compiler_params=_cparams(("parallel", "parallel"), 32),
        name="adaln",
    )(cvec, w_ada, b_ada.reshape(DEPTH, 1, 3 * D_MODEL))


ROPE_Q = HEAD_DIM // 4


def _rope_tables():
    inv = np.power(ROPE_THETA, -np.arange(0, 2 * ROPE_Q, 2, dtype=np.float64) / (2 * ROPE_Q))
    t = np.arange(SEQ)
    ang = np.concatenate([(t // GRID_W).astype(np.float64)[:, None] * inv,
                          (t % GRID_W).astype(np.float64)[:, None] * inv], axis=1)
    cos = np.tile(np.cos(ang), (1, 4))
    sin = np.concatenate([-np.tile(np.sin(ang), (1, 2)), np.tile(np.sin(ang), (1, 2))], axis=1)
    return jnp.asarray(cos, F32), jnp.asarray(sin, F32)


def _pair_interleave(w, dup=False):
    d, n = w.shape
    h = n // HEAD_DIM
    w = w.reshape(d, h, 2, 2, ROPE_Q)
    if dup:
        w = jnp.broadcast_to(w[:, :, None], (d, h, 2, 2, 2, ROPE_Q))
    else:
        w = w.reshape(d, h // 2, 2, 2, 2, ROPE_Q)
    return jnp.transpose(w, (0, 1, 4, 2, 3, 5)).reshape(d, -1)


def _rope_chunk(x, cos, sin):
    return x * cos + pltpu.roll(x, LANES // 2, axis=1) * sin


def _inproj_kernel(*refs, tn, rope_plan):
    has_rope = bool(rope_plan)
    if has_rope:
        x_ref, g_ref, sh_ref, sc_ref, w_ref, cos_ref, sin_ref, o_ref, h_scr = refs
    else:
        x_ref, g_ref, sh_ref, sc_ref, w_ref, o_ref, h_scr = refs
    j = pl.program_id(1)

    @pl.when(j == 0)
    def _():
        x = x_ref[...]
        ms = jnp.mean(x * x, axis=-1, keepdims=True)
        y = x * lax.rsqrt(ms + RMS_EPS) * g_ref[...]
        h_scr[...] = (y * (1.0 + sc_ref[0]) + sh_ref[0]).astype(BF16)

    acc = jnp.dot(h_scr[...], w_ref[...], preferred_element_type=F32)

    def store_plain():
        o_ref[...] = acc.astype(BF16)

    def store_rope(flags):
        cos = cos_ref[...]
        sin = sin_ref[...]
        for c, roped in enumerate(flags):
            blk = acc[:, c * LANES:(c + 1) * LANES]
            if roped:
                blk = _rope_chunk(blk, cos, sin)
            o_ref[:, c * LANES:(c + 1) * LANES] = blk.astype(BF16)

    if not has_rope:
        store_plain()
    else:
        any_rope = None
        for flags, tiles in rope_plan:
            hit = functools.reduce(jnp.logical_or, [j == t for t in tiles])
            pl.when(hit)(functools.partial(store_rope, flags))
            any_rope = hit if any_rope is None else jnp.logical_or(any_rope, hit)
        pl.when(jnp.logical_not(any_rope))(store_plain)


def _inproj(x2d, norm_g, mods, w, *, rows_per_mod, mod_row0, tm, tn, rope_cols=(), rope_tabs=None):
    rows, _ = x2d.shape
    n = w.shape[1]
    assert rows % tm == 0 and n % tn == 0 and rows_per_mod % tm == 0
    tiles_per_mod = rows_per_mod // tm
    assert all(lo % LANES == 0 and hi % LANES == 0 for lo, hi in rope_cols)
    plan = {}
    for t in range(n // tn):
        flags = tuple(any(lo <= t * tn + c * LANES < hi for lo, hi in rope_cols) for c in range(tn // LANES))
        if any(flags):
            plan.setdefault(flags, []).append(t)
    rope_plan = tuple((flags, tuple(tiles)) for flags, tiles in plan.items())
    has_rope = bool(rope_plan)
    in_specs = [
        pl.BlockSpec((tm, D_MODEL), lambda i, j: (i, 0)),
        pl.BlockSpec((1, D_MODEL), lambda i, j: (0, 0)),
        pl.BlockSpec((1, 1, D_MODEL), lambda i, j: (mod_row0 + i // tiles_per_mod, 0, 0)),
        pl.BlockSpec((1, 1, D_MODEL), lambda i, j: (mod_row0 + i // tiles_per_mod, 0, 1)),
        pl.BlockSpec((D_MODEL, tn), lambda i, j: (0, j)),
    ]
    args = [x2d, norm_g.reshape(1, D_MODEL), mods, mods, w]
    if has_rope:
        assert SEQ % tm == 0
        seq_tiles = SEQ // tm
        in_specs += [pl.BlockSpec((tm, LANES), lambda i, j: (i % seq_tiles, 0))] * 2
        args += list(rope_tabs)
    vmem = (2 * tm * D_MODEL * 4 + tm * D_MODEL * 2 + 2 * D_MODEL * tn * 2 + 2 * tm * tn * 2
            + 2 * tm * tn * 4 + 4 * tm * LANES * 4) // MIB + 8
    return pl.pallas_call(
        functools.partial(_inproj_kernel, tn=tn, rope_plan=rope_plan),
        grid=(rows // tm, n // tn),
        in_specs=in_specs,
        out_specs=pl.BlockSpec((tm, tn), lambda i, j: (i, j)),
        out_shape=jax.ShapeDtypeStruct((rows, n), BF16),
        scratch_shapes=[pltpu.VMEM((tm, D_MODEL), BF16)],
        compiler_params=_cparams(("parallel", "arbitrary"), vmem),
        name="inproj",
    )(*args)


def _outproj_kernel(o_ref, z_ref, w_ref, x_ref, gt_ref, *rest, final):
    z = z_ref[...].astype(F32)
    a = (o_ref[...].astype(F32) * _silu(z)).astype(BF16)
    y = jnp.dot(a, w_ref[...], preferred_element_type=F32)
    xn = x_ref[...] + gt_ref[0] * y
    if final:
        fg_ref, out_ref = rest
        ms = jnp.mean(xn * xn, axis=-1, keepdims=True)
        xn = xn * lax.rsqrt(ms + RMS_EPS) * fg_ref[...]
    else:
        (out_ref,) = rest
    out_ref[...] = xn


def _outproj(o2d, proj2d, z_col, w_out, x2d, mods, *, rows_per_mod, mod_row0, tm, final_g=None, in_place=True):
    rows = x2d.shape[0]
    assert rows % tm == 0 and rows_per_mod % tm == 0 and z_col % D_INNER == 0
    tiles_per_mod = rows_per_mod // tm
    zb = z_col // D_INNER
    final = final_g is not None
    in_specs = [
        pl.BlockSpec((tm, D_INNER), lambda i: (i, 0)),
        pl.BlockSpec((tm, D_INNER), lambda i: (i, zb)),
        pl.BlockSpec((D_INNER, D_MODEL), lambda i: (0, 0)),
        pl.BlockSpec((tm, D_MODEL), lambda i: (i, 0)),
        pl.BlockSpec((1, 1, D_MODEL), lambda i: (mod_row0 + i // tiles_per_mod, 0, 2)),
    ]
    args = [o2d, proj2d, w_out, x2d, mods]
    if final:
        in_specs.append(pl.BlockSpec((1, D_MODEL), lambda i: (0, 0)))
        args.append(final_g.reshape(1, D_MODEL))
    vmem = (4 * tm * D_INNER * 2 + 2 * D_INNER * D_MODEL * 2 + 4 * tm * D_MODEL * 4
            + 3 * tm * D_INNER * 4) // MIB + 8
    return pl.pallas_call(
        functools.partial(_outproj_kernel, final=final),
        grid=(rows // tm,),
        in_specs=in_specs,
        out_specs=pl.BlockSpec((tm, D_MODEL), lambda i: (i, 0)),
        out_shape=jax.ShapeDtypeStruct((rows, D_MODEL), F32),
        input_output_aliases={3: 0} if in_place else {},
        compiler_params=_cparams(("parallel",), vmem),
        name="outproj",
    )(*args)


def _half_masks():
    lane = lax.broadcasted_iota(jnp.int32, (1, LANES), 1)
    return lane < HEAD_DIM, lane >= HEAD_DIM


def _nt(a, b):
    return lax.dot_general(a, b, (((1,), (1,)), ((), ())), preferred_element_type=F32)


LOG2E = math.log2(math.e)
QK_SCALE = HEAD_DIM ** -0.5 * LOG2E


def _split_heads(q, interleaved=False):
    lo, hi = _half_masks()
    if interleaved:
        lane = lax.broadcasted_iota(jnp.int32, (1, LANES), 1)
        lo = (lane % HEAD_DIM) < HEAD_DIM // 2
        hi = jnp.logical_not(lo)
    qs = q.astype(F32) * QK_SCALE
    return jnp.concatenate([jnp.where(lo, qs, 0.0), jnp.where(hi, qs, 0.0)], axis=0).astype(BF16)


def _ones_col(rows):
    lane = lax.broadcasted_iota(jnp.int32, (rows, LANES), 1)
    return jnp.where(lane == 0, 1.0, 0.0).astype(BF16)


def _diff_attn_kernel(q_ref, kl_ref, vl_ref, kc_ref, vc_ref, lqk_ref, g_ref, o_ref, sa_scr, sb_scr, *, lam_init, tq):
    nq = q_ref.shape[1] // tq
    lqk = lqk_ref[...]
    s1 = jnp.sum(lqk[0:1] * lqk[1:2], axis=-1, keepdims=True)
    s2 = jnp.sum(lqk[2:3] * lqk[3:4], axis=-1, keepdims=True)
    lam = jnp.exp(s1) - jnp.exp(s2) + lam_init
    bufs = (sa_scr, sb_scr)

    def logits(i):
        s_scr = bufs[i % 2]
        qq = _split_heads(q_ref[0, i * tq:(i + 1) * tq, :], interleaved=True)
        s_scr[:, :CTX_LEN] = _nt(qq, kc_ref[0])
        s_scr[:, CTX_LEN:] = _nt(qq, kl_ref[0])

    def finish(i):
        s = bufs[i % 2][...]
        m = jnp.max(s, axis=-1, keepdims=True)
        p = jnp.exp2(s - m)
        den = jnp.sum(p, axis=-1, keepdims=True)
        a = (p[:tq] - (lam * den[:tq] / den[tq:]) * p[tq:]).astype(BF16)
        o = (jnp.dot(a[:, :CTX_LEN], vc_ref[0], preferred_element_type=F32)
             + jnp.dot(a[:, CTX_LEN:], vl_ref[0], preferred_element_type=F32)) / den[:tq]
        ms = jnp.mean(o * o, axis=-1, keepdims=True)
        o = o * lax.rsqrt(ms + RMS_EPS) * g_ref[...] * (1.0 - lam_init)
        o_ref[0, i * tq:(i + 1) * tq, :] = o.astype(BF16)

    logits(0)
    for i in range(nq):
        if i + 1 < nq:
            logits(i + 1)
        finish(i)


def _diff_attn(proj_lat, proj_ctx, lqk, subln_g, *, tq, lam_init):
    bsz = proj_lat.shape[0]
    kb, vb = D_INNER // LANES, 2 * D_INNER // LANES
    nk = CTX_LEN + SEQ
    score_bytes = 2 * tq * nk * 4
    vmem = (5 * score_bytes) // MIB + 16
    return pl.pallas_call(
        functools.partial(_diff_attn_kernel, lam_init=lam_init, tq=tq),
        grid=(bsz, DIFF_HEADS),
        in_specs=[pl.BlockSpec((1, SEQ, LANES), lambda b, h: (b, 0, h)),
                  pl.BlockSpec((1, SEQ, LANES), lambda b, h: (b, 0, kb + h)),
                  pl.BlockSpec((1, SEQ, LANES), lambda b, h: (b, 0, vb + h)),
                  pl.BlockSpec((1, CTX_LEN, LANES), lambda b, h: (b, 0, kb + h)),
                  pl.BlockSpec((1, CTX_LEN, LANES), lambda b, h: (b, 0, vb + h)),
                  pl.BlockSpec((4, HEAD_DIM), lambda b, h: (0, 0)),
                  pl.BlockSpec((1, LANES), lambda b, h: (0, 0))],
        out_specs=pl.BlockSpec((1, SEQ, LANES), lambda b, h: (b, 0, h)),
        out_shape=jax.ShapeDtypeStruct((bsz, SEQ, D_INNER), BF16),
        scratch_shapes=[pltpu.VMEM((2 * tq, nk), F32), pltpu.VMEM((2 * tq, nk), F32)],
        compiler_params=_cparams(("parallel", "parallel"), vmem),
        name="diff_attn",
    )(proj_lat, proj_lat, proj_lat, proj_ctx, proj_ctx, lqk, subln_g.reshape(1, LANES))


CTX_HB = 4


def _ctx_attn_kernel(*refs, diff, lam_init):
    if diff:
        q_ref, k_ref, v_ref, lqk_ref, g_ref, o_ref = refs
        lqk = lqk_ref[...]
        lam = (jnp.exp(jnp.sum(lqk[0:1] * lqk[1:2], axis=-1, keepdims=True))
               - jnp.exp(jnp.sum(lqk[2:3] * lqk[3:4], axis=-1, keepdims=True)) + lam_init)
    else:
        q_ref, k_ref, v_ref, o_ref = refs
    t = q_ref.shape[1]
    lo, _ = _half_masks()
    for hb in range(CTX_HB):
        sl = slice(hb * LANES, (hb + 1) * LANES)
        qq = _split_heads(q_ref[0, :, sl], interleaved=diff)
        s = _nt(qq, k_ref[0, :, sl])
        p = jnp.exp2(s - jnp.max(s, axis=-1, keepdims=True))
        den = jnp.sum(p, axis=-1, keepdims=True)
        if diff:
            a = (p[:t] - (lam * den[:t] / den[t:]) * p[t:]).astype(BF16)
            o = jnp.dot(a, v_ref[0, :, sl], preferred_element_type=F32) / den[:t]
            ms = jnp.mean(o * o, axis=-1, keepdims=True)
            o = o * lax.rsqrt(ms + RMS_EPS) * g_ref[...] * (1.0 - lam_init)
        else:
            o = jnp.dot(p.astype(BF16), v_ref[0, :, sl], preferred_element_type=F32) / den
            o = jnp.where(lo, o[:t], o[t:])
        o_ref[0, :, sl] = o.astype(BF16)


def _ctx_attn(proj_ctx, lqk=None, subln_g=None, lam_init=0.0):
    bsz = proj_ctx.shape[0]
    diff = lqk is not None
    w = CTX_HB * LANES
    kb, vb = D_INNER // w, 2 * D_INNER // w
    in_specs = [pl.BlockSpec((1, CTX_LEN, w), lambda b, h: (b, 0, h)),
                pl.BlockSpec((1, CTX_LEN, w), lambda b, h: (b, 0, kb + h)),
                pl.BlockSpec((1, CTX_LEN, w), lambda b, h: (b, 0, vb + h))]
    args = [proj_ctx, proj_ctx, proj_ctx]
    if diff:
        in_specs += [pl.BlockSpec((4, HEAD_DIM), lambda b, h: (0, 0)), pl.BlockSpec((1, LANES), lambda b, h: (0, 0))]
        args += [lqk, subln_g.reshape(1, LANES)]
    return pl.pallas_call(
        functools.partial(_ctx_attn_kernel, diff=diff, lam_init=lam_init),
        grid=(bsz, D_INNER // w),
        in_specs=in_specs,
        out_specs=pl.BlockSpec((1, CTX_LEN, w), lambda b, h: (b, 0, h)),
        out_shape=jax.ShapeDtypeStruct((bsz, CTX_LEN, D_INNER), BF16),
        compiler_params=_cparams(("parallel", "parallel"), 32),
        name="ctx_attn",
    )(*args)


NA_ROWS = SEQ // GRID_W
NA_BLK_ROWS = 4
NA_BLK_Q = NA_BLK_ROWS * GRID_W
NA_NBLK = NA_ROWS // NA_BLK_ROWS
NA_WIN_R = NA_BLK_ROWS + NA_WIN_ROWS
NA_KWIN = NA_WIN_R * GRID_W
NA_CTX_CHUNK = 512
NA_MAX_LANE = LANES + 1


def _na_win_start(i):
    lo, hi = 0, NA_ROWS - NA_WIN_R
    start = i * NA_BLK_ROWS - NA_WIN_ROWS // 2
    return np.clip(start, lo, hi) if isinstance(i, (int, np.integer)) else jnp.clip(start, lo, hi)


def _na_bias_table(rpb):
    c = np.arange(GRID_W)
    cs = np.clip(c - NA_WIN_COLS // 2, 0, GRID_W - NA_WIN_COLS)
    kc = np.arange(GRID_W)
    col_ok = (kc[None, :] >= cs[:, None]) & (kc[None, :] < cs[:, None] + NA_WIN_COLS)
    dc_idx = np.clip(kc[None, :] - c[:, None], -(NA_WIN_COLS - 1), NA_WIN_COLS - 1) + NA_WIN_COLS - 1
    dr_idx = np.zeros((3, NA_BLK_ROWS, NA_WIN_R), np.int32)
    row_ok = np.zeros((3, NA_BLK_ROWS, NA_WIN_R), bool)
    for pat, i in enumerate((0, 1, NA_NBLK - 1)):
        ws = int(_na_win_start(i))
        for qr in range(NA_BLK_ROWS):
            r = i * NA_BLK_ROWS + qr
            rs = min(max(r - NA_WIN_ROWS // 2, 0), NA_ROWS - NA_WIN_ROWS)
            for a in range(NA_WIN_R):
                row_ok[pat, qr, a] = rs <= ws + a < rs + NA_WIN_ROWS
                dr_idx[pat, qr, a] = min(max(ws + a - r, -(NA_WIN_ROWS - 1)), NA_WIN_ROWS - 1) + NA_WIN_ROWS - 1
    t1 = jnp.where(col_ok[None, None], rpb[:, :, dc_idx] * LOG2E, NEG_BIG).astype(F32)
    masked = jnp.full((rpb.shape[0], GRID_W, GRID_W), NEG_BIG, F32)
    row_blocks = [jnp.concatenate([t1[:, dr_idx[pat, qr, a]] if row_ok[pat, qr, a] else masked
                                   for a in range(NA_WIN_R)], axis=-1)
                  for pat in range(3) for qr in range(NA_BLK_ROWS)]
    return jnp.stack(row_blocks, axis=1).reshape(rpb.shape[0], 3, NA_BLK_Q, NA_KWIN)


def _na_attn_kernel(q_ref, k_ref, v_ref, kc_ref, vc_ref, bias_ref, o_ref, oc_scr):
    lo, _ = _half_masks()
    kctx = kc_ref[0]
    vctx = jnp.concatenate([vc_ref[0], _ones_col(CTX_LEN)], axis=1)
    lane2 = lax.broadcasted_iota(jnp.int32, (1, 2 * LANES), 1)

    def ctx_chunk(j, carry):
        r0 = pl.multiple_of(jnp.asarray(j, jnp.int32) * NA_CTX_CHUNK, NA_CTX_CHUNK)
        qq = _split_heads(q_ref[0, pl.ds(r0, NA_CTX_CHUNK), :])
        s = _nt(qq, kctx)
        m = jnp.max(s, axis=-1, keepdims=True)
        oc = jnp.dot(jnp.exp2(s - m).astype(BF16), vctx, preferred_element_type=F32)
        oc = jnp.where(lane2 == NA_MAX_LANE, m, oc)
        oc_scr[0, pl.ds(r0, NA_CTX_CHUNK), :] = oc[:NA_CTX_CHUNK]
        oc_scr[1, pl.ds(r0, NA_CTX_CHUNK), :] = oc[NA_CTX_CHUNK:]
        return carry

    lax.fori_loop(0, SEQ // NA_CTX_CHUNK, ctx_chunk, 0, unroll=True)

    def block(i, carry):
        i = jnp.asarray(i, jnp.int32)
        pat = jnp.where(i == 0, 0, jnp.where(i == NA_NBLK - 1, 2, 1))
        t0 = pl.multiple_of(i * NA_BLK_Q, NA_BLK_Q)
        s0 = pl.multiple_of(_na_win_start(i) * GRID_W, NA_BLK_Q)
        qq = _split_heads(q_ref[0, pl.ds(t0, NA_BLK_Q), :])
        kw = k_ref[0, pl.ds(s0, NA_KWIN), :]
        vw = jnp.concatenate([v_ref[0, pl.ds(s0, NA_KWIN), :], _ones_col(NA_KWIN)], axis=1)
        s = _nt(qq, kw) + jnp.concatenate([bias_ref[0, pat], bias_ref[1, pat]], axis=0)
        oc = jnp.concatenate([oc_scr[0, pl.ds(t0, NA_BLK_Q), :], oc_scr[1, pl.ds(t0, NA_BLK_Q), :]], axis=0)
        mc = oc[:, NA_MAX_LANE:NA_MAX_LANE + 1]
        m = jnp.maximum(jnp.max(s, axis=-1, keepdims=True), mc)
        tot = (jnp.dot(jnp.exp2(s - m).astype(BF16), vw, preferred_element_type=F32)
               + jnp.exp2(mc - m) * oc)
        out = tot[:, :LANES] / tot[:, LANES:LANES + 1]
        o_ref[0, pl.ds(t0, NA_BLK_Q), :] = jnp.where(lo, out[:NA_BLK_Q], out[NA_BLK_Q:]).astype(BF16)
        return carry

    lax.fori_loop(0, NA_NBLK, block, 0, unroll=True)


def _na_attn(proj_lat, proj_ctx, bias_tab):
    bsz = proj_lat.shape[0]
    pairs = NA_HEADS // 2
    kb, vb = D_INNER // LANES, 2 * D_INNER // LANES
    return pl.pallas_call(
        _na_attn_kernel,
        grid=(pairs, bsz),
        in_specs=[pl.BlockSpec((1, SEQ, LANES), lambda h, b: (b, 0, h)),
                  pl.BlockSpec((1, SEQ, LANES), lambda h, b: (b, 0, kb + h)),
                  pl.BlockSpec((1, SEQ, LANES), lambda h, b: (b, 0, vb + h)),
                  pl.BlockSpec((1, CTX_LEN, LANES), lambda h, b: (b, 0, kb + h)),
                  pl.BlockSpec((1, CTX_LEN, LANES), lambda h, b: (b, 0, vb + h)),
                  pl.BlockSpec((2, 3, NA_BLK_Q, NA_KWIN), lambda h, b: (h, 0, 0, 0))],
        out_specs=pl.BlockSpec((1, SEQ, LANES), lambda h, b: (b, 0, h)),
        out_shape=jax.ShapeDtypeStruct((bsz, SEQ, D_INNER), BF16),
        scratch_shapes=[pltpu.VMEM((2, SEQ, 2 * LANES), F32)],
        compiler_params=_cparams(("parallel", "arbitrary"), 48),
        name="na_attn",
    )(proj_lat, proj_lat, proj_lat, proj_ctx, proj_ctx, bias_tab)


SWA_TQ = 256
SWA_BAND = SWA_TQ + 2 * SWA_WINDOW
SWA_K_COL = 2 * D_INNER
SWA_V_COL = SWA_K_COL + SWA_KV_HEADS * LANES
SWA_N = SWA_V_COL + SWA_KV_HEADS * LANES


SWA_GROUP = SWA_Q_HEADS // SWA_KV_HEADS
SWA_GW = SWA_GROUP * HEAD_DIM


def _swa_attn_kernel(q_ref, k_ref, v_ref, kc_ref, vc_ref, sink_ref, o_ref):
    lo, _ = _half_masks()
    kvh = pl.program_id(1)
    n_pair = SWA_GROUP // 2
    kctx = kc_ref[0]
    vctx = jnp.concatenate([vc_ref[0], _ones_col(CTX_LEN)], axis=1)
    sinks = sink_ref[pl.ds(pl.multiple_of(kvh * SWA_GROUP, SWA_GROUP), SWA_GROUP), :][:, 0:1] * LOG2E
    sink = jnp.concatenate([jnp.broadcast_to(sinks[g:g + 1], (SWA_TQ, 1)) for g in range(SWA_GROUP)], axis=0)

    def block(i, carry):
        i = jnp.asarray(i, jnp.int32)
        t0 = pl.multiple_of(i * SWA_TQ, SWA_TQ)
        s0 = pl.multiple_of(jnp.clip(i * SWA_TQ - SWA_WINDOW, 0, SEQ - SWA_BAND), SWA_WINDOW)
        q = q_ref[0, pl.ds(t0, SWA_TQ), :]
        qq = jnp.concatenate([_split_heads(q[:, p * LANES:(p + 1) * LANES], interleaved=True)
                              for p in range(n_pair)], axis=0)
        kw = k_ref[0, pl.ds(s0, SWA_BAND), :]
        vw = jnp.concatenate([v_ref[0, pl.ds(s0, SWA_BAND), :], _ones_col(SWA_BAND)], axis=1)
        tpos = t0 + lax.broadcasted_iota(jnp.int32, (SWA_TQ, SWA_BAND), 0)
        spos = s0 + lax.broadcasted_iota(jnp.int32, (SWA_TQ, SWA_BAND), 1)
        bias = jnp.where(jnp.abs(tpos - spos) <= SWA_WINDOW, 0.0, NEG_BIG)
        s_lat = _nt(qq, kw) + jnp.concatenate([bias] * SWA_GROUP, axis=0)
        s_ctx = _nt(qq, kctx)
        m = jnp.maximum(jnp.maximum(jnp.max(s_lat, axis=-1, keepdims=True),
                                    jnp.max(s_ctx, axis=-1, keepdims=True)), sink)
        tot = (jnp.dot(jnp.exp2(s_lat - m).astype(BF16), vw, preferred_element_type=F32)
               + jnp.dot(jnp.exp2(s_ctx - m).astype(BF16), vctx, preferred_element_type=F32))
        out = tot[:, :LANES] / (tot[:, LANES:LANES + 1] + jnp.exp2(sink - m))
        pairs = [jnp.where(lo, out[(2 * p) * SWA_TQ:(2 * p + 1) * SWA_TQ],
                           out[(2 * p + 1) * SWA_TQ:(2 * p + 2) * SWA_TQ]) for p in range(n_pair)]
        o_ref[0, pl.ds(t0, SWA_TQ), :] = jnp.concatenate(pairs, axis=1).astype(BF16)
        return carry

    lax.fori_loop(0, SEQ // SWA_TQ, block, 0, unroll=4)


def _swa_attn(proj_lat, proj_ctx, sink_rows):
    bsz = proj_lat.shape[0]
    kb, vb = SWA_K_COL // LANES, SWA_V_COL // LANES
    return pl.pallas_call(
        _swa_attn_kernel,
        grid=(bsz, SWA_KV_HEADS),
        in_specs=[pl.BlockSpec((1, SEQ, SWA_GW), lambda b, h: (b, 0, h)),
                  pl.BlockSpec((1, SEQ, LANES), lambda b, h: (b, 0, kb + h)),
                  pl.BlockSpec((1, SEQ, LANES), lambda b, h: (b, 0, vb + h)),
                  pl.BlockSpec((1, CTX_LEN, LANES), lambda b, h: (b, 0, kb + h)),
                  pl.BlockSpec((1, CTX_LEN, LANES), lambda b, h: (b, 0, vb + h)),
                  pl.BlockSpec((SWA_Q_HEADS, LANES), lambda b, h: (0, 0))],
        out_specs=pl.BlockSpec((1, SEQ, SWA_GW), lambda b, h: (b, 0, h)),
        out_shape=jax.ShapeDtypeStruct((bsz, SEQ, D_INNER), BF16),
        compiler_params=_cparams(("parallel", "parallel"), 48),
        name="swa_attn",
    )(proj_lat, proj_lat, proj_lat, proj_ctx, proj_ctx, sink_rows)


GATE_LANES = 128


def _mlstm_prep_kernel(xm_ref, cw_ref, cb_ref, bdq_ref, bdk_ref, bdv_ref, wg_ref, bg_ref,
                       q_ref, kt_ref, v_ref, xc_ref, g_ref):
    h = pl.program_id(1)
    t = xm_ref.shape[1]
    xm_b = xm_ref[0]
    xm = xm_b.astype(F32)
    tpos = lax.broadcasted_iota(jnp.int32, (t, 1), 0)
    conv = jnp.zeros_like(xm) + cb_ref[...]
    for j in range(MLSTM_CONV):
        d = j - MLSTM_CONV // 2
        sh = pltpu.roll(xm, (-d) % t, axis=0) if d else xm
        ok = (tpos + d >= 0) & (tpos + d < t)
        conv = conv + jnp.where(ok, sh, 0.0) * cw_ref[j:j + 1, :]
    xc = _silu(conv)
    xc_b = xc.astype(BF16)
    q = jnp.dot(xc_b, bdq_ref[0], preferred_element_type=F32)
    k = jnp.dot(xc_b, bdk_ref[0], preferred_element_type=F32)
    v = jnp.dot(xm_b, bdv_ref[0], preferred_element_type=F32)
    q_b, k_b, v_b = q.astype(BF16), k.astype(BF16), v.astype(BF16)
    q_ref[0] = q_b
    v_ref[0] = v_b
    xc_ref[0] = xc_b
    ks = k * (MLSTM_HEAD_DIM ** -0.5)
    for c in range(t // SCAN_CHUNK):
        kt_ref[0, 0, c] = ks[c * SCAN_CHUNK:(c + 1) * SCAN_CHUNK, :].T.astype(BF16)
    part = (jnp.dot(q_b, wg_ref[0, 0], preferred_element_type=F32)
            + jnp.dot(k_b, wg_ref[1, 0], preferred_element_type=F32)
            + jnp.dot(v_b, wg_ref[2, 0], preferred_element_type=F32))

    @pl.when(h == 0)
    def _():
        g_ref[0] = part + bg_ref[...]

    @pl.when(h != 0)
    def _():
        g_ref[0] = g_ref[0] + part


def _mlstm_prep(proj3, conv_w, conv_b, bdq, bdk, bdv, wg, bg):
    bsz, t, _ = proj3.shape
    hd = MLSTM_HEAD_DIM
    nc = t // SCAN_CHUNK
    seq_spec = pl.BlockSpec((1, t, hd), lambda b, h: (b, 0, h))
    return pl.pallas_call(
        _mlstm_prep_kernel,
        grid=(bsz, MLSTM_HEADS),
        in_specs=[seq_spec,
                  pl.BlockSpec((MLSTM_CONV, hd), lambda b, h: (0, h)),
                  pl.BlockSpec((1, hd), lambda b, h: (0, h)),
                  pl.BlockSpec((1, hd, hd), lambda b, h: (h, 0, 0)),
                  pl.BlockSpec((1, hd, hd), lambda b, h: (h, 0, 0)),
                  pl.BlockSpec((1, hd, hd), lambda b, h: (h, 0, 0)),
                  pl.BlockSpec((3, 1, hd, GATE_LANES), lambda b, h: (0, h, 0, 0)),
                  pl.BlockSpec((1, GATE_LANES), lambda b, h: (0, 0))],
        out_specs=[seq_spec,
                   pl.BlockSpec((1, 1, nc, hd, SCAN_CHUNK), lambda b, h: (b, h, 0, 0, 0)),
                   seq_spec, seq_spec,
                   pl.BlockSpec((1, t, GATE_LANES), lambda b, h: (b, 0, 0))],
        out_shape=[jax.ShapeDtypeStruct((bsz, t, D_INNER), BF16),
                   jax.ShapeDtypeStruct((bsz, MLSTM_HEADS, nc, hd, SCAN_CHUNK), BF16),
                   jax.ShapeDtypeStruct((bsz, t, D_INNER), BF16),
                   jax.ShapeDtypeStruct((bsz, t, D_INNER), BF16),
                   jax.ShapeDtypeStruct((bsz, t, GATE_LANES), F32)],
        compiler_params=_cparams(("parallel", "arbitrary"), 48),
        name="mlstm_prep",
    )(proj3, conv_w, conv_b.reshape(1, D_INNER), bdq, bdk, bdv, wg, bg)


def _lane_scan(x, op, fill, reverse):
    lane = lax.broadcasted_iota(jnp.int32, x.shape, 1)
    sh = 1
    while sh < LANES:
        if reverse:
            y = jnp.where(lane < LANES - sh, pltpu.roll(x, LANES - sh, axis=1), fill)
        else:
            y = jnp.where(lane >= sh, pltpu.roll(x, sh, axis=1), fill)
        x = op(x, y)
        sh *= 2
    return x


GP_ROWS = 8


def _mlstm_gates_kernel(gc_ref, gl_ref, oc_ref, ol_ref):
    nh = MLSTM_HEADS
    ppc = SCAN_CHUNK // LANES
    gct = gc_ref[0].T
    glt = gl_ref[0].T
    n_c = gc_ref.shape[1] // SCAN_CHUNK
    n_l = gl_ref.shape[1] // SCAN_CHUNK
    zeros = jnp.zeros((nh, SCAN_CHUNK), F32)
    pieces = [(gct, j) for j in range(n_c * ppc)] + [(glt, j) for j in range(n_l * ppc)]
    chunks = [(oc_ref, c) for c in range(n_c)] + [(ol_ref, c) for c in range(n_l)]
    n_all = len(chunks)

    def stack(row0):
        return jnp.concatenate([src[row0:row0 + nh, j * LANES:(j + 1) * LANES] for src, j in pieces], axis=0)

    def piece(x, k):
        return x[k * nh:(k + 1) * nh]

    def earlier(k, rev):
        c = k // ppc
        return range(k + 1, (c + 1) * ppc) if rev else range(c * ppc, k)

    for d in range(2):
        rev = d == 1
        edge = (lambda x: x[:, 0:1]) if rev else (lambda x: x[:, LANES - 1:LANES])
        ig = stack(2 * d * nh)
        fg = stack((2 * d + 1) * nh)
        lf = jnp.minimum(fg, 0.0) - jnp.log1p(jnp.exp(-jnp.abs(fg)))
        bp = _lane_scan(lf, jnp.add, 0.0, rev)
        b = jnp.concatenate([piece(bp, k) + sum(piece(edge(bp), o) for o in earlier(k, rev))
                             for k in range(len(pieces))], axis=0)
        cmp_ = _lane_scan(ig - b, jnp.maximum, -jnp.inf, rev)
        cm = jnp.concatenate([functools.reduce(jnp.maximum, [piece(edge(cmp_), o) for o in earlier(k, rev)],
                                               piece(cmp_, k)) for k in range(len(pieces))], axis=0)
        last = (lambda c: c * ppc) if rev else (lambda c: c * ppc + ppc - 1)
        b_end_c = [piece(edge(b), last(c)) for c in range(n_all)]
        b_end = jnp.concatenate([b_end_c[k // ppc] for k in range(len(pieces))], axis=0)
        g = b_end - b + ig
        gmax_p = jnp.max(g, axis=-1, keepdims=True)
        gmax_c = [functools.reduce(jnp.maximum, [piece(gmax_p, c * ppc + p) for p in range(ppc)])
                  for c in range(n_all)]
        order = list(range(n_all))
        if rev:
            order = list(reversed(range(n_c))) + list(reversed(range(n_c, n_all)))
        m = jnp.zeros((nh, 1), F32)
        m_in, m_out = [None] * n_all, [None] * n_all
        for c in order:
            m_in[c] = m
            m = jnp.maximum(b_end_c[c] + m, gmax_c[c])
            m_out[c] = m
        m_in = jnp.concatenate([m_in[k // ppc] for k in range(len(pieces))], axis=0)
        m_out = jnp.concatenate([m_out[k // ppc] for k in range(len(pieces))], axis=0)
        mx = jnp.maximum(cm, m_in)
        rows = (-mx, b - ig, jnp.exp(m_in - mx), jnp.exp(-(b + mx)), jnp.exp(g - m_out),
                jnp.broadcast_to(jnp.exp(b_end + m_in - m_out), b.shape))
        for k, (dst, c) in enumerate(chunks):
            for r, val in enumerate(rows):
                dst[0, d, c, r] = jnp.concatenate([piece(val, k * ppc + p) for p in range(ppc)], axis=1)
            dst[0, d, c, 6] = zeros
            dst[0, d, c, 7] = zeros


def _mlstm_gates(gates_ctx, gates_lat):
    bsz = gates_lat.shape[0]
    n_c, n_l = CTX_LEN // SCAN_CHUNK, SEQ // SCAN_CHUNK

    def out(n):
        shape = (bsz, 2, n, GP_ROWS, MLSTM_HEADS, SCAN_CHUNK)
        return (pl.BlockSpec((1,) + shape[1:], lambda b: (b, 0, 0, 0, 0, 0)), jax.ShapeDtypeStruct(shape, F32))

    (spec_c, shape_c), (spec_l, shape_l) = out(n_c), out(n_l)
    return pl.pallas_call(
        _mlstm_gates_kernel,
        grid=(bsz,),
        in_specs=[pl.BlockSpec((1, CTX_LEN, GATE_LANES), lambda b: (b, 0, 0)),
                  pl.BlockSpec((1, SEQ, GATE_LANES), lambda b: (b, 0, 0))],
        out_specs=[spec_c, spec_l],
        out_shape=[shape_c, shape_l],
        compiler_params=_cparams(("parallel",), 32),
        name="mlstm_gates",
    )(gates_ctx, gates_lat)


def _mlstm_scan_kernel(qc_ref, ktc_ref, vc_ref, xcc_ref, gpc_ref,
                       ql_ref, ktl_ref, vl_ref, xcl_ref, gpl_ref, ng_ref, sk_ref,
                       oc_ref, ol_ref, hc_scr, hl_scr, st_scr):
    L = SCAN_CHUNK
    hd = MLSTM_HEAD_DIM
    n_c = qc_ref.shape[1] // L
    n_l = ql_ref.shape[1] // L
    lane = lax.broadcasted_iota(jnp.int32, (L, LANES), 1)
    ones_col = jnp.where(lane == 0, 1.0, 0.0).astype(BF16)
    tt = lax.broadcasted_iota(jnp.int32, (L, L), 0)
    ss = lax.broadcasted_iota(jnp.int32, (L, L), 1)
    masks = (ss <= tt, ss >= tt)

    hc_scr[...] = jnp.zeros_like(hc_scr)
    hl_scr[...] = jnp.zeros_like(hl_scr)
    st_scr[...] = jnp.zeros_like(st_scr)

    def chunk(d, q_ref, kt_ref, v_ref, gp_ref, h_scr, c):
        rows = gp_ref[0, d, 0, c]
        cols = rows.T
        a_col, wst, einv = cols[:, 0:1], cols[:, 2:3], cols[:, 3:4]
        c_row, ws_row, decay = rows[1:2, :], rows[4:5, :], rows[5:6, 0:1]
        r0 = pl.multiple_of(c * L, L)
        q = q_ref[0, pl.ds(r0, L), :]
        kt = kt_ref[0, 0, c]
        vx = jnp.concatenate([v_ref[0, pl.ds(r0, L), :], ones_col], axis=1)
        w = jnp.exp(jnp.where(masks[d], a_col - c_row, -jnp.inf))
        s = (jnp.dot(q, kt, preferred_element_type=F32) * w).astype(BF16)
        kw = (kt.astype(F32) * ws_row).astype(BF16)
        both = jnp.dot(jnp.concatenate([s, kw], axis=0), vx, preferred_element_type=F32)
        state = st_scr[d]
        tot = wst * jnp.dot(q, state.astype(BF16), preferred_element_type=F32) + both[:L]
        den = tot[:, hd:hd + 1]
        h = tot[:, :hd] / jnp.maximum(jnp.abs(den), einv)
        h_scr[pl.ds(r0, L), :] = h_scr[pl.ds(r0, L), :] + h
        st_scr[d] = decay * state + both[L:]

    for j in range(n_c):
        chunk(0, qc_ref, ktc_ref, vc_ref, gpc_ref, hc_scr, j)
        chunk(1, qc_ref, ktc_ref, vc_ref, gpc_ref, hc_scr, n_c - 1 - j)

    def lat_step(j, carry):
        chunk(0, ql_ref, ktl_ref, vl_ref, gpl_ref, hl_scr, j)
        chunk(1, ql_ref, ktl_ref, vl_ref, gpl_ref, hl_scr, n_l - 1 - j)
        return carry

    lax.fori_loop(0, n_l, lat_step, 0, unroll=4)

    ng = ng_ref[...]
    sk = sk_ref[...]

    def finish(h_scr, xc_ref, o_ref, r0, n):
        h = h_scr[pl.ds(r0, n), :]
        mu = jnp.mean(h, axis=-1, keepdims=True)
        hcen = h - mu
        hn = hcen * lax.rsqrt(jnp.mean(hcen * hcen, axis=-1, keepdims=True) + RMS_EPS)
        o_ref[0, pl.ds(r0, n), :] = (hn * ng + sk * xc_ref[0, pl.ds(r0, n), :].astype(F32)).astype(BF16)

    finish(hc_scr, xcc_ref, oc_ref, 0, CTX_LEN)

    def fin_step(i, carry):
        finish(hl_scr, xcl_ref, ol_ref, pl.multiple_of(i * 256, 256), 256)
        return carry

    lax.fori_loop(0, SEQ // 256, fin_step, 0, unroll=2)


def _mlstm_scan(prep_c, gp_c, prep_l, gp_l, norm_g, skip):
    bsz = prep_l[0].shape[0]
    hd = MLSTM_HEAD_DIM

    def seq_specs(t):
        nc = t // SCAN_CHUNK
        seq = pl.BlockSpec((1, t, hd), lambda b, h: (b, 0, h))
        return [seq,
                pl.BlockSpec((1, 1, nc, hd, SCAN_CHUNK), lambda b, h: (b, h, 0, 0, 0)),
                seq, seq,
                pl.BlockSpec((1, 2, 1, nc, GP_ROWS, SCAN_CHUNK), lambda b, h: (b, 0, h, 0, 0, 0))]

    vec = pl.BlockSpec((1, hd), lambda b, h: (0, h))
    return pl.pallas_call(
        _mlstm_scan_kernel,
        grid=(bsz, MLSTM_HEADS),
        in_specs=seq_specs(CTX_LEN) + seq_specs(SEQ) + [vec, vec],
        out_specs=[pl.BlockSpec((1, CTX_LEN, hd), lambda b, h: (b, 0, h)),
                   pl.BlockSpec((1, SEQ, hd), lambda b, h: (b, 0, h))],
        out_shape=[jax.ShapeDtypeStruct((bsz, CTX_LEN, D_INNER), BF16),
                   jax.ShapeDtypeStruct((bsz, SEQ, D_INNER), BF16)],
        scratch_shapes=[pltpu.VMEM((CTX_LEN, hd), F32), pltpu.VMEM((SEQ, hd), F32),
                        pltpu.VMEM((2, hd, hd + LANES), F32)],
        compiler_params=_cparams(("parallel", "parallel"), 40),
        name="mlstm_scan",
    )(*prep_c, gp_c, *prep_l, gp_l, norm_g.reshape(1, D_INNER), skip.reshape(1, D_INNER))


def _blockdiag_dense(w):
    per_head = MLSTM_HEAD_DIM // MLSTM_QKV_BLOCK
    wh = w.reshape(MLSTM_HEADS, per_head, MLSTM_QKV_BLOCK, MLSTM_QKV_BLOCK)
    eye = jnp.eye(per_head, dtype=w.dtype)
    dense = wh[:, :, :, None, :] * eye[None, :, None, :, None]
    return dense.reshape(MLSTM_HEADS, MLSTM_HEAD_DIM, MLSTM_HEAD_DIM).astype(BF16)


def _mlstm_mixer(proj_lat, proj_ctx, conv_w, conv_b, wq, wk, wv, w_gate, b_gate, norm_g, skip):
    n_gate = 4 * MLSTM_HEADS
    wg = jnp.pad(w_gate, ((0, 0), (0, GATE_LANES - n_gate))).astype(BF16)
    wg = wg.reshape(3, MLSTM_HEADS, MLSTM_HEAD_DIM, GATE_LANES)
    bg = jnp.pad(b_gate, (0, GATE_LANES - n_gate)).reshape(1, GATE_LANES)
    bds = (_blockdiag_dense(wq), _blockdiag_dense(wk), _blockdiag_dense(wv))
    *prep_c, gates_c = _mlstm_prep(proj_ctx, conv_w, conv_b, *bds, wg, bg)
    *prep_l, gates_l = _mlstm_prep(proj_lat, conv_w, conv_b, *bds, wg, bg)
    gp_c, gp_l = _mlstm_gates(gates_c, gates_l)
    gp_c = jnp.transpose(gp_c, (0, 1, 4, 2, 3, 5))
    gp_l = jnp.transpose(gp_l, (0, 1, 4, 2, 3, 5))
    return _mlstm_scan(prep_c, gp_c, prep_l, gp_l, norm_g, skip)


def kernel(x, c, ctx, c_ctx, norm_g, w_ada, b_ada, final_g, mlstm_w_in, mlstm_conv_w, mlstm_conv_b, mlstm_wq, mlstm_wk, mlstm_wv, mlstm_w_gate, mlstm_b_gate, mlstm_norm_g, mlstm_skip, mlstm_w_out, diff_w_in, diff_lq1, diff_lk1, diff_lq2, diff_lk2, diff_subln_g, diff_w_out, na_w_in, na_rpb, na_w_out, swa_w_in, swa_sink, swa_w_out):
    bsz = x.shape[0]
    ctx_row = bsz
    mod_rows = -(-(bsz + 1) // 8) * 8
    cvec = jnp.zeros((mod_rows, D_MODEL), F32).at[:bsz].set(c).at[ctx_row].set(c_ctx)
    mods_all = _adaln(cvec, w_ada, b_ada)
    rope_tabs = _rope_tables()

    xl = x.reshape(bsz * SEQ, D_MODEL)
    xc = ctx.reshape(bsz * CTX_LEN, D_MODEL)
    lat = dict(rows_per_mod=SEQ, mod_row0=0)
    con = dict(rows_per_mod=bsz * CTX_LEN, mod_row0=ctx_row)
    tm_ctx = min(1024, bsz * CTX_LEN)

    def in_both(i, w, tn, rope_cols=()):
        mods = mods_all[i].reshape(mod_rows, 1, 3 * D_MODEL)
        wb = w.astype(BF16)
        pl_ = _inproj(xl, norm_g[i], mods, wb, tm=1024, tn=tn, rope_cols=rope_cols, rope_tabs=rope_tabs, **lat)
        pc_ = _inproj(xc, norm_g[i], mods, wb, tm=tm_ctx, tn=tn, **con)
        return mods, pl_, pc_

    def out_both(mods, o_lat, o_ctx, p_lat, p_ctx, z_col, w_out, need_ctx, final=None, in_place=True):
        wb = w_out.astype(BF16)
        new_l = _outproj(o_lat.reshape(bsz * SEQ, D_INNER), p_lat, z_col, wb, xl, mods, tm=512,
                         final_g=final, in_place=in_place, **lat)
        new_c = xc
        if need_ctx:
            new_c = _outproj(o_ctx.reshape(bsz * CTX_LEN, D_INNER), p_ctx, z_col, wb, xc, mods, tm=512, **con)
        return new_l, new_c

    mods, p_l, p_c = in_both(0, mlstm_w_in[0], 1024)
    o_c, o_l = _mlstm_mixer(p_l.reshape(bsz, SEQ, -1), p_c.reshape(bsz, CTX_LEN, -1), mlstm_conv_w[0],
                            mlstm_conv_b[0], mlstm_wq[0], mlstm_wk[0], mlstm_wv[0], mlstm_w_gate[0],
                            mlstm_b_gate[0], mlstm_norm_g[0], mlstm_skip[0])
    xl, xc = out_both(mods, o_l, o_c, p_l, p_c, D_INNER, mlstm_w_out[0], True, in_place=False)

    wd = diff_w_in[0]
    wd = jnp.concatenate([_pair_interleave(wd[:, :D_INNER]), _pair_interleave(wd[:, D_INNER:2 * D_INNER]),
                          wd[:, 2 * D_INNER:]], axis=1)
    mods, p_l, p_c = in_both(1, wd, 1024, rope_cols=((0, 2 * D_INNER),))
    p_l3, p_c3 = p_l.reshape(bsz, SEQ, -1), p_c.reshape(bsz, CTX_LEN, -1)
    lam_init = 0.8 - 0.6 * math.exp(-0.3 * 1)
    lqk = jnp.stack([diff_lq1[0], diff_lk1[0], diff_lq2[0], diff_lk2[0]])
    o_l = _diff_attn(p_l3, p_c3, lqk, diff_subln_g[0], tq=512, lam_init=lam_init)
    o_c = _ctx_attn(p_c3, lqk, diff_subln_g[0], lam_init)
    xl, xc = out_both(mods, o_l, o_c, p_l, p_c, 3 * D_INNER, diff_w_out[0], True)

    mods, p_l, p_c = in_both(2, na_w_in[0], 1024)
    p_l3, p_c3 = p_l.reshape(bsz, SEQ, -1), p_c.reshape(bsz, CTX_LEN, -1)
    o_l = _na_attn(p_l3, p_c3, _na_bias_table(na_rpb[0]))
    o_c = _ctx_attn(p_c3)
    xl, xc = out_both(mods, o_l, o_c, p_l, p_c, 3 * D_INNER, na_w_out[0], True)

    kvw = SWA_KV_HEADS * HEAD_DIM
    w = swa_w_in[0]
    wq, wk, wv, wz = (w[:, :D_INNER], w[:, D_INNER:D_INNER + kvw], w[:, D_INNER + kvw:D_INNER + 2 * kvw],
                      w[:, D_INNER + 2 * kvw:])

    def twice(m):
        m = m.reshape(D_MODEL, SWA_KV_HEADS, 1, HEAD_DIM)
        return jnp.broadcast_to(m, (D_MODEL, SWA_KV_HEADS, 2, HEAD_DIM)).reshape(D_MODEL, SWA_KV_HEADS * LANES)

    w3 = jnp.concatenate([_pair_interleave(wq), wz, _pair_interleave(wk, dup=True), twice(wv)], axis=1)
    mods, p_l, p_c = in_both(3, w3, 1024, rope_cols=((0, D_INNER), (SWA_K_COL, SWA_V_COL)))
    sink_rows = jnp.broadcast_to(swa_sink[0][:, None], (SWA_Q_HEADS, LANES))
    o_l = _swa_attn(p_l.reshape(bsz, SEQ, -1), p_c.reshape(bsz, CTX_LEN, -1), sink_rows)
    xl, _ = out_both(mods, o_l, None, p_l, p_c, D_INNER, swa_w_out[0], False, final=final_g)
    return xl.reshape(bsz, SEQ, D_MODEL)
```

```python
import functools
import math

import numpy as np
import jax
import jax.numpy as jnp
from jax import lax
from jax.experimental import pallas as pl
from jax.experimental.pallas import tpu as pltpu

F32 = jnp.float32
BF16 = jnp.bfloat16

D_MODEL = 1024
SEQ = 2048
CTX_LEN = 256
DEPTH = 4
D_INNER = 2 * D_MODEL
GRID_W = 64
RMS_EPS = 1e-6
ROPE_THETA = 10000.0

MLSTM_HEADS = 8
MLSTM_HEAD_DIM = D_INNER // MLSTM_HEADS
MLSTM_QKV_BLOCK = 4
MLSTM_CONV = 5
SCAN_CHUNK = 256

DIFF_HEADS = 16
HEAD_DIM = 64
NA_HEADS = 32
NA_WIN_ROWS = 8
NA_WIN_COLS = 16
SWA_Q_HEADS = 32
SWA_KV_HEADS = 4
SWA_WINDOW = 128

LANES = 128
MIB = 1024 * 1024
NEG_BIG = -1e30


def _cparams(semantics, vmem_mib):
    return pltpu.CompilerParams(dimension_semantics=semantics, vmem_limit_bytes=vmem_mib * MIB)


def _silu(v):
    return v * jax.nn.sigmoid(v)


def _adaln_kernel(c_ref, w_ref, b_ref, o_ref):
    s = _silu(c_ref[...])
    o_ref[0] = jnp.dot(s, w_ref[0], preferred_element_type=F32,
                       precision=lax.Precision.HIGHEST) + b_ref[0]


def _adaln(cvec, w_ada, b_ada):
    rows = cvec.shape[0]
    return pl.pallas_call(
        _adaln_kernel,
        grid=(DEPTH, 3),
        in_specs=[
            pl.BlockSpec((rows, D_MODEL), lambda i, j: (0, 0)),
            pl.BlockSpec((1, D_MODEL, D_MODEL), lambda i, j: (i, 0, j)),
            pl.BlockSpec((1, 1, D_MODEL), lambda i, j: (i, 0, j)),
        ],
        out_specs=pl.BlockSpec((1, rows, D_MODEL), lambda i, j: (i, 0, j)),
        out_shape=jax.ShapeDtypeStruct((DEPTH, rows, 3 * D_MODEL), F32),
        compiler_params=_cparams(("parallel", "parallel"), 32),
        name="adaln",
    )(cvec, w_ada, b_ada.reshape(DEPTH, 1, 3 * D_MODEL))


ROPE_Q = HEAD_DIM // 4


def _rope_tables():
    inv = np.power(ROPE_THETA, -np.arange(0, 2 * ROPE_Q, 2, dtype=np.float64) / (2 * ROPE_Q))
    t = np.arange(SEQ)
    ang = np.concatenate([(t // GRID_W).astype(np.float64)[:, None] * inv,
                          (t % GRID_W).astype(np.float64)[:, None] * inv], axis=1)
    cos = np.tile(np.cos(ang), (1, 4))
    sin = np.concatenate([-np.tile(np.sin(ang), (1, 2)), np.tile(np.sin(ang), (1, 2))], axis=1)
    return jnp.asarray(cos, F32), jnp.asarray(sin, F32)


def _pair_interleave(w, dup=False):
    d, n = w.shape
    h = n // HEAD_DIM
    w = w.reshape(d, h, 2, 2, ROPE_Q)
    if dup:
        w = jnp.broadcast_to(w[:, :, None], (d, h, 2, 2, 2, ROPE_Q))
    else:
        w = w.reshape(d, h // 2, 2, 2, 2, ROPE_Q)
    return jnp.transpose(w, (0, 1, 4, 2, 3, 5)).reshape(d, -1)


def _rope_chunk(x, cos, sin):
    return x * cos + pltpu.roll(x, LANES // 2, axis=1) * sin


def _inproj_kernel(*refs, tn, rope_plan):
    has_rope = bool(rope_plan)
    if has_rope:
        x_ref, g_ref, sh_ref, sc_ref, w_ref, cos_ref, sin_ref, o_ref, h_scr = refs
    else:
        x_ref, g_ref, sh_ref, sc_ref, w_ref, o_ref, h_scr = refs
    j = pl.program_id(1)

    @pl.when(j == 0)
    def _():
        x = x_ref[...]
        ms = jnp.mean(x * x, axis=-1, keepdims=True)
        y = x * lax.rsqrt(ms + RMS_EPS) * g_ref[...]
        h_scr[...] = (y * (1.0 + sc_ref[0]) + sh_ref[0]).astype(BF16)

    acc = jnp.dot(h_scr[...], w_ref[...], preferred_element_type=F32)

    def store_plain():
        o_ref[...] = acc.astype(BF16)

    def store_rope(flags):
        cos = cos_ref[...]
        sin = sin_ref[...]
        for c, roped in enumerate(flags):
            blk = acc[:, c * LANES:(c + 1) * LANES]
            if roped:
                blk = _rope_chunk(blk, cos, sin)
            o_ref[:, c * LANES:(c + 1) * LANES] = blk.astype(BF16)

    if not has_rope:
        store_plain()
    else:
        any_rope = None
        for flags, tiles in rope_plan:
            hit = functools.reduce(jnp.logical_or, [j == t for t in tiles])
            pl.when(hit)(functools.partial(store_rope, flags))
            any_rope = hit if any_rope is None else jnp.logical_or(any_rope, hit)
        pl.when(jnp.logical_not(any_rope))(store_plain)


def _inproj(x2d, norm_g, mods, w, *, rows_per_mod, mod_row0, tm, tn, rope_cols=(), rope_tabs=None):
    rows, _ = x2d.shape
    n = w.shape[1]
    assert rows % tm == 0 and n % tn == 0 and rows_per_mod % tm == 0
    tiles_per_mod = rows_per_mod // tm
    assert all(lo % LANES == 0 and hi % LANES == 0 for lo, hi in rope_cols)
    plan = {}
    for t in range(n // tn):
        flags = tuple(any(lo <= t * tn + c * LANES < hi for lo, hi in rope_cols) for c in range(tn // LANES))
        if any(flags):
            plan.setdefault(flags, []).append(t)
    rope_plan = tuple((flags, tuple(tiles)) for flags, tiles in plan.items())
    has_rope = bool(rope_plan)
    in_specs = [
        pl.BlockSpec((tm, D_MODEL), lambda i, j: (i, 0)),
        pl.BlockSpec((1, D_MODEL), lambda i, j: (0, 0)),
        pl.BlockSpec((1, 1, D_MODEL), lambda i, j: (mod_row0 + i // tiles_per_mod, 0, 0)),
        pl.BlockSpec((1, 1, D_MODEL), lambda i, j: (mod_row0 + i // tiles_per_mod, 0, 1)),
        pl.BlockSpec((D_MODEL, tn), lambda i, j: (0, j)),
    ]
    args = [x2d, norm_g.reshape(1, D_MODEL), mods, mods, w]
    if has_rope:
        assert SEQ % tm == 0
        seq_tiles = SEQ // tm
        in_specs += [pl.BlockSpec((tm, LANES), lambda i, j: (i % seq_tiles, 0))] * 2
        args += list(rope_tabs)
    vmem = (2 * tm * D_MODEL * 4 + tm * D_MODEL * 2 + 2 * D_MODEL * tn * 2 + 2 * tm * tn * 2
            + tm * tn * 4 + (4 * tm * LANES * 4 if has_rope else 0)) // MIB + 8
    return pl.pallas_call(
        functools.partial(_inproj_kernel, tn=tn, rope_plan=rope_plan),
        grid=(rows // tm, n // tn),
        in_specs=in_specs,
        out_specs=pl.BlockSpec((tm, tn), lambda i, j: (i, j)),
        out_shape=jax.ShapeDtypeStruct((rows, n), BF16),
        scratch_shapes=[pltpu.VMEM((tm, D_MODEL), BF16)],
        compiler_params=_cparams(("parallel", "arbitrary"), vmem),
        name="inproj",
    )(*args)


def _outproj_kernel(o_ref, z_ref, w_ref, x_ref, gt_ref, *rest, final):
    z = z_ref[...].astype(F32)
    a = (o_ref[...].astype(F32) * _silu(z)).astype(BF16)
    y = jnp.dot(a, w_ref[...], preferred_element_type=F32)
    xn = x_ref[...] + gt_ref[0] * y
    if final:
        fg_ref, out_ref = rest
        ms = jnp.mean(xn * xn, axis=-1, keepdims=True)
        xn = xn * lax.rsqrt(ms + RMS_EPS) * fg_ref[...]
    else:
        (out_ref,) = rest
    out_ref[...] = xn


def _outproj(o2d, proj2d, z_col, w_out, x2d, mods, *, rows_per_mod, mod_row0, tm, final_g=None, in_place=True):
    rows = x2d.shape[0]
    assert rows % tm == 0 and rows_per_mod % tm == 0 and z_col % D_INNER == 0
    tiles_per_mod = rows_per_mod // tm
    zb = z_col // D_INNER
    final = final_g is not None
    in_specs = [
        pl.BlockSpec((tm, D_INNER), lambda i: (i, 0)),
        pl.BlockSpec((tm, D_INNER), lambda i: (i, zb)),
        pl.BlockSpec((D_INNER, D_MODEL), lambda i: (0, 0)),
        pl.BlockSpec((tm, D_MODEL), lambda i: (i, 0)),
        pl.BlockSpec((1, 1, D_MODEL), lambda i: (mod_row0 + i // tiles_per_mod, 0, 2)),
    ]
    args = [o2d, proj2d, w_out, x2d, mods]
    if final:
        in_specs.append(pl.BlockSpec((1, D_MODEL), lambda i: (0, 0)))
        args.append(final_g.reshape(1, D_MODEL))
    vmem = (4 * tm * D_INNER * 2 + 2 * D_INNER * D_MODEL * 2 + 4 * tm * D_MODEL * 4
            + 3 * tm * D_INNER * 4) // MIB + 8
    return pl.pallas_call(
        functools.partial(_outproj_kernel, final=final),
        grid=(rows // tm,),
        in_specs=in_specs,
        out_specs=pl.BlockSpec((tm, D_MODEL), lambda i: (i, 0)),
        out_shape=jax.ShapeDtypeStruct((rows, D_MODEL), F32),
        input_output_aliases={3: 0} if in_place else {},
        compiler_params=_cparams(("parallel",), vmem),
        name="outproj",
    )(*args)


def _half_masks():
    lane = lax.broadcasted_iota(jnp.int32, (1, LANES), 1)
    return lane < HEAD_DIM, lane >= HEAD_DIM


def _nt(a, b):
    return lax.dot_general(a, b, (((1,), (1,)), ((), ())), preferred_element_type=F32)


LOG2E = math.log2(math.e)
QK_SCALE = HEAD_DIM ** -0.5 * LOG2E


def _split_heads(q, interleaved=False):
    lo, hi = _half_masks()
    if interleaved:
        lane = lax.broadcasted_iota(jnp.int32, (1, LANES), 1)
        lo = (lane % HEAD_DIM) < HEAD_DIM // 2
        hi = jnp.logical_not(lo)
    qs = q.astype(F32) * QK_SCALE
    return jnp.concatenate([jnp.where(lo, qs, 0.0), jnp.where(hi, qs, 0.0)], axis=0).astype(BF16)


def _ones_col(rows):
    lane = lax.broadcasted_iota(jnp.int32, (rows, LANES), 1)
    return jnp.where(lane == 0, 1.0, 0.0).astype(BF16)


def _diff_attn_kernel(q_ref, kl_ref, vl_ref, kc_ref, vc_ref, lqk_ref, g_ref, o_ref, sa_scr, sb_scr, *, lam_init, tq):
    nq = q_ref.shape[1] // tq
    lqk = lqk_ref[...]
    s1 = jnp.sum(lqk[0:1] * lqk[1:2], axis=-1, keepdims=True)
    s2 = jnp.sum(lqk[2:3] * lqk[3:4], axis=-1, keepdims=True)
    lam = jnp.exp(s1) - jnp.exp(s2) + lam_init
    bufs = (sa_scr, sb_scr)

    def logits(i):
        s_scr = bufs[i % 2]
        qq = _split_heads(q_ref[0, i * tq:(i + 1) * tq, :], interleaved=True)
        s_scr[:, :CTX_LEN] = _nt(qq, kc_ref[0])
        s_scr[:, CTX_LEN:] = _nt(qq, kl_ref[0])

    def finish(i):
        s = bufs[i % 2][...]
        m = jnp.max(s, axis=-1, keepdims=True)
        p = jnp.exp2(s - m)
        den = jnp.sum(p, axis=-1, keepdims=True)
        a = (p[:tq] - (lam * den[:tq] / den[tq:]) * p[tq:]).astype(BF16)
        o = (jnp.dot(a[:, :CTX_LEN], vc_ref[0], preferred_element_type=F32)
             + jnp.dot(a[:, CTX_LEN:], vl_ref[0], preferred_element_type=F32)) / den[:tq]
        ms = jnp.mean(o * o, axis=-1, keepdims=True)
        o = o * lax.rsqrt(ms + RMS_EPS) * g_ref[...] * (1.0 - lam_init)
        o_ref[0, i * tq:(i + 1) * tq, :] = o.astype(BF16)

    logits(0)
    for i in range(nq):
        if i + 1 < nq:
            logits(i + 1)
        finish(i)


def _diff_attn(proj_lat, proj_ctx, lqk, subln_g, *, tq, lam_init):
    bsz = proj_lat.shape[0]
    kb, vb = D_INNER // LANES, 2 * D_INNER // LANES
    nk = CTX_LEN + SEQ
    score_bytes = 2 * tq * nk * 4
    vmem = (5 * score_bytes) // MIB + 16
    return pl.pallas_call(
        functools.partial(_diff_attn_kernel, lam_init=lam_init, tq=tq),
        grid=(bsz, DIFF_HEADS),
        in_specs=[pl.BlockSpec((1, SEQ, LANES), lambda b, h: (b, 0, h)),
                  pl.BlockSpec((1, SEQ, LANES), lambda b, h: (b, 0, kb + h)),
                  pl.BlockSpec((1, SEQ, LANES), lambda b, h: (b, 0, vb + h)),
                  pl.BlockSpec((1, CTX_LEN, LANES), lambda b, h: (b, 0, kb + h)),
                  pl.BlockSpec((1, CTX_LEN, LANES), lambda b, h: (b, 0, vb + h)),
                  pl.BlockSpec((4, HEAD_DIM), lambda b, h: (0, 0)),
                  pl.BlockSpec((1, LANES), lambda b, h: (0, 0))],
        out_specs=pl.BlockSpec((1, SEQ, LANES), lambda b, h: (b, 0, h)),
        out_shape=jax.ShapeDtypeStruct((bsz, SEQ, D_INNER), BF16),
        scratch_shapes=[pltpu.VMEM((2 * tq, nk), F32), pltpu.VMEM((2 * tq, nk), F32)],
        compiler_params=_cparams(("parallel", "parallel"), vmem),
        name="diff_attn",
    )(proj_lat, proj_lat, proj_lat, proj_ctx, proj_ctx, lqk, subln_g.reshape(1, LANES))


CTX_HB = 4


def _ctx_attn_kernel(*refs, diff, lam_init):
    if diff:
        q_ref, k_ref, v_ref, lqk_ref, g_ref, o_ref = refs
        lqk = lqk_ref[...]
        lam = (jnp.exp(jnp.sum(lqk[0:1] * lqk[1:2], axis=-1, keepdims=True))
               - jnp.exp(jnp.sum(lqk[2:3] * lqk[3:4], axis=-1, keepdims=True)) + lam_init)
    else:
        q_ref, k_ref, v_ref, o_ref = refs
    t = q_ref.shape[1]
    lo, _ = _half_masks()
    for hb in range(CTX_HB):
        sl = slice(hb * LANES, (hb + 1) * LANES)
        qq = _split_heads(q_ref[0, :, sl], interleaved=diff)
        s = _nt(qq, k_ref[0, :, sl])
        p = jnp.exp2(s - jnp.max(s, axis=-1, keepdims=True))
        den = jnp.sum(p, axis=-1, keepdims=True)
        if diff:
            a = (p[:t] - (lam * den[:t] / den[t:]) * p[t:]).astype(BF16)
            o = jnp.dot(a, v_ref[0, :, sl], preferred_element_type=F32) / den[:t]
            ms = jnp.mean(o * o, axis=-1, keepdims=True)
            o = o * lax.rsqrt(ms + RMS_EPS) * g_ref[...] * (1.0 - lam_init)
        else:
            o = jnp.dot(p.astype(BF16), v_ref[0, :, sl], preferred_element_type=F32) / den
            o = jnp.where(lo, o[:t], o[t:])
        o_ref[0, :, sl] = o.astype(BF16)


def _ctx_attn(proj_ctx, lqk=None, subln_g=None, lam_init=0.0):
    bsz = proj_ctx.shape[0]
    diff = lqk is not None
    w = CTX_HB * LANES
    kb, vb = D_INNER // w, 2 * D_INNER // w
    in_specs = [pl.BlockSpec((1, CTX_LEN, w), lambda b, h: (b, 0, h)),
                pl.BlockSpec((1, CTX_LEN, w), lambda b, h: (b, 0, kb + h)),
                pl.BlockSpec((1, CTX_LEN, w), lambda b, h: (b, 0, vb + h))]
    args = [proj_ctx, proj_ctx, proj_ctx]
    if diff:
        in_specs += [pl.BlockSpec((4, HEAD_DIM), lambda b, h: (0, 0)), pl.BlockSpec((1, LANES), lambda b, h: (0, 0))]
        args += [lqk, subln_g.reshape(1, LANES)]
    return pl.pallas_call(
        functools.partial(_ctx_attn_kernel, diff=diff, lam_init=lam_init),
        grid=(bsz, D_INNER // w),
        in_specs=in_specs,
        out_specs=pl.BlockSpec((1, CTX_LEN, w), lambda b, h: (b, 0, h)),
        out_shape=jax.ShapeDtypeStruct((bsz, CTX_LEN, D_INNER), BF16),
        compiler_params=_cparams(("parallel", "parallel"), 32),
        name="ctx_attn",
    )(*args)


NA_ROWS = SEQ // GRID_W
NA_BLK_ROWS = 4
NA_BLK_Q = NA_BLK_ROWS * GRID_W
NA_NBLK = NA_ROWS // NA_BLK_ROWS
NA_WIN_R = NA_BLK_ROWS + NA_WIN_ROWS
NA_KWIN = NA_WIN_R * GRID_W
NA_CTX_CHUNK = 512
NA_MAX_LANE = LANES + 1


def _na_win_start(i):
    lo, hi = 0, NA_ROWS - NA_WIN_R
    start = i * NA_BLK_ROWS - NA_WIN_ROWS // 2
    return np.clip(start, lo, hi) if isinstance(i, (int, np.integer)) else jnp.clip(start, lo, hi)


def _na_bias_table(rpb):
    c = np.arange(GRID_W)
    cs = np.clip(c - NA_WIN_COLS // 2, 0, GRID_W - NA_WIN_COLS)
    kc = np.arange(GRID_W)
    col_ok = (kc[None, :] >= cs[:, None]) & (kc[None, :] < cs[:, None] + NA_WIN_COLS)
    dc_idx = np.clip(kc[None, :] - c[:, None], -(NA_WIN_COLS - 1), NA_WIN_COLS - 1) + NA_WIN_COLS - 1
    dr_idx = np.zeros((3, NA_BLK_ROWS, NA_WIN_R), np.int32)
    row_ok = np.zeros((3, NA_BLK_ROWS, NA_WIN_R), bool)
    for pat, i in enumerate((0, 1, NA_NBLK - 1)):
        ws = int(_na_win_start(i))
        for qr in range(NA_BLK_ROWS):
            r = i * NA_BLK_ROWS + qr
            rs = min(max(r - NA_WIN_ROWS // 2, 0), NA_ROWS - NA_WIN_ROWS)
            for a in range(NA_WIN_R):
                row_ok[pat, qr, a] = rs <= ws + a < rs + NA_WIN_ROWS
                dr_idx[pat, qr, a] = min(max(ws + a - r, -(NA_WIN_ROWS - 1)), NA_WIN_ROWS - 1) + NA_WIN_ROWS - 1
    t1 = jnp.where(col_ok[None, None], rpb[:, :, dc_idx] * LOG2E, NEG_BIG).astype(F32)
    masked = jnp.full((rpb.shape[0], GRID_W, GRID_W), NEG_BIG, F32)
    row_blocks = [jnp.concatenate([t1[:, dr_idx[pat, qr, a]] if row_ok[pat, qr, a] else masked
                                   for a in range(NA_WIN_R)], axis=-1)
                  for pat in range(3) for qr in range(NA_BLK_ROWS)]
    return jnp.stack(row_blocks, axis=1).reshape(rpb.shape[0], 3, NA_BLK_Q, NA_KWIN)


def _na_attn_kernel(q_ref, k_ref, v_ref, kc_ref, vc_ref, bias_ref, o_ref, oc_scr):
    lo, _ = _half_masks()
    kctx = kc_ref[0]
    vctx = jnp.concatenate([vc_ref[0], _ones_col(CTX_LEN)], axis=1)
    lane2 = lax.broadcasted_iota(jnp.int32, (1, 2 * LANES), 1)

    def ctx_chunk(j, carry):
        r0 = pl.multiple_of(jnp.asarray(j, jnp.int32) * NA_CTX_CHUNK, NA_CTX_CHUNK)
        qq = _split_heads(q_ref[0, pl.ds(r0, NA_CTX_CHUNK), :])
        s = _nt(qq, kctx)
        m = jnp.max(s, axis=-1, keepdims=True)
        oc = jnp.dot(jnp.exp2(s - m).astype(BF16), vctx, preferred_element_type=F32)
        oc = jnp.where(lane2 == NA_MAX_LANE, m, oc)
        oc_scr[0, pl.ds(r0, NA_CTX_CHUNK), :] = oc[:NA_CTX_CHUNK]
        oc_scr[1, pl.ds(r0, NA_CTX_CHUNK), :] = oc[NA_CTX_CHUNK:]
        return carry

    lax.fori_loop(0, SEQ // NA_CTX_CHUNK, ctx_chunk, 0, unroll=True)

    def block(i, carry):
        i = jnp.asarray(i, jnp.int32)
        pat = jnp.where(i == 0, 0, jnp.where(i == NA_NBLK - 1, 2, 1))
        t0 = pl.multiple_of(i * NA_BLK_Q, NA_BLK_Q)
        s0 = pl.multiple_of(_na_win_start(i) * GRID_W, NA_BLK_Q)
        qq = _split_heads(q_ref[0, pl.ds(t0, NA_BLK_Q), :])
        kw = k_ref[0, pl.ds(s0, NA_KWIN), :]
        vw = jnp.concatenate([v_ref[0, pl.ds(s0, NA_KWIN), :], _ones_col(NA_KWIN)], axis=1)
        s = _nt(qq, kw) + jnp.concatenate([bias_ref[0, pat], bias_ref[1, pat]], axis=0)
        oc = jnp.concatenate([oc_scr[0, pl.ds(t0, NA_BLK_Q), :], oc_scr[1, pl.ds(t0, NA_BLK_Q), :]], axis=0)
        mc = oc[:, NA_MAX_LANE:NA_MAX_LANE + 1]
        m = jnp.maximum(jnp.max(s, axis=-1, keepdims=True), mc)
        tot = (jnp.dot(jnp.exp2(s - m).astype(BF16), vw, preferred_element_type=F32)
               + jnp.exp2(mc - m) * oc)
        out = tot[:, :LANES] / tot[:, LANES:LANES + 1]
        o_ref[0, pl.ds(t0, NA_BLK_Q), :] = jnp.where(lo, out[:NA_BLK_Q], out[NA_BLK_Q:]).astype(BF16)
        return carry

    lax.fori_loop(0, NA_NBLK, block, 0, unroll=True)


def _na_attn(proj_lat, proj_ctx, bias_tab):
    bsz = proj_lat.shape[0]
    pairs = NA_HEADS // 2
    kb, vb = D_INNER // LANES, 2 * D_INNER // LANES
    return pl.pallas_call(
        _na_attn_kernel,
        grid=(pairs, bsz),
        in_specs=[pl.BlockSpec((1, SEQ, LANES), lambda h, b: (b, 0, h)),
                  pl.BlockSpec((1, SEQ, LANES), lambda h, b: (b, 0, kb + h)),
                  pl.BlockSpec((1, SEQ, LANES), lambda h, b: (b, 0, vb + h)),
                  pl.BlockSpec((1, CTX_LEN, LANES), lambda h, b: (b, 0, kb + h)),
                  pl.BlockSpec((1, CTX_LEN, LANES), lambda h, b: (b, 0, vb + h)),
                  pl.BlockSpec((2, 3, NA_BLK_Q, NA_KWIN), lambda h, b: (h, 0, 0, 0))],
        out_specs=pl.BlockSpec((1, SEQ, LANES), lambda h, b: (b, 0, h)),
        out_shape=jax.ShapeDtypeStruct((bsz, SEQ, D_INNER), BF16),
        scratch_shapes=[pltpu.VMEM((2, SEQ, 2 * LANES), F32)],
        compiler_params=_cparams(("parallel", "arbitrary"), 48),
        name="na_attn",
    )(proj_lat, proj_lat, proj_lat, proj_ctx, proj_ctx, bias_tab)


SWA_TQ = 256
SWA_BAND = SWA_TQ + 2 * SWA_WINDOW
SWA_K_COL = 2 * D_INNER
SWA_V_COL = SWA_K_COL + SWA_KV_HEADS * LANES
SWA_N = SWA_V_COL + SWA_KV_HEADS * LANES


SWA_GROUP = SWA_Q_HEADS // SWA_KV_HEADS
SWA_GW = SWA_GROUP * HEAD_DIM


def _swa_attn_kernel(q_ref, k_ref, v_ref, kc_ref, vc_ref, sink_ref, o_ref):
    lo, _ = _half_masks()
    kvh = pl.program_id(1)
    n_pair = SWA_GROUP // 2
    kctx = kc_ref[0]
    vctx = jnp.concatenate([vc_ref[0], _ones_col(CTX_LEN)], axis=1)
    sinks = sink_ref[pl.ds(pl.multiple_of(kvh * SWA_GROUP, SWA_GROUP), SWA_GROUP), :][:, 0:1] * LOG2E
    sink = jnp.concatenate([jnp.broadcast_to(sinks[g:g + 1], (SWA_TQ, 1)) for g in range(SWA_GROUP)], axis=0)

    def block(i, carry):
        i = jnp.asarray(i, jnp.int32)
        t0 = pl.multiple_of(i * SWA_TQ, SWA_TQ)
        s0 = pl.multiple_of(jnp.clip(i * SWA_TQ - SWA_WINDOW, 0, SEQ - SWA_BAND), SWA_WINDOW)
        q = q_ref[0, pl.ds(t0, SWA_TQ), :]
        qq = jnp.concatenate([_split_heads(q[:, p * LANES:(p + 1) * LANES], interleaved=True)
                              for p in range(n_pair)], axis=0)
        kw = k_ref[0, pl.ds(s0, SWA_BAND), :]
        vw = jnp.concatenate([v_ref[0, pl.ds(s0, SWA_BAND), :], _ones_col(SWA_BAND)], axis=1)
        tpos = t0 + lax.broadcasted_iota(jnp.int32, (SWA_TQ, SWA_BAND), 0)
        spos = s0 + lax.broadcasted_iota(jnp.int32, (SWA_TQ, SWA_BAND), 1)
        bias = jnp.where(jnp.abs(tpos - spos) <= SWA_WINDOW, 0.0, NEG_BIG)
        s_lat = _nt(qq, kw) + jnp.concatenate([bias] * SWA_GROUP, axis=0)
        s_ctx = _nt(qq, kctx)
        m = jnp.maximum(jnp.maximum(jnp.max(s_lat, axis=-1, keepdims=True),
                                    jnp.max(s_ctx, axis=-1, keepdims=True)), sink)
        tot = (jnp.dot(jnp.exp2(s_lat - m).astype(BF16), vw, preferred_element_type=F32)
               + jnp.dot(jnp.exp2(s_ctx - m).astype(BF16), vctx, preferred_element_type=F32))
        out = tot[:, :LANES] / (tot[:, LANES:LANES + 1] + jnp.exp2(sink - m))
        pairs = [jnp.where(lo, out[(2 * p) * SWA_TQ:(2 * p + 1) * SWA_TQ],
                           out[(2 * p + 1) * SWA_TQ:(2 * p + 2) * SWA_TQ]) for p in range(n_pair)]
        o_ref[0, pl.ds(t0, SWA_TQ), :] = jnp.concatenate(pairs, axis=1).astype(BF16)
        return carry

    lax.fori_loop(0, SEQ // SWA_TQ, block, 0, unroll=True)


def _swa_attn(proj_lat, proj_ctx, sink_rows):
    bsz = proj_lat.shape[0]
    kb, vb = SWA_K_COL // LANES, SWA_V_COL // LANES
    return pl.pallas_call(
        _swa_attn_kernel,
        grid=(bsz, SWA_KV_HEADS),
        in_specs=[pl.BlockSpec((1, SEQ, SWA_GW), lambda b, h: (b, 0, h)),
                  pl.BlockSpec((1, SEQ, LANES), lambda b, h: (b, 0, kb + h)),
                  pl.BlockSpec((1, SEQ, LANES), lambda b, h: (b, 0, vb + h)),
                  pl.BlockSpec((1, CTX_LEN, LANES), lambda b, h: (b, 0, kb + h)),
                  pl.BlockSpec((1, CTX_LEN, LANES), lambda b, h: (b, 0, vb + h)),
                  pl.BlockSpec((SWA_Q_HEADS, LANES), lambda b, h: (0, 0))],
        out_specs=pl.BlockSpec((1, SEQ, SWA_GW), lambda b, h: (b, 0, h)),
        out_shape=jax.ShapeDtypeStruct((bsz, SEQ, D_INNER), BF16),
        compiler_params=_cparams(("parallel", "parallel"), 48),
        name="swa_attn",
    )(proj_lat, proj_lat, proj_lat, proj_ctx, proj_ctx, sink_rows)


GATE_LANES = 128


def _mlstm_prep_kernel(xm_ref, cw_ref, cb_ref, bdq_ref, bdk_ref, bdv_ref, wg_ref, bg_ref,
                       q_ref, kt_ref, v_ref, xc_ref, g_ref):
    h = pl.program_id(1)
    t = xm_ref.shape[1]
    xm_b = xm_ref[0]
    xm = xm_b.astype(F32)
    tpos = lax.broadcasted_iota(jnp.int32, (t, 1), 0)
    conv = jnp.zeros_like(xm) + cb_ref[...]
    for j in range(MLSTM_CONV):
        d = j - MLSTM_CONV // 2
        sh = pltpu.roll(xm, (-d) % t, axis=0) if d else xm
        ok = (tpos + d >= 0) & (tpos + d < t)
        conv = conv + jnp.where(ok, sh, 0.0) * cw_ref[j:j + 1, :]
    xc = _silu(conv)
    xc_b = xc.astype(BF16)
    q = jnp.dot(xc_b, bdq_ref[0], preferred_element_type=F32)
    k = jnp.dot(xc_b, bdk_ref[0], preferred_element_type=F32)
    v = jnp.dot(xm_b, bdv_ref[0], preferred_element_type=F32)
    q_b, k_b, v_b = q.astype(BF16), k.astype(BF16), v.astype(BF16)
    q_ref[0] = q_b
    v_ref[0] = v_b
    xc_ref[0] = xc_b
    ks = k * (MLSTM_HEAD_DIM ** -0.5)
    for c in range(t // SCAN_CHUNK):
        kt_ref[0, 0, c] = ks[c * SCAN_CHUNK:(c + 1) * SCAN_CHUNK, :].T.astype(BF16)
    part = (jnp.dot(q_b, wg_ref[0, 0], preferred_element_type=F32)
            + jnp.dot(k_b, wg_ref[1, 0], preferred_element_type=F32)
            + jnp.dot(v_b, wg_ref[2, 0], preferred_element_type=F32))

    @pl.when(h == 0)
    def _():
        g_ref[0] = part + bg_ref[...]

    @pl.when(h != 0)
    def _():
        g_ref[0] = g_ref[0] + part


def _mlstm_prep(proj3, conv_w, conv_b, bdq, bdk, bdv, wg, bg):
    bsz, t, _ = proj3.shape
    hd = MLSTM_HEAD_DIM
    nc = t // SCAN_CHUNK
    seq_spec = pl.BlockSpec((1, t, hd), lambda b, h: (b, 0, h))
    return pl.pallas_call(
        _mlstm_prep_kernel,
        grid=(bsz, MLSTM_HEADS),
        in_specs=[seq_spec,
                  pl.BlockSpec((MLSTM_CONV, hd), lambda b, h: (0, h)),
                  pl.BlockSpec((1, hd), lambda b, h: (0, h)),
                  pl.BlockSpec((1, hd, hd), lambda b, h: (h, 0, 0)),
                  pl.BlockSpec((1, hd, hd), lambda b, h: (h, 0, 0)),
                  pl.BlockSpec((1, hd, hd), lambda b, h: (h, 0, 0)),
                  pl.BlockSpec((3, 1, hd, GATE_LANES), lambda b, h: (0, h, 0, 0)),
                  pl.BlockSpec((1, GATE_LANES), lambda b, h: (0, 0))],
        out_specs=[seq_spec,
                   pl.BlockSpec((1, 1, nc, hd, SCAN_CHUNK), lambda b, h: (b, h, 0, 0, 0)),
                   seq_spec, seq_spec,
                   pl.BlockSpec((1, t, GATE_LANES), lambda b, h: (b, 0, 0))],
        out_shape=[jax.ShapeDtypeStruct((bsz, t, D_INNER), BF16),
                   jax.ShapeDtypeStruct((bsz, MLSTM_HEADS, nc, hd, SCAN_CHUNK), BF16),
                   jax.ShapeDtypeStruct((bsz, t, D_INNER), BF16),
                   jax.ShapeDtypeStruct((bsz, t, D_INNER), BF16),
                   jax.ShapeDtypeStruct((bsz, t, GATE_LANES), F32)],
        compiler_params=_cparams(("parallel", "arbitrary"), 48),
        name="mlstm_prep",
    )(proj3, conv_w, conv_b.reshape(1, D_INNER), bdq, bdk, bdv, wg, bg)


def _lane_scan(x, op, fill, reverse):
    lane = lax.broadcasted_iota(jnp.int32, x.shape, 1)
    sh = 1
    while sh < LANES:
        if reverse:
            y = jnp.where(lane < LANES - sh, pltpu.roll(x, LANES - sh, axis=1), fill)
        else:
            y = jnp.where(lane >= sh, pltpu.roll(x, sh, axis=1), fill)
        x = op(x, y)
        sh *= 2
    return x


GP_ROWS = 8


def _mlstm_gates_kernel(gc_ref, gl_ref, oc_ref, ol_ref):
    nh = MLSTM_HEADS
    ppc = SCAN_CHUNK // LANES
    gct = gc_ref[0].T
    glt = gl_ref[0].T
    n_c = gc_ref.shape[1] // SCAN_CHUNK
    n_l = gl_ref.shape[1] // SCAN_CHUNK
    zeros = jnp.zeros((nh, SCAN_CHUNK), F32)
    pieces = [(gct, j) for j in range(n_c * ppc)] + [(glt, j) for j in range(n_l * ppc)]
    chunks = [(oc_ref, c) for c in range(n_c)] + [(ol_ref, c) for c in range(n_l)]
    n_all = len(chunks)

    def stack(row0):
        return jnp.concatenate([src[row0:row0 + nh, j * LANES:(j + 1) * LANES] for src, j in pieces], axis=0)

    def piece(x, k):
        return x[k * nh:(k + 1) * nh]

    def earlier(k, rev):
        c = k // ppc
        return range(k + 1, (c + 1) * ppc) if rev else range(c * ppc, k)

    for d in range(2):
        rev = d == 1
        edge = (lambda x: x[:, 0:1]) if rev else (lambda x: x[:, LANES - 1:LANES])
        ig = stack(2 * d * nh)
        fg = stack((2 * d + 1) * nh)
        lf = jnp.minimum(fg, 0.0) - jnp.log1p(jnp.exp(-jnp.abs(fg)))
        bp = _lane_scan(lf, jnp.add, 0.0, rev)
        b = jnp.concatenate([piece(bp, k) + sum(piece(edge(bp), o) for o in earlier(k, rev))
                             for k in range(len(pieces))], axis=0)
        cmp_ = _lane_scan(ig - b, jnp.maximum, -jnp.inf, rev)
        cm = jnp.concatenate([functools.reduce(jnp.maximum, [piece(edge(cmp_), o) for o in earlier(k, rev)],
                                               piece(cmp_, k)) for k in range(len(pieces))], axis=0)
        last = (lambda c: c * ppc) if rev else (lambda c: c * ppc + ppc - 1)
        b_end_c = [piece(edge(b), last(c)) for c in range(n_all)]
        b_end = jnp.concatenate([b_end_c[k // ppc] for k in range(len(pieces))], axis=0)
        g = b_end - b + ig
        gmax_p = jnp.max(g, axis=-1, keepdims=True)
        gmax_c = [functools.reduce(jnp.maximum, [piece(gmax_p, c * ppc + p) for p in range(ppc)])
                  for c in range(n_all)]
        order = list(range(n_all))
        if rev:
            order = list(reversed(range(n_c))) + list(reversed(range(n_c, n_all)))
        m = jnp.zeros((nh, 1), F32)
        m_in, m_out = [None] * n_all, [None] * n_all
        for c in order:
            m_in[c] = m
            m = jnp.maximum(b_end_c[c] + m, gmax_c[c])
            m_out[c] = m
        m_in = jnp.concatenate([m_in[k // ppc] for k in range(len(pieces))], axis=0)
        m_out = jnp.concatenate([m_out[k // ppc] for k in range(len(pieces))], axis=0)
        mx = jnp.maximum(cm, m_in)
        rows = (-mx, b - ig, jnp.exp(m_in - mx), jnp.exp(-(b + mx)), jnp.exp(g - m_out),
                jnp.broadcast_to(jnp.exp(b_end + m_in - m_out), b.shape))
        for k, (dst, c) in enumerate(chunks):
            for r, val in enumerate(rows):
                dst[0, d, c, r] = jnp.concatenate([piece(val, k * ppc + p) for p in range(ppc)], axis=1)
            dst[0, d, c, 6] = zeros
            dst[0, d, c, 7] = zeros


def _mlstm_gates(gates_ctx, gates_lat):
    bsz = gates_lat.shape[0]
    n_c, n_l = CTX_LEN // SCAN_CHUNK, SEQ // SCAN_CHUNK

    def out(n):
        shape = (bsz, 2, n, GP_ROWS, MLSTM_HEADS, SCAN_CHUNK)
        return (pl.BlockSpec((1,) + shape[1:], lambda b: (b, 0, 0, 0, 0, 0)), jax.ShapeDtypeStruct(shape, F32))

    (spec_c, shape_c), (spec_l, shape_l) = out(n_c), out(n_l)
    return pl.pallas_call(
        _mlstm_gates_kernel,
        grid=(bsz,),
        in_specs=[pl.BlockSpec((1, CTX_LEN, GATE_LANES), lambda b: (b, 0, 0)),
                  pl.BlockSpec((1, SEQ, GATE_LANES), lambda b: (b, 0, 0))],
        out_specs=[spec_c, spec_l],
        out_shape=[shape_c, shape_l],
        compiler_params=_cparams(("parallel",), 32),
        name="mlstm_gates",
    )(gates_ctx, gates_lat)


def _mlstm_scan_kernel(qc_ref, ktc_ref, vc_ref, xcc_ref, gpc_ref,
                       ql_ref, ktl_ref, vl_ref, xcl_ref, gpl_ref, ng_ref, sk_ref,
                       oc_ref, ol_ref, hc_scr, hl_scr, st_scr):
    L = SCAN_CHUNK
    hd = MLSTM_HEAD_DIM
    n_c = qc_ref.shape[1] // L
    n_l = ql_ref.shape[1] // L
    lane = lax.broadcasted_iota(jnp.int32, (L, LANES), 1)
    ones_col = jnp.where(lane == 0, 1.0, 0.0).astype(BF16)
    tt = lax.broadcasted_iota(jnp.int32, (L, L), 0)
    ss = lax.broadcasted_iota(jnp.int32, (L, L), 1)
    masks = (ss <= tt, ss >= tt)

    hc_scr[...] = jnp.zeros_like(hc_scr)
    hl_scr[...] = jnp.zeros_like(hl_scr)
    st_scr[...] = jnp.zeros_like(st_scr)

    def chunk(d, q_ref, kt_ref, v_ref, gp_ref, h_scr, c):
        rows = gp_ref[0, d, 0, c]
        cols = rows.T
        a_col, wst, einv = cols[:, 0:1], cols[:, 2:3], cols[:, 3:4]
        c_row, ws_row, decay = rows[1:2, :], rows[4:5, :], rows[5:6, 0:1]
        r0 = pl.multiple_of(c * L, L)
        q = q_ref[0, pl.ds(r0, L), :]
        kt = kt_ref[0, 0, c]
        vx = jnp.concatenate([v_ref[0, pl.ds(r0, L), :], ones_col], axis=1)
        w = jnp.exp(jnp.where(masks[d], a_col - c_row, -jnp.inf))
        s = (jnp.dot(q, kt, preferred_element_type=F32) * w).astype(BF16)
        kw = (kt.astype(F32) * ws_row).astype(BF16)
        both = jnp.dot(jnp.concatenate([s, kw], axis=0), vx, preferred_element_type=F32)
        state = st_scr[d]
        tot = wst * jnp.dot(q, state.astype(BF16), preferred_element_type=F32) + both[:L]
        den = tot[:, hd:hd + 1]
        h = tot[:, :hd] / jnp.maximum(jnp.abs(den), einv)
        h_scr[pl.ds(r0, L), :] = h_scr[pl.ds(r0, L), :] + h
        st_scr[d] = decay * state + both[L:]

    for j in range(n_c):
        chunk(0, qc_ref, ktc_ref, vc_ref, gpc_ref, hc_scr, j)
        chunk(1, qc_ref, ktc_ref, vc_ref, gpc_ref, hc_scr, n_c - 1 - j)

    def lat_step(j, carry):
        chunk(0, ql_ref, ktl_ref, vl_ref, gpl_ref, hl_scr, j)
        chunk(1, ql_ref, ktl_ref, vl_ref, gpl_ref, hl_scr, n_l - 1 - j)
        return carry

    lax.fori_loop(0, n_l, lat_step, 0, unroll=True)

    ng = ng_ref[...]
    sk = sk_ref[...]

    def finish(h_scr, xc_ref, o_ref, r0, n):
        h = h_scr[pl.ds(r0, n), :]
        mu = jnp.mean(h, axis=-1, keepdims=True)
        hcen = h - mu
        hn = hcen * lax.rsqrt(jnp.mean(hcen * hcen, axis=-1, keepdims=True) + RMS_EPS)
        o_ref[0, pl.ds(r0, n), :] = (hn * ng + sk * xc_ref[0, pl.ds(r0, n), :].astype(F32)).astype(BF16)

    finish(hc_scr, xcc_ref, oc_ref, 0, CTX_LEN)

    def fin_step(i, carry):
        finish(hl_scr, xcl_ref, ol_ref, pl.multiple_of(i * 256, 256), 256)
        return carry

    lax.fori_loop(0, SEQ // 256, fin_step, 0, unroll=2)


def _mlstm_scan(prep_c, gp_c, prep_l, gp_l, norm_g, skip):
    bsz = prep_l[0].shape[0]
    hd = MLSTM_HEAD_DIM

    def seq_specs(t):
        nc = t // SCAN_CHUNK
        seq = pl.BlockSpec((1, t, hd), lambda b, h: (b, 0, h))
        return [seq,
                pl.BlockSpec((1, 1, nc, hd, SCAN_CHUNK), lambda b, h: (b, h, 0, 0, 0)),
                seq, seq,
                pl.BlockSpec((1, 2, 1, nc, GP_ROWS, SCAN_CHUNK), lambda b, h: (b, 0, h, 0, 0, 0))]

    vec = pl.BlockSpec((1, hd), lambda b, h: (0, h))
    return pl.pallas_call(
        _mlstm_scan_kernel,
        grid=(bsz, MLSTM_HEADS),
        in_specs=seq_specs(CTX_LEN) + seq_specs(SEQ) + [vec, vec],
        out_specs=[pl.BlockSpec((1, CTX_LEN, hd), lambda b, h: (b, 0, h)),
                   pl.BlockSpec((1, SEQ, hd), lambda b, h: (b, 0, h))],
        out_shape=[jax.ShapeDtypeStruct((bsz, CTX_LEN, D_INNER), BF16),
                   jax.ShapeDtypeStruct((bsz, SEQ, D_INNER), BF16)],
        scratch_shapes=[pltpu.VMEM((CTX_LEN, hd), F32), pltpu.VMEM((SEQ, hd), F32),
                        pltpu.VMEM((2, hd, hd + LANES), F32)],
        compiler_params=_cparams(("parallel", "parallel"), 40),
        name="mlstm_scan",
    )(*prep_c, gp_c, *prep_l, gp_l, norm_g.reshape(1, D_INNER), skip.reshape(1, D_INNER))


def _blockdiag_dense(w):
    per_head = MLSTM_HEAD_DIM // MLSTM_QKV_BLOCK
    wh = w.reshape(MLSTM_HEADS, per_head, MLSTM_QKV_BLOCK, MLSTM_QKV_BLOCK)
    eye = jnp.eye(per_head, dtype=w.dtype)
    dense = wh[:, :, :, None, :] * eye[None, :, None, :, None]
    return dense.reshape(MLSTM_HEADS, MLSTM_HEAD_DIM, MLSTM_HEAD_DIM).astype(BF16)


def _mlstm_mixer(proj_lat, proj_ctx, conv_w, conv_b, wq, wk, wv, w_gate, b_gate, norm_g, skip):
    n_gate = 4 * MLSTM_HEADS
    wg = jnp.pad(w_gate, ((0, 0), (0, GATE_LANES - n_gate))).astype(BF16)
    wg = wg.reshape(3, MLSTM_HEADS, MLSTM_HEAD_DIM, GATE_LANES)
    bg = jnp.pad(b_gate, (0, GATE_LANES - n_gate)).reshape(1, GATE_LANES)
    bds = (_blockdiag_dense(wq), _blockdiag_dense(wk), _blockdiag_dense(wv))
    *prep_c, gates_c = _mlstm_prep(proj_ctx, conv_w, conv_b, *bds, wg, bg)
    *prep_l, gates_l = _mlstm_prep(proj_lat, conv_w, conv_b, *bds, wg, bg)
    gp_c, gp_l = _mlstm_gates(gates_c, gates_l)
    gp_c = jnp.transpose(gp_c, (0, 1, 4, 2, 3, 5))
    gp_l = jnp.transpose(gp_l, (0, 1, 4, 2, 3, 5))
    return _mlstm_scan(prep_c, gp_c, prep_l, gp_l, norm_g, skip)


def kernel(x, c, ctx, c_ctx, norm_g, w_ada, b_ada, final_g, mlstm_w_in, mlstm_conv_w, mlstm_conv_b, mlstm_wq, mlstm_wk, mlstm_wv, mlstm_w_gate, mlstm_b_gate, mlstm_norm_g, mlstm_skip, mlstm_w_out, diff_w_in, diff_lq1, diff_lk1, diff_lq2, diff_lk2, diff_subln_g, diff_w_out, na_w_in, na_rpb, na_w_out, swa_w_in, swa_sink, swa_w_out):
    bsz = x.shape[0]
    ctx_row = bsz
    mod_rows = -(-(bsz + 1) // 8) * 8
    cvec = jnp.zeros((mod_rows, D_MODEL), F32).at[:bsz].set(c).at[ctx_row].set(c_ctx)
    mods_all = _adaln(cvec, w_ada, b_ada)
    rope_tabs = _rope_tables()

    xl = x.reshape(bsz * SEQ, D_MODEL)
    xc = ctx.reshape(bsz * CTX_LEN, D_MODEL)
    lat = dict(rows_per_mod=SEQ, mod_row0=0)
    con = dict(rows_per_mod=bsz * CTX_LEN, mod_row0=ctx_row)
    tm_ctx = min(1024, bsz * CTX_LEN)

    def in_both(i, w, tn, rope_cols=()):
        mods = mods_all[i].reshape(mod_rows, 1, 3 * D_MODEL)
        wb = w.astype(BF16)
        pl_ = _inproj(xl, norm_g[i], mods, wb, tm=SEQ, tn=tn, rope_cols=rope_cols, rope_tabs=rope_tabs, **lat)
        pc_ = _inproj(xc, norm_g[i], mods, wb, tm=tm_ctx, tn=tn, **con)
        return mods, pl_, pc_

    def out_both(mods, o_lat, o_ctx, p_lat, p_ctx, z_col, w_out, need_ctx, final=None, in_place=True):
        wb = w_out.astype(BF16)
        new_l = _outproj(o_lat.reshape(bsz * SEQ, D_INNER), p_lat, z_col, wb, xl, mods, tm=512,
                         final_g=final, in_place=in_place, **lat)
        new_c = xc
        if need_ctx:
            new_c = _outproj(o_ctx.reshape(bsz * CTX_LEN, D_INNER), p_ctx, z_col, wb, xc, mods, tm=512, **con)
        return new_l, new_c

    mods, p_l, p_c = in_both(0, mlstm_w_in[0], 1024)
    o_c, o_l = _mlstm_mixer(p_l.reshape(bsz, SEQ, -1), p_c.reshape(bsz, CTX_LEN, -1), mlstm_conv_w[0],
                            mlstm_conv_b[0], mlstm_wq[0], mlstm_wk[0], mlstm_wv[0], mlstm_w_gate[0],
                            mlstm_b_gate[0], mlstm_norm_g[0], mlstm_skip[0])
    xl, xc = out_both(mods, o_l, o_c, p_l, p_c, D_INNER, mlstm_w_out[0], True, in_place=False)

    wd = diff_w_in[0]
    wd = jnp.concatenate([_pair_interleave(wd[:, :D_INNER]), _pair_interleave(wd[:, D_INNER:2 * D_INNER]),
                          wd[:, 2 * D_INNER:]], axis=1)
    mods, p_l, p_c = in_both(1, wd, 1024, rope_cols=((0, 2 * D_INNER),))
    p_l3, p_c3 = p_l.reshape(bsz, SEQ, -1), p_c.reshape(bsz, CTX_LEN, -1)
    lam_init = 0.8 - 0.6 * math.exp(-0.3 * 1)
    lqk = jnp.stack([diff_lq1[0], diff_lk1[0], diff_lq2[0], diff_lk2[0]])
    o_l = _diff_attn(p_l3, p_c3, lqk, diff_subln_g[0], tq=512, lam_init=lam_init)
    o_c = _ctx_attn(p_c3, lqk, diff_subln_g[0], lam_init)
    xl, xc = out_both(mods, o_l, o_c, p_l, p_c, 3 * D_INNER, diff_w_out[0], True)

    mods, p_l, p_c = in_both(2, na_w_in[0], 1024)
    p_l3, p_c3 = p_l.reshape(bsz, SEQ, -1), p_c.reshape(bsz, CTX_LEN, -1)
    o_l = _na_attn(p_l3, p_c3, _na_bias_table(na_rpb[0]))
    o_c = _ctx_attn(p_c3)
    xl, xc = out_both(mods, o_l, o_c, p_l, p_c, 3 * D_INNER, na_w_out[0], True)

    kvw = SWA_KV_HEADS * HEAD_DIM
    w = swa_w_in[0]
    wq, wk, wv, wz = (w[:, :D_INNER], w[:, D_INNER:D_INNER + kvw], w[:, D_INNER + kvw:D_INNER + 2 * kvw],
                      w[:, D_INNER + 2 * kvw:])

    def twice(m):
        m = m.reshape(D_MODEL, SWA_KV_HEADS, 1, HEAD_DIM)
        return jnp.broadcast_to(m, (D_MODEL, SWA_KV_HEADS, 2, HEAD_DIM)).reshape(D_MODEL, SWA_KV_HEADS * LANES)

    w3 = jnp.concatenate([_pair_interleave(wq), wz, _pair_interleave(wk, dup=True), twice(wv)], axis=1)
    mods, p_l, p_c = in_both(3, w3, 1024, rope_cols=((0, D_INNER), (SWA_K_COL, SWA_V_COL)))
    sink_rows = jnp.broadcast_to(swa_sink[0][:, None], (SWA_Q_HEADS, LANES))
    o_l = _swa_attn(p_l.reshape(bsz, SEQ, -1), p_c.reshape(bsz, CTX_LEN, -1), sink_rows)
    xl, _ = out_both(mods, o_l, None, p_l, p_c, D_INNER, swa_w_out[0], False, final=final_g)
    return xl.reshape(bsz, SEQ, D_MODEL)
```

```python
import functools
import math

import numpy as np
import jax
import jax.numpy as jnp
from jax import lax
from jax.experimental import pallas as pl
from jax.experimental.pallas import tpu as pltpu

F32 = jnp.float32
BF16 = jnp.bfloat16

D_MODEL = 1024
SEQ = 2048
CTX_LEN = 256
DEPTH = 4
D_INNER = 2 * D_MODEL
GRID_W = 64
RMS_EPS = 1e-6
ROPE_THETA = 10000.0

MLSTM_HEADS = 8
MLSTM_HEAD_DIM = D_INNER // MLSTM_HEADS
MLSTM_QKV_BLOCK = 4
MLSTM_CONV = 5
SCAN_CHUNK = 256

DIFF_HEADS = 16
HEAD_DIM = 64
NA_HEADS = 32
NA_WIN_ROWS = 8
NA_WIN_COLS = 16
SWA_Q_HEADS = 32
SWA_KV_HEADS = 4
SWA_WINDOW = 128

LANES = 128
MIB = 1024 * 1024
NEG_BIG = -1e30


def _cparams(semantics, vmem_mib):
    return pltpu.CompilerParams(dimension_semantics=semantics, vmem_limit_bytes=vmem_mib * MIB)


def _silu(v):
    return v * jax.nn.sigmoid(v)


def _adaln_kernel(c_ref, w_ref, b_ref, o_ref):
    s = _silu(c_ref[...])
    o_ref[0] = jnp.dot(s, w_ref[0], preferred_element_type=F32,
                       precision=lax.Precision.HIGHEST) + b_ref[0]


def _adaln(cvec, w_ada, b_ada):
    rows = cvec.shape[0]
    return pl.pallas_call(
        _adaln_kernel,
        grid=(DEPTH, 3),
        in_specs=[
            pl.BlockSpec((rows, D_MODEL), lambda i, j: (0, 0)),
            pl.BlockSpec((1, D_MODEL, D_MODEL), lambda i, j: (i, 0, j)),
            pl.BlockSpec((1, 1, D_MODEL), lambda i, j: (i, 0, j)),
        ],
        out_specs=pl.BlockSpec((1, rows, D_MODEL), lambda i, j: (i, 0, j)),
        out_shape=jax.ShapeDtypeStruct((DEPTH, rows, 3 * D_MODEL), F32),
        compiler_params=_cparams(("parallel", "parallel"), 32),
        name="adaln",
    )(cvec, w_ada, b_ada.reshape(DEPTH, 1, 3 * D_MODEL))


ROPE_Q = HEAD_DIM // 4


def _rope_tables():
    inv = np.power(ROPE_THETA, -np.arange(0, 2 * ROPE_Q, 2, dtype=np.float64) / (2 * ROPE_Q))
    t = np.arange(SEQ)
    ang = np.concatenate([(t // GRID_W).astype(np.float64)[:, None] * inv,
                          (t % GRID_W).astype(np.float64)[:, None] * inv], axis=1)
    cos = np.tile(np.cos(ang), (1, 4))
    sin = np.concatenate([-np.tile(np.sin(ang), (1, 2)), np.tile(np.sin(ang), (1, 2))], axis=1)
    return jnp.asarray(cos, F32), jnp.asarray(sin, F32)


def _pair_interleave(w, dup=False):
    d, n = w.shape
    h = n // HEAD_DIM
    w = w.reshape(d, h, 2, 2, ROPE_Q)
    if dup:
        w = jnp.broadcast_to(w[:, :, None], (d, h, 2, 2, 2, ROPE_Q))
    else:
        w = w.reshape(d, h // 2, 2, 2, 2, ROPE_Q)
    return jnp.transpose(w, (0, 1, 4, 2, 3, 5)).reshape(d, -1)


def _rope_chunk(x, cos, sin):
    return x * cos + pltpu.roll(x, LANES // 2, axis=1) * sin


def _inproj_kernel(*refs, tn, rope_plan):
    has_rope = bool(rope_plan)
    if has_rope:
        x_ref, g_ref, sh_ref, sc_ref, w_ref, cos_ref, sin_ref, o_ref, h_scr = refs
    else:
        x_ref, g_ref, sh_ref, sc_ref, w_ref, o_ref, h_scr = refs
    j = pl.program_id(1)

    @pl.when(j == 0)
    def _():
        x = x_ref[...]
        ms = jnp.mean(x * x, axis=-1, keepdims=True)
        y = x * lax.rsqrt(ms + RMS_EPS) * g_ref[...]
        h_scr[...] = (y * (1.0 + sc_ref[0]) + sh_ref[0]).astype(BF16)

    acc = jnp.dot(h_scr[...], w_ref[...], preferred_element_type=F32)

    def store_plain():
        o_ref[...] = acc.astype(BF16)

    def store_rope(flags):
        cos = cos_ref[...]
        sin = sin_ref[...]
        for c, roped in enumerate(flags):
            blk = acc[:, c * LANES:(c + 1) * LANES]
            if roped:
                blk = _rope_chunk(blk, cos, sin)
            o_ref[:, c * LANES:(c + 1) * LANES] = blk.astype(BF16)

    if not has_rope:
        store_plain()
    else:
        any_rope = None
        for flags, tiles in rope_plan:
            hit = functools.reduce(jnp.logical_or, [j == t for t in tiles])
            pl.when(hit)(functools.partial(store_rope, flags))
            any_rope = hit if any_rope is None else jnp.logical_or(any_rope, hit)
        pl.when(jnp.logical_not(any_rope))(store_plain)


def _inproj(x2d, norm_g, mods, w, *, rows_per_mod, mod_row0, tm, tn, rope_cols=(), rope_tabs=None):
    rows, _ = x2d.shape
    n = w.shape[1]
    assert rows % tm == 0 and n % tn == 0 and rows_per_mod % tm == 0
    tiles_per_mod = rows_per_mod // tm
    assert all(lo % LANES == 0 and hi % LANES == 0 for lo, hi in rope_cols)
    plan = {}
    for t in range(n // tn):
        flags = tuple(any(lo <= t * tn + c * LANES < hi for lo, hi in rope_cols) for c in range(tn // LANES))
        if any(flags):
            plan.setdefault(flags, []).append(t)
    rope_plan = tuple((flags, tuple(tiles)) for flags, tiles in plan.items())
    has_rope = bool(rope_plan)
    in_specs = [
        pl.BlockSpec((tm, D_MODEL), lambda i, j: (i, 0)),
        pl.BlockSpec((1, D_MODEL), lambda i, j: (0, 0)),
        pl.BlockSpec((1, 1, D_MODEL), lambda i, j: (mod_row0 + i // tiles_per_mod, 0, 0)),
        pl.BlockSpec((1, 1, D_MODEL), lambda i, j: (mod_row0 + i // tiles_per_mod, 0, 1)),
        pl.BlockSpec((D_MODEL, tn), lambda i, j: (0, j)),
    ]
    args = [x2d, norm_g.reshape(1, D_MODEL), mods, mods, w]
    if has_rope:
        assert SEQ % tm == 0
        seq_tiles = SEQ // tm
        in_specs += [pl.BlockSpec((tm, LANES), lambda i, j: (i % seq_tiles, 0))] * 2
        args += list(rope_tabs)
    vmem = (2 * tm * D_MODEL * 4 + tm * D_MODEL * 2 + 2 * D_MODEL * tn * 2 + 2 * tm * tn * 2
            + tm * tn * 4 + (4 * tm * LANES * 4 if has_rope else 0)) // MIB + 8
    return pl.pallas_call(
        functools.partial(_inproj_kernel, tn=tn, rope_plan=rope_plan),
        grid=(rows // tm, n // tn),
        in_specs=in_specs,
        out_specs=pl.BlockSpec((tm, tn), lambda i, j: (i, j)),
        out_shape=jax.ShapeDtypeStruct((rows, n), BF16),
        scratch_shapes=[pltpu.VMEM((tm, D_MODEL), BF16)],
        compiler_params=_cparams(("parallel", "arbitrary"), vmem),
        name="inproj",
    )(*args)


def _gate(o, z_blk):
    return o * _silu(z_blk.astype(F32))


def _outproj_kernel(o_ref, w_ref, x_ref, gt_ref, *rest, final):
    y = jnp.dot(o_ref[...], w_ref[...], preferred_element_type=F32)
    xn = x_ref[...] + gt_ref[0] * y
    if final:
        fg_ref, out_ref = rest
        ms = jnp.mean(xn * xn, axis=-1, keepdims=True)
        xn = xn * lax.rsqrt(ms + RMS_EPS) * fg_ref[...]
    else:
        (out_ref,) = rest
    out_ref[...] = xn


def _outproj(o2d, w_out, x2d, mods, *, rows_per_mod, mod_row0, tm, final_g=None, in_place=True):
    rows = x2d.shape[0]
    assert rows % tm == 0 and rows_per_mod % tm == 0
    tiles_per_mod = rows_per_mod // tm
    final = final_g is not None
    in_specs = [
        pl.BlockSpec((tm, D_INNER), lambda i: (i, 0)),
        pl.BlockSpec((D_INNER, D_MODEL), lambda i: (0, 0)),
        pl.BlockSpec((tm, D_MODEL), lambda i: (i, 0)),
        pl.BlockSpec((1, 1, D_MODEL), lambda i: (mod_row0 + i // tiles_per_mod, 0, 2)),
    ]
    args = [o2d, w_out, x2d, mods]
    if final:
        in_specs.append(pl.BlockSpec((1, D_MODEL), lambda i: (0, 0)))
        args.append(final_g.reshape(1, D_MODEL))
    vmem = (2 * tm * D_INNER * 2 + 2 * D_INNER * D_MODEL * 2 + 5 * tm * D_MODEL * 4) // MIB + 8
    return pl.pallas_call(
        functools.partial(_outproj_kernel, final=final),
        grid=(rows // tm,),
        in_specs=in_specs,
        out_specs=pl.BlockSpec((tm, D_MODEL), lambda i: (i, 0)),
        out_shape=jax.ShapeDtypeStruct((rows, D_MODEL), F32),
        input_output_aliases={2: 0} if in_place else {},
        compiler_params=_cparams(("parallel",), vmem),
        name="outproj",
    )(*args)


def _half_masks():
    lane = lax.broadcasted_iota(jnp.int32, (1, LANES), 1)
    return lane < HEAD_DIM, lane >= HEAD_DIM


def _nt(a, b):
    return lax.dot_general(a, b, (((1,), (1,)), ((), ())), preferred_element_type=F32)


LOG2E = math.log2(math.e)
QK_SCALE = HEAD_DIM ** -0.5 * LOG2E


def _split_heads(q, interleaved=False):
    lo, hi = _half_masks()
    if interleaved:
        lane = lax.broadcasted_iota(jnp.int32, (1, LANES), 1)
        lo = (lane % HEAD_DIM) < HEAD_DIM // 2
        hi = jnp.logical_not(lo)
    qs = q.astype(F32) * QK_SCALE
    return jnp.concatenate([jnp.where(lo, qs, 0.0), jnp.where(hi, qs, 0.0)], axis=0).astype(BF16)


def _ones_col(rows):
    lane = lax.broadcasted_iota(jnp.int32, (rows, LANES), 1)
    return jnp.where(lane == 0, 1.0, 0.0).astype(BF16)


def _diff_attn_kernel(q_ref, kl_ref, vl_ref, z_ref, kc_ref, vc_ref, lqk_ref, g_ref, o_ref, sa_scr, sb_scr,
                      *, lam_init, tq):
    nq = q_ref.shape[1] // tq
    lqk = lqk_ref[...]
    s1 = jnp.sum(lqk[0:1] * lqk[1:2], axis=-1, keepdims=True)
    s2 = jnp.sum(lqk[2:3] * lqk[3:4], axis=-1, keepdims=True)
    lam = jnp.exp(s1) - jnp.exp(s2) + lam_init
    bufs = (sa_scr, sb_scr)

    def logits(i):
        s_scr = bufs[i % 2]
        qq = _split_heads(q_ref[0, i * tq:(i + 1) * tq, :], interleaved=True)
        s_scr[:, :CTX_LEN] = _nt(qq, kc_ref[0])
        s_scr[:, CTX_LEN:] = _nt(qq, kl_ref[0])

    def finish(i):
        s = bufs[i % 2][...]
        m = jnp.max(s, axis=-1, keepdims=True)
        p = jnp.exp2(s - m)
        den = jnp.sum(p, axis=-1, keepdims=True)
        a = (p[:tq] - (lam * den[:tq] / den[tq:]) * p[tq:]).astype(BF16)
        o = (jnp.dot(a[:, :CTX_LEN], vc_ref[0], preferred_element_type=F32)
             + jnp.dot(a[:, CTX_LEN:], vl_ref[0], preferred_element_type=F32)) / den[:tq]
        ms = jnp.mean(o * o, axis=-1, keepdims=True)
        o = o * lax.rsqrt(ms + RMS_EPS) * g_ref[...] * (1.0 - lam_init)
        o_ref[0, i * tq:(i + 1) * tq, :] = _gate(o, z_ref[0, i * tq:(i + 1) * tq, :]).astype(BF16)

    logits(0)
    for i in range(nq):
        if i + 1 < nq:
            logits(i + 1)
        finish(i)


def _diff_attn(proj_lat, proj_ctx, lqk, subln_g, *, tq, lam_init):
    bsz = proj_lat.shape[0]
    kb, vb, zb = D_INNER // LANES, 2 * D_INNER // LANES, 3 * D_INNER // LANES
    nk = CTX_LEN + SEQ
    score_bytes = 2 * tq * nk * 4
    vmem = (5 * score_bytes) // MIB + 16
    return pl.pallas_call(
        functools.partial(_diff_attn_kernel, lam_init=lam_init, tq=tq),
        grid=(bsz, DIFF_HEADS),
        in_specs=[pl.BlockSpec((1, SEQ, LANES), lambda b, h: (b, 0, h)),
                  pl.BlockSpec((1, SEQ, LANES), lambda b, h: (b, 0, kb + h)),
                  pl.BlockSpec((1, SEQ, LANES), lambda b, h: (b, 0, vb + h)),
                  pl.BlockSpec((1, SEQ, LANES), lambda b, h: (b, 0, zb + h)),
                  pl.BlockSpec((1, CTX_LEN, LANES), lambda b, h: (b, 0, kb + h)),
                  pl.BlockSpec((1, CTX_LEN, LANES), lambda b, h: (b, 0, vb + h)),
                  pl.BlockSpec((4, HEAD_DIM), lambda b, h: (0, 0)),
                  pl.BlockSpec((1, LANES), lambda b, h: (0, 0))],
        out_specs=pl.BlockSpec((1, SEQ, LANES), lambda b, h: (b, 0, h)),
        out_shape=jax.ShapeDtypeStruct((bsz, SEQ, D_INNER), BF16),
        scratch_shapes=[pltpu.VMEM((2 * tq, nk), F32), pltpu.VMEM((2 * tq, nk), F32)],
        compiler_params=_cparams(("parallel", "parallel"), vmem),
        name="diff_attn",
    )(proj_lat, proj_lat, proj_lat, proj_lat, proj_ctx, proj_ctx, lqk, subln_g.reshape(1, LANES))


CTX_HB = 4


def _ctx_attn_kernel(*refs, diff, lam_init):
    if diff:
        q_ref, k_ref, v_ref, z_ref, lqk_ref, g_ref, o_ref = refs
        lqk = lqk_ref[...]
        lam = (jnp.exp(jnp.sum(lqk[0:1] * lqk[1:2], axis=-1, keepdims=True))
               - jnp.exp(jnp.sum(lqk[2:3] * lqk[3:4], axis=-1, keepdims=True)) + lam_init)
    else:
        q_ref, k_ref, v_ref, z_ref, o_ref = refs
    t = q_ref.shape[1]
    lo, _ = _half_masks()
    for hb in range(CTX_HB):
        sl = slice(hb * LANES, (hb + 1) * LANES)
        qq = _split_heads(q_ref[0, :, sl], interleaved=diff)
        s = _nt(qq, k_ref[0, :, sl])
        p = jnp.exp2(s - jnp.max(s, axis=-1, keepdims=True))
        den = jnp.sum(p, axis=-1, keepdims=True)
        if diff:
            a = (p[:t] - (lam * den[:t] / den[t:]) * p[t:]).astype(BF16)
            o = jnp.dot(a, v_ref[0, :, sl], preferred_element_type=F32) / den[:t]
            ms = jnp.mean(o * o, axis=-1, keepdims=True)
            o = o * lax.rsqrt(ms + RMS_EPS) * g_ref[...] * (1.0 - lam_init)
        else:
            o = jnp.dot(p.astype(BF16), v_ref[0, :, sl], preferred_element_type=F32) / den
            o = jnp.where(lo, o[:t], o[t:])
        o_ref[0, :, sl] = _gate(o, z_ref[0, :, sl]).astype(BF16)


def _ctx_attn(proj_ctx, lqk=None, subln_g=None, lam_init=0.0):
    bsz = proj_ctx.shape[0]
    diff = lqk is not None
    w = CTX_HB * LANES
    kb, vb, zb = D_INNER // w, 2 * D_INNER // w, 3 * D_INNER // w
    in_specs = [pl.BlockSpec((1, CTX_LEN, w), lambda b, h: (b, 0, h)),
                pl.BlockSpec((1, CTX_LEN, w), lambda b, h: (b, 0, kb + h)),
                pl.BlockSpec((1, CTX_LEN, w), lambda b, h: (b, 0, vb + h)),
                pl.BlockSpec((1, CTX_LEN, w), lambda b, h: (b, 0, zb + h))]
    args = [proj_ctx, proj_ctx, proj_ctx, proj_ctx]
    if diff:
        in_specs += [pl.BlockSpec((4, HEAD_DIM), lambda b, h: (0, 0)), pl.BlockSpec((1, LANES), lambda b, h: (0, 0))]
        args += [lqk, subln_g.reshape(1, LANES)]
    return pl.pallas_call(
        functools.partial(_ctx_attn_kernel, diff=diff, lam_init=lam_init),
        grid=(bsz, D_INNER // w),
        in_specs=in_specs,
        out_specs=pl.BlockSpec((1, CTX_LEN, w), lambda b, h: (b, 0, h)),
        out_shape=jax.ShapeDtypeStruct((bsz, CTX_LEN, D_INNER), BF16),
        compiler_params=_cparams(("parallel", "parallel"), 32),
        name="ctx_attn",
    )(*args)


NA_ROWS = SEQ // GRID_W
NA_BLK_ROWS = 4
NA_BLK_Q = NA_BLK_ROWS * GRID_W
NA_NBLK = NA_ROWS // NA_BLK_ROWS
NA_WIN_R = NA_BLK_ROWS + NA_WIN_ROWS
NA_KWIN = NA_WIN_R * GRID_W
NA_CTX_CHUNK = 512
NA_MAX_LANE = LANES + 1


def _na_win_start(i):
    lo, hi = 0, NA_ROWS - NA_WIN_R
    start = i * NA_BLK_ROWS - NA_WIN_ROWS // 2
    return np.clip(start, lo, hi) if isinstance(i, (int, np.integer)) else jnp.clip(start, lo, hi)


NA_NDR = 2 * NA_WIN_ROWS - 1


def _na_bias_plan():
    plan = np.full((3, NA_BLK_ROWS, NA_WIN_R), -1, np.int32)
    for pat, i in enumerate((0, 1, NA_NBLK - 1)):
        ws = int(_na_win_start(i))
        for qr in range(NA_BLK_ROWS):
            r = i * NA_BLK_ROWS + qr
            rs = min(max(r - NA_WIN_ROWS // 2, 0), NA_ROWS - NA_WIN_ROWS)
            for a in range(NA_WIN_R):
                if rs <= ws + a < rs + NA_WIN_ROWS:
                    plan[pat, qr, a] = ws + a - r + NA_WIN_ROWS - 1
    return plan


def _na_bias_tiles(rpb):
    c = np.arange(GRID_W)
    cs = np.clip(c - NA_WIN_COLS // 2, 0, GRID_W - NA_WIN_COLS)
    kc = np.arange(GRID_W)
    col_ok = (kc[None, :] >= cs[:, None]) & (kc[None, :] < cs[:, None] + NA_WIN_COLS)
    dc_idx = np.clip(kc[None, :] - c[:, None], -(NA_WIN_COLS - 1), NA_WIN_COLS - 1) + NA_WIN_COLS - 1
    return jnp.where(col_ok[None, None], rpb[:, :, dc_idx] * LOG2E, NEG_BIG).astype(F32)


def _na_attn_kernel(q_ref, k_ref, v_ref, z_ref, kc_ref, vc_ref, t1_ref, o_ref, oc_scr, bias_ref):
    lo, _ = _half_masks()

    @pl.when(pl.program_id(1) == 0)
    def _():
        plan = _na_bias_plan()
        masked = jnp.full((GRID_W, GRID_W), NEG_BIG, F32)
        for hl in range(2):
            for pat in range(3):
                for qr in range(NA_BLK_ROWS):
                    tiles = [t1_ref[hl, int(d)] if d >= 0 else masked for d in plan[pat, qr]]
                    bias_ref[hl, pat, qr * GRID_W:(qr + 1) * GRID_W, :] = jnp.concatenate(tiles, axis=1)
    kctx = kc_ref[0]
    vctx = jnp.concatenate([vc_ref[0], _ones_col(CTX_LEN)], axis=1)
    lane2 = lax.broadcasted_iota(jnp.int32, (1, 2 * LANES), 1)

    def ctx_chunk(j, carry):
        r0 = pl.multiple_of(jnp.asarray(j, jnp.int32) * NA_CTX_CHUNK, NA_CTX_CHUNK)
        qq = _split_heads(q_ref[0, pl.ds(r0, NA_CTX_CHUNK), :])
        s = _nt(qq, kctx)
        m = jnp.max(s, axis=-1, keepdims=True)
        oc = jnp.dot(jnp.exp2(s - m).astype(BF16), vctx, preferred_element_type=F32)
        oc = jnp.where(lane2 == NA_MAX_LANE, m, oc)
        oc_scr[0, pl.ds(r0, NA_CTX_CHUNK), :] = oc[:NA_CTX_CHUNK]
        oc_scr[1, pl.ds(r0, NA_CTX_CHUNK), :] = oc[NA_CTX_CHUNK:]
        return carry

    lax.fori_loop(0, SEQ // NA_CTX_CHUNK, ctx_chunk, 0, unroll=True)

    def block(i, carry):
        i = jnp.asarray(i, jnp.int32)
        pat = jnp.where(i == 0, 0, jnp.where(i == NA_NBLK - 1, 2, 1))
        t0 = pl.multiple_of(i * NA_BLK_Q, NA_BLK_Q)
        s0 = pl.multiple_of(_na_win_start(i) * GRID_W, NA_BLK_Q)
        qq = _split_heads(q_ref[0, pl.ds(t0, NA_BLK_Q), :])
        kw = k_ref[0, pl.ds(s0, NA_KWIN), :]
        vw = jnp.concatenate([v_ref[0, pl.ds(s0, NA_KWIN), :], _ones_col(NA_KWIN)], axis=1)
        s = _nt(qq, kw) + jnp.concatenate([bias_ref[0, pat], bias_ref[1, pat]], axis=0)
        oc = jnp.concatenate([oc_scr[0, pl.ds(t0, NA_BLK_Q), :], oc_scr[1, pl.ds(t0, NA_BLK_Q), :]], axis=0)
        mc = oc[:, NA_MAX_LANE:NA_MAX_LANE + 1]
        m = jnp.maximum(jnp.max(s, axis=-1, keepdims=True), mc)
        tot = (jnp.dot(jnp.exp2(s - m).astype(BF16), vw, preferred_element_type=F32)
               + jnp.exp2(mc - m) * oc)
        out = tot[:, :LANES] / tot[:, LANES:LANES + 1]
        o = jnp.where(lo, out[:NA_BLK_Q], out[NA_BLK_Q:])
        o_ref[0, pl.ds(t0, NA_BLK_Q), :] = _gate(o, z_ref[0, pl.ds(t0, NA_BLK_Q), :]).astype(BF16)
        return carry

    lax.fori_loop(0, NA_NBLK, block, 0, unroll=True)


def _na_attn(proj_lat, proj_ctx, bias_tiles):
    bsz = proj_lat.shape[0]
    pairs = NA_HEADS // 2
    kb, vb, zb = D_INNER // LANES, 2 * D_INNER // LANES, 3 * D_INNER // LANES
    return pl.pallas_call(
        _na_attn_kernel,
        grid=(pairs, bsz),
        in_specs=[pl.BlockSpec((1, SEQ, LANES), lambda h, b: (b, 0, h)),
                  pl.BlockSpec((1, SEQ, LANES), lambda h, b: (b, 0, kb + h)),
                  pl.BlockSpec((1, SEQ, LANES), lambda h, b: (b, 0, vb + h)),
                  pl.BlockSpec((1, SEQ, LANES), lambda h, b: (b, 0, zb + h)),
                  pl.BlockSpec((1, CTX_LEN, LANES), lambda h, b: (b, 0, kb + h)),
                  pl.BlockSpec((1, CTX_LEN, LANES), lambda h, b: (b, 0, vb + h)),
                  pl.BlockSpec((2, NA_NDR, GRID_W, GRID_W), lambda h, b: (h, 0, 0, 0))],
        out_specs=pl.BlockSpec((1, SEQ, LANES), lambda h, b: (b, 0, h)),
        out_shape=jax.ShapeDtypeStruct((bsz, SEQ, D_INNER), BF16),
        scratch_shapes=[pltpu.VMEM((2, SEQ, 2 * LANES), F32), pltpu.VMEM((2, 3, NA_BLK_Q, NA_KWIN), F32)],
        compiler_params=_cparams(("parallel", "arbitrary"), 48),
        name="na_attn",
    )(proj_lat, proj_lat, proj_lat, proj_lat, proj_ctx, proj_ctx, bias_tiles)


SWA_TQ = 256
SWA_BAND = SWA_TQ + 2 * SWA_WINDOW
SWA_K_COL = 2 * D_INNER
SWA_V_COL = SWA_K_COL + SWA_KV_HEADS * LANES
SWA_N = SWA_V_COL + SWA_KV_HEADS * LANES


SWA_GROUP = SWA_Q_HEADS // SWA_KV_HEADS
SWA_GW = SWA_GROUP * HEAD_DIM


def _swa_attn_kernel(q_ref, k_ref, v_ref, z_ref, kc_ref, vc_ref, sink_ref, o_ref):
    lo, _ = _half_masks()
    kvh = pl.program_id(1)
    n_pair = SWA_GROUP // 2
    kctx = kc_ref[0]
    vctx = jnp.concatenate([vc_ref[0], _ones_col(CTX_LEN)], axis=1)
    sinks = sink_ref[pl.ds(pl.multiple_of(kvh * SWA_GROUP, SWA_GROUP), SWA_GROUP), :][:, 0:1] * LOG2E
    sink = jnp.concatenate([jnp.broadcast_to(sinks[g:g + 1], (SWA_TQ, 1)) for g in range(SWA_GROUP)], axis=0)

    def block(i, carry):
        i = jnp.asarray(i, jnp.int32)
        t0 = pl.multiple_of(i * SWA_TQ, SWA_TQ)
        s0 = pl.multiple_of(jnp.clip(i * SWA_TQ - SWA_WINDOW, 0, SEQ - SWA_BAND), SWA_WINDOW)
        q = q_ref[0, pl.ds(t0, SWA_TQ), :]
        qq = jnp.concatenate([_split_heads(q[:, p * LANES:(p + 1) * LANES], interleaved=True)
                              for p in range(n_pair)], axis=0)
        kw = k_ref[0, pl.ds(s0, SWA_BAND), :]
        vw = jnp.concatenate([v_ref[0, pl.ds(s0, SWA_BAND), :], _ones_col(SWA_BAND)], axis=1)
        tpos = t0 + lax.broadcasted_iota(jnp.int32, (SWA_TQ, SWA_BAND), 0)
        spos = s0 + lax.broadcasted_iota(jnp.int32, (SWA_TQ, SWA_BAND), 1)
        bias = jnp.where(jnp.abs(tpos - spos) <= SWA_WINDOW, 0.0, NEG_BIG)
        s_lat = _nt(qq, kw) + jnp.concatenate([bias] * SWA_GROUP, axis=0)
        s_ctx = _nt(qq, kctx)
        m = jnp.maximum(jnp.maximum(jnp.max(s_lat, axis=-1, keepdims=True),
                                    jnp.max(s_ctx, axis=-1, keepdims=True)), sink)
        tot = (jnp.dot(jnp.exp2(s_lat - m).astype(BF16), vw, preferred_element_type=F32)
               + jnp.dot(jnp.exp2(s_ctx - m).astype(BF16), vctx, preferred_element_type=F32))
        out = tot[:, :LANES] / (tot[:, LANES:LANES + 1] + jnp.exp2(sink - m))
        pairs = [jnp.where(lo, out[(2 * p) * SWA_TQ:(2 * p + 1) * SWA_TQ],
                           out[(2 * p + 1) * SWA_TQ:(2 * p + 2) * SWA_TQ]) for p in range(n_pair)]
        o = jnp.concatenate(pairs, axis=1)
        o_ref[0, pl.ds(t0, SWA_TQ), :] = _gate(o, z_ref[0, pl.ds(t0, SWA_TQ), :]).astype(BF16)
        return carry

    lax.fori_loop(0, SEQ // SWA_TQ, block, 0, unroll=True)


def _swa_attn(proj_lat, proj_ctx, sink_rows):
    bsz = proj_lat.shape[0]
    kb, vb, zb = SWA_K_COL // LANES, SWA_V_COL // LANES, D_INNER // SWA_GW
    return pl.pallas_call(
        _swa_attn_kernel,
        grid=(bsz, SWA_KV_HEADS),
        in_specs=[pl.BlockSpec((1, SEQ, SWA_GW), lambda b, h: (b, 0, h)),
                  pl.BlockSpec((1, SEQ, LANES), lambda b, h: (b, 0, kb + h)),
                  pl.BlockSpec((1, SEQ, LANES), lambda b, h: (b, 0, vb + h)),
                  pl.BlockSpec((1, SEQ, SWA_GW), lambda b, h: (b, 0, zb + h)),
                  pl.BlockSpec((1, CTX_LEN, LANES), lambda b, h: (b, 0, kb + h)),
                  pl.BlockSpec((1, CTX_LEN, LANES), lambda b, h: (b, 0, vb + h)),
                  pl.BlockSpec((SWA_Q_HEADS, LANES), lambda b, h: (0, 0))],
        out_specs=pl.BlockSpec((1, SEQ, SWA_GW), lambda b, h: (b, 0, h)),
        out_shape=jax.ShapeDtypeStruct((bsz, SEQ, D_INNER), BF16),
        compiler_params=_cparams(("parallel", "parallel"), 48),
        name="swa_attn",
    )(proj_lat, proj_lat, proj_lat, proj_lat, proj_ctx, proj_ctx, sink_rows)


GATE_LANES = 128


def _mlstm_prep_kernel(xm_ref, cw_ref, cb_ref, bdq_ref, bdk_ref, bdv_ref, wg_ref, bg_ref,
                       q_ref, kt_ref, v_ref, xc_ref, g_ref):
    h = pl.program_id(1)
    t = xm_ref.shape[1]
    xm_b = xm_ref[0]
    xm = xm_b.astype(F32)
    tpos = lax.broadcasted_iota(jnp.int32, (t, 1), 0)
    conv = jnp.zeros_like(xm) + cb_ref[...]
    for j in range(MLSTM_CONV):
        d = j - MLSTM_CONV // 2
        sh = pltpu.roll(xm, (-d) % t, axis=0) if d else xm
        ok = (tpos + d >= 0) & (tpos + d < t)
        conv = conv + jnp.where(ok, sh, 0.0) * cw_ref[j:j + 1, :]
    xc = _silu(conv)
    xc_b = xc.astype(BF16)
    q = jnp.dot(xc_b, bdq_ref[0], preferred_element_type=F32)
    k = jnp.dot(xc_b, bdk_ref[0], preferred_element_type=F32)
    v = jnp.dot(xm_b, bdv_ref[0], preferred_element_type=F32)
    q_b, k_b, v_b = q.astype(BF16), k.astype(BF16), v.astype(BF16)
    q_ref[0] = q_b
    v_ref[0] = v_b
    xc_ref[0] = xc_b
    ks = k * (MLSTM_HEAD_DIM ** -0.5)
    for c in range(t // SCAN_CHUNK):
        kt_ref[0, 0, c] = ks[c * SCAN_CHUNK:(c + 1) * SCAN_CHUNK, :].T.astype(BF16)
    part = (jnp.dot(q_b, wg_ref[0, 0], preferred_element_type=F32)
            + jnp.dot(k_b, wg_ref[1, 0], preferred_element_type=F32)
            + jnp.dot(v_b, wg_ref[2, 0], preferred_element_type=F32))

    @pl.when(h == 0)
    def _():
        g_ref[0] = part + bg_ref[...]

    @pl.when(h != 0)
    def _():
        g_ref[0] = g_ref[0] + part


def _mlstm_prep(proj3, conv_w, conv_b, bdq, bdk, bdv, wg, bg):
    bsz, t, _ = proj3.shape
    hd = MLSTM_HEAD_DIM
    nc = t // SCAN_CHUNK
    seq_spec = pl.BlockSpec((1, t, hd), lambda b, h: (b, 0, h))
    return pl.pallas_call(
        _mlstm_prep_kernel,
        grid=(bsz, MLSTM_HEADS),
        in_specs=[seq_spec,
                  pl.BlockSpec((MLSTM_CONV, hd), lambda b, h: (0, h)),
                  pl.BlockSpec((1, hd), lambda b, h: (0, h)),
                  pl.BlockSpec((1, hd, hd), lambda b, h: (h, 0, 0)),
                  pl.BlockSpec((1, hd, hd), lambda b, h: (h, 0, 0)),
                  pl.BlockSpec((1, hd, hd), lambda b, h: (h, 0, 0)),
                  pl.BlockSpec((3, 1, hd, GATE_LANES), lambda b, h: (0, h, 0, 0)),
                  pl.BlockSpec((1, GATE_LANES), lambda b, h: (0, 0))],
        out_specs=[seq_spec,
                   pl.BlockSpec((1, 1, nc, hd, SCAN_CHUNK), lambda b, h: (b, h, 0, 0, 0)),
                   seq_spec, seq_spec,
                   pl.BlockSpec((1, t, GATE_LANES), lambda b, h: (b, 0, 0))],
        out_shape=[jax.ShapeDtypeStruct((bsz, t, D_INNER), BF16),
                   jax.ShapeDtypeStruct((bsz, MLSTM_HEADS, nc, hd, SCAN_CHUNK), BF16),
                   jax.ShapeDtypeStruct((bsz, t, D_INNER), BF16),
                   jax.ShapeDtypeStruct((bsz, t, D_INNER), BF16),
                   jax.ShapeDtypeStruct((bsz, t, GATE_LANES), F32)],
        compiler_params=_cparams(("parallel", "arbitrary"), 48),
        name="mlstm_prep",
    )(proj3, conv_w, conv_b.reshape(1, D_INNER), bdq, bdk, bdv, wg, bg)


def _lane_scan(x, op, fill, reverse):
    lane = lax.broadcasted_iota(jnp.int32, x.shape, 1)
    sh = 1
    while sh < LANES:
        if reverse:
            y = jnp.where(lane < LANES - sh, pltpu.roll(x, LANES - sh, axis=1), fill)
        else:
            y = jnp.where(lane >= sh, pltpu.roll(x, sh, axis=1), fill)
        x = op(x, y)
        sh *= 2
    return x


GP_ROWS = 8


def _mlstm_gates_kernel(gc_ref, gl_ref, oc_ref, ol_ref):
    nh = MLSTM_HEADS
    ppc = SCAN_CHUNK // LANES
    gct = gc_ref[0].T
    glt = gl_ref[0].T
    n_c = gc_ref.shape[1] // SCAN_CHUNK
    n_l = gl_ref.shape[1] // SCAN_CHUNK
    zeros = jnp.zeros((nh, SCAN_CHUNK), F32)
    pieces = [(gct, j) for j in range(n_c * ppc)] + [(glt, j) for j in range(n_l * ppc)]
    chunks = [(oc_ref, c) for c in range(n_c)] + [(ol_ref, c) for c in range(n_l)]
    n_all = len(chunks)

    def stack(row0):
        return jnp.concatenate([src[row0:row0 + nh, j * LANES:(j + 1) * LANES] for src, j in pieces], axis=0)

    def piece(x, k):
        return x[k * nh:(k + 1) * nh]

    def earlier(k, rev):
        c = k // ppc
        return range(k + 1, (c + 1) * ppc) if rev else range(c * ppc, k)

    for d in range(2):
        rev = d == 1
        edge = (lambda x: x[:, 0:1]) if rev else (lambda x: x[:, LANES - 1:LANES])
        ig = stack(2 * d * nh)
        fg = stack((2 * d + 1) * nh)
        lf = jnp.minimum(fg, 0.0) - jnp.log1p(jnp.exp(-jnp.abs(fg)))
        bp = _lane_scan(lf, jnp.add, 0.0, rev)
        b = jnp.concatenate([piece(bp, k) + sum(piece(edge(bp), o) for o in earlier(k, rev))
                             for k in range(len(pieces))], axis=0)
        cmp_ = _lane_scan(ig - b, jnp.maximum, -jnp.inf, rev)
        cm = jnp.concatenate([functools.reduce(jnp.maximum, [piece(edge(cmp_), o) for o in earlier(k, rev)],
                                               piece(cmp_, k)) for k in range(len(pieces))], axis=0)
        last = (lambda c: c * ppc) if rev else (lambda c: c * ppc + ppc - 1)
        b_end_c = [piece(edge(b), last(c)) for c in range(n_all)]
        b_end = jnp.concatenate([b_end_c[k // ppc] for k in range(len(pieces))], axis=0)
        g = b_end - b + ig
        gmax_p = jnp.max(g, axis=-1, keepdims=True)
        gmax_c = [functools.reduce(jnp.maximum, [piece(gmax_p, c * ppc + p) for p in range(ppc)])
                  for c in range(n_all)]
        order = list(range(n_all))
        if rev:
            order = list(reversed(range(n_c))) + list(reversed(range(n_c, n_all)))
        m = jnp.zeros((nh, 1), F32)
        m_in, m_out = [None] * n_all, [None] * n_all
        for c in order:
            m_in[c] = m
            m = jnp.maximum(b_end_c[c] + m, gmax_c[c])
            m_out[c] = m
        m_in = jnp.concatenate([m_in[k // ppc] for k in range(len(pieces))], axis=0)
        m_out = jnp.concatenate([m_out[k // ppc] for k in range(len(pieces))], axis=0)
        mx = jnp.maximum(cm, m_in)
        rows = (-mx, b - ig, jnp.exp(m_in - mx), jnp.exp(-(b + mx)), jnp.exp(g - m_out),
                jnp.broadcast_to(jnp.exp(b_end + m_in - m_out), b.shape))
        for k, (dst, c) in enumerate(chunks):
            for r, val in enumerate(rows):
                dst[0, d, c, r] = jnp.concatenate([piece(val, k * ppc + p) for p in range(ppc)], axis=1)
            dst[0, d, c, 6] = zeros
            dst[0, d, c, 7] = zeros


def _mlstm_gates(gates_ctx, gates_lat):
    bsz = gates_lat.shape[0]
    n_c, n_l = CTX_LEN // SCAN_CHUNK, SEQ // SCAN_CHUNK

    def out(n):
        shape = (bsz, 2, n, GP_ROWS, MLSTM_HEADS, SCAN_CHUNK)
        return (pl.BlockSpec((1,) + shape[1:], lambda b: (b, 0, 0, 0, 0, 0)), jax.ShapeDtypeStruct(shape, F32))

    (spec_c, shape_c), (spec_l, shape_l) = out(n_c), out(n_l)
    return pl.pallas_call(
        _mlstm_gates_kernel,
        grid=(bsz,),
        in_specs=[pl.BlockSpec((1, CTX_LEN, GATE_LANES), lambda b: (b, 0, 0)),
                  pl.BlockSpec((1, SEQ, GATE_LANES), lambda b: (b, 0, 0))],
        out_specs=[spec_c, spec_l],
        out_shape=[shape_c, shape_l],
        compiler_params=_cparams(("parallel",), 32),
        name="mlstm_gates",
    )(gates_ctx, gates_lat)


def _mlstm_scan_kernel(qc_ref, ktc_ref, vc_ref, xcc_ref, gpc_ref,
                       ql_ref, ktl_ref, vl_ref, xcl_ref, gpl_ref, ng_ref, sk_ref, zc_ref, zl_ref,
                       oc_ref, ol_ref, hc_scr, hl_scr, st_scr):
    L = SCAN_CHUNK
    hd = MLSTM_HEAD_DIM
    n_c = qc_ref.shape[1] // L
    n_l = ql_ref.shape[1] // L
    lane = lax.broadcasted_iota(jnp.int32, (L, LANES), 1)
    ones_col = jnp.where(lane == 0, 1.0, 0.0).astype(BF16)
    tt = lax.broadcasted_iota(jnp.int32, (L, L), 0)
    ss = lax.broadcasted_iota(jnp.int32, (L, L), 1)
    masks = (ss <= tt, ss >= tt)

    hc_scr[...] = jnp.zeros_like(hc_scr)
    hl_scr[...] = jnp.zeros_like(hl_scr)
    st_scr[...] = jnp.zeros_like(st_scr)

    def chunk(d, q_ref, kt_ref, v_ref, gp_ref, h_scr, c):
        rows = gp_ref[0, d, 0, c]
        cols = rows.T
        a_col, wst, einv = cols[:, 0:1], cols[:, 2:3], cols[:, 3:4]
        c_row, ws_row, decay = rows[1:2, :], rows[4:5, :], rows[5:6, 0:1]
        r0 = pl.multiple_of(c * L, L)
        q = q_ref[0, pl.ds(r0, L), :]
        kt = kt_ref[0, 0, c]
        vx = jnp.concatenate([v_ref[0, pl.ds(r0, L), :], ones_col], axis=1)
        w = jnp.exp(jnp.where(masks[d], a_col - c_row, -jnp.inf))
        s = (jnp.dot(q, kt, preferred_element_type=F32) * w).astype(BF16)
        kw = (kt.astype(F32) * ws_row).astype(BF16)
        both = jnp.dot(jnp.concatenate([s, kw], axis=0), vx, preferred_element_type=F32)
        state = st_scr[d]
        tot = wst * jnp.dot(q, state.astype(BF16), preferred_element_type=F32) + both[:L]
        den = tot[:, hd:hd + 1]
        h = tot[:, :hd] / jnp.maximum(jnp.abs(den), einv)
        h_scr[pl.ds(r0, L), :] = h_scr[pl.ds(r0, L), :] + h
        st_scr[d] = decay * state + both[L:]

    for j in range(n_c):
        chunk(0, qc_ref, ktc_ref, vc_ref, gpc_ref, hc_scr, j)
        chunk(1, qc_ref, ktc_ref, vc_ref, gpc_ref, hc_scr, n_c - 1 - j)

    def lat_step(j, carry):
        chunk(0, ql_ref, ktl_ref, vl_ref, gpl_ref, hl_scr, j)
        chunk(1, ql_ref, ktl_ref, vl_ref, gpl_ref, hl_scr, n_l - 1 - j)
        return carry

    lax.fori_loop(0, n_l, lat_step, 0, unroll=True)

    ng = ng_ref[...]
    sk = sk_ref[...]

    def finish(h_scr, xc_ref, z_ref, o_ref, r0, n):
        h = h_scr[pl.ds(r0, n), :]
        mu = jnp.mean(h, axis=-1, keepdims=True)
        hcen = h - mu
        hn = hcen * lax.rsqrt(jnp.mean(hcen * hcen, axis=-1, keepdims=True) + RMS_EPS)
        o = hn * ng + sk * xc_ref[0, pl.ds(r0, n), :].astype(F32)
        o_ref[0, pl.ds(r0, n), :] = _gate(o, z_ref[0, pl.ds(r0, n), :]).astype(BF16)

    finish(hc_scr, xcc_ref, zc_ref, oc_ref, 0, CTX_LEN)

    def fin_step(i, carry):
        finish(hl_scr, xcl_ref, zl_ref, ol_ref, pl.multiple_of(jnp.asarray(i, jnp.int32) * 256, 256), 256)
        return carry

    lax.fori_loop(0, SEQ // 256, fin_step, 0, unroll=2)


def _mlstm_scan(prep_c, gp_c, prep_l, gp_l, norm_g, skip, proj_c, proj_l):
    bsz = prep_l[0].shape[0]
    hd = MLSTM_HEAD_DIM

    def z_spec(t):
        return pl.BlockSpec((1, t, hd), lambda b, h: (b, 0, MLSTM_HEADS + h))

    def seq_specs(t):
        nc = t // SCAN_CHUNK
        seq = pl.BlockSpec((1, t, hd), lambda b, h: (b, 0, h))
        return [seq,
                pl.BlockSpec((1, 1, nc, hd, SCAN_CHUNK), lambda b, h: (b, h, 0, 0, 0)),
                seq, seq,
                pl.BlockSpec((1, 2, 1, nc, GP_ROWS, SCAN_CHUNK), lambda b, h: (b, 0, h, 0, 0, 0))]

    vec = pl.BlockSpec((1, hd), lambda b, h: (0, h))
    return pl.pallas_call(
        _mlstm_scan_kernel,
        grid=(bsz, MLSTM_HEADS),
        in_specs=seq_specs(CTX_LEN) + seq_specs(SEQ) + [vec, vec, z_spec(CTX_LEN), z_spec(SEQ)],
        out_specs=[pl.BlockSpec((1, CTX_LEN, hd), lambda b, h: (b, 0, h)),
                   pl.BlockSpec((1, SEQ, hd), lambda b, h: (b, 0, h))],
        out_shape=[jax.ShapeDtypeStruct((bsz, CTX_LEN, D_INNER), BF16),
                   jax.ShapeDtypeStruct((bsz, SEQ, D_INNER), BF16)],
        scratch_shapes=[pltpu.VMEM((CTX_LEN, hd), F32), pltpu.VMEM((SEQ, hd), F32),
                        pltpu.VMEM((2, hd, hd + LANES), F32)],
        compiler_params=_cparams(("parallel", "parallel"), 40),
        name="mlstm_scan",
    )(*prep_c, gp_c, *prep_l, gp_l, norm_g.reshape(1, D_INNER), skip.reshape(1, D_INNER), proj_c, proj_l)


def _blockdiag_dense(w):
    per_head = MLSTM_HEAD_DIM // MLSTM_QKV_BLOCK
    wh = w.reshape(MLSTM_HEADS, per_head, MLSTM_QKV_BLOCK, MLSTM_QKV_BLOCK)
    eye = jnp.eye(per_head, dtype=w.dtype)
    dense = wh[:, :, :, None, :] * eye[None, :, None, :, None]
    return dense.reshape(MLSTM_HEADS, MLSTM_HEAD_DIM, MLSTM_HEAD_DIM).astype(BF16)


def _mlstm_mixer(proj_lat, proj_ctx, conv_w, conv_b, wq, wk, wv, w_gate, b_gate, norm_g, skip):
    n_gate = 4 * MLSTM_HEADS
    wg = jnp.pad(w_gate, ((0, 0), (0, GATE_LANES - n_gate))).astype(BF16)
    wg = wg.reshape(3, MLSTM_HEADS, MLSTM_HEAD_DIM, GATE_LANES)
    bg = jnp.pad(b_gate, (0, GATE_LANES - n_gate)).reshape(1, GATE_LANES)
    bds = (_blockdiag_dense(wq), _blockdiag_dense(wk), _blockdiag_dense(wv))
    *prep_c, gates_c = _mlstm_prep(proj_ctx, conv_w, conv_b, *bds, wg, bg)
    *prep_l, gates_l = _mlstm_prep(proj_lat, conv_w, conv_b, *bds, wg, bg)
    gp_c, gp_l = _mlstm_gates(gates_c, gates_l)
    gp_c = jnp.transpose(gp_c, (0, 1, 4, 2, 3, 5))
    gp_l = jnp.transpose(gp_l, (0, 1, 4, 2, 3, 5))
    return _mlstm_scan(prep_c, gp_c, prep_l, gp_l, norm_g, skip, proj_ctx, proj_lat)


def kernel(x, c, ctx, c_ctx, norm_g, w_ada, b_ada, final_g, mlstm_w_in, mlstm_conv_w, mlstm_conv_b, mlstm_wq, mlstm_wk, mlstm_wv, mlstm_w_gate, mlstm_b_gate, mlstm_norm_g, mlstm_skip, mlstm_w_out, diff_w_in, diff_lq1, diff_lk1, diff_lq2, diff_lk2, diff_subln_g, diff_w_out, na_w_in, na_rpb, na_w_out, swa_w_in, swa_sink, swa_w_out):
    bsz = x.shape[0]
    ctx_row = bsz
    mod_rows = -(-(bsz + 1) // 8) * 8
    cvec = jnp.zeros((mod_rows, D_MODEL), F32).at[:bsz].set(c).at[ctx_row].set(c_ctx)
    mods_all = _adaln(cvec, w_ada, b_ada)
    rope_tabs = _rope_tables()

    xl = x.reshape(bsz * SEQ, D_MODEL)
    xc = ctx.reshape(bsz * CTX_LEN, D_MODEL)
    lat = dict(rows_per_mod=SEQ, mod_row0=0)
    con = dict(rows_per_mod=bsz * CTX_LEN, mod_row0=ctx_row)
    tm_ctx = min(1024, bsz * CTX_LEN)

    def in_both(i, w, tn, rope_cols=()):
        mods = mods_all[i].reshape(mod_rows, 1, 3 * D_MODEL)
        wb = w.astype(BF16)
        pl_ = _inproj(xl, norm_g[i], mods, wb, tm=SEQ, tn=tn, rope_cols=rope_cols, rope_tabs=rope_tabs, **lat)
        pc_ = _inproj(xc, norm_g[i], mods, wb, tm=tm_ctx, tn=tn, **con)
        return mods, pl_, pc_

    def out_both(mods, o_lat, o_ctx, w_out, need_ctx, final=None, in_place=True):
        wb = w_out.astype(BF16)
        new_l = _outproj(o_lat.reshape(bsz * SEQ, D_INNER), wb, xl, mods, tm=1024,
                         final_g=final, in_place=in_place, **lat)
        new_c = xc
        if need_ctx:
            new_c = _outproj(o_ctx.reshape(bsz * CTX_LEN, D_INNER), wb, xc, mods, tm=tm_ctx, **con)
        return new_l, new_c

    mods, p_l, p_c = in_both(0, mlstm_w_in[0], 1024)
    o_c, o_l = _mlstm_mixer(p_l.reshape(bsz, SEQ, -1), p_c.reshape(bsz, CTX_LEN, -1), mlstm_conv_w[0],
                            mlstm_conv_b[0], mlstm_wq[0], mlstm_wk[0], mlstm_wv[0], mlstm_w_gate[0],
                            mlstm_b_gate[0], mlstm_norm_g[0], mlstm_skip[0])
    xl, xc = out_both(mods, o_l, o_c, mlstm_w_out[0], True, in_place=False)

    wd = diff_w_in[0]
    wd = jnp.concatenate([_pair_interleave(wd[:, :D_INNER]), _pair_interleave(wd[:, D_INNER:2 * D_INNER]),
                          wd[:, 2 * D_INNER:]], axis=1)
    mods, p_l, p_c = in_both(1, wd, 1024, rope_cols=((0, 2 * D_INNER),))
    p_l3, p_c3 = p_l.reshape(bsz, SEQ, -1), p_c.reshape(bsz, CTX_LEN, -1)
    lam_init = 0.8 - 0.6 * math.exp(-0.3 * 1)
    lqk = jnp.stack([diff_lq1[0], diff_lk1[0], diff_lq2[0], diff_lk2[0]])
    o_l = _diff_attn(p_l3, p_c3, lqk, diff_subln_g[0], tq=512, lam_init=lam_init)
    o_c = _ctx_attn(p_c3, lqk, diff_subln_g[0], lam_init)
    xl, xc = out_both(mods, o_l, o_c, diff_w_out[0], True)

    mods, p_l, p_c = in_both(2, na_w_in[0], 1024)
    p_l3, p_c3 = p_l.reshape(bsz, SEQ, -1), p_c.reshape(bsz, CTX_LEN, -1)
    o_l = _na_attn(p_l3, p_c3, _na_bias_tiles(na_rpb[0]))
    o_c = _ctx_attn(p_c3)
    xl, xc = out_both(mods, o_l, o_c, na_w_out[0], True)

    kvw = SWA_KV_HEADS * HEAD_DIM
    w = swa_w_in[0]
    wq, wk, wv, wz = (w[:, :D_INNER], w[:, D_INNER:D_INNER + kvw], w[:, D_INNER + kvw:D_INNER + 2 * kvw],
                      w[:, D_INNER + 2 * kvw:])

    def twice(m):
        m = m.reshape(D_MODEL, SWA_KV_HEADS, 1, HEAD_DIM)
        return jnp.broadcast_to(m, (D_MODEL, SWA_KV_HEADS, 2, HEAD_DIM)).reshape(D_MODEL, SWA_KV_HEADS * LANES)

    w3 = jnp.concatenate([_pair_interleave(wq), wz, _pair_interleave(wk, dup=True), twice(wv)], axis=1)
    mods, p_l, p_c = in_both(3, w3, 1024, rope_cols=((0, D_INNER), (SWA_K_COL, SWA_V_COL)))
    sink_rows = jnp.broadcast_to(swa_sink[0][:, None], (SWA_Q_HEADS, LANES))
    o_l = _swa_attn(p_l.reshape(bsz, SEQ, -1), p_c.reshape(bsz, CTX_LEN, -1), sink_rows)
    xl, _ = out_both(mods, o_l, None, swa_w_out[0], False, final=final_g)
    return xl.reshape(bsz, SEQ, D_MODEL)
```

```python
import functools
import math

import numpy as np
import jax
import jax.numpy as jnp
from jax import lax
from jax.experimental import pallas as pl
from jax.experimental.pallas import tpu as pltpu

F32 = jnp.float32
BF16 = jnp.bfloat16

D_MODEL = 1024
SEQ = 2048
CTX_LEN = 256
DEPTH = 4
D_INNER = 2 * D_MODEL
GRID_W = 64
RMS_EPS = 1e-6
ROPE_THETA = 10000.0

MLSTM_HEADS = 8
MLSTM_HEAD_DIM = D_INNER // MLSTM_HEADS
MLSTM_QKV_BLOCK = 4
MLSTM_CONV = 5
SCAN_CHUNK = 256

DIFF_HEADS = 16
HEAD_DIM = 64
NA_HEADS = 32
NA_WIN_ROWS = 8
NA_WIN_COLS = 16
SWA_Q_HEADS = 32
SWA_KV_HEADS = 4
SWA_WINDOW = 128

LANES = 128
MIB = 1024 * 1024
NEG_BIG = -1e30


def _cparams(semantics, vmem_mib):
    return pltpu.CompilerParams(dimension_semantics=semantics, vmem_limit_bytes=vmem_mib * MIB)


def _silu(v):
    h = 0.5 * v
    return h + h * jnp.tanh(h)


def _adaln_kernel(c_ref, w_ref, b_ref, o_ref):
    s = _silu(c_ref[...])
    o_ref[0] = jnp.dot(s, w_ref[0], preferred_element_type=F32,
                       precision=lax.Precision.HIGHEST) + b_ref[0]


def _adaln(cvec, w_ada, b_ada):
    rows = cvec.shape[0]
    return pl.pallas_call(
        _adaln_kernel,
        grid=(DEPTH, 3),
        in_specs=[
            pl.BlockSpec((rows, D_MODEL), lambda i, j: (0, 0)),
            pl.BlockSpec((1, D_MODEL, D_MODEL), lambda i, j: (i, 0, j)),
            pl.BlockSpec((1, 1, D_MODEL), lambda i, j: (i, 0, j)),
        ],
        out_specs=pl.BlockSpec((1, rows, D_MODEL), lambda i, j: (i, 0, j)),
        out_shape=jax.ShapeDtypeStruct((DEPTH, rows, 3 * D_MODEL), F32),
        compiler_params=_cparams(("parallel", "parallel"), 32),
        name="adaln",
    )(cvec, w_ada, b_ada.reshape(DEPTH, 1, 3 * D_MODEL))


ROPE_Q = HEAD_DIM // 4


def _rope_tables():
    inv = np.power(ROPE_THETA, -np.arange(0, 2 * ROPE_Q, 2, dtype=np.float64) / (2 * ROPE_Q))
    t = np.arange(SEQ)
    ang = np.concatenate([(t // GRID_W).astype(np.float64)[:, None] * inv,
                          (t % GRID_W).astype(np.float64)[:, None] * inv], axis=1)
    cos = np.tile(np.cos(ang), (1, 4))
    sin = np.concatenate([-np.tile(np.sin(ang), (1, 2)), np.tile(np.sin(ang), (1, 2))], axis=1)
    return jnp.asarray(cos, F32), jnp.asarray(sin, F32)


def _pair_interleave(w, dup=False):
    d, n = w.shape
    h = n // HEAD_DIM
    w = w.reshape(d, h, 2, 2, ROPE_Q)
    if dup:
        w = jnp.broadcast_to(w[:, :, None], (d, h, 2, 2, 2, ROPE_Q))
    else:
        w = w.reshape(d, h // 2, 2, 2, 2, ROPE_Q)
    return jnp.transpose(w, (0, 1, 4, 2, 3, 5)).reshape(d, -1)


def _rope_chunk(x, cos, sin):
    return x * cos + pltpu.roll(x, LANES // 2, axis=1) * sin


def _inproj_kernel(*refs, tn, rope_plan):
    has_rope = bool(rope_plan)
    if has_rope:
        x_ref, g_ref, sh_ref, sc_ref, w_ref, cos_ref, sin_ref, o_ref, h_scr = refs
    else:
        x_ref, g_ref, sh_ref, sc_ref, w_ref, o_ref, h_scr = refs
    j = pl.program_id(1)

    @pl.when(j == 0)
    def _():
        x = x_ref[...]
        ms = jnp.mean(x * x, axis=-1, keepdims=True)
        y = x * lax.rsqrt(ms + RMS_EPS) * g_ref[...]
        h_scr[...] = (y * (1.0 + sc_ref[0]) + sh_ref[0]).astype(BF16)

    acc = jnp.dot(h_scr[...], w_ref[...], preferred_element_type=F32)

    def store_plain():
        o_ref[...] = acc.astype(BF16)

    def store_rope(flags):
        cos = cos_ref[...]
        sin = sin_ref[...]
        for c, roped in enumerate(flags):
            blk = acc[:, c * LANES:(c + 1) * LANES]
            if roped:
                blk = _rope_chunk(blk, cos, sin)
            o_ref[:, c * LANES:(c + 1) * LANES] = blk.astype(BF16)

    if not has_rope:
        store_plain()
    else:
        any_rope = None
        for flags, tiles in rope_plan:
            hit = functools.reduce(jnp.logical_or, [j == t for t in tiles])
            pl.when(hit)(functools.partial(store_rope, flags))
            any_rope = hit if any_rope is None else jnp.logical_or(any_rope, hit)
        pl.when(jnp.logical_not(any_rope))(store_plain)


def _inproj(x2d, norm_g, mods, w, *, rows_per_mod, mod_row0, tm, tn, rope_cols=(), rope_tabs=None):
    rows, _ = x2d.shape
    n = w.shape[1]
    assert rows % tm == 0 and n % tn == 0 and rows_per_mod % tm == 0
    tiles_per_mod = rows_per_mod // tm
    assert all(lo % LANES == 0 and hi % LANES == 0 for lo, hi in rope_cols)
    plan = {}
    for t in range(n // tn):
        flags = tuple(any(lo <= t * tn + c * LANES < hi for lo, hi in rope_cols) for c in range(tn // LANES))
        if any(flags):
            plan.setdefault(flags, []).append(t)
    rope_plan = tuple((flags, tuple(tiles)) for flags, tiles in plan.items())
    has_rope = bool(rope_plan)
    in_specs = [
        pl.BlockSpec((tm, D_MODEL), lambda i, j: (i, 0)),
        pl.BlockSpec((1, D_MODEL), lambda i, j: (0, 0)),
        pl.BlockSpec((1, 1, D_MODEL), lambda i, j: (mod_row0 + i // tiles_per_mod, 0, 0)),
        pl.BlockSpec((1, 1, D_MODEL), lambda i, j: (mod_row0 + i // tiles_per_mod, 0, 1)),
        pl.BlockSpec((D_MODEL, tn), lambda i, j: (0, j)),
    ]
    args = [x2d, norm_g.reshape(1, D_MODEL), mods, mods, w]
    if has_rope:
        assert SEQ % tm == 0
        seq_tiles = SEQ // tm
        in_specs += [pl.BlockSpec((tm, LANES), lambda i, j: (i % seq_tiles, 0))] * 2
        args += list(rope_tabs)
    vmem = (2 * tm * D_MODEL * 4 + tm * D_MODEL * 2 + 2 * D_MODEL * tn * 2 + 2 * tm * tn * 2
            + tm * tn * 4 + (4 * tm * LANES * 4 if has_rope else 0)) // MIB + 8
    return pl.pallas_call(
        functools.partial(_inproj_kernel, tn=tn, rope_plan=rope_plan),
        grid=(rows // tm, n // tn),
        in_specs=in_specs,
        out_specs=pl.BlockSpec((tm, tn), lambda i, j: (i, j)),
        out_shape=jax.ShapeDtypeStruct((rows, n), BF16),
        scratch_shapes=[pltpu.VMEM((tm, D_MODEL), BF16)],
        compiler_params=_cparams(("parallel", "arbitrary"), vmem),
        name="inproj",
    )(*args)


def _gate(o, z_blk):
    return o * _silu(z_blk.astype(F32))


def _outproj_kernel(o_ref, w_ref, x_ref, gt_ref, *rest, final):
    y = jnp.dot(o_ref[...], w_ref[...], preferred_element_type=F32)
    xn = x_ref[...] + gt_ref[0] * y
    if final:
        fg_ref, out_ref = rest
        ms = jnp.mean(xn * xn, axis=-1, keepdims=True)
        xn = xn * lax.rsqrt(ms + RMS_EPS) * fg_ref[...]
    else:
        (out_ref,) = rest
    out_ref[...] = xn


def _outproj(o2d, w_out, x2d, mods, *, rows_per_mod, mod_row0, tm, final_g=None, in_place=True):
    rows = x2d.shape[0]
    assert rows % tm == 0 and rows_per_mod % tm == 0
    tiles_per_mod = rows_per_mod // tm
    final = final_g is not None
    in_specs = [
        pl.BlockSpec((tm, D_INNER), lambda i: (i, 0)),
        pl.BlockSpec((D_INNER, D_MODEL), lambda i: (0, 0)),
        pl.BlockSpec((tm, D_MODEL), lambda i: (i, 0)),
        pl.BlockSpec((1, 1, D_MODEL), lambda i: (mod_row0 + i // tiles_per_mod, 0, 2)),
    ]
    args = [o2d, w_out, x2d, mods]
    if final:
        in_specs.append(pl.BlockSpec((1, D_MODEL), lambda i: (0, 0)))
        args.append(final_g.reshape(1, D_MODEL))
    vmem = (2 * tm * D_INNER * 2 + 2 * D_INNER * D_MODEL * 2 + 5 * tm * D_MODEL * 4) // MIB + 8
    return pl.pallas_call(
        functools.partial(_outproj_kernel, final=final),
        grid=(rows // tm,),
        in_specs=in_specs,
        out_specs=pl.BlockSpec((tm, D_MODEL), lambda i: (i, 0)),
        out_shape=jax.ShapeDtypeStruct((rows, D_MODEL), F32),
        input_output_aliases={2: 0} if in_place else {},
        compiler_params=_cparams(("parallel",), vmem),
        name="outproj",
    )(*args)


def _half_masks():
    lane = lax.broadcasted_iota(jnp.int32, (1, LANES), 1)
    return lane < HEAD_DIM, lane >= HEAD_DIM


def _nt(a, b):
    return lax.dot_general(a, b, (((1,), (1,)), ((), ())), preferred_element_type=F32)


LOG2E = math.log2(math.e)
QK_SCALE = HEAD_DIM ** -0.5 * LOG2E


def _split_heads(q, interleaved=False):
    lo, hi = _half_masks()
    if interleaved:
        lane = lax.broadcasted_iota(jnp.int32, (1, LANES), 1)
        lo = (lane % HEAD_DIM) < HEAD_DIM // 2
        hi = jnp.logical_not(lo)
    qs = q.astype(F32) * QK_SCALE
    return jnp.concatenate([jnp.where(lo, qs, 0.0), jnp.where(hi, qs, 0.0)], axis=0).astype(BF16)


def _ones_col(rows):
    lane = lax.broadcasted_iota(jnp.int32, (rows, LANES), 1)
    return jnp.where(lane == 0, 1.0, 0.0).astype(BF16)


def _diff_attn_kernel(q_ref, kl_ref, vl_ref, z_ref, kc_ref, vc_ref, lqk_ref, g_ref, o_ref, sa_scr, sb_scr,
                      *, lam_init, tq):
    nq = q_ref.shape[1] // tq
    lqk = lqk_ref[...]
    s1 = jnp.sum(lqk[0:1] * lqk[1:2], axis=-1, keepdims=True)
    s2 = jnp.sum(lqk[2:3] * lqk[3:4], axis=-1, keepdims=True)
    lam = jnp.exp(s1) - jnp.exp(s2) + lam_init
    bufs = (sa_scr, sb_scr)

    def logits(i):
        s_scr = bufs[i % 2]
        qq = _split_heads(q_ref[0, i * tq:(i + 1) * tq, :], interleaved=True)
        s_scr[:, :CTX_LEN] = _nt(qq, kc_ref[0])
        s_scr[:, CTX_LEN:] = _nt(qq, kl_ref[0])

    def finish(i):
        s = bufs[i % 2][...]
        m = jnp.max(s, axis=-1, keepdims=True)
        p = jnp.exp2(s - m)
        den = jnp.sum(p, axis=-1, keepdims=True)
        a = (p[:tq] - (lam * den[:tq] / den[tq:]) * p[tq:]).astype(BF16)
        o = (jnp.dot(a[:, :CTX_LEN], vc_ref[0], preferred_element_type=F32)
             + jnp.dot(a[:, CTX_LEN:], vl_ref[0], preferred_element_type=F32)) / den[:tq]
        ms = jnp.mean(o * o, axis=-1, keepdims=True)
        o = o * lax.rsqrt(ms + RMS_EPS) * g_ref[...] * (1.0 - lam_init)
        o_ref[0, i * tq:(i + 1) * tq, :] = _gate(o, z_ref[0, i * tq:(i + 1) * tq, :]).astype(BF16)

    logits(0)
    for i in range(nq):
        if i + 1 < nq:
            logits(i + 1)
        finish(i)


def _diff_attn(proj_lat, proj_ctx, lqk, subln_g, *, tq, lam_init):
    bsz = proj_lat.shape[0]
    kb, vb, zb = D_INNER // LANES, 2 * D_INNER // LANES, 3 * D_INNER // LANES
    nk = CTX_LEN + SEQ
    score_bytes = 2 * tq * nk * 4
    vmem = (5 * score_bytes) // MIB + 16
    return pl.pallas_call(
        functools.partial(_diff_attn_kernel, lam_init=lam_init, tq=tq),
        grid=(bsz, DIFF_HEADS),
        in_specs=[pl.BlockSpec((1, SEQ, LANES), lambda b, h: (b, 0, h)),
                  pl.BlockSpec((1, SEQ, LANES), lambda b, h: (b, 0, kb + h)),
                  pl.BlockSpec((1, SEQ, LANES), lambda b, h: (b, 0, vb + h)),
                  pl.BlockSpec((1, SEQ, LANES), lambda b, h: (b, 0, zb + h)),
                  pl.BlockSpec((1, CTX_LEN, LANES), lambda b, h: (b, 0, kb + h)),
                  pl.BlockSpec((1, CTX_LEN, LANES), lambda b, h: (b, 0, vb + h)),
                  pl.BlockSpec((4, HEAD_DIM), lambda b, h: (0, 0)),
                  pl.BlockSpec((1, LANES), lambda b, h: (0, 0))],
        out_specs=pl.BlockSpec((1, SEQ, LANES), lambda b, h: (b, 0, h)),
        out_shape=jax.ShapeDtypeStruct((bsz, SEQ, D_INNER), BF16),
        scratch_shapes=[pltpu.VMEM((2 * tq, nk), F32), pltpu.VMEM((2 * tq, nk), F32)],
        compiler_params=_cparams(("parallel", "parallel"), vmem),
        name="diff_attn",
    )(proj_lat, proj_lat, proj_lat, proj_lat, proj_ctx, proj_ctx, lqk, subln_g.reshape(1, LANES))


CTX_HB = 4


def _ctx_attn_kernel(*refs, diff, lam_init):
    if diff:
        q_ref, k_ref, v_ref, z_ref, lqk_ref, g_ref, o_ref = refs
        lqk = lqk_ref[...]
        lam = (jnp.exp(jnp.sum(lqk[0:1] * lqk[1:2], axis=-1, keepdims=True))
               - jnp.exp(jnp.sum(lqk[2:3] * lqk[3:4], axis=-1, keepdims=True)) + lam_init)
    else:
        q_ref, k_ref, v_ref, z_ref, o_ref = refs
    t = q_ref.shape[1]
    lo, _ = _half_masks()
    for hb in range(CTX_HB):
        sl = slice(hb * LANES, (hb + 1) * LANES)
        qq = _split_heads(q_ref[0, :, sl], interleaved=diff)
        s = _nt(qq, k_ref[0, :, sl])
        p = jnp.exp2(s - jnp.max(s, axis=-1, keepdims=True))
        den = jnp.sum(p, axis=-1, keepdims=True)
        if diff:
            a = (p[:t] - (lam * den[:t] / den[t:]) * p[t:]).astype(BF16)
            o = jnp.dot(a, v_ref[0, :, sl], preferred_element_type=F32) / den[:t]
            ms = jnp.mean(o * o, axis=-1, keepdims=True)
            o = o * lax.rsqrt(ms + RMS_EPS) * g_ref[...] * (1.0 - lam_init)
        else:
            o = jnp.dot(p.astype(BF16), v_ref[0, :, sl], preferred_element_type=F32) / den
            o = jnp.where(lo, o[:t], o[t:])
        o_ref[0, :, sl] = _gate(o, z_ref[0, :, sl]).astype(BF16)


def _ctx_attn(proj_ctx, lqk=None, subln_g=None, lam_init=0.0):
    bsz = proj_ctx.shape[0]
    diff = lqk is not None
    w = CTX_HB * LANES
    kb, vb, zb = D_INNER // w, 2 * D_INNER // w, 3 * D_INNER // w
    in_specs = [pl.BlockSpec((1, CTX_LEN, w), lambda b, h: (b, 0, h)),
                pl.BlockSpec((1, CTX_LEN, w), lambda b, h: (b, 0, kb + h)),
                pl.BlockSpec((1, CTX_LEN, w), lambda b, h: (b, 0, vb + h)),
                pl.BlockSpec((1, CTX_LEN, w), lambda b, h: (b, 0, zb + h))]
    args = [proj_ctx, proj_ctx, proj_ctx, proj_ctx]
    if diff:
        in_specs += [pl.BlockSpec((4, HEAD_DIM), lambda b, h: (0, 0)), pl.BlockSpec((1, LANES), lambda b, h: (0, 0))]
        args += [lqk, subln_g.reshape(1, LANES)]
    return pl.pallas_call(
        functools.partial(_ctx_attn_kernel, diff=diff, lam_init=lam_init),
        grid=(bsz, D_INNER // w),
        in_specs=in_specs,
        out_specs=pl.BlockSpec((1, CTX_LEN, w), lambda b, h: (b, 0, h)),
        out_shape=jax.ShapeDtypeStruct((bsz, CTX_LEN, D_INNER), BF16),
        compiler_params=_cparams(("parallel", "parallel"), 32),
        name="ctx_attn",
    )(*args)


NA_ROWS = SEQ // GRID_W
NA_BLK_ROWS = 4
NA_BLK_Q = NA_BLK_ROWS * GRID_W
NA_NBLK = NA_ROWS // NA_BLK_ROWS
NA_WIN_R = NA_BLK_ROWS + NA_WIN_ROWS
NA_KWIN = NA_WIN_R * GRID_W
NA_CTX_CHUNK = 512
NA_MAX_LANE = LANES + 1


def _na_win_start(i):
    lo, hi = 0, NA_ROWS - NA_WIN_R
    start = i * NA_BLK_ROWS - NA_WIN_ROWS // 2
    return np.clip(start, lo, hi) if isinstance(i, (int, np.integer)) else jnp.clip(start, lo, hi)


NA_NDR = 2 * NA_WIN_ROWS - 1


def _na_bias_plan():
    plan = np.full((3, NA_BLK_ROWS, NA_WIN_R), -1, np.int32)
    for pat, i in enumerate((0, 1, NA_NBLK - 1)):
        ws = int(_na_win_start(i))
        for qr in range(NA_BLK_ROWS):
            r = i * NA_BLK_ROWS + qr
            rs = min(max(r - NA_WIN_ROWS // 2, 0), NA_ROWS - NA_WIN_ROWS)
            for a in range(NA_WIN_R):
                if rs <= ws + a < rs + NA_WIN_ROWS:
                    plan[pat, qr, a] = ws + a - r + NA_WIN_ROWS - 1
    return plan


def _na_bias_tiles(rpb):
    c = np.arange(GRID_W)
    cs = np.clip(c - NA_WIN_COLS // 2, 0, GRID_W - NA_WIN_COLS)
    kc = np.arange(GRID_W)
    col_ok = (kc[None, :] >= cs[:, None]) & (kc[None, :] < cs[:, None] + NA_WIN_COLS)
    dc_idx = np.clip(kc[None, :] - c[:, None], -(NA_WIN_COLS - 1), NA_WIN_COLS - 1) + NA_WIN_COLS - 1
    return jnp.where(col_ok[None, None], rpb[:, :, dc_idx] * LOG2E, NEG_BIG).astype(F32)


def _na_attn_kernel(q_ref, k_ref, v_ref, z_ref, kc_ref, vc_ref, t1_ref, o_ref, oc_scr, bias_ref):
    lo, _ = _half_masks()

    @pl.when(pl.program_id(1) == 0)
    def _():
        plan = _na_bias_plan()
        masked = jnp.full((GRID_W, GRID_W), NEG_BIG, F32)
        for hl in range(2):
            for pat in range(3):
                for qr in range(NA_BLK_ROWS):
                    tiles = [t1_ref[hl, int(d)] if d >= 0 else masked for d in plan[pat, qr]]
                    bias_ref[hl, pat, qr * GRID_W:(qr + 1) * GRID_W, :] = jnp.concatenate(tiles, axis=1)
    kctx = kc_ref[0]
    vctx = jnp.concatenate([vc_ref[0], _ones_col(CTX_LEN)], axis=1)
    lane2 = lax.broadcasted_iota(jnp.int32, (1, 2 * LANES), 1)

    def ctx_chunk(j, carry):
        r0 = pl.multiple_of(jnp.asarray(j, jnp.int32) * NA_CTX_CHUNK, NA_CTX_CHUNK)
        qq = _split_heads(q_ref[0, pl.ds(r0, NA_CTX_CHUNK), :])
        s = _nt(qq, kctx)
        m = jnp.max(s, axis=-1, keepdims=True)
        oc = jnp.dot(jnp.exp2(s - m).astype(BF16), vctx, preferred_element_type=F32)
        oc = jnp.where(lane2 == NA_MAX_LANE, m, oc)
        oc_scr[0, pl.ds(r0, NA_CTX_CHUNK), :] = oc[:NA_CTX_CHUNK]
        oc_scr[1, pl.ds(r0, NA_CTX_CHUNK), :] = oc[NA_CTX_CHUNK:]
        return carry

    lax.fori_loop(0, SEQ // NA_CTX_CHUNK, ctx_chunk, 0, unroll=True)

    def block(i, carry):
        i = jnp.asarray(i, jnp.int32)
        pat = jnp.where(i == 0, 0, jnp.where(i == NA_NBLK - 1, 2, 1))
        t0 = pl.multiple_of(i * NA_BLK_Q, NA_BLK_Q)
        s0 = pl.multiple_of(_na_win_start(i) * GRID_W, NA_BLK_Q)
        qq = _split_heads(q_ref[0, pl.ds(t0, NA_BLK_Q), :])
        kw = k_ref[0, pl.ds(s0, NA_KWIN), :]
        vw = jnp.concatenate([v_ref[0, pl.ds(s0, NA_KWIN), :], _ones_col(NA_KWIN)], axis=1)
        s = _nt(qq, kw) + jnp.concatenate([bias_ref[0, pat], bias_ref[1, pat]], axis=0)
        oc = jnp.concatenate([oc_scr[0, pl.ds(t0, NA_BLK_Q), :], oc_scr[1, pl.ds(t0, NA_BLK_Q), :]], axis=0)
        mc = oc[:, NA_MAX_LANE:NA_MAX_LANE + 1]
        m = jnp.maximum(jnp.max(s, axis=-1, keepdims=True), mc)
        tot = (jnp.dot(jnp.exp2(s - m).astype(BF16), vw, preferred_element_type=F32)
               + jnp.exp2(mc - m) * oc)
        out = tot[:, :LANES] / tot[:, LANES:LANES + 1]
        o = jnp.where(lo, out[:NA_BLK_Q], out[NA_BLK_Q:])
        o_ref[0, pl.ds(t0, NA_BLK_Q), :] = _gate(o, z_ref[0, pl.ds(t0, NA_BLK_Q), :]).astype(BF16)
        return carry

    lax.fori_loop(0, NA_NBLK, block, 0, unroll=True)


def _na_attn(proj_lat, proj_ctx, bias_tiles):
    bsz = proj_lat.shape[0]
    pairs = NA_HEADS // 2
    kb, vb, zb = D_INNER // LANES, 2 * D_INNER // LANES, 3 * D_INNER // LANES
    return pl.pallas_call(
        _na_attn_kernel,
        grid=(pairs, bsz),
        in_specs=[pl.BlockSpec((1, SEQ, LANES), lambda h, b: (b, 0, h)),
                  pl.BlockSpec((1, SEQ, LANES), lambda h, b: (b, 0, kb + h)),
                  pl.BlockSpec((1, SEQ, LANES), lambda h, b: (b, 0, vb + h)),
                  pl.BlockSpec((1, SEQ, LANES), lambda h, b: (b, 0, zb + h)),
                  pl.BlockSpec((1, CTX_LEN, LANES), lambda h, b: (b, 0, kb + h)),
                  pl.BlockSpec((1, CTX_LEN, LANES), lambda h, b: (b, 0, vb + h)),
                  pl.BlockSpec((2, NA_NDR, GRID_W, GRID_W), lambda h, b: (h, 0, 0, 0))],
        out_specs=pl.BlockSpec((1, SEQ, LANES), lambda h, b: (b, 0, h)),
        out_shape=jax.ShapeDtypeStruct((bsz, SEQ, D_INNER), BF16),
        scratch_shapes=[pltpu.VMEM((2, SEQ, 2 * LANES), F32), pltpu.VMEM((2, 3, NA_BLK_Q, NA_KWIN), F32)],
        compiler_params=_cparams(("parallel", "arbitrary"), 48),
        name="na_attn",
    )(proj_lat, proj_lat, proj_lat, proj_lat, proj_ctx, proj_ctx, bias_tiles)


SWA_TQ = 256
SWA_BAND = SWA_TQ + 2 * SWA_WINDOW
SWA_K_COL = 2 * D_INNER
SWA_V_COL = SWA_K_COL + SWA_KV_HEADS * LANES
SWA_N = SWA_V_COL + SWA_KV_HEADS * LANES


SWA_GROUP = SWA_Q_HEADS // SWA_KV_HEADS
SWA_GW = SWA_GROUP * HEAD_DIM


def _swa_attn_kernel(q_ref, k_ref, v_ref, z_ref, kc_ref, vc_ref, sink_ref, o_ref):
    lo, _ = _half_masks()
    kvh = pl.program_id(1)
    n_pair = SWA_GROUP // 2
    kctx = kc_ref[0]
    vctx = jnp.concatenate([vc_ref[0], _ones_col(CTX_LEN)], axis=1)
    sinks = sink_ref[pl.ds(pl.multiple_of(kvh * SWA_GROUP, SWA_GROUP), SWA_GROUP), :][:, 0:1] * LOG2E
    sink = jnp.concatenate([jnp.broadcast_to(sinks[g:g + 1], (SWA_TQ, 1)) for g in range(SWA_GROUP)], axis=0)

    def block(i, carry):
        i = jnp.asarray(i, jnp.int32)
        t0 = pl.multiple_of(i * SWA_TQ, SWA_TQ)
        s0 = pl.multiple_of(jnp.clip(i * SWA_TQ - SWA_WINDOW, 0, SEQ - SWA_BAND), SWA_WINDOW)
        q = q_ref[0, pl.ds(t0, SWA_TQ), :]
        qq = jnp.concatenate([_split_heads(q[:, p * LANES:(p + 1) * LANES], interleaved=True)
                              for p in range(n_pair)], axis=0)
        kw = k_ref[0, pl.ds(s0, SWA_BAND), :]
        vw = jnp.concatenate([v_ref[0, pl.ds(s0, SWA_BAND), :], _ones_col(SWA_BAND)], axis=1)
        tpos = t0 + lax.broadcasted_iota(jnp.int32, (SWA_TQ, SWA_BAND), 0)
        spos = s0 + lax.broadcasted_iota(jnp.int32, (SWA_TQ, SWA_BAND), 1)
        bias = jnp.where(jnp.abs(tpos - spos) <= SWA_WINDOW, 0.0, NEG_BIG)
        s_lat = _nt(qq, kw) + jnp.concatenate([bias] * SWA_GROUP, axis=0)
        s_ctx = _nt(qq, kctx)
        m = jnp.maximum(jnp.maximum(jnp.max(s_lat, axis=-1, keepdims=True),
                                    jnp.max(s_ctx, axis=-1, keepdims=True)), sink)
        tot = (jnp.dot(jnp.exp2(s_lat - m).astype(BF16), vw, preferred_element_type=F32)
               + jnp.dot(jnp.exp2(s_ctx - m).astype(BF16), vctx, preferred_element_type=F32))
        out = tot[:, :LANES] / (tot[:, LANES:LANES + 1] + jnp.exp2(sink - m))
        pairs = [jnp.where(lo, out[(2 * p) * SWA_TQ:(2 * p + 1) * SWA_TQ],
                           out[(2 * p + 1) * SWA_TQ:(2 * p + 2) * SWA_TQ]) for p in range(n_pair)]
        o = jnp.concatenate(pairs, axis=1)
        o_ref[0, pl.ds(t0, SWA_TQ), :] = _gate(o, z_ref[0, pl.ds(t0, SWA_TQ), :]).astype(BF16)
        return carry

    lax.fori_loop(0, SEQ // SWA_TQ, block, 0, unroll=True)


def _swa_attn(proj_lat, proj_ctx, sink_rows):
    bsz = proj_lat.shape[0]
    kb, vb, zb = SWA_K_COL // LANES, SWA_V_COL // LANES, D_INNER // SWA_GW
    return pl.pallas_call(
        _swa_attn_kernel,
        grid=(bsz, SWA_KV_HEADS),
        in_specs=[pl.BlockSpec((1, SEQ, SWA_GW), lambda b, h: (b, 0, h)),
                  pl.BlockSpec((1, SEQ, LANES), lambda b, h: (b, 0, kb + h)),
                  pl.BlockSpec((1, SEQ, LANES), lambda b, h: (b, 0, vb + h)),
                  pl.BlockSpec((1, SEQ, SWA_GW), lambda b, h: (b, 0, zb + h)),
                  pl.BlockSpec((1, CTX_LEN, LANES), lambda b, h: (b, 0, kb + h)),
                  pl.BlockSpec((1, CTX_LEN, LANES), lambda b, h: (b, 0, vb + h)),
                  pl.BlockSpec((SWA_Q_HEADS, LANES), lambda b, h: (0, 0))],
        out_specs=pl.BlockSpec((1, SEQ, SWA_GW), lambda b, h: (b, 0, h)),
        out_shape=jax.ShapeDtypeStruct((bsz, SEQ, D_INNER), BF16),
        compiler_params=_cparams(("parallel", "parallel"), 48),
        name="swa_attn",
    )(proj_lat, proj_lat, proj_lat, proj_lat, proj_ctx, proj_ctx, sink_rows)


GATE_LANES = 128


def _mlstm_prep_kernel(xm_ref, cw_ref, cb_ref, bdq_ref, bdk_ref, bdv_ref, wg_ref, bg_ref,
                       q_ref, kt_ref, v_ref, xc_ref, g_ref):
    h = pl.program_id(1)
    t = xm_ref.shape[1]
    xm_b = xm_ref[0]
    xm = xm_b.astype(F32)

    tpos = lax.broadcasted_iota(jnp.int32, (t, 1), 0)
    conv = jnp.zeros_like(xm) + cb_ref[...]
    for j in range(MLSTM_CONV):
        d = j - MLSTM_CONV // 2
        sh = pltpu.roll(xm, (-d) % t, axis=0) if d else xm
        ok = (tpos + d >= 0) & (tpos + d < t)
        conv = conv + jnp.where(ok, sh, 0.0) * cw_ref[j:j + 1, :]
    xc = _silu(conv)
    xc_b = xc.astype(BF16)
    q = jnp.dot(xc_b, bdq_ref[0], preferred_element_type=F32)
    k = jnp.dot(xc_b, bdk_ref[0], preferred_element_type=F32)
    v = jnp.dot(xm_b, bdv_ref[0], preferred_element_type=F32)
    q_b, k_b, v_b = q.astype(BF16), k.astype(BF16), v.astype(BF16)
    q_ref[0] = q_b
    v_ref[0] = v_b
    xc_ref[0] = xc_b
    ks = k * (MLSTM_HEAD_DIM ** -0.5)
    for c in range(t // SCAN_CHUNK):
        kt_ref[0, 0, c] = ks[c * SCAN_CHUNK:(c + 1) * SCAN_CHUNK, :].T.astype(BF16)
    part = (jnp.dot(q_b, wg_ref[0, 0], preferred_element_type=F32)
            + jnp.dot(k_b, wg_ref[1, 0], preferred_element_type=F32)
            + jnp.dot(v_b, wg_ref[2, 0], preferred_element_type=F32))

    @pl.when(h == 0)
    def _():
        g_ref[0] = part + bg_ref[...]

    @pl.when(h != 0)
    def _():
        g_ref[0] = g_ref[0] + part


def _mlstm_prep(proj3, conv_w, conv_b, bdq, bdk, bdv, wg, bg):
    bsz, t, _ = proj3.shape
    hd = MLSTM_HEAD_DIM
    nc = t // SCAN_CHUNK
    seq_spec = pl.BlockSpec((1, t, hd), lambda b, h: (b, 0, h))
    return pl.pallas_call(
        _mlstm_prep_kernel,
        grid=(bsz, MLSTM_HEADS),
        in_specs=[seq_spec,
                  pl.BlockSpec((MLSTM_CONV, hd), lambda b, h: (0, h)),
                  pl.BlockSpec((1, hd), lambda b, h: (0, h)),
                  pl.BlockSpec((1, hd, hd), lambda b, h: (h, 0, 0)),
                  pl.BlockSpec((1, hd, hd), lambda b, h: (h, 0, 0)),
                  pl.BlockSpec((1, hd, hd), lambda b, h: (h, 0, 0)),
                  pl.BlockSpec((3, 1, hd, GATE_LANES), lambda b, h: (0, h, 0, 0)),
                  pl.BlockSpec((1, GATE_LANES), lambda b, h: (0, 0))],
        out_specs=[seq_spec,
                   pl.BlockSpec((1, 1, nc, hd, SCAN_CHUNK), lambda b, h: (b, h, 0, 0, 0)),
                   seq_spec, seq_spec,
                   pl.BlockSpec((1, t, GATE_LANES), lambda b, h: (b, 0, 0))],
        out_shape=[jax.ShapeDtypeStruct((bsz, t, D_INNER), BF16),
                   jax.ShapeDtypeStruct((bsz, MLSTM_HEADS, nc, hd, SCAN_CHUNK), BF16),
                   jax.ShapeDtypeStruct((bsz, t, D_INNER), BF16),
                   jax.ShapeDtypeStruct((bsz, t, D_INNER), BF16),
                   jax.ShapeDtypeStruct((bsz, t, GATE_LANES), F32)],
        compiler_params=_cparams(("parallel", "arbitrary"), 48),
        name="mlstm_prep",
    )(proj3, conv_w, conv_b.reshape(1, D_INNER), bdq, bdk, bdv, wg, bg)


def _lane_scan(x, op, fill, reverse):
    lane = lax.broadcasted_iota(jnp.int32, x.shape, 1)
    sh = 1
    while sh < LANES:
        if reverse:
            y = jnp.where(lane < LANES - sh, pltpu.roll(x, LANES - sh, axis=1), fill)
        else:
            y = jnp.where(lane >= sh, pltpu.roll(x, sh, axis=1), fill)
        x = op(x, y)
        sh *= 2
    return x


GP_ROWS = 8


def _mlstm_gates_kernel(gc_ref, gl_ref, oc_ref, ol_ref):
    nh = MLSTM_HEADS
    ppc = SCAN_CHUNK // LANES
    gct = gc_ref[0].T
    glt = gl_ref[0].T
    n_c = gc_ref.shape[1] // SCAN_CHUNK
    n_l = gl_ref.shape[1] // SCAN_CHUNK
    zeros = jnp.zeros((nh, SCAN_CHUNK), F32)
    pieces = [(gct, j) for j in range(n_c * ppc)] + [(glt, j) for j in range(n_l * ppc)]
    chunks = [(oc_ref, c) for c in range(n_c)] + [(ol_ref, c) for c in range(n_l)]
    n_all = len(chunks)

    def stack(row0):
        return jnp.concatenate([src[row0:row0 + nh, j * LANES:(j + 1) * LANES] for src, j in pieces], axis=0)

    def piece(x, k):
        return x[k * nh:(k + 1) * nh]

    def earlier(k, rev):
        c = k // ppc
        return range(k + 1, (c + 1) * ppc) if rev else range(c * ppc, k)

    for d in range(2):
        rev = d == 1
        edge = (lambda x: x[:, 0:1]) if rev else (lambda x: x[:, LANES - 1:LANES])
        ig = stack(2 * d * nh)
        fg = stack((2 * d + 1) * nh)
        lf = jnp.minimum(fg, 0.0) - jnp.log1p(jnp.exp(-jnp.abs(fg)))
        bp = _lane_scan(lf, jnp.add, 0.0, rev)
        b = jnp.concatenate([piece(bp, k) + sum(piece(edge(bp), o) for o in earlier(k, rev))
                             for k in range(len(pieces))], axis=0)
        cmp_ = _lane_scan(ig - b, jnp.maximum, -jnp.inf, rev)
        cm = jnp.concatenate([functools.reduce(jnp.maximum, [piece(edge(cmp_), o) for o in earlier(k, rev)],
                                               piece(cmp_, k)) for k in range(len(pieces))], axis=0)
        last = (lambda c: c * ppc) if rev else (lambda c: c * ppc + ppc - 1)
        b_end_c = [piece(edge(b), last(c)) for c in range(n_all)]
        b_end = jnp.concatenate([b_end_c[k // ppc] for k in range(len(pieces))], axis=0)
        g = b_end - b + ig
        gmax_p = jnp.max(g, axis=-1, keepdims=True)
        gmax_c = [functools.reduce(jnp.maximum, [piece(gmax_p, c * ppc + p) for p in range(ppc)])
                  for c in range(n_all)]
        order = list(range(n_all))
        if rev:
            order = list(reversed(range(n_c))) + list(reversed(range(n_c, n_all)))
        m = jnp.zeros((nh, 1), F32)
        m_in, m_out = [None] * n_all, [None] * n_all
        for c in order:
            m_in[c] = m
            m = jnp.maximum(b_end_c[c] + m, gmax_c[c])
            m_out[c] = m
        m_in = jnp.concatenate([m_in[k // ppc] for k in range(len(pieces))], axis=0)
        m_out = jnp.concatenate([m_out[k // ppc] for k in range(len(pieces))], axis=0)
        mx = jnp.maximum(cm, m_in)
        rows = (-mx, b - ig, jnp.exp(m_in - mx), jnp.exp(-(b + mx)), jnp.exp(g - m_out),
                jnp.broadcast_to(jnp.exp(b_end + m_in - m_out), b.shape))
        for k, (dst, c) in enumerate(chunks):
            for r, val in enumerate(rows):
                dst[0, d, c, r] = jnp.concatenate([piece(val, k * ppc + p) for p in range(ppc)], axis=1)
            dst[0, d, c, 6] = zeros
            dst[0, d, c, 7] = zeros


def _mlstm_gates(gates_ctx, gates_lat):
    bsz = gates_lat.shape[0]
    n_c, n_l = CTX_LEN // SCAN_CHUNK, SEQ // SCAN_CHUNK

    def out(n):
        shape = (bsz, 2, n, GP_ROWS, MLSTM_HEADS, SCAN_CHUNK)
        return (pl.BlockSpec((1,) + shape[1:], lambda b: (b, 0, 0, 0, 0, 0)), jax.ShapeDtypeStruct(shape, F32))

    (spec_c, shape_c), (spec_l, shape_l) = out(n_c), out(n_l)
    return pl.pallas_call(
        _mlstm_gates_kernel,
        grid=(bsz,),
        in_specs=[pl.BlockSpec((1, CTX_LEN, GATE_LANES), lambda b: (b, 0, 0)),
                  pl.BlockSpec((1, SEQ, GATE_LANES), lambda b: (b, 0, 0))],
        out_specs=[spec_c, spec_l],
        out_shape=[shape_c, shape_l],
        compiler_params=_cparams(("parallel",), 32),
        name="mlstm_gates",
    )(gates_ctx, gates_lat)


def _mlstm_scan_kernel(qc_ref, ktc_ref, vc_ref, xcc_ref, gpc_ref,
                       ql_ref, ktl_ref, vl_ref, xcl_ref, gpl_ref, ng_ref, sk_ref, zc_ref, zl_ref,
                       oc_ref, ol_ref, hc_scr, hl_scr, st_scr):
    L = SCAN_CHUNK
    hd = MLSTM_HEAD_DIM
    n_c = qc_ref.shape[1] // L
    n_l = ql_ref.shape[1] // L
    lane = lax.broadcasted_iota(jnp.int32, (L, LANES), 1)
    ones_col = jnp.where(lane == 0, 1.0, 0.0).astype(BF16)
    tt = lax.broadcasted_iota(jnp.int32, (L, L), 0)
    ss = lax.broadcasted_iota(jnp.int32, (L, L), 1)
    masks = (ss <= tt, ss >= tt)

    hc_scr[...] = jnp.zeros_like(hc_scr)
    hl_scr[...] = jnp.zeros_like(hl_scr)
    st_scr[...] = jnp.zeros_like(st_scr)

    def chunk(d, q_ref, kt_ref, v_ref, gp_ref, h_scr, c):
        rows = gp_ref[0, d, 0, c]
        cols = rows.T
        a_col, wst, einv = cols[:, 0:1], cols[:, 2:3], cols[:, 3:4]
        c_row, ws_row, decay = rows[1:2, :], rows[4:5, :], rows[5:6, 0:1]
        r0 = pl.multiple_of(c * L, L)
        q = q_ref[0, pl.ds(r0, L), :]
        kt = kt_ref[0, 0, c]
        vx = jnp.concatenate([v_ref[0, pl.ds(r0, L), :], ones_col], axis=1)
        w = jnp.exp(jnp.where(masks[d], a_col - c_row, -jnp.inf))
        s = (jnp.dot(q, kt, preferred_element_type=F32) * w).astype(BF16)
        kw = (kt.astype(F32) * ws_row).astype(BF16)
        both = jnp.dot(jnp.concatenate([s, kw], axis=0), vx, preferred_element_type=F32)
        state = st_scr[d]
        tot = wst * jnp.dot(q, state.astype(BF16), preferred_element_type=F32) + both[:L]
        den = tot[:, hd:hd + 1]
        h = tot[:, :hd] / jnp.maximum(jnp.abs(den), einv)
        h_scr[pl.ds(r0, L), :] = h_scr[pl.ds(r0, L), :] + h
        st_scr[d] = decay * state + both[L:]

    for j in range(n_c):
        chunk(0, qc_ref, ktc_ref, vc_ref, gpc_ref, hc_scr, j)
        chunk(1, qc_ref, ktc_ref, vc_ref, gpc_ref, hc_scr, n_c - 1 - j)

    def lat_step(j, carry):
        chunk(0, ql_ref, ktl_ref, vl_ref, gpl_ref, hl_scr, j)
        chunk(1, ql_ref, ktl_ref, vl_ref, gpl_ref, hl_scr, n_l - 1 - j)
        return carry

    lax.fori_loop(0, n_l, lat_step, 0, unroll=True)

    ng = ng_ref[...]
    sk = sk_ref[...]

    def finish(h_scr, xc_ref, z_ref, o_ref, r0, n):
        h = h_scr[pl.ds(r0, n), :]
        mu = jnp.mean(h, axis=-1, keepdims=True)
        hcen = h - mu
        hn = hcen * lax.rsqrt(jnp.mean(hcen * hcen, axis=-1, keepdims=True) + RMS_EPS)
        o = hn * ng + sk * xc_ref[0, pl.ds(r0, n), :].astype(F32)
        o_ref[0, pl.ds(r0, n), :] = _gate(o, z_ref[0, pl.ds(r0, n), :]).astype(BF16)

    finish(hc_scr, xcc_ref, zc_ref, oc_ref, 0, CTX_LEN)

    def fin_step(i, carry):
        finish(hl_scr, xcl_ref, zl_ref, ol_ref, pl.multiple_of(jnp.asarray(i, jnp.int32) * 256, 256), 256)
        return carry

    lax.fori_loop(0, SEQ // 256, fin_step, 0, unroll=2)


def _mlstm_scan(prep_c, gp_c, prep_l, gp_l, norm_g, skip, proj_c, proj_l):
    bsz = prep_l[0].shape[0]
    hd = MLSTM_HEAD_DIM

    def z_spec(t):
        return pl.BlockSpec((1, t, hd), lambda b, h: (b, 0, MLSTM_HEADS + h))

    def seq_specs(t):
        nc = t // SCAN_CHUNK
        seq = pl.BlockSpec((1, t, hd), lambda b, h: (b, 0, h))
        return [seq,
                pl.BlockSpec((1, 1, nc, hd, SCAN_CHUNK), lambda b, h: (b, h, 0, 0, 0)),
                seq, seq,
                pl.BlockSpec((1, 2, 1, nc, GP_ROWS, SCAN_CHUNK), lambda b, h: (b, 0, h, 0, 0, 0))]

    vec = pl.BlockSpec((1, hd), lambda b, h: (0, h))
    return pl.pallas_call(
        _mlstm_scan_kernel,
        grid=(bsz, MLSTM_HEADS),
        in_specs=seq_specs(CTX_LEN) + seq_specs(SEQ) + [vec, vec, z_spec(CTX_LEN), z_spec(SEQ)],
        out_specs=[pl.BlockSpec((1, CTX_LEN, hd), lambda b, h: (b, 0, h)),
                   pl.BlockSpec((1, SEQ, hd), lambda b, h: (b, 0, h))],
        out_shape=[jax.ShapeDtypeStruct((bsz, CTX_LEN, D_INNER), BF16),
                   jax.ShapeDtypeStruct((bsz, SEQ, D_INNER), BF16)],
        scratch_shapes=[pltpu.VMEM((CTX_LEN, hd), F32), pltpu.VMEM((SEQ, hd), F32),
                        pltpu.VMEM((2, hd, hd + LANES), F32)],
        compiler_params=_cparams(("parallel", "parallel"), 40),
        name="mlstm_scan",
    )(*prep_c, gp_c, *prep_l, gp_l, norm_g.reshape(1, D_INNER), skip.reshape(1, D_INNER), proj_c, proj_l)


def _blockdiag_dense(w):
    per_head = MLSTM_HEAD_DIM // MLSTM_QKV_BLOCK
    wh = w.reshape(MLSTM_HEADS, per_head, MLSTM_QKV_BLOCK, MLSTM_QKV_BLOCK)
    eye = jnp.eye(per_head, dtype=w.dtype)
    dense = wh[:, :, :, None, :] * eye[None, :, None, :, None]
    return dense.reshape(MLSTM_HEADS, MLSTM_HEAD_DIM, MLSTM_HEAD_DIM).astype(BF16)


def _mlstm_mixer(proj_lat, proj_ctx, conv_w, conv_b, wq, wk, wv, w_gate, b_gate, norm_g, skip):
    n_gate = 4 * MLSTM_HEADS
    wg = jnp.pad(w_gate, ((0, 0), (0, GATE_LANES - n_gate))).astype(BF16)
    wg = wg.reshape(3, MLSTM_HEADS, MLSTM_HEAD_DIM, GATE_LANES)
    bg = jnp.pad(b_gate, (0, GATE_LANES - n_gate)).reshape(1, GATE_LANES)
    bds = (_blockdiag_dense(wq), _blockdiag_dense(wk), _blockdiag_dense(wv))
    *prep_c, gates_c = _mlstm_prep(proj_ctx, conv_w, conv_b, *bds, wg, bg)
    *prep_l, gates_l = _mlstm_prep(proj_lat, conv_w, conv_b, *bds, wg, bg)
    gp_c, gp_l = _mlstm_gates(gates_c, gates_l)
    gp_c = jnp.transpose(gp_c, (0, 1, 4, 2, 3, 5))
    gp_l = jnp.transpose(gp_l, (0, 1, 4, 2, 3, 5))
    return _mlstm_scan(prep_c, gp_c, prep_l, gp_l, norm_g, skip, proj_ctx, proj_lat)


def kernel(x, c, ctx, c_ctx, norm_g, w_ada, b_ada, final_g, mlstm_w_in, mlstm_conv_w, mlstm_conv_b, mlstm_wq, mlstm_wk, mlstm_wv, mlstm_w_gate, mlstm_b_gate, mlstm_norm_g, mlstm_skip, mlstm_w_out, diff_w_in, diff_lq1, diff_lk1, diff_lq2, diff_lk2, diff_subln_g, diff_w_out, na_w_in, na_rpb, na_w_out, swa_w_in, swa_sink, swa_w_out):
    bsz = x.shape[0]
    ctx_row = bsz
    mod_rows = -(-(bsz + 1) // 8) * 8
    cvec = jnp.zeros((mod_rows, D_MODEL), F32).at[:bsz].set(c).at[ctx_row].set(c_ctx)
    mods_all = _adaln(cvec, w_ada, b_ada)
    rope_tabs = _rope_tables()

    xl = x.reshape(bsz * SEQ, D_MODEL)
    xc = ctx.reshape(bsz * CTX_LEN, D_MODEL)
    lat = dict(rows_per_mod=SEQ, mod_row0=0)
    con = dict(rows_per_mod=bsz * CTX_LEN, mod_row0=ctx_row)
    tm_ctx = min(1024, bsz * CTX_LEN)

    def in_both(i, w, tn, rope_cols=()):
        mods = mods_all[i].reshape(mod_rows, 1, 3 * D_MODEL)
        wb = w.astype(BF16)
        pl_ = _inproj(xl, norm_g[i], mods, wb, tm=SEQ, tn=tn, rope_cols=rope_cols, rope_tabs=rope_tabs, **lat)
        pc_ = _inproj(xc, norm_g[i], mods, wb, tm=tm_ctx, tn=tn, **con)
        return mods, pl_, pc_

    def out_both(mods, o_lat, o_ctx, w_out, need_ctx, final=None, in_place=True):
        wb = w_out.astype(BF16)
        new_l = _outproj(o_lat.reshape(bsz * SEQ, D_INNER), wb, xl, mods, tm=1024,
                         final_g=final, in_place=in_place, **lat)
        new_c = xc
        if need_ctx:
            new_c = _outproj(o_ctx.reshape(bsz * CTX_LEN, D_INNER), wb, xc, mods, tm=tm_ctx, **con)
        return new_l, new_c

    mods, p_l, p_c = in_both(0, mlstm_w_in[0], 1024)
    o_c, o_l = _mlstm_mixer(p_l.reshape(bsz, SEQ, -1), p_c.reshape(bsz, CTX_LEN, -1), mlstm_conv_w[0],
                            mlstm_conv_b[0], mlstm_wq[0], mlstm_wk[0], mlstm_wv[0], mlstm_w_gate[0],
                            mlstm_b_gate[0], mlstm_norm_g[0], mlstm_skip[0])
    xl, xc = out_both(mods, o_l, o_c, mlstm_w_out[0], True, in_place=False)

    wd = diff_w_in[0]
    wd = jnp.concatenate([_pair_interleave(wd[:, :D_INNER]), _pair_interleave(wd[:, D_INNER:2 * D_INNER]),
                          wd[:, 2 * D_INNER:]], axis=1)
    mods, p_l, p_c = in_both(1, wd, 1024, rope_cols=((0, 2 * D_INNER),))
    p_l3, p_c3 = p_l.reshape(bsz, SEQ, -1), p_c.reshape(bsz, CTX_LEN, -1)
    lam_init = 0.8 - 0.6 * math.exp(-0.3 * 1)
    lqk = jnp.stack([diff_lq1[0], diff_lk1[0], diff_lq2[0], diff_lk2[0]])
    o_l = _diff_attn(p_l3, p_c3, lqk, diff_subln_g[0], tq=512, lam_init=lam_init)
    o_c = _ctx_attn(p_c3, lqk, diff_subln_g[0], lam_init)
    xl, xc = out_both(mods, o_l, o_c, diff_w_out[0], True)

    mods, p_l, p_c = in_both(2, na_w_in[0], 1024)
    p_l3, p_c3 = p_l.reshape(bsz, SEQ, -1), p_c.reshape(bsz, CTX_LEN, -1)
    o_l = _na_attn(p_l3, p_c3, _na_bias_tiles(na_rpb[0]))
    o_c = _ctx_attn(p_c3)
    xl, xc = out_both(mods, o_l, o_c, na_w_out[0], True)

    kvw = SWA_KV_HEADS * HEAD_DIM
    w = swa_w_in[0]
    wq, wk, wv, wz = (w[:, :D_INNER], w[:, D_INNER:D_INNER + kvw], w[:, D_INNER + kvw:D_INNER + 2 * kvw],
                      w[:, D_INNER + 2 * kvw:])

    def twice(m):
        m = m.reshape(D_MODEL, SWA_KV_HEADS, 1, HEAD_DIM)
        return jnp.broadcast_to(m, (D_MODEL, SWA_KV_HEADS, 2, HEAD_DIM)).reshape(D_MODEL, SWA_KV_HEADS * LANES)

    w3 = jnp.concatenate([_pair_interleave(wq), wz, _pair_interleave(wk, dup=True), twice(wv)], axis=1)
    mods, p_l, p_c = in_both(3, w3, 1024, rope_cols=((0, D_INNER), (SWA_K_COL, SWA_V_COL)))
    sink_rows = jnp.broadcast_to(swa_sink[0][:, None], (SWA_Q_HEADS, LANES))
    o_l = _swa_attn(p_l.reshape(bsz, SEQ, -1), p_c.reshape(bsz, CTX_LEN, -1), sink_rows)
    xl, _ = out_both(mods, o_l, None, swa_w_out[0], False, final=final_g)
    return xl.reshape(bsz, SEQ, D_MODEL)
```

```python
import functools
import math

import numpy as np
import jax
import jax.numpy as jnp
from jax import lax
from jax.experimental import pallas as pl
from jax.experimental.pallas import tpu as pltpu

F32 = jnp.float32
BF16 = jnp.bfloat16

D_MODEL = 1024
SEQ = 2048
CTX_LEN = 256
DEPTH = 4
D_INNER = 2 * D_MODEL
GRID_W = 64
RMS_EPS = 1e-6
ROPE_THETA = 10000.0

MLSTM_HEADS = 8
MLSTM_HEAD_DIM = D_INNER // MLSTM_HEADS
MLSTM_QKV_BLOCK = 4
MLSTM_CONV = 5
SCAN_CHUNK = 256

DIFF_HEADS = 16
HEAD_DIM = 64
NA_HEADS = 32
NA_WIN_ROWS = 8
NA_WIN_COLS = 16
SWA_Q_HEADS = 32
SWA_KV_HEADS = 4
SWA_WINDOW = 128

LANES = 128
MIB = 1024 * 1024
NEG_BIG = -1e30


def _cparams(semantics, vmem_mib):
    return pltpu.CompilerParams(dimension_semantics=semantics, vmem_limit_bytes=vmem_mib * MIB)


def _silu(v):
    h = 0.5 * v
    return h + h * jnp.tanh(h)


def _adaln_kernel(c_ref, w_ref, b_ref, o_ref):
    s = _silu(c_ref[...])
    o_ref[0] = jnp.dot(s, w_ref[0], preferred_element_type=F32,
                       precision=lax.Precision.HIGHEST) + b_ref[0]


def _adaln(cvec, w_ada, b_ada):
    rows = cvec.shape[0]
    return pl.pallas_call(
        _adaln_kernel,
        grid=(DEPTH, 3),
        in_specs=[
            pl.BlockSpec((rows, D_MODEL), lambda i, j: (0, 0)),
            pl.BlockSpec((1, D_MODEL, D_MODEL), lambda i, j: (i, 0, j)),
            pl.BlockSpec((1, 1, D_MODEL), lambda i, j: (i, 0, j)),
        ],
        out_specs=pl.BlockSpec((1, rows, D_MODEL), lambda i, j: (i, 0, j)),
        out_shape=jax.ShapeDtypeStruct((DEPTH, rows, 3 * D_MODEL), F32),
        compiler_params=_cparams(("parallel", "parallel"), 32),
        name="adaln",
    )(cvec, w_ada, b_ada.reshape(DEPTH, 1, 3 * D_MODEL))


ROPE_Q = HEAD_DIM // 4


def _rope_tables():
    inv = np.power(ROPE_THETA, -np.arange(0, 2 * ROPE_Q, 2, dtype=np.float64) / (2 * ROPE_Q))
    t = np.arange(SEQ)
    ang = np.concatenate([(t // GRID_W).astype(np.float64)[:, None] * inv,
                          (t % GRID_W).astype(np.float64)[:, None] * inv], axis=1)
    cos = np.tile(np.cos(ang), (1, 4))
    sin = np.concatenate([-np.tile(np.sin(ang), (1, 2)), np.tile(np.sin(ang), (1, 2))], axis=1)
    return jnp.asarray(cos, F32), jnp.asarray(sin, F32)


def _pair_interleave(w, dup=False):
    d, n = w.shape
    h = n // HEAD_DIM
    w = w.reshape(d, h, 2, 2, ROPE_Q)
    if dup:
        w = jnp.broadcast_to(w[:, :, None], (d, h, 2, 2, 2, ROPE_Q))
    else:
        w = w.reshape(d, h // 2, 2, 2, 2, ROPE_Q)
    return jnp.transpose(w, (0, 1, 4, 2, 3, 5)).reshape(d, -1)


def _rope_chunk(x, cos, sin):
    return x * cos + pltpu.roll(x, LANES // 2, axis=1) * sin


def _inproj_kernel(*refs, tn, rope_plan):
    has_rope = bool(rope_plan)
    if has_rope:
        x_ref, g_ref, sh_ref, sc_ref, w_ref, cos_ref, sin_ref, o_ref, h_scr = refs
    else:
        x_ref, g_ref, sh_ref, sc_ref, w_ref, o_ref, h_scr = refs
    j = pl.program_id(1)

    @pl.when(j == 0)
    def _():
        x = x_ref[...]
        ms = jnp.mean(x * x, axis=-1, keepdims=True)
        y = x * lax.rsqrt(ms + RMS_EPS) * g_ref[...]
        h_scr[...] = (y * (1.0 + sc_ref[0]) + sh_ref[0]).astype(BF16)

    acc = jnp.dot(h_scr[...], w_ref[...], preferred_element_type=F32)

    def store_plain():
        o_ref[...] = acc.astype(BF16)

    def store_rope(flags):
        cos = cos_ref[...]
        sin = sin_ref[...]
        for c, roped in enumerate(flags):
            blk = acc[:, c * LANES:(c + 1) * LANES]
            if roped:
                blk = _rope_chunk(blk, cos, sin)
            o_ref[:, c * LANES:(c + 1) * LANES] = blk.astype(BF16)

    if not has_rope:
        store_plain()
    else:
        any_rope = None
        for flags, tiles in rope_plan:
            hit = functools.reduce(jnp.logical_or, [j == t for t in tiles])
            pl.when(hit)(functools.partial(store_rope, flags))
            any_rope = hit if any_rope is None else jnp.logical_or(any_rope, hit)
        pl.when(jnp.logical_not(any_rope))(store_plain)


def _inproj(x2d, norm_g, mods, w, *, rows_per_mod, mod_row0, tm, tn, rope_cols=(), rope_tabs=None):
    rows, _ = x2d.shape
    n = w.shape[1]
    assert rows % tm == 0 and n % tn == 0 and rows_per_mod % tm == 0
    tiles_per_mod = rows_per_mod // tm
    assert all(lo % LANES == 0 and hi % LANES == 0 for lo, hi in rope_cols)
    plan = {}
    for t in range(n // tn):
        flags = tuple(any(lo <= t * tn + c * LANES < hi for lo, hi in rope_cols) for c in range(tn // LANES))
        if any(flags):
            plan.setdefault(flags, []).append(t)
    rope_plan = tuple((flags, tuple(tiles)) for flags, tiles in plan.items())
    has_rope = bool(rope_plan)
    in_specs = [
        pl.BlockSpec((tm, D_MODEL), lambda i, j: (i, 0)),
        pl.BlockSpec((1, D_MODEL), lambda i, j: (0, 0)),
        pl.BlockSpec((1, 1, D_MODEL), lambda i, j: (mod_row0 + i // tiles_per_mod, 0, 0)),
        pl.BlockSpec((1, 1, D_MODEL), lambda i, j: (mod_row0 + i // tiles_per_mod, 0, 1)),
        pl.BlockSpec((D_MODEL, tn), lambda i, j: (0, j)),
    ]
    args = [x2d, norm_g.reshape(1, D_MODEL), mods, mods, w]
    if has_rope:
        assert SEQ % tm == 0
        seq_tiles = SEQ // tm
        in_specs += [pl.BlockSpec((tm, LANES), lambda i, j: (i % seq_tiles, 0))] * 2
        args += list(rope_tabs)
    vmem = (2 * tm * D_MODEL * 4 + tm * D_MODEL * 2 + 2 * D_MODEL * tn * 2 + 2 * tm * tn * 2
            + tm * tn * 4 + (4 * tm * LANES * 4 if has_rope else 0)) // MIB + 8
    return pl.pallas_call(
        functools.partial(_inproj_kernel, tn=tn, rope_plan=rope_plan),
        grid=(rows // tm, n // tn),
        in_specs=in_specs,
        out_specs=pl.BlockSpec((tm, tn), lambda i, j: (i, j)),
        out_shape=jax.ShapeDtypeStruct((rows, n), BF16),
        scratch_shapes=[pltpu.VMEM((tm, D_MODEL), BF16)],
        compiler_params=_cparams(("parallel", "arbitrary"), vmem),
        name="inproj",
    )(*args)


def _gate(o, z_blk):
    return o * _silu(z_blk.astype(F32))


def _outproj_kernel(o_ref, w_ref, x_ref, gt_ref, *rest, final):
    y = jnp.dot(o_ref[...], w_ref[...], preferred_element_type=F32)
    xn = x_ref[...] + gt_ref[0] * y
    if final:
        fg_ref, out_ref = rest
        ms = jnp.mean(xn * xn, axis=-1, keepdims=True)
        xn = xn * lax.rsqrt(ms + RMS_EPS) * fg_ref[...]
    else:
        (out_ref,) = rest
    out_ref[...] = xn


def _outproj(o2d, w_out, x2d, mods, *, rows_per_mod, mod_row0, tm, final_g=None, in_place=True):
    rows = x2d.shape[0]
    assert rows % tm == 0 and rows_per_mod % tm == 0
    tiles_per_mod = rows_per_mod // tm
    final = final_g is not None
    in_specs = [
        pl.BlockSpec((tm, D_INNER), lambda i: (i, 0)),
        pl.BlockSpec((D_INNER, D_MODEL), lambda i: (0, 0)),
        pl.BlockSpec((tm, D_MODEL), lambda i: (i, 0)),
        pl.BlockSpec((1, 1, D_MODEL), lambda i: (mod_row0 + i // tiles_per_mod, 0, 2)),
    ]
    args = [o2d, w_out, x2d, mods]
    if final:
        in_specs.append(pl.BlockSpec((1, D_MODEL), lambda i: (0, 0)))
        args.append(final_g.reshape(1, D_MODEL))
    vmem = (2 * tm * D_INNER * 2 + 2 * D_INNER * D_MODEL * 2 + 5 * tm * D_MODEL * 4) // MIB + 8
    return pl.pallas_call(
        functools.partial(_outproj_kernel, final=final),
        grid=(rows // tm,),
        in_specs=in_specs,
        out_specs=pl.BlockSpec((tm, D_MODEL), lambda i: (i, 0)),
        out_shape=jax.ShapeDtypeStruct((rows, D_MODEL), F32),
        input_output_aliases={2: 0} if in_place else {},
        compiler_params=_cparams(("parallel",), vmem),
        name="outproj",
    )(*args)


def _half_masks():
    lane = lax.broadcasted_iota(jnp.int32, (1, LANES), 1)
    return lane < HEAD_DIM, lane >= HEAD_DIM


def _nt(a, b):
    return lax.dot_general(a, b, (((1,), (1,)), ((), ())), preferred_element_type=F32)


LOG2E = math.log2(math.e)
QK_SCALE = HEAD_DIM ** -0.5 * LOG2E


def _split_heads(q, interleaved=False):
    lo, hi = _half_masks()
    if interleaved:
        lane = lax.broadcasted_iota(jnp.int32, (1, LANES), 1)
        lo = (lane % HEAD_DIM) < HEAD_DIM // 2
        hi = jnp.logical_not(lo)
    qs = q.astype(F32) * QK_SCALE
    return jnp.concatenate([jnp.where(lo, qs, 0.0), jnp.where(hi, qs, 0.0)], axis=0).astype(BF16)


def _ones_col(rows):
    lane = lax.broadcasted_iota(jnp.int32, (rows, LANES), 1)
    return jnp.where(lane == 0, 1.0, 0.0).astype(BF16)


def _diff_attn_kernel(q_ref, kl_ref, vl_ref, z_ref, kc_ref, vc_ref, lqk_ref, g_ref, o_ref, sa_scr, sb_scr,
                      *, lam_init, tq):
    nq = q_ref.shape[1] // tq
    lqk = lqk_ref[...]
    s1 = jnp.sum(lqk[0:1] * lqk[1:2], axis=-1, keepdims=True)
    s2 = jnp.sum(lqk[2:3] * lqk[3:4], axis=-1, keepdims=True)
    lam = jnp.exp(s1) - jnp.exp(s2) + lam_init
    bufs = (sa_scr, sb_scr)

    def logits(i):
        s_scr = bufs[i % 2]
        qq = _split_heads(q_ref[0, i * tq:(i + 1) * tq, :], interleaved=True)
        s_scr[:, :CTX_LEN] = _nt(qq, kc_ref[0])
        s_scr[:, CTX_LEN:] = _nt(qq, kl_ref[0])

    def finish(i):
        s = bufs[i % 2][...]
        m = jnp.max(s, axis=-1, keepdims=True)
        p = jnp.exp2(s - m)
        den = jnp.sum(p, axis=-1, keepdims=True)
        r = 1.0 / den
        pc = p * jnp.concatenate([r[:tq], -lam * r[tq:]], axis=0)
        a = (pc[:tq] + pc[tq:]).astype(BF16)
        o = (jnp.dot(a[:, :CTX_LEN], vc_ref[0], preferred_element_type=F32)
             + jnp.dot(a[:, CTX_LEN:], vl_ref[0], preferred_element_type=F32))
        ms = jnp.mean(o * o, axis=-1, keepdims=True)
        o = o * lax.rsqrt(ms + RMS_EPS) * g_ref[...] * (1.0 - lam_init)
        o_ref[0, i * tq:(i + 1) * tq, :] = _gate(o, z_ref[0, i * tq:(i + 1) * tq, :]).astype(BF16)

    logits(0)
    for i in range(nq):
        if i + 1 < nq:
            logits(i + 1)
        finish(i)


def _diff_attn(proj_lat, proj_ctx, lqk, subln_g, *, tq, lam_init):
    bsz = proj_lat.shape[0]
    kb, vb, zb = D_INNER // LANES, 2 * D_INNER // LANES, 3 * D_INNER // LANES
    nk = CTX_LEN + SEQ
    score_bytes = 2 * tq * nk * 4
    vmem = (5 * score_bytes) // MIB + 16
    return pl.pallas_call(
        functools.partial(_diff_attn_kernel, lam_init=lam_init, tq=tq),
        grid=(bsz, DIFF_HEADS),
        in_specs=[pl.BlockSpec((1, SEQ, LANES), lambda b, h: (b, 0, h)),
                  pl.BlockSpec((1, SEQ, LANES), lambda b, h: (b, 0, kb + h)),
                  pl.BlockSpec((1, SEQ, LANES), lambda b, h: (b, 0, vb + h)),
                  pl.BlockSpec((1, SEQ, LANES), lambda b, h: (b, 0, zb + h)),
                  pl.BlockSpec((1, CTX_LEN, LANES), lambda b, h: (b, 0, kb + h)),
                  pl.BlockSpec((1, CTX_LEN, LANES), lambda b, h: (b, 0, vb + h)),
                  pl.BlockSpec((4, HEAD_DIM), lambda b, h: (0, 0)),
                  pl.BlockSpec((1, LANES), lambda b, h: (0, 0))],
        out_specs=pl.BlockSpec((1, SEQ, LANES), lambda b, h: (b, 0, h)),
        out_shape=jax.ShapeDtypeStruct((bsz, SEQ, D_INNER), BF16),
        scratch_shapes=[pltpu.VMEM((2 * tq, nk), F32), pltpu.VMEM((2 * tq, nk), F32)],
        compiler_params=_cparams(("parallel", "parallel"), vmem),
        name="diff_attn",
    )(proj_lat, proj_lat, proj_lat, proj_lat, proj_ctx, proj_ctx, lqk, subln_g.reshape(1, LANES))


CTX_HB = 4


def _ctx_attn_kernel(*refs, diff, lam_init):
    if diff:
        q_ref, k_ref, v_ref, z_ref, lqk_ref, g_ref, o_ref = refs
        lqk = lqk_ref[...]
        lam = (jnp.exp(jnp.sum(lqk[0:1] * lqk[1:2], axis=-1, keepdims=True))
               - jnp.exp(jnp.sum(lqk[2:3] * lqk[3:4], axis=-1, keepdims=True)) + lam_init)
    else:
        q_ref, k_ref, v_ref, z_ref, o_ref = refs
    t = q_ref.shape[1]
    lo, _ = _half_masks()
    for hb in range(CTX_HB):
        sl = slice(hb * LANES, (hb + 1) * LANES)
        qq = _split_heads(q_ref[0, :, sl], interleaved=diff)
        s = _nt(qq, k_ref[0, :, sl])
        p = jnp.exp2(s - jnp.max(s, axis=-1, keepdims=True))
        den = jnp.sum(p, axis=-1, keepdims=True)
        if diff:
            a = (p[:t] - (lam * den[:t] / den[t:]) * p[t:]).astype(BF16)
            o = jnp.dot(a, v_ref[0, :, sl], preferred_element_type=F32) / den[:t]
            ms = jnp.mean(o * o, axis=-1, keepdims=True)
            o = o * lax.rsqrt(ms + RMS_EPS) * g_ref[...] * (1.0 - lam_init)
        else:
            o = jnp.dot(p.astype(BF16), v_ref[0, :, sl], preferred_element_type=F32) / den
            o = jnp.where(lo, o[:t], o[t:])
        o_ref[0, :, sl] = _gate(o, z_ref[0, :, sl]).astype(BF16)


def _ctx_attn(proj_ctx, lqk=None, subln_g=None, lam_init=0.0):
    bsz = proj_ctx.shape[0]
    diff = lqk is not None
    w = CTX_HB * LANES
    kb, vb, zb = D_INNER // w, 2 * D_INNER // w, 3 * D_INNER // w
    in_specs = [pl.BlockSpec((1, CTX_LEN, w), lambda b, h: (b, 0, h)),
                pl.BlockSpec((1, CTX_LEN, w), lambda b, h: (b, 0, kb + h)),
                pl.BlockSpec((1, CTX_LEN, w), lambda b, h: (b, 0, vb + h)),
                pl.BlockSpec((1, CTX_LEN, w), lambda b, h: (b, 0, zb + h))]
    args = [proj_ctx, proj_ctx, proj_ctx, proj_ctx]
    if diff:
        in_specs += [pl.BlockSpec((4, HEAD_DIM), lambda b, h: (0, 0)), pl.BlockSpec((1, LANES), lambda b, h: (0, 0))]
        args += [lqk, subln_g.reshape(1, LANES)]
    return pl.pallas_call(
        functools.partial(_ctx_attn_kernel, diff=diff, lam_init=lam_init),
        grid=(bsz, D_INNER // w),
        in_specs=in_specs,
        out_specs=pl.BlockSpec((1, CTX_LEN, w), lambda b, h: (b, 0, h)),
        out_shape=jax.ShapeDtypeStruct((bsz, CTX_LEN, D_INNER), BF16),
        compiler_params=_cparams(("parallel", "parallel"), 32),
        name="ctx_attn",
    )(*args)


NA_ROWS = SEQ // GRID_W
NA_BLK_ROWS = 4
NA_BLK_Q = NA_BLK_ROWS * GRID_W
NA_NBLK = NA_ROWS // NA_BLK_ROWS
NA_WIN_R = NA_BLK_ROWS + NA_WIN_ROWS
NA_KWIN = NA_WIN_R * GRID_W
NA_CTX_CHUNK = 512
NA_MAX_LANE = LANES + 1


def _na_win_start(i):
    lo, hi = 0, NA_ROWS - NA_WIN_R
    start = i * NA_BLK_ROWS - NA_WIN_ROWS // 2
    return np.clip(start, lo, hi) if isinstance(i, (int, np.integer)) else jnp.clip(start, lo, hi)


NA_NDR = 2 * NA_WIN_ROWS - 1


def _na_bias_plan():
    plan = np.full((3, NA_BLK_ROWS, NA_WIN_R), -1, np.int32)
    for pat, i in enumerate((0, 1, NA_NBLK - 1)):
        ws = int(_na_win_start(i))
        for qr in range(NA_BLK_ROWS):
            r = i * NA_BLK_ROWS + qr
            rs = min(max(r - NA_WIN_ROWS // 2, 0), NA_ROWS - NA_WIN_ROWS)
            for a in range(NA_WIN_R):
                if rs <= ws + a < rs + NA_WIN_ROWS:
                    plan[pat, qr, a] = ws + a - r + NA_WIN_ROWS - 1
    return plan


def _na_bias_tiles(rpb):
    c = np.arange(GRID_W)
    cs = np.clip(c - NA_WIN_COLS // 2, 0, GRID_W - NA_WIN_COLS)
    kc = np.arange(GRID_W)
    col_ok = (kc[None, :] >= cs[:, None]) & (kc[None, :] < cs[:, None] + NA_WIN_COLS)
    dc_idx = np.clip(kc[None, :] - c[:, None], -(NA_WIN_COLS - 1), NA_WIN_COLS - 1) + NA_WIN_COLS - 1
    return jnp.where(col_ok[None, None], rpb[:, :, dc_idx] * LOG2E, NEG_BIG).astype(F32)


def _na_attn_kernel(q_ref, k_ref, v_ref, z_ref, kc_ref, vc_ref, t1_ref, o_ref, oc_scr, bias_ref):
    lo, _ = _half_masks()

    @pl.when(pl.program_id(1) == 0)
    def _():
        plan = _na_bias_plan()
        masked = jnp.full((GRID_W, GRID_W), NEG_BIG, F32)
        for hl in range(2):
            for pat in range(3):
                for qr in range(NA_BLK_ROWS):
                    tiles = [t1_ref[hl, int(d)] if d >= 0 else masked for d in plan[pat, qr]]
                    bias_ref[hl, pat, qr * GRID_W:(qr + 1) * GRID_W, :] = jnp.concatenate(tiles, axis=1)
    kctx = kc_ref[0]
    vctx = jnp.concatenate([vc_ref[0], _ones_col(CTX_LEN)], axis=1)
    lane2 = lax.broadcasted_iota(jnp.int32, (1, 2 * LANES), 1)

    def ctx_chunk(j, carry):
        r0 = pl.multiple_of(jnp.asarray(j, jnp.int32) * NA_CTX_CHUNK, NA_CTX_CHUNK)
        qq = _split_heads(q_ref[0, pl.ds(r0, NA_CTX_CHUNK), :])
        s = _nt(qq, kctx)
        m = jnp.max(s, axis=-1, keepdims=True)
        oc = jnp.dot(jnp.exp2(s - m).astype(BF16), vctx, preferred_element_type=F32)
        oc = jnp.where(lane2 == NA_MAX_LANE, m, oc)
        oc_scr[0, pl.ds(r0, NA_CTX_CHUNK), :] = oc[:NA_CTX_CHUNK]
        oc_scr[1, pl.ds(r0, NA_CTX_CHUNK), :] = oc[NA_CTX_CHUNK:]
        return carry

    lax.fori_loop(0, SEQ // NA_CTX_CHUNK, ctx_chunk, 0, unroll=True)

    def block(i, carry):
        i = jnp.asarray(i, jnp.int32)
        pat = jnp.where(i == 0, 0, jnp.where(i == NA_NBLK - 1, 2, 1))
        t0 = pl.multiple_of(i * NA_BLK_Q, NA_BLK_Q)
        s0 = pl.multiple_of(_na_win_start(i) * GRID_W, NA_BLK_Q)
        qq = _split_heads(q_ref[0, pl.ds(t0, NA_BLK_Q), :])
        kw = k_ref[0, pl.ds(s0, NA_KWIN), :]
        vw = jnp.concatenate([v_ref[0, pl.ds(s0, NA_KWIN), :], _ones_col(NA_KWIN)], axis=1)
        s = _nt(qq, kw) + jnp.concatenate([bias_ref[0, pat], bias_ref[1, pat]], axis=0)
        oc = jnp.concatenate([oc_scr[0, pl.ds(t0, NA_BLK_Q), :], oc_scr[1, pl.ds(t0, NA_BLK_Q), :]], axis=0)
        mc = oc[:, NA_MAX_LANE:NA_MAX_LANE + 1]
        m = jnp.maximum(jnp.max(s, axis=-1, keepdims=True), mc)
        tot = (jnp.dot(jnp.exp2(s - m).astype(BF16), vw, preferred_element_type=F32)
               + jnp.exp2(mc - m) * oc)
        out = tot[:, :LANES] / tot[:, LANES:LANES + 1]
        o = jnp.where(lo, out[:NA_BLK_Q], out[NA_BLK_Q:])
        o_ref[0, pl.ds(t0, NA_BLK_Q), :] = _gate(o, z_ref[0, pl.ds(t0, NA_BLK_Q), :]).astype(BF16)
        return carry

    lax.fori_loop(0, NA_NBLK, block, 0, unroll=True)


def _na_attn(proj_lat, proj_ctx, bias_tiles):
    bsz = proj_lat.shape[0]
    pairs = NA_HEADS // 2
    kb, vb, zb = D_INNER // LANES, 2 * D_INNER // LANES, 3 * D_INNER // LANES
    return pl.pallas_call(
        _na_attn_kernel,
        grid=(pairs, bsz),
        in_specs=[pl.BlockSpec((1, SEQ, LANES), lambda h, b: (b, 0, h)),
                  pl.BlockSpec((1, SEQ, LANES), lambda h, b: (b, 0, kb + h)),
                  pl.BlockSpec((1, SEQ, LANES), lambda h, b: (b, 0, vb + h)),
                  pl.BlockSpec((1, SEQ, LANES), lambda h, b: (b, 0, zb + h)),
                  pl.BlockSpec((1, CTX_LEN, LANES), lambda h, b: (b, 0, kb + h)),
                  pl.BlockSpec((1, CTX_LEN, LANES), lambda h, b: (b, 0, vb + h)),
                  pl.BlockSpec((2, NA_NDR, GRID_W, GRID_W), lambda h, b: (h, 0, 0, 0))],
        out_specs=pl.BlockSpec((1, SEQ, LANES), lambda h, b: (b, 0, h)),
        out_shape=jax.ShapeDtypeStruct((bsz, SEQ, D_INNER), BF16),
        scratch_shapes=[pltpu.VMEM((2, SEQ, 2 * LANES), F32), pltpu.VMEM((2, 3, NA_BLK_Q, NA_KWIN), F32)],
        compiler_params=_cparams(("parallel", "arbitrary"), 48),
        name="na_attn",
    )(proj_lat, proj_lat, proj_lat, proj_lat, proj_ctx, proj_ctx, bias_tiles)


SWA_TQ = 256
SWA_BAND = SWA_TQ + 2 * SWA_WINDOW
SWA_K_COL = 2 * D_INNER
SWA_V_COL = SWA_K_COL + SWA_KV_HEADS * LANES
SWA_N = SWA_V_COL + SWA_KV_HEADS * LANES


SWA_GROUP = SWA_Q_HEADS // SWA_KV_HEADS
SWA_GW = SWA_GROUP * HEAD_DIM


def _swa_attn_kernel(q_ref, k_ref, v_ref, z_ref, kc_ref, vc_ref, sink_ref, o_ref):
    lo, _ = _half_masks()
    kvh = pl.program_id(1)
    n_pair = SWA_GROUP // 2
    kctx = kc_ref[0]
    vctx = jnp.concatenate([vc_ref[0], _ones_col(CTX_LEN)], axis=1)
    sinks = sink_ref[pl.ds(pl.multiple_of(kvh * SWA_GROUP, SWA_GROUP), SWA_GROUP), :][:, 0:1] * LOG2E
    sink = jnp.concatenate([jnp.broadcast_to(sinks[g:g + 1], (SWA_TQ, 1)) for g in range(SWA_GROUP)], axis=0)

    def block(i, carry):
        i = jnp.asarray(i, jnp.int32)
        t0 = pl.multiple_of(i * SWA_TQ, SWA_TQ)
        s0 = pl.multiple_of(jnp.clip(i * SWA_TQ - SWA_WINDOW, 0, SEQ - SWA_BAND), SWA_WINDOW)
        q = q_ref[0, pl.ds(t0, SWA_TQ), :]
        qq = jnp.concatenate([_split_heads(q[:, p * LANES:(p + 1) * LANES], interleaved=True)
                              for p in range(n_pair)], axis=0)
        kw = k_ref[0, pl.ds(s0, SWA_BAND), :]
        vw = jnp.concatenate([v_ref[0, pl.ds(s0, SWA_BAND), :], _ones_col(SWA_BAND)], axis=1)
        tpos = t0 + lax.broadcasted_iota(jnp.int32, (SWA_TQ, SWA_BAND), 0)
        spos = s0 + lax.broadcasted_iota(jnp.int32, (SWA_TQ, SWA_BAND), 1)
        bias = jnp.where(jnp.abs(tpos - spos) <= SWA_WINDOW, 0.0, NEG_BIG)
        s_lat = _nt(qq, kw) + jnp.concatenate([bias] * SWA_GROUP, axis=0)
        s_ctx = _nt(qq, kctx)
        m = jnp.maximum(jnp.maximum(jnp.max(s_lat, axis=-1, keepdims=True),
                                    jnp.max(s_ctx, axis=-1, keepdims=True)), sink)
        tot = (jnp.dot(jnp.exp2(s_lat - m).astype(BF16), vw, preferred_element_type=F32)
               + jnp.dot(jnp.exp2(s_ctx - m).astype(BF16), vctx, preferred_element_type=F32))
        out = tot[:, :LANES] / (tot[:, LANES:LANES + 1] + jnp.exp2(sink - m))
        pairs = [jnp.where(lo, out[(2 * p) * SWA_TQ:(2 * p + 1) * SWA_TQ],
                           out[(2 * p + 1) * SWA_TQ:(2 * p + 2) * SWA_TQ]) for p in range(n_pair)]
        o = jnp.concatenate(pairs, axis=1)
        o_ref[0, pl.ds(t0, SWA_TQ), :] = _gate(o, z_ref[0, pl.ds(t0, SWA_TQ), :]).astype(BF16)
        return carry

    lax.fori_loop(0, SEQ // SWA_TQ, block, 0, unroll=True)


def _swa_attn(proj_lat, proj_ctx, sink_rows):
    bsz = proj_lat.shape[0]
    kb, vb, zb = SWA_K_COL // LANES, SWA_V_COL // LANES, D_INNER // SWA_GW
    return pl.pallas_call(
        _swa_attn_kernel,
        grid=(bsz, SWA_KV_HEADS),
        in_specs=[pl.BlockSpec((1, SEQ, SWA_GW), lambda b, h: (b, 0, h)),
                  pl.BlockSpec((1, SEQ, LANES), lambda b, h: (b, 0, kb + h)),
                  pl.BlockSpec((1, SEQ, LANES), lambda b, h: (b, 0, vb + h)),
                  pl.BlockSpec((1, SEQ, SWA_GW), lambda b, h: (b, 0, zb + h)),
                  pl.BlockSpec((1, CTX_LEN, LANES), lambda b, h: (b, 0, kb + h)),
                  pl.BlockSpec((1, CTX_LEN, LANES), lambda b, h: (b, 0, vb + h)),
                  pl.BlockSpec((SWA_Q_HEADS, LANES), lambda b, h: (0, 0))],
        out_specs=pl.BlockSpec((1, SEQ, SWA_GW), lambda b, h: (b, 0, h)),
        out_shape=jax.ShapeDtypeStruct((bsz, SEQ, D_INNER), BF16),
        compiler_params=_cparams(("parallel", "parallel"), 48),
        name="swa_attn",
    )(proj_lat, proj_lat, proj_lat, proj_lat, proj_ctx, proj_ctx, sink_rows)


GATE_LANES = 128


def _mlstm_prep_kernel(xm_ref, cw_ref, cb_ref, bdq_ref, bdk_ref, bdv_ref, wg_ref, bg_ref,
                       q_ref, kt_ref, v_ref, xc_ref, g_ref):
    h = pl.program_id(1)
    t = xm_ref.shape[1]
    xm_b = xm_ref[0]
    xm = xm_b.astype(F32)

    tpos = lax.broadcasted_iota(jnp.int32, (t, 1), 0)
    conv = jnp.zeros_like(xm) + cb_ref[...]
    for j in range(MLSTM_CONV):
        d = j - MLSTM_CONV // 2
        sh = pltpu.roll(xm, (-d) % t, axis=0) if d else xm
        ok = (tpos + d >= 0) & (tpos + d < t)
        conv = conv + jnp.where(ok, sh, 0.0) * cw_ref[j:j + 1, :]
    xc = _silu(conv)
    xc_b = xc.astype(BF16)
    q = jnp.dot(xc_b, bdq_ref[0], preferred_element_type=F32)
    k = jnp.dot(xc_b, bdk_ref[0], preferred_element_type=F32)
    v = jnp.dot(xm_b, bdv_ref[0], preferred_element_type=F32)
    q_b, k_b, v_b = q.astype(BF16), k.astype(BF16), v.astype(BF16)
    q_ref[0] = q_b
    v_ref[0] = v_b
    xc_ref[0] = xc_b
    ks = k * (MLSTM_HEAD_DIM ** -0.5)
    for c in range(t // SCAN_CHUNK):
        kt_ref[0, 0, c] = ks[c * SCAN_CHUNK:(c + 1) * SCAN_CHUNK, :].T.astype(BF16)
    part = (jnp.dot(q_b, wg_ref[0, 0], preferred_element_type=F32)
            + jnp.dot(k_b, wg_ref[1, 0], preferred_element_type=F32)
            + jnp.dot(v_b, wg_ref[2, 0], preferred_element_type=F32))

    @pl.when(h == 0)
    def _():
        g_ref[0] = part + bg_ref[...]

    @pl.when(h != 0)
    def _():
        g_ref[0] = g_ref[0] + part


def _mlstm_prep(proj3, conv_w, conv_b, bdq, bdk, bdv, wg, bg):
    bsz, t, _ = proj3.shape
    hd = MLSTM_HEAD_DIM
    nc = t // SCAN_CHUNK
    seq_spec = pl.BlockSpec((1, t, hd), lambda b, h: (b, 0, h))
    return pl.pallas_call(
        _mlstm_prep_kernel,
        grid=(bsz, MLSTM_HEADS),
        in_specs=[seq_spec,
                  pl.BlockSpec((MLSTM_CONV, hd), lambda b, h: (0, h)),
                  pl.BlockSpec((1, hd), lambda b, h: (0, h)),
                  pl.BlockSpec((1, hd, hd), lambda b, h: (h, 0, 0)),
                  pl.BlockSpec((1, hd, hd), lambda b, h: (h, 0, 0)),
                  pl.BlockSpec((1, hd, hd), lambda b, h: (h, 0, 0)),
                  pl.BlockSpec((3, 1, hd, GATE_LANES), lambda b, h: (0, h, 0, 0)),
                  pl.BlockSpec((1, GATE_LANES), lambda b, h: (0, 0))],
        out_specs=[seq_spec,
                   pl.BlockSpec((1, 1, nc, hd, SCAN_CHUNK), lambda b, h: (b, h, 0, 0, 0)),
                   seq_spec, seq_spec,
                   pl.BlockSpec((1, t, GATE_LANES), lambda b, h: (b, 0, 0))],
        out_shape=[jax.ShapeDtypeStruct((bsz, t, D_INNER), BF16),
                   jax.ShapeDtypeStruct((bsz, MLSTM_HEADS, nc, hd, SCAN_CHUNK), BF16),
                   jax.ShapeDtypeStruct((bsz, t, D_INNER), BF16),
                   jax.ShapeDtypeStruct((bsz, t, D_INNER), BF16),
                   jax.ShapeDtypeStruct((bsz, t, GATE_LANES), F32)],
        compiler_params=_cparams(("parallel", "arbitrary"), 48),
        name="mlstm_prep",
    )(proj3, conv_w, conv_b.reshape(1, D_INNER), bdq, bdk, bdv, wg, bg)


def _lane_scan(x, op, fill, reverse):
    lane = lax.broadcasted_iota(jnp.int32, x.shape, 1)
    sh = 1
    while sh < LANES:
        if reverse:
            y = jnp.where(lane < LANES - sh, pltpu.roll(x, LANES - sh, axis=1), fill)
        else:
            y = jnp.where(lane >= sh, pltpu.roll(x, sh, axis=1), fill)
        x = op(x, y)
        sh *= 2
    return x


GP_ROWS = 8


def _mlstm_gates_kernel(gc_ref, gl_ref, oc_ref, ol_ref):
    nh = MLSTM_HEADS
    ppc = SCAN_CHUNK // LANES
    gct = gc_ref[0].T
    glt = gl_ref[0].T
    n_c = gc_ref.shape[1] // SCAN_CHUNK
    n_l = gl_ref.shape[1] // SCAN_CHUNK
    zeros = jnp.zeros((nh, SCAN_CHUNK), F32)
    pieces = [(gct, j) for j in range(n_c * ppc)] + [(glt, j) for j in range(n_l * ppc)]
    chunks = [(oc_ref, c) for c in range(n_c)] + [(ol_ref, c) for c in range(n_l)]
    n_all = len(chunks)

    def stack(row0):
        return jnp.concatenate([src[row0:row0 + nh, j * LANES:(j + 1) * LANES] for src, j in pieces], axis=0)

    def piece(x, k):
        return x[k * nh:(k + 1) * nh]

    def earlier(k, rev):
        c = k // ppc
        return range(k + 1, (c + 1) * ppc) if rev else range(c * ppc, k)

    for d in range(2):
        rev = d == 1
        edge = (lambda x: x[:, 0:1]) if rev else (lambda x: x[:, LANES - 1:LANES])
        ig = stack(2 * d * nh)
        fg = stack((2 * d + 1) * nh)
        lf = jnp.minimum(fg, 0.0) - jnp.log1p(jnp.exp(-jnp.abs(fg)))
        bp = _lane_scan(lf, jnp.add, 0.0, rev)
        b = jnp.concatenate([piece(bp, k) + sum(piece(edge(bp), o) for o in earlier(k, rev))
                             for k in range(len(pieces))], axis=0)
        cmp_ = _lane_scan(ig - b, jnp.maximum, -jnp.inf, rev)
        cm = jnp.concatenate([functools.reduce(jnp.maximum, [piece(edge(cmp_), o) for o in earlier(k, rev)],
                                               piece(cmp_, k)) for k in range(len(pieces))], axis=0)
        last = (lambda c: c * ppc) if rev else (lambda c: c * ppc + ppc - 1)
        b_end_c = [piece(edge(b), last(c)) for c in range(n_all)]
        b_end = jnp.concatenate([b_end_c[k // ppc] for k in range(len(pieces))], axis=0)
        g = b_end - b + ig
        gmax_p = jnp.max(g, axis=-1, keepdims=True)
        gmax_c = [functools.reduce(jnp.maximum, [piece(gmax_p, c * ppc + p) for p in range(ppc)])
                  for c in range(n_all)]
        order = list(range(n_all))
        if rev:
            order = list(reversed(range(n_c))) + list(reversed(range(n_c, n_all)))
        m = jnp.zeros((nh, 1), F32)
        m_in, m_out = [None] * n_all, [None] * n_all
        for c in order:
            m_in[c] = m
            m = jnp.maximum(b_end_c[c] + m, gmax_c[c])
            m_out[c] = m
        m_in = jnp.concatenate([m_in[k // ppc] for k in range(len(pieces))], axis=0)
        m_out = jnp.concatenate([m_out[k // ppc] for k in range(len(pieces))], axis=0)
        mx = jnp.maximum(cm, m_in)
        rows = (-mx, b - ig, jnp.exp(m_in - mx), jnp.exp(-(b + mx)), jnp.exp(g - m_out),
                jnp.broadcast_to(jnp.exp(b_end + m_in - m_out), b.shape))
        for k, (dst, c) in enumerate(chunks):
            for r, val in enumerate(rows):
                dst[0, d, c, r] = jnp.concatenate([piece(val, k * ppc + p) for p in range(ppc)], axis=1)
            dst[0, d, c, 6] = zeros
            dst[0, d, c, 7] = zeros


def _mlstm_gates(gates_ctx, gates_lat):
    bsz = gates_lat.shape[0]
    n_c, n_l = CTX_LEN // SCAN_CHUNK, SEQ // SCAN_CHUNK

    def out(n):
        shape = (bsz, 2, n, GP_ROWS, MLSTM_HEADS, SCAN_CHUNK)
        return (pl.BlockSpec((1,) + shape[1:], lambda b: (b, 0, 0, 0, 0, 0)), jax.ShapeDtypeStruct(shape, F32))

    (spec_c, shape_c), (spec_l, shape_l) = out(n_c), out(n_l)
    return pl.pallas_call(
        _mlstm_gates_kernel,
        grid=(bsz,),
        in_specs=[pl.BlockSpec((1, CTX_LEN, GATE_LANES), lambda b: (b, 0, 0)),
                  pl.BlockSpec((1, SEQ, GATE_LANES), lambda b: (b, 0, 0))],
        out_specs=[spec_c, spec_l],
        out_shape=[shape_c, shape_l],
        compiler_params=_cparams(("parallel",), 32),
        name="mlstm_gates",
    )(gates_ctx, gates_lat)


def _mlstm_scan_kernel(qc_ref, ktc_ref, vc_ref, xcc_ref, gpc_ref,
                       ql_ref, ktl_ref, vl_ref, xcl_ref, gpl_ref, ng_ref, sk_ref, zc_ref, zl_ref,
                       oc_ref, ol_ref, hc_scr, hl_scr, st_scr):
    L = SCAN_CHUNK
    hd = MLSTM_HEAD_DIM
    n_c = qc_ref.shape[1] // L
    n_l = ql_ref.shape[1] // L
    lane = lax.broadcasted_iota(jnp.int32, (L, LANES), 1)
    ones_col = jnp.where(lane == 0, 1.0, 0.0).astype(BF16)
    tt = lax.broadcasted_iota(jnp.int32, (L, L), 0)
    ss = lax.broadcasted_iota(jnp.int32, (L, L), 1)
    masks = (ss <= tt, ss >= tt)

    hc_scr[...] = jnp.zeros_like(hc_scr)
    hl_scr[...] = jnp.zeros_like(hl_scr)
    st_scr[...] = jnp.zeros_like(st_scr)

    def chunk(d, q_ref, kt_ref, v_ref, gp_ref, h_scr, c):
        rows = gp_ref[0, d, 0, c]
        cols = rows.T
        a_col, wst, einv = cols[:, 0:1], cols[:, 2:3], cols[:, 3:4]
        c_row, ws_row, decay = rows[1:2, :], rows[4:5, :], rows[5:6, 0:1]
        r0 = pl.multiple_of(c * L, L)
        q = q_ref[0, pl.ds(r0, L), :]
        kt = kt_ref[0, 0, c]
        vx = jnp.concatenate([v_ref[0, pl.ds(r0, L), :], ones_col], axis=1)
        w = jnp.exp(jnp.where(masks[d], a_col - c_row, -jnp.inf))
        s = (jnp.dot(q, kt, preferred_element_type=F32) * w).astype(BF16)
        kw = (kt.astype(F32) * ws_row).astype(BF16)
        both = jnp.dot(jnp.concatenate([s, kw], axis=0), vx, preferred_element_type=F32)
        state = st_scr[d]
        tot = wst * jnp.dot(q, state.astype(BF16), preferred_element_type=F32) + both[:L]
        den = tot[:, hd:hd + 1]
        h = tot[:, :hd] / jnp.maximum(jnp.abs(den), einv)
        h_scr[pl.ds(r0, L), :] = h_scr[pl.ds(r0, L), :] + h
        st_scr[d] = decay * state + both[L:]

    for j in range(n_c):
        chunk(0, qc_ref, ktc_ref, vc_ref, gpc_ref, hc_scr, j)
        chunk(1, qc_ref, ktc_ref, vc_ref, gpc_ref, hc_scr, n_c - 1 - j)

    def lat_step(j, carry):
        chunk(0, ql_ref, ktl_ref, vl_ref, gpl_ref, hl_scr, j)
        chunk(1, ql_ref, ktl_ref, vl_ref, gpl_ref, hl_scr, n_l - 1 - j)
        return carry

    lax.fori_loop(0, n_l, lat_step, 0, unroll=True)

    ng = ng_ref[...]
    sk = sk_ref[...]

    def finish(h_scr, xc_ref, z_ref, o_ref, r0, n):
        h = h_scr[pl.ds(r0, n), :]
        mu = jnp.mean(h, axis=-1, keepdims=True)
        hcen = h - mu
        hn = hcen * lax.rsqrt(jnp.mean(hcen * hcen, axis=-1, keepdims=True) + RMS_EPS)
        o = hn * ng + sk * xc_ref[0, pl.ds(r0, n), :].astype(F32)
        o_ref[0, pl.ds(r0, n), :] = _gate(o, z_ref[0, pl.ds(r0, n), :]).astype(BF16)

    finish(hc_scr, xcc_ref, zc_ref, oc_ref, 0, CTX_LEN)

    def fin_step(i, carry):
        finish(hl_scr, xcl_ref, zl_ref, ol_ref, pl.multiple_of(jnp.asarray(i, jnp.int32) * 256, 256), 256)
        return carry

    lax.fori_loop(0, SEQ // 256, fin_step, 0, unroll=2)


def _mlstm_scan(prep_c, gp_c, prep_l, gp_l, norm_g, skip, proj_c, proj_l):
    bsz = prep_l[0].shape[0]
    hd = MLSTM_HEAD_DIM

    def z_spec(t):
        return pl.BlockSpec((1, t, hd), lambda b, h: (b, 0, MLSTM_HEADS + h))

    def seq_specs(t):
        nc = t // SCAN_CHUNK
        seq = pl.BlockSpec((1, t, hd), lambda b, h: (b, 0, h))
        return [seq,
                pl.BlockSpec((1, 1, nc, hd, SCAN_CHUNK), lambda b, h: (b, h, 0, 0, 0)),
                seq, seq,
                pl.BlockSpec((1, 2, 1, nc, GP_ROWS, SCAN_CHUNK), lambda b, h: (b, 0, h, 0, 0, 0))]

    vec = pl.BlockSpec((1, hd), lambda b, h: (0, h))
    return pl.pallas_call(
        _mlstm_scan_kernel,
        grid=(bsz, MLSTM_HEADS),
        in_specs=seq_specs(CTX_LEN) + seq_specs(SEQ) + [vec, vec, z_spec(CTX_LEN), z_spec(SEQ)],
        out_specs=[pl.BlockSpec((1, CTX_LEN, hd), lambda b, h: (b, 0, h)),
                   pl.BlockSpec((1, SEQ, hd), lambda b, h: (b, 0, h))],
        out_shape=[jax.ShapeDtypeStruct((bsz, CTX_LEN, D_INNER), BF16),
                   jax.ShapeDtypeStruct((bsz, SEQ, D_INNER), BF16)],
        scratch_shapes=[pltpu.VMEM((CTX_LEN, hd), F32), pltpu.VMEM((SEQ, hd), F32),
                        pltpu.VMEM((2, hd, hd + LANES), F32)],
        compiler_params=_cparams(("parallel", "parallel"), 40),
        name="mlstm_scan",
    )(*prep_c, gp_c, *prep_l, gp_l, norm_g.reshape(1, D_INNER), skip.reshape(1, D_INNER), proj_c, proj_l)


def _blockdiag_dense(w):
    per_head = MLSTM_HEAD_DIM // MLSTM_QKV_BLOCK
    wh = w.reshape(MLSTM_HEADS, per_head, MLSTM_QKV_BLOCK, MLSTM_QKV_BLOCK)
    eye = jnp.eye(per_head, dtype=w.dtype)
    dense = wh[:, :, :, None, :] * eye[None, :, None, :, None]
    return dense.reshape(MLSTM_HEADS, MLSTM_HEAD_DIM, MLSTM_HEAD_DIM).astype(BF16)


def _mlstm_mixer(proj_lat, proj_ctx, conv_w, conv_b, wq, wk, wv, w_gate, b_gate, norm_g, skip):
    n_gate = 4 * MLSTM_HEADS
    wg = jnp.pad(w_gate, ((0, 0), (0, GATE_LANES - n_gate))).astype(BF16)
    wg = wg.reshape(3, MLSTM_HEADS, MLSTM_HEAD_DIM, GATE_LANES)
    bg = jnp.pad(b_gate, (0, GATE_LANES - n_gate)).reshape(1, GATE_LANES)
    bds = (_blockdiag_dense(wq), _blockdiag_dense(wk), _blockdiag_dense(wv))
    *prep_c, gates_c = _mlstm_prep(proj_ctx, conv_w, conv_b, *bds, wg, bg)
    *prep_l, gates_l = _mlstm_prep(proj_lat, conv_w, conv_b, *bds, wg, bg)
    gp_c, gp_l = _mlstm_gates(gates_c, gates_l)
    gp_c = jnp.transpose(gp_c, (0, 1, 4, 2, 3, 5))
    gp_l = jnp.transpose(gp_l, (0, 1, 4, 2, 3, 5))
    return _mlstm_scan(prep_c, gp_c, prep_l, gp_l, norm_g, skip, proj_ctx, proj_lat)


def kernel(x, c, ctx, c_ctx, norm_g, w_ada, b_ada, final_g, mlstm_w_in, mlstm_conv_w, mlstm_conv_b, mlstm_wq, mlstm_wk, mlstm_wv, mlstm_w_gate, mlstm_b_gate, mlstm_norm_g, mlstm_skip, mlstm_w_out, diff_w_in, diff_lq1, diff_lk1, diff_lq2, diff_lk2, diff_subln_g, diff_w_out, na_w_in, na_rpb, na_w_out, swa_w_in, swa_sink, swa_w_out):
    bsz = x.shape[0]
    ctx_row = bsz
    mod_rows = -(-(bsz + 1) // 8) * 8
    cvec = jnp.zeros((mod_rows, D_MODEL), F32).at[:bsz].set(c).at[ctx_row].set(c_ctx)
    mods_all = _adaln(cvec, w_ada, b_ada)
    rope_tabs = _rope_tables()

    xl = x.reshape(bsz * SEQ, D_MODEL)
    xc = ctx.reshape(bsz * CTX_LEN, D_MODEL)
    lat = dict(rows_per_mod=SEQ, mod_row0=0)
    con = dict(rows_per_mod=bsz * CTX_LEN, mod_row0=ctx_row)
    tm_ctx = min(1024, bsz * CTX_LEN)

    def in_both(i, w, tn, rope_cols=()):
        mods = mods_all[i].reshape(mod_rows, 1, 3 * D_MODEL)
        wb = w.astype(BF16)
        pl_ = _inproj(xl, norm_g[i], mods, wb, tm=SEQ, tn=tn, rope_cols=rope_cols, rope_tabs=rope_tabs, **lat)
        pc_ = _inproj(xc, norm_g[i], mods, wb, tm=tm_ctx, tn=tn, **con)
        return mods, pl_, pc_

    def out_both(mods, o_lat, o_ctx, w_out, need_ctx, final=None, in_place=True):
        wb = w_out.astype(BF16)
        new_l = _outproj(o_lat.reshape(bsz * SEQ, D_INNER), wb, xl, mods, tm=1024,
                         final_g=final, in_place=in_place, **lat)
        new_c = xc
        if need_ctx:
            new_c = _outproj(o_ctx.reshape(bsz * CTX_LEN, D_INNER), wb, xc, mods, tm=tm_ctx, **con)
        return new_l, new_c

    mods, p_l, p_c = in_both(0, mlstm_w_in[0], 1024)
    o_c, o_l = _mlstm_mixer(p_l.reshape(bsz, SEQ, -1), p_c.reshape(bsz, CTX_LEN, -1), mlstm_conv_w[0],
                            mlstm_conv_b[0], mlstm_wq[0], mlstm_wk[0], mlstm_wv[0], mlstm_w_gate[0],
                            mlstm_b_gate[0], mlstm_norm_g[0], mlstm_skip[0])
    xl, xc = out_both(mods, o_l, o_c, mlstm_w_out[0], True, in_place=False)

    wd = diff_w_in[0]
    wd = jnp.concatenate([_pair_interleave(wd[:, :D_INNER]), _pair_interleave(wd[:, D_INNER:2 * D_INNER]),
                          wd[:, 2 * D_INNER:]], axis=1)
    mods, p_l, p_c = in_both(1, wd, 1024, rope_cols=((0, 2 * D_INNER),))
    p_l3, p_c3 = p_l.reshape(bsz, SEQ, -1), p_c.reshape(bsz, CTX_LEN, -1)
    lam_init = 0.8 - 0.6 * math.exp(-0.3 * 1)
    lqk = jnp.stack([diff_lq1[0], diff_lk1[0], diff_lq2[0], diff_lk2[0]])
    o_l = _diff_attn(p_l3, p_c3, lqk, diff_subln_g[0], tq=512, lam_init=lam_init)
    o_c = _ctx_attn(p_c3, lqk, diff_subln_g[0], lam_init)
    xl, xc = out_both(mods, o_l, o_c, diff_w_out[0], True)

    mods, p_l, p_c = in_both(2, na_w_in[0], 1024)
    p_l3, p_c3 = p_l.reshape(bsz, SEQ, -1), p_c.reshape(bsz, CTX_LEN, -1)
    o_l = _na_attn(p_l3, p_c3, _na_bias_tiles(na_rpb[0]))
    o_c = _ctx_attn(p_c3)
    xl, xc = out_both(mods, o_l, o_c, na_w_out[0], True)

    kvw = SWA_KV_HEADS * HEAD_DIM
    w = swa_w_in[0]
    wq, wk, wv, wz = (w[:, :D_INNER], w[:, D_INNER:D_INNER + kvw], w[:, D_INNER + kvw:D_INNER + 2 * kvw],
                      w[:, D_INNER + 2 * kvw:])

    def twice(m):
        m = m.reshape(D_MODEL, SWA_KV_HEADS, 1, HEAD_DIM)
        return jnp.broadcast_to(m, (D_MODEL, SWA_KV_HEADS, 2, HEAD_DIM)).reshape(D_MODEL, SWA_KV_HEADS * LANES)

    w3 = jnp.concatenate([_pair_interleave(wq), wz, _pair_interleave(wk, dup=True), twice(wv)], axis=1)
    mods, p_l, p_c = in_both(3, w3, 1024, rope_cols=((0, D_INNER), (SWA_K_COL, SWA_V_COL)))
    sink_rows = jnp.broadcast_to(swa_sink[0][:, None], (SWA_Q_HEADS, LANES))
    o_l = _swa_attn(p_l.reshape(bsz, SEQ, -1), p_c.reshape(bsz, CTX_LEN, -1), sink_rows)
    xl, _ = out_both(mods, o_l, None, swa_w_out[0], False, final=final_g)
    return xl.reshape(bsz, SEQ, D_MODEL)
```

```python
import functools
import math

import numpy as np
import jax
import jax.numpy as jnp
from jax import lax
from jax.experimental import pallas as pl
from jax.experimental.pallas import tpu as pltpu

F32 = jnp.float32
BF16 = jnp.bfloat16

D_MODEL = 1024
SEQ = 2048
CTX_LEN = 256
DEPTH = 4
D_INNER = 2 * D_MODEL
GRID_W = 64
RMS_EPS = 1e-6
ROPE_THETA = 10000.0

MLSTM_HEADS = 8
MLSTM_HEAD_DIM = D_INNER // MLSTM_HEADS
MLSTM_QKV_BLOCK = 4
MLSTM_CONV = 5
SCAN_CHUNK = 256

DIFF_HEADS = 16
DIFF_HPS = 2
HEAD_DIM = 64
NA_HEADS = 32
NA_WIN_ROWS = 8
NA_WIN_COLS = 16
SWA_Q_HEADS = 32
SWA_KV_HEADS = 4
SWA_WINDOW = 128

LANES = 128
MIB = 1024 * 1024
NEG_BIG = -1e30


def _cparams(semantics, vmem_mib):
    return pltpu.CompilerParams(dimension_semantics=semantics, vmem_limit_bytes=vmem_mib * MIB)


def _silu(v):
    h = 0.5 * v
    return h + h * jnp.tanh(h)


def _adaln_kernel(c_ref, w_ref, b_ref, o_ref):
    s = _silu(c_ref[...])
    o_ref[0] = jnp.dot(s, w_ref[0], preferred_element_type=F32,
                       precision=lax.Precision.HIGHEST) + b_ref[0]


def _adaln(cvec, w_ada, b_ada):
    rows = cvec.shape[0]
    return pl.pallas_call(
        _adaln_kernel,
        grid=(DEPTH, 3),
        in_specs=[
            pl.BlockSpec((rows, D_MODEL), lambda i, j: (0, 0)),
            pl.BlockSpec((1, D_MODEL, D_MODEL), lambda i, j: (i, 0, j)),
            pl.BlockSpec((1, 1, D_MODEL), lambda i, j: (i, 0, j)),
        ],
        out_specs=pl.BlockSpec((1, rows, D_MODEL), lambda i, j: (i, 0, j)),
        out_shape=jax.ShapeDtypeStruct((DEPTH, rows, 3 * D_MODEL), F32),
        compiler_params=_cparams(("parallel", "parallel"), 32),
        name="adaln",
    )(cvec, w_ada, b_ada.reshape(DEPTH, 1, 3 * D_MODEL))


ROPE_Q = HEAD_DIM // 4


def _rope_tables():
    inv = np.power(ROPE_THETA, -np.arange(0, 2 * ROPE_Q, 2, dtype=np.float64) / (2 * ROPE_Q))
    t = np.arange(SEQ)
    ang = np.concatenate([(t // GRID_W).astype(np.float64)[:, None] * inv,
                          (t % GRID_W).astype(np.float64)[:, None] * inv], axis=1)
    cos = np.tile(np.cos(ang), (1, 4))
    sin = np.concatenate([-np.tile(np.sin(ang), (1, 2)), np.tile(np.sin(ang), (1, 2))], axis=1)
    return jnp.asarray(cos, F32), jnp.asarray(sin, F32)


def _pair_interleave(w, dup=False):
    d, n = w.shape
    h = n // HEAD_DIM
    w = w.reshape(d, h, 2, 2, ROPE_Q)
    if dup:
        w = jnp.broadcast_to(w[:, :, None], (d, h, 2, 2, 2, ROPE_Q))
    else:
        w = w.reshape(d, h // 2, 2, 2, 2, ROPE_Q)
    return jnp.transpose(w, (0, 1, 4, 2, 3, 5)).reshape(d, -1)


def _rope_chunk(x, cos, sin):
    return x * cos + pltpu.roll(x, LANES // 2, axis=1) * sin


def _inproj_kernel(*refs, tn, rope_plan):
    has_rope = bool(rope_plan)
    if has_rope:
        x_ref, g_ref, sh_ref, sc_ref, w_ref, cos_ref, sin_ref, o_ref, h_scr = refs
    else:
        x_ref, g_ref, sh_ref, sc_ref, w_ref, o_ref, h_scr = refs
    j = pl.program_id(1)

    @pl.when(j == 0)
    def _():
        x = x_ref[...]
        ms = jnp.mean(x * x, axis=-1, keepdims=True)
        y = x * lax.rsqrt(ms + RMS_EPS) * g_ref[...]
        h_scr[...] = (y * (1.0 + sc_ref[0]) + sh_ref[0]).astype(BF16)

    acc = jnp.dot(h_scr[...], w_ref[...], preferred_element_type=F32)

    def store_plain():
        o_ref[...] = acc.astype(BF16)

    def store_rope(flags):
        cos = cos_ref[...]
        sin = sin_ref[...]
        for c, roped in enumerate(flags):
            blk = acc[:, c * LANES:(c + 1) * LANES]
            if roped:
                blk = _rope_chunk(blk, cos, sin)
            o_ref[:, c * LANES:(c + 1) * LANES] = blk.astype(BF16)

    if not has_rope:
        store_plain()
    else:
        any_rope = None
        for flags, tiles in rope_plan:
            hit = functools.reduce(jnp.logical_or, [j == t for t in tiles])
            pl.when(hit)(functools.partial(store_rope, flags))
            any_rope = hit if any_rope is None else jnp.logical_or(any_rope, hit)
        pl.when(jnp.logical_not(any_rope))(store_plain)


def _inproj(x2d, norm_g, mods, w, *, rows_per_mod, mod_row0, tm, tn, rope_cols=(), rope_tabs=None):
    rows, _ = x2d.shape
    n = w.shape[1]
    assert rows % tm == 0 and n % tn == 0 and rows_per_mod % tm == 0
    tiles_per_mod = rows_per_mod // tm
    assert all(lo % LANES == 0 and hi % LANES == 0 for lo, hi in rope_cols)
    plan = {}
    for t in range(n // tn):
        flags = tuple(any(lo <= t * tn + c * LANES < hi for lo, hi in rope_cols) for c in range(tn // LANES))
        if any(flags):
            plan.setdefault(flags, []).append(t)
    rope_plan = tuple((flags, tuple(tiles)) for flags, tiles in plan.items())
    has_rope = bool(rope_plan)
    in_specs = [
        pl.BlockSpec((tm, D_MODEL), lambda i, j: (i, 0)),
        pl.BlockSpec((1, D_MODEL), lambda i, j: (0, 0)),
        pl.BlockSpec((1, 1, D_MODEL), lambda i, j: (mod_row0 + i // tiles_per_mod, 0, 0)),
        pl.BlockSpec((1, 1, D_MODEL), lambda i, j: (mod_row0 + i // tiles_per_mod, 0, 1)),
        pl.BlockSpec((D_MODEL, tn), lambda i, j: (0, j)),
    ]
    args = [x2d, norm_g.reshape(1, D_MODEL), mods, mods, w]
    if has_rope:
        assert SEQ % tm == 0
        seq_tiles = SEQ // tm
        in_specs += [pl.BlockSpec((tm, LANES), lambda i, j: (i % seq_tiles, 0))] * 2
        args += list(rope_tabs)
    vmem = (2 * tm * D_MODEL * 4 + tm * D_MODEL * 2 + 2 * D_MODEL * tn * 2 + 2 * tm * tn * 2
            + tm * tn * 4 + (4 * tm * LANES * 4 if has_rope else 0)) // MIB + 8
    return pl.pallas_call(
        functools.partial(_inproj_kernel, tn=tn, rope_plan=rope_plan),
        grid=(rows // tm, n // tn),
        in_specs=in_specs,
        out_specs=pl.BlockSpec((tm, tn), lambda i, j: (i, j)),
        out_shape=jax.ShapeDtypeStruct((rows, n), BF16),
        scratch_shapes=[pltpu.VMEM((tm, D_MODEL), BF16)],
        compiler_params=_cparams(("parallel", "arbitrary"), vmem),
        name="inproj",
    )(*args)


def _gate(o, z_blk):
    return o * _silu(z_blk.astype(F32))


def _outproj_kernel(o_ref, w_ref, x_ref, gt_ref, *rest, final):
    y = jnp.dot(o_ref[...], w_ref[...], preferred_element_type=F32)
    xn = x_ref[...] + gt_ref[0] * y
    if final:
        fg_ref, out_ref = rest
        ms = jnp.mean(xn * xn, axis=-1, keepdims=True)
        xn = xn * lax.rsqrt(ms + RMS_EPS) * fg_ref[...]
    else:
        (out_ref,) = rest
    out_ref[...] = xn


def _outproj(o2d, w_out, x2d, mods, *, rows_per_mod, mod_row0, tm, final_g=None, in_place=True):
    rows = x2d.shape[0]
    assert rows % tm == 0 and rows_per_mod % tm == 0
    tiles_per_mod = rows_per_mod // tm
    final = final_g is not None
    in_specs = [
        pl.BlockSpec((tm, D_INNER), lambda i: (i, 0)),
        pl.BlockSpec((D_INNER, D_MODEL), lambda i: (0, 0)),
        pl.BlockSpec((tm, D_MODEL), lambda i: (i, 0)),
        pl.BlockSpec((1, 1, D_MODEL), lambda i: (mod_row0 + i // tiles_per_mod, 0, 2)),
    ]
    args = [o2d, w_out, x2d, mods]
    if final:
        in_specs.append(pl.BlockSpec((1, D_MODEL), lambda i: (0, 0)))
        args.append(final_g.reshape(1, D_MODEL))
    vmem = (2 * tm * D_INNER * 2 + 2 * D_INNER * D_MODEL * 2 + 5 * tm * D_MODEL * 4) // MIB + 8
    return pl.pallas_call(
        functools.partial(_outproj_kernel, final=final),
        grid=(rows // tm,),
        in_specs=in_specs,
        out_specs=pl.BlockSpec((tm, D_MODEL), lambda i: (i, 0)),
        out_shape=jax.ShapeDtypeStruct((rows, D_MODEL), F32),
        input_output_aliases={2: 0} if in_place else {},
        compiler_params=_cparams(("parallel",), vmem),
        name="outproj",
    )(*args)


def _half_masks():
    lane = lax.broadcasted_iota(jnp.int32, (1, LANES), 1)
    return lane < HEAD_DIM, lane >= HEAD_DIM


def _nt(a, b):
    return lax.dot_general(a, b, (((1,), (1,)), ((), ())), preferred_element_type=F32)


LOG2E = math.log2(math.e)
QK_SCALE = HEAD_DIM ** -0.5 * LOG2E


def _split_heads(q, interleaved=False):
    lo, hi = _half_masks()
    if interleaved:
        lane = lax.broadcasted_iota(jnp.int32, (1, LANES), 1)
        lo = (lane % HEAD_DIM) < HEAD_DIM // 2
        hi = jnp.logical_not(lo)
    qs = q.astype(F32) * QK_SCALE
    return jnp.concatenate([jnp.where(lo, qs, 0.0), jnp.where(hi, qs, 0.0)], axis=0).astype(BF16)


def _ones_col(rows):
    lane = lax.broadcasted_iota(jnp.int32, (rows, LANES), 1)
    return jnp.where(lane == 0, 1.0, 0.0).astype(BF16)


def _diff_attn_kernel(q_ref, kl_ref, vl_ref, z_ref, kc_ref, vc_ref, lqk_ref, g_ref, o_ref, sa_scr, sb_scr,
                      *, lam_init, tq):
    nq = q_ref.shape[1] // tq
    lqk = lqk_ref[...]
    s1 = jnp.sum(lqk[0:1] * lqk[1:2], axis=-1, keepdims=True)
    s2 = jnp.sum(lqk[2:3] * lqk[3:4], axis=-1, keepdims=True)
    lam = jnp.exp(s1) - jnp.exp(s2) + lam_init
    bufs = (sa_scr, sb_scr)
    units = [(hh, i) for hh in range(DIFF_HPS) for i in range(nq)]

    def logits(n):
        hh, i = units[n]
        hs = slice(hh * LANES, (hh + 1) * LANES)
        s_scr = bufs[n % 2]
        qq = _split_heads(q_ref[0, i * tq:(i + 1) * tq, hs], interleaved=True)
        s_scr[:, :CTX_LEN] = _nt(qq, kc_ref[0, :, hs])
        s_scr[:, CTX_LEN:] = _nt(qq, kl_ref[0, :, hs])

    def finish(n):
        hh, i = units[n]
        hs = slice(hh * LANES, (hh + 1) * LANES)
        s = bufs[n % 2][...]
        m = jnp.max(s, axis=-1, keepdims=True)
        p = jnp.exp2(s - m)
        den = jnp.sum(p, axis=-1, keepdims=True)
        r = 1.0 / den
        pc = p * jnp.concatenate([r[:tq], -lam * r[tq:]], axis=0)
        a = (pc[:tq] + pc[tq:]).astype(BF16)
        o = (jnp.dot(a[:, :CTX_LEN], vc_ref[0, :, hs], preferred_element_type=F32)
             + jnp.dot(a[:, CTX_LEN:], vl_ref[0, :, hs], preferred_element_type=F32))
        ms = jnp.mean(o * o, axis=-1, keepdims=True)
        o = o * lax.rsqrt(ms + RMS_EPS) * g_ref[...] * (1.0 - lam_init)
        o_ref[0, i * tq:(i + 1) * tq, hs] = _gate(o, z_ref[0, i * tq:(i + 1) * tq, hs]).astype(BF16)

    logits(0)
    for n in range(len(units)):
        if n + 1 < len(units):
            logits(n + 1)
        finish(n)


def _diff_attn(proj_lat, proj_ctx, lqk, subln_g, *, tq, lam_init):
    bsz = proj_lat.shape[0]
    w = DIFF_HPS * LANES
    kb, vb, zb = D_INNER // w, 2 * D_INNER // w, 3 * D_INNER // w
    nk = CTX_LEN + SEQ
    score_bytes = 2 * tq * nk * 4
    vmem = (5 * score_bytes) // MIB + 16
    return pl.pallas_call(
        functools.partial(_diff_attn_kernel, lam_init=lam_init, tq=tq),
        grid=(bsz, DIFF_HEADS // DIFF_HPS),
        in_specs=[pl.BlockSpec((1, SEQ, w), lambda b, h: (b, 0, h)),
                  pl.BlockSpec((1, SEQ, w), lambda b, h: (b, 0, kb + h)),
                  pl.BlockSpec((1, SEQ, w), lambda b, h: (b, 0, vb + h)),
                  pl.BlockSpec((1, SEQ, w), lambda b, h: (b, 0, zb + h)),
                  pl.BlockSpec((1, CTX_LEN, w), lambda b, h: (b, 0, kb + h)),
                  pl.BlockSpec((1, CTX_LEN, w), lambda b, h: (b, 0, vb + h)),
                  pl.BlockSpec((4, HEAD_DIM), lambda b, h: (0, 0)),
                  pl.BlockSpec((1, LANES), lambda b, h: (0, 0))],
        out_specs=pl.BlockSpec((1, SEQ, w), lambda b, h: (b, 0, h)),
        out_shape=jax.ShapeDtypeStruct((bsz, SEQ, D_INNER), BF16),
        scratch_shapes=[pltpu.VMEM((2 * tq, nk), F32), pltpu.VMEM((2 * tq, nk), F32)],
        compiler_params=_cparams(("parallel", "parallel"), vmem),
        name="diff_attn",
    )(proj_lat, proj_lat, proj_lat, proj_lat, proj_ctx, proj_ctx, lqk, subln_g.reshape(1, LANES))


CTX_HB = 4


def _ctx_attn_kernel(*refs, diff, lam_init):
    if diff:
        q_ref, k_ref, v_ref, z_ref, lqk_ref, g_ref, o_ref = refs
        lqk = lqk_ref[...]
        lam = (jnp.exp(jnp.sum(lqk[0:1] * lqk[1:2], axis=-1, keepdims=True))
               - jnp.exp(jnp.sum(lqk[2:3] * lqk[3:4], axis=-1, keepdims=True)) + lam_init)
    else:
        q_ref, k_ref, v_ref, z_ref, o_ref = refs
    t = q_ref.shape[1]
    lo, _ = _half_masks()
    for hb in range(CTX_HB):
        sl = slice(hb * LANES, (hb + 1) * LANES)
        qq = _split_heads(q_ref[0, :, sl], interleaved=diff)
        s = _nt(qq, k_ref[0, :, sl])
        p = jnp.exp2(s - jnp.max(s, axis=-1, keepdims=True))
        den = jnp.sum(p, axis=-1, keepdims=True)
        if diff:
            a = (p[:t] - (lam * den[:t] / den[t:]) * p[t:]).astype(BF16)
            o = jnp.dot(a, v_ref[0, :, sl], preferred_element_type=F32) / den[:t]
            ms = jnp.mean(o * o, axis=-1, keepdims=True)
            o = o * lax.rsqrt(ms + RMS_EPS) * g_ref[...] * (1.0 - lam_init)
        else:
            o = jnp.dot(p.astype(BF16), v_ref[0, :, sl], preferred_element_type=F32) / den
            o = jnp.where(lo, o[:t], o[t:])
        o_ref[0, :, sl] = _gate(o, z_ref[0, :, sl]).astype(BF16)


def _ctx_attn(proj_ctx, lqk=None, subln_g=None, lam_init=0.0):
    bsz = proj_ctx.shape[0]
    diff = lqk is not None
    w = CTX_HB * LANES
    kb, vb, zb = D_INNER // w, 2 * D_INNER // w, 3 * D_INNER // w
    in_specs = [pl.BlockSpec((1, CTX_LEN, w), lambda b, h: (b, 0, h)),
                pl.BlockSpec((1, CTX_LEN, w), lambda b, h: (b, 0, kb + h)),
                pl.BlockSpec((1, CTX_LEN, w), lambda b, h: (b, 0, vb + h)),
                pl.BlockSpec((1, CTX_LEN, w), lambda b, h: (b, 0, zb + h))]
    args = [proj_ctx, proj_ctx, proj_ctx, proj_ctx]
    if diff:
        in_specs += [pl.BlockSpec((4, HEAD_DIM), lambda b, h: (0, 0)), pl.BlockSpec((1, LANES), lambda b, h: (0, 0))]
        args += [lqk, subln_g.reshape(1, LANES)]
    return pl.pallas_call(
        functools.partial(_ctx_attn_kernel, diff=diff, lam_init=lam_init),
        grid=(bsz, D_INNER // w),
        in_specs=in_specs,
        out_specs=pl.BlockSpec((1, CTX_LEN, w), lambda b, h: (b, 0, h)),
        out_shape=jax.ShapeDtypeStruct((bsz, CTX_LEN, D_INNER), BF16),
        compiler_params=_cparams(("parallel", "parallel"), 32),
        name="ctx_attn",
    )(*args)


NA_ROWS = SEQ // GRID_W
NA_BLK_ROWS = 4
NA_BLK_Q = NA_BLK_ROWS * GRID_W
NA_NBLK = NA_ROWS // NA_BLK_ROWS
NA_WIN_R = NA_BLK_ROWS + NA_WIN_ROWS
NA_KWIN = NA_WIN_R * GRID_W
NA_CTX_CHUNK = 512
NA_MAX_LANE = LANES + 1


def _na_win_start(i):
    lo, hi = 0, NA_ROWS - NA_WIN_R
    start = i * NA_BLK_ROWS - NA_WIN_ROWS // 2
    return np.clip(start, lo, hi) if isinstance(i, (int, np.integer)) else jnp.clip(start, lo, hi)


NA_NDR = 2 * NA_WIN_ROWS - 1


def _na_bias_plan():
    plan = np.full((3, NA_BLK_ROWS, NA_WIN_R), -1, np.int32)
    for pat, i in enumerate((0, 1, NA_NBLK - 1)):
        ws = int(_na_win_start(i))
        for qr in range(NA_BLK_ROWS):
            r = i * NA_BLK_ROWS + qr
            rs = min(max(r - NA_WIN_ROWS // 2, 0), NA_ROWS - NA_WIN_ROWS)
            for a in range(NA_WIN_R):
                if rs <= ws + a < rs + NA_WIN_ROWS:
                    plan[pat, qr, a] = ws + a - r + NA_WIN_ROWS - 1
    return plan


def _na_bias_tiles(rpb):
    c = np.arange(GRID_W)
    cs = np.clip(c - NA_WIN_COLS // 2, 0, GRID_W - NA_WIN_COLS)
    kc = np.arange(GRID_W)
    col_ok = (kc[None, :] >= cs[:, None]) & (kc[None, :] < cs[:, None] + NA_WIN_COLS)
    dc_idx = np.clip(kc[None, :] - c[:, None], -(NA_WIN_COLS - 1), NA_WIN_COLS - 1) + NA_WIN_COLS - 1
    return jnp.where(col_ok[None, None], rpb[:, :, dc_idx] * LOG2E, NEG_BIG).astype(F32)


def _na_attn_kernel(q_ref, k_ref, v_ref, z_ref, kc_ref, vc_ref, t1_ref, o_ref, oc_scr, bias_ref):
    lo, _ = _half_masks()

    @pl.when(pl.program_id(1) == 0)
    def _():
        plan = _na_bias_plan()
        masked = jnp.full((GRID_W, GRID_W), NEG_BIG, F32)
        for hl in range(2):
            for pat in range(3):
                for qr in range(NA_BLK_ROWS):
                    tiles = [t1_ref[hl, int(d)] if d >= 0 else masked for d in plan[pat, qr]]
                    bias_ref[hl, pat, qr * GRID_W:(qr + 1) * GRID_W, :] = jnp.concatenate(tiles, axis=1)
    kctx = kc_ref[0]
    vctx = jnp.concatenate([vc_ref[0], _ones_col(CTX_LEN)], axis=1)
    lane2 = lax.broadcasted_iota(jnp.int32, (1, 2 * LANES), 1)

    def ctx_chunk(j, carry):
        r0 = pl.multiple_of(jnp.asarray(j, jnp.int32) * NA_CTX_CHUNK, NA_CTX_CHUNK)
        qq = _split_heads(q_ref[0, pl.ds(r0, NA_CTX_CHUNK), :])
        s = _nt(qq, kctx)
        m = jnp.max(s, axis=-1, keepdims=True)
        oc = jnp.dot(jnp.exp2(s - m).astype(BF16), vctx, preferred_element_type=F32)
        oc = jnp.where(lane2 == NA_MAX_LANE, m, oc)
        oc_scr[0, pl.ds(r0, NA_CTX_CHUNK), :] = oc[:NA_CTX_CHUNK]
        oc_scr[1, pl.ds(r0, NA_CTX_CHUNK), :] = oc[NA_CTX_CHUNK:]
        return carry

    lax.fori_loop(0, SEQ // NA_CTX_CHUNK, ctx_chunk, 0, unroll=True)

    def block(i, carry):
        i = jnp.asarray(i, jnp.int32)
        pat = jnp.where(i == 0, 0, jnp.where(i == NA_NBLK - 1, 2, 1))
        t0 = pl.multiple_of(i * NA_BLK_Q, NA_BLK_Q)
        s0 = pl.multiple_of(_na_win_start(i) * GRID_W, NA_BLK_Q)
        qq = _split_heads(q_ref[0, pl.ds(t0, NA_BLK_Q), :])
        kw = k_ref[0, pl.ds(s0, NA_KWIN), :]
        vw = jnp.concatenate([v_ref[0, pl.ds(s0, NA_KWIN), :], _ones_col(NA_KWIN)], axis=1)
        s = _nt(qq, kw) + jnp.concatenate([bias_ref[0, pat], bias_ref[1, pat]], axis=0)
        oc = jnp.concatenate([oc_scr[0, pl.ds(t0, NA_BLK_Q), :], oc_scr[1, pl.ds(t0, NA_BLK_Q), :]], axis=0)
        mc = oc[:, NA_MAX_LANE:NA_MAX_LANE + 1]
        m = jnp.maximum(jnp.max(s, axis=-1, keepdims=True), mc)
        tot = (jnp.dot(jnp.exp2(s - m).astype(BF16), vw, preferred_element_type=F32)
               + jnp.exp2(mc - m) * oc)
        out = tot[:, :LANES] / tot[:, LANES:LANES + 1]
        o = jnp.where(lo, out[:NA_BLK_Q], out[NA_BLK_Q:])
        o_ref[0, pl.ds(t0, NA_BLK_Q), :] = _gate(o, z_ref[0, pl.ds(t0, NA_BLK_Q), :]).astype(BF16)
        return carry

    lax.fori_loop(0, NA_NBLK, block, 0, unroll=True)


def _na_attn(proj_lat, proj_ctx, bias_tiles):
    bsz = proj_lat.shape[0]
    pairs = NA_HEADS // 2
    kb, vb, zb = D_INNER // LANES, 2 * D_INNER // LANES, 3 * D_INNER // LANES
    return pl.pallas_call(
        _na_attn_kernel,
        grid=(pairs, bsz),
        in_specs=[pl.BlockSpec((1, SEQ, LANES), lambda h, b: (b, 0, h)),
                  pl.BlockSpec((1, SEQ, LANES), lambda h, b: (b, 0, kb + h)),
                  pl.BlockSpec((1, SEQ, LANES), lambda h, b: (b, 0, vb + h)),
                  pl.BlockSpec((1, SEQ, LANES), lambda h, b: (b, 0, zb + h)),
                  pl.BlockSpec((1, CTX_LEN, LANES), lambda h, b: (b, 0, kb + h)),
                  pl.BlockSpec((1, CTX_LEN, LANES), lambda h, b: (b, 0, vb + h)),
                  pl.BlockSpec((2, NA_NDR, GRID_W, GRID_W), lambda h, b: (h, 0, 0, 0))],
        out_specs=pl.BlockSpec((1, SEQ, LANES), lambda h, b: (b, 0, h)),
        out_shape=jax.ShapeDtypeStruct((bsz, SEQ, D_INNER), BF16),
        scratch_shapes=[pltpu.VMEM((2, SEQ, 2 * LANES), F32), pltpu.VMEM((2, 3, NA_BLK_Q, NA_KWIN), F32)],
        compiler_params=_cparams(("parallel", "arbitrary"), 48),
        name="na_attn",
    )(proj_lat, proj_lat, proj_lat, proj_lat, proj_ctx, proj_ctx, bias_tiles)


SWA_TQ = 256
SWA_BAND = SWA_TQ + 2 * SWA_WINDOW
SWA_K_COL = 2 * D_INNER
SWA_V_COL = SWA_K_COL + SWA_KV_HEADS * LANES
SWA_N = SWA_V_COL + SWA_KV_HEADS * LANES


SWA_GROUP = SWA_Q_HEADS // SWA_KV_HEADS
SWA_GW = SWA_GROUP * HEAD_DIM


def _swa_attn_kernel(q_ref, k_ref, v_ref, z_ref, kc_ref, vc_ref, sink_ref, o_ref):
    lo, _ = _half_masks()
    kvh = pl.program_id(1)
    n_pair = SWA_GROUP // 2
    kctx = kc_ref[0]
    vctx = jnp.concatenate([vc_ref[0], _ones_col(CTX_LEN)], axis=1)
    sinks = sink_ref[pl.ds(pl.multiple_of(kvh * SWA_GROUP, SWA_GROUP), SWA_GROUP), :][:, 0:1] * LOG2E
    sink = jnp.concatenate([jnp.broadcast_to(sinks[g:g + 1], (SWA_TQ, 1)) for g in range(SWA_GROUP)], axis=0)

    def block(i, carry):
        i = jnp.asarray(i, jnp.int32)
        t0 = pl.multiple_of(i * SWA_TQ, SWA_TQ)
        s0 = pl.multiple_of(jnp.clip(i * SWA_TQ - SWA_WINDOW, 0, SEQ - SWA_BAND), SWA_WINDOW)
        q = q_ref[0, pl.ds(t0, SWA_TQ), :]
        qq = jnp.concatenate([_split_heads(q[:, p * LANES:(p + 1) * LANES], interleaved=True)
                              for p in range(n_pair)], axis=0)
        kw = k_ref[0, pl.ds(s0, SWA_BAND), :]
        vw = jnp.concatenate([v_ref[0, pl.ds(s0, SWA_BAND), :], _ones_col(SWA_BAND)], axis=1)
        tpos = t0 + lax.broadcasted_iota(jnp.int32, (SWA_TQ, SWA_BAND), 0)
        spos = s0 + lax.broadcasted_iota(jnp.int32, (SWA_TQ, SWA_BAND), 1)
        bias = jnp.where(jnp.abs(tpos - spos) <= SWA_WINDOW, 0.0, NEG_BIG)
        s_lat = _nt(qq, kw) + jnp.concatenate([bias] * SWA_GROUP, axis=0)
        s_ctx = _nt(qq, kctx)
        m = jnp.maximum(jnp.maximum(jnp.max(s_lat, axis=-1, keepdims=True),
                                    jnp.max(s_ctx, axis=-1, keepdims=True)), sink)
        tot = (jnp.dot(jnp.exp2(s_lat - m).astype(BF16), vw, preferred_element_type=F32)
               + jnp.dot(jnp.exp2(s_ctx - m).astype(BF16), vctx, preferred_element_type=F32))
        out = tot[:, :LANES] / (tot[:, LANES:LANES + 1] + jnp.exp2(sink - m))
        pairs = [jnp.where(lo, out[(2 * p) * SWA_TQ:(2 * p + 1) * SWA_TQ],
                           out[(2 * p + 1) * SWA_TQ:(2 * p + 2) * SWA_TQ]) for p in range(n_pair)]
        o = jnp.concatenate(pairs, axis=1)
        o_ref[0, pl.ds(t0, SWA_TQ), :] = _gate(o, z_ref[0, pl.ds(t0, SWA_TQ), :]).astype(BF16)
        return carry

    lax.fori_loop(0, SEQ // SWA_TQ, block, 0, unroll=True)


def _swa_attn(proj_lat, proj_ctx, sink_rows):
    bsz = proj_lat.shape[0]
    kb, vb, zb = SWA_K_COL // LANES, SWA_V_COL // LANES, D_INNER // SWA_GW
    return pl.pallas_call(
        _swa_attn_kernel,
        grid=(bsz, SWA_KV_HEADS),
        in_specs=[pl.BlockSpec((1, SEQ, SWA_GW), lambda b, h: (b, 0, h)),
                  pl.BlockSpec((1, SEQ, LANES), lambda b, h: (b, 0, kb + h)),
                  pl.BlockSpec((1, SEQ, LANES), lambda b, h: (b, 0, vb + h)),
                  pl.BlockSpec((1, SEQ, SWA_GW), lambda b, h: (b, 0, zb + h)),
                  pl.BlockSpec((1, CTX_LEN, LANES), lambda b, h: (b, 0, kb + h)),
                  pl.BlockSpec((1, CTX_LEN, LANES), lambda b, h: (b, 0, vb + h)),
                  pl.BlockSpec((SWA_Q_HEADS, LANES), lambda b, h: (0, 0))],
        out_specs=pl.BlockSpec((1, SEQ, SWA_GW), lambda b, h: (b, 0, h)),
        out_shape=jax.ShapeDtypeStruct((bsz, SEQ, D_INNER), BF16),
        compiler_params=_cparams(("parallel", "parallel"), 48),
        name="swa_attn",
    )(proj_lat, proj_lat, proj_lat, proj_lat, proj_ctx, proj_ctx, sink_rows)


GATE_LANES = 128


def _mlstm_prep_kernel(xm_ref, cw_ref, cb_ref, bdq_ref, bdk_ref, bdv_ref, wg_ref, bg_ref,
                       q_ref, kt_ref, v_ref, xc_ref, g_ref):
    h = pl.program_id(1)
    t = xm_ref.shape[1]
    xm_b = xm_ref[0]
    xm = xm_b.astype(F32)

    tpos = lax.broadcasted_iota(jnp.int32, (t, 1), 0)
    conv = jnp.zeros_like(xm) + cb_ref[...]
    for j in range(MLSTM_CONV):
        d = j - MLSTM_CONV // 2
        sh = pltpu.roll(xm, (-d) % t, axis=0) if d else xm
        ok = (tpos + d >= 0) & (tpos + d < t)
        conv = conv + jnp.where(ok, sh, 0.0) * cw_ref[j:j + 1, :]
    xc = _silu(conv)
    xc_b = xc.astype(BF16)
    q = jnp.dot(xc_b, bdq_ref[0], preferred_element_type=F32)
    k = jnp.dot(xc_b, bdk_ref[0], preferred_element_type=F32)
    v = jnp.dot(xm_b, bdv_ref[0], preferred_element_type=F32)
    q_b, k_b, v_b = q.astype(BF16), k.astype(BF16), v.astype(BF16)
    q_ref[0] = q_b
    v_ref[0] = v_b
    xc_ref[0] = xc_b
    ks = k * (MLSTM_HEAD_DIM ** -0.5)
    for c in range(t // SCAN_CHUNK):
        kt_ref[0, 0, c] = ks[c * SCAN_CHUNK:(c + 1) * SCAN_CHUNK, :].T.astype(BF16)
    part = (jnp.dot(q_b, wg_ref[0, 0], preferred_element_type=F32)
            + jnp.dot(k_b, wg_ref[1, 0], preferred_element_type=F32)
            + jnp.dot(v_b, wg_ref[2, 0], preferred_element_type=F32))

    @pl.when(h == 0)
    def _():
        g_ref[0] = part + bg_ref[...]

    @pl.when(h != 0)
    def _():
        g_ref[0] = g_ref[0] + part


def _mlstm_prep(proj3, conv_w, conv_b, bdq, bdk, bdv, wg, bg):
    bsz, t, _ = proj3.shape
    hd = MLSTM_HEAD_DIM
    nc = t // SCAN_CHUNK
    seq_spec = pl.BlockSpec((1, t, hd), lambda b, h: (b, 0, h))
    return pl.pallas_call(
        _mlstm_prep_kernel,
        grid=(bsz, MLSTM_HEADS),
        in_specs=[seq_spec,
                  pl.BlockSpec((MLSTM_CONV, hd), lambda b, h: (0, h)),
                  pl.BlockSpec((1, hd), lambda b, h: (0, h)),
                  pl.BlockSpec((1, hd, hd), lambda b, h: (h, 0, 0)),
                  pl.BlockSpec((1, hd, hd), lambda b, h: (h, 0, 0)),
                  pl.BlockSpec((1, hd, hd), lambda b, h: (h, 0, 0)),
                  pl.BlockSpec((3, 1, hd, GATE_LANES), lambda b, h: (0, h, 0, 0)),
                  pl.BlockSpec((1, GATE_LANES), lambda b, h: (0, 0))],
        out_specs=[seq_spec,
                   pl.BlockSpec((1, 1, nc, hd, SCAN_CHUNK), lambda b, h: (b, h, 0, 0, 0)),
                   seq_spec, seq_spec,
                   pl.BlockSpec((1, t, GATE_LANES), lambda b, h: (b, 0, 0))],
        out_shape=[jax.ShapeDtypeStruct((bsz, t, D_INNER), BF16),
                   jax.ShapeDtypeStruct((bsz, MLSTM_HEADS, nc, hd, SCAN_CHUNK), BF16),
                   jax.ShapeDtypeStruct((bsz, t, D_INNER), BF16),
                   jax.ShapeDtypeStruct((bsz, t, D_INNER), BF16),
                   jax.ShapeDtypeStruct((bsz, t, GATE_LANES), F32)],
        compiler_params=_cparams(("parallel", "arbitrary"), 48),
        name="mlstm_prep",
    )(proj3, conv_w, conv_b.reshape(1, D_INNER), bdq, bdk, bdv, wg, bg)


def _lane_scan(x, op, fill, reverse):
    lane = lax.broadcasted_iota(jnp.int32, x.shape, 1)
    sh = 1
    while sh < LANES:
        if reverse:
            y = jnp.where(lane < LANES - sh, pltpu.roll(x, LANES - sh, axis=1), fill)
        else:
            y = jnp.where(lane >= sh, pltpu.roll(x, sh, axis=1), fill)
        x = op(x, y)
        sh *= 2
    return x


GP_ROWS = 8


def _mlstm_gates_kernel(gc_ref, gl_ref, oc_ref, ol_ref):
    nh = MLSTM_HEADS
    ppc = SCAN_CHUNK // LANES
    gct = gc_ref[0].T
    glt = gl_ref[0].T
    n_c = gc_ref.shape[1] // SCAN_CHUNK
    n_l = gl_ref.shape[1] // SCAN_CHUNK
    zeros = jnp.zeros((nh, SCAN_CHUNK), F32)
    pieces = [(gct, j) for j in range(n_c * ppc)] + [(glt, j) for j in range(n_l * ppc)]
    chunks = [(oc_ref, c) for c in range(n_c)] + [(ol_ref, c) for c in range(n_l)]
    n_all = len(chunks)

    def stack(row0):
        return jnp.concatenate([src[row0:row0 + nh, j * LANES:(j + 1) * LANES] for src, j in pieces], axis=0)

    def piece(x, k):
        return x[k * nh:(k + 1) * nh]

    def earlier(k, rev):
        c = k // ppc
        return range(k + 1, (c + 1) * ppc) if rev else range(c * ppc, k)

    for d in range(2):
        rev = d == 1
        edge = (lambda x: x[:, 0:1]) if rev else (lambda x: x[:, LANES - 1:LANES])
        ig = stack(2 * d * nh)
        fg = stack((2 * d + 1) * nh)
        lf = jnp.minimum(fg, 0.0) - jnp.log1p(jnp.exp(-jnp.abs(fg)))
        bp = _lane_scan(lf, jnp.add, 0.0, rev)
        b = jnp.concatenate([piece(bp, k) + sum(piece(edge(bp), o) for o in earlier(k, rev))
                             for k in range(len(pieces))], axis=0)
        cmp_ = _lane_scan(ig - b, jnp.maximum, -jnp.inf, rev)
        cm = jnp.concatenate([functools.reduce(jnp.maximum, [piece(edge(cmp_), o) for o in earlier(k, rev)],
                                               piece(cmp_, k)) for k in range(len(pieces))], axis=0)
        last = (lambda c: c * ppc) if rev else (lambda c: c * ppc + ppc - 1)
        b_end_c = [piece(edge(b), last(c)) for c in range(n_all)]
        b_end = jnp.concatenate([b_end_c[k // ppc] for k in range(len(pieces))], axis=0)
        g = b_end - b + ig
        gmax_p = jnp.max(g, axis=-1, keepdims=True)
        gmax_c = [functools.reduce(jnp.maximum, [piece(gmax_p, c * ppc + p) for p in range(ppc)])
                  for c in range(n_all)]
        order = list(range(n_all))
        if rev:
            order = list(reversed(range(n_c))) + list(reversed(range(n_c, n_all)))
        m = jnp.zeros((nh, 1), F32)
        m_in, m_out = [None] * n_all, [None] * n_all
        for c in order:
            m_in[c] = m
            m = jnp.maximum(b_end_c[c] + m, gmax_c[c])
            m_out[c] = m
        m_in = jnp.concatenate([m_in[k // ppc] for k in range(len(pieces))], axis=0)
        m_out = jnp.concatenate([m_out[k // ppc] for k in range(len(pieces))], axis=0)
        mx = jnp.maximum(cm, m_in)
        rows = (-mx, b - ig, jnp.exp(m_in - mx), jnp.exp(-(b + mx)), jnp.exp(g - m_out),
                jnp.broadcast_to(jnp.exp(b_end + m_in - m_out), b.shape))
        for k, (dst, c) in enumerate(chunks):
            for r, val in enumerate(rows):
                dst[0, d, c, r] = jnp.concatenate([piece(val, k * ppc + p) for p in range(ppc)], axis=1)
            dst[0, d, c, 6] = zeros
            dst[0, d, c, 7] = zeros


def _mlstm_gates(gates_ctx, gates_lat):
    bsz = gates_lat.shape[0]
    n_c, n_l = CTX_LEN // SCAN_CHUNK, SEQ // SCAN_CHUNK

    def out(n):
        shape = (bsz, 2, n, GP_ROWS, MLSTM_HEADS, SCAN_CHUNK)
        return (pl.BlockSpec((1,) + shape[1:], lambda b: (b, 0, 0, 0, 0, 0)), jax.ShapeDtypeStruct(shape, F32))

    (spec_c, shape_c), (spec_l, shape_l) = out(n_c), out(n_l)
    return pl.pallas_call(
        _mlstm_gates_kernel,
        grid=(bsz,),
        in_specs=[pl.BlockSpec((1, CTX_LEN, GATE_LANES), lambda b: (b, 0, 0)),
                  pl.BlockSpec((1, SEQ, GATE_LANES), lambda b: (b, 0, 0))],
        out_specs=[spec_c, spec_l],
        out_shape=[shape_c, shape_l],
        compiler_params=_cparams(("parallel",), 32),
        name="mlstm_gates",
    )(gates_ctx, gates_lat)


def _mlstm_scan_kernel(qc_ref, ktc_ref, vc_ref, xcc_ref, gpc_ref,
                       ql_ref, ktl_ref, vl_ref, xcl_ref, gpl_ref, ng_ref, sk_ref, zc_ref, zl_ref,
                       oc_ref, ol_ref, hc_scr, hl_scr, st_scr):
    L = SCAN_CHUNK
    hd = MLSTM_HEAD_DIM
    n_c = qc_ref.shape[1] // L
    n_l = ql_ref.shape[1] // L
    lane = lax.broadcasted_iota(jnp.int32, (L, LANES), 1)
    ones_col = jnp.where(lane == 0, 1.0, 0.0).astype(BF16)
    tt = lax.broadcasted_iota(jnp.int32, (L, L), 0)
    ss = lax.broadcasted_iota(jnp.int32, (L, L), 1)
    masks = (ss <= tt, ss >= tt)

    hc_scr[...] = jnp.zeros_like(hc_scr)
    hl_scr[...] = jnp.zeros_like(hl_scr)
    st_scr[...] = jnp.zeros_like(st_scr)

    def chunk(d, q_ref, kt_ref, v_ref, gp_ref, h_scr, c):
        rows = gp_ref[0, d, 0, c]
        cols = rows.T
        a_col, wst, einv = cols[:, 0:1], cols[:, 2:3], cols[:, 3:4]
        c_row, ws_row, decay = rows[1:2, :], rows[4:5, :], rows[5:6, 0:1]
        r0 = pl.multiple_of(c * L, L)
        q = q_ref[0, pl.ds(r0, L), :]
        kt = kt_ref[0, 0, c]
        vx = jnp.concatenate([v_ref[0, pl.ds(r0, L), :], ones_col], axis=1)
        w = jnp.exp(jnp.where(masks[d], a_col - c_row, -jnp.inf))
        s = (jnp.dot(q, kt, preferred_element_type=F32) * w).astype(BF16)
        kw = (kt.astype(F32) * ws_row).astype(BF16)
        both = jnp.dot(jnp.concatenate([s, kw], axis=0), vx, preferred_element_type=F32)
        state = st_scr[d]
        tot = wst * jnp.dot(q, state.astype(BF16), preferred_element_type=F32) + both[:L]
        den = tot[:, hd:hd + 1]
        h = tot[:, :hd] / jnp.maximum(jnp.abs(den), einv)
        h_scr[pl.ds(r0, L), :] = h_scr[pl.ds(r0, L), :] + h
        st_scr[d] = decay * state + both[L:]

    for j in range(n_c):
        chunk(0, qc_ref, ktc_ref, vc_ref, gpc_ref, hc_scr, j)
        chunk(1, qc_ref, ktc_ref, vc_ref, gpc_ref, hc_scr, n_c - 1 - j)

    def lat_step(j, carry):
        chunk(0, ql_ref, ktl_ref, vl_ref, gpl_ref, hl_scr, j)
        chunk(1, ql_ref, ktl_ref, vl_ref, gpl_ref, hl_scr, n_l - 1 - j)
        return carry

    lax.fori_loop(0, n_l, lat_step, 0, unroll=True)

    ng = ng_ref[...]
    sk = sk_ref[...]

    def finish(h_scr, xc_ref, z_ref, o_ref, r0, n):
        h = h_scr[pl.ds(r0, n), :]
        mu = jnp.mean(h, axis=-1, keepdims=True)
        hcen = h - mu
        hn = hcen * lax.rsqrt(jnp.mean(hcen * hcen, axis=-1, keepdims=True) + RMS_EPS)
        o = hn * ng + sk * xc_ref[0, pl.ds(r0, n), :].astype(F32)
        o_ref[0, pl.ds(r0, n), :] = _gate(o, z_ref[0, pl.ds(r0, n), :]).astype(BF16)

    finish(hc_scr, xcc_ref, zc_ref, oc_ref, 0, CTX_LEN)

    def fin_step(i, carry):
        finish(hl_scr, xcl_ref, zl_ref, ol_ref, pl.multiple_of(jnp.asarray(i, jnp.int32) * 256, 256), 256)
        return carry

    lax.fori_loop(0, SEQ // 256, fin_step, 0, unroll=2)


def _mlstm_scan(prep_c, gp_c, prep_l, gp_l, norm_g, skip, proj_c, proj_l):
    bsz = prep_l[0].shape[0]
    hd = MLSTM_HEAD_DIM

    def z_spec(t):
        return pl.BlockSpec((1, t, hd), lambda b, h: (b, 0, MLSTM_HEADS + h))

    def seq_specs(t):
        nc = t // SCAN_CHUNK
        seq = pl.BlockSpec((1, t, hd), lambda b, h: (b, 0, h))
        return [seq,
                pl.BlockSpec((1, 1, nc, hd, SCAN_CHUNK), lambda b, h: (b, h, 0, 0, 0)),
                seq, seq,
                pl.BlockSpec((1, 2, 1, nc, GP_ROWS, SCAN_CHUNK), lambda b, h: (b, 0, h, 0, 0, 0))]

    vec = pl.BlockSpec((1, hd), lambda b, h: (0, h))
    return pl.pallas_call(
        _mlstm_scan_kernel,
        grid=(bsz, MLSTM_HEADS),
        in_specs=seq_specs(CTX_LEN) + seq_specs(SEQ) + [vec, vec, z_spec(CTX_LEN), z_spec(SEQ)],
        out_specs=[pl.BlockSpec((1, CTX_LEN, hd), lambda b, h: (b, 0, h)),
                   pl.BlockSpec((1, SEQ, hd), lambda b, h: (b, 0, h))],
        out_shape=[jax.ShapeDtypeStruct((bsz, CTX_LEN, D_INNER), BF16),
                   jax.ShapeDtypeStruct((bsz, SEQ, D_INNER), BF16)],
        scratch_shapes=[pltpu.VMEM((CTX_LEN, hd), F32), pltpu.VMEM((SEQ, hd), F32),
                        pltpu.VMEM((2, hd, hd + LANES), F32)],
        compiler_params=_cparams(("parallel", "parallel"), 40),
        name="mlstm_scan",
    )(*prep_c, gp_c, *prep_l, gp_l, norm_g.reshape(1, D_INNER), skip.reshape(1, D_INNER), proj_c, proj_l)


def _blockdiag_dense(w):
    per_head = MLSTM_HEAD_DIM // MLSTM_QKV_BLOCK
    wh = w.reshape(MLSTM_HEADS, per_head, MLSTM_QKV_BLOCK, MLSTM_QKV_BLOCK)
    eye = jnp.eye(per_head, dtype=w.dtype)
    dense = wh[:, :, :, None, :] * eye[None, :, None, :, None]
    return dense.reshape(MLSTM_HEADS, MLSTM_HEAD_DIM, MLSTM_HEAD_DIM).astype(BF16)


def _mlstm_mixer(proj_lat, proj_ctx, conv_w, conv_b, wq, wk, wv, w_gate, b_gate, norm_g, skip):
    n_gate = 4 * MLSTM_HEADS
    wg = jnp.pad(w_gate, ((0, 0), (0, GATE_LANES - n_gate))).astype(BF16)
    wg = wg.reshape(3, MLSTM_HEADS, MLSTM_HEAD_DIM, GATE_LANES)
    bg = jnp.pad(b_gate, (0, GATE_LANES - n_gate)).reshape(1, GATE_LANES)
    bds = (_blockdiag_dense(wq), _blockdiag_dense(wk), _blockdiag_dense(wv))
    *prep_c, gates_c = _mlstm_prep(proj_ctx, conv_w, conv_b, *bds, wg, bg)
    *prep_l, gates_l = _mlstm_prep(proj_lat, conv_w, conv_b, *bds, wg, bg)
    gp_c, gp_l = _mlstm_gates(gates_c, gates_l)
    gp_c = jnp.transpose(gp_c, (0, 1, 4, 2, 3, 5))
    gp_l = jnp.transpose(gp_l, (0, 1, 4, 2, 3, 5))
    return _mlstm_scan(prep_c, gp_c, prep_l, gp_l, norm_g, skip, proj_ctx, proj_lat)


def kernel(x, c, ctx, c_ctx, norm_g, w_ada, b_ada, final_g, mlstm_w_in, mlstm_conv_w, mlstm_conv_b, mlstm_wq, mlstm_wk, mlstm_wv, mlstm_w_gate, mlstm_b_gate, mlstm_norm_g, mlstm_skip, mlstm_w_out, diff_w_in, diff_lq1, diff_lk1, diff_lq2, diff_lk2, diff_subln_g, diff_w_out, na_w_in, na_rpb, na_w_out, swa_w_in, swa_sink, swa_w_out):
    bsz = x.shape[0]
    ctx_row = bsz
    mod_rows = -(-(bsz + 1) // 8) * 8
    cvec = jnp.zeros((mod_rows, D_MODEL), F32).at[:bsz].set(c).at[ctx_row].set(c_ctx)
    mods_all = _adaln(cvec, w_ada, b_ada)
    rope_tabs = _rope_tables()

    xl = x.reshape(bsz * SEQ, D_MODEL)
    xc = ctx.reshape(bsz * CTX_LEN, D_MODEL)
    lat = dict(rows_per_mod=SEQ, mod_row0=0)
    con = dict(rows_per_mod=bsz * CTX_LEN, mod_row0=ctx_row)
    tm_ctx = min(1024, bsz * CTX_LEN)

    def in_both(i, w, tn, rope_cols=()):
        mods = mods_all[i].reshape(mod_rows, 1, 3 * D_MODEL)
        wb = w.astype(BF16)
        pl_ = _inproj(xl, norm_g[i], mods, wb, tm=SEQ, tn=tn, rope_cols=rope_cols, rope_tabs=rope_tabs, **lat)
        pc_ = _inproj(xc, norm_g[i], mods, wb, tm=tm_ctx, tn=tn, **con)
        return mods, pl_, pc_

    def out_both(mods, o_lat, o_ctx, w_out, need_ctx, final=None, in_place=True):
        wb = w_out.astype(BF16)
        new_l = _outproj(o_lat.reshape(bsz * SEQ, D_INNER), wb, xl, mods, tm=1024,
                         final_g=final, in_place=in_place, **lat)
        new_c = xc
        if need_ctx:
            new_c = _outproj(o_ctx.reshape(bsz * CTX_LEN, D_INNER), wb, xc, mods, tm=tm_ctx, **con)
        return new_l, new_c

    mods, p_l, p_c = in_both(0, mlstm_w_in[0], 1024)
    o_c, o_l = _mlstm_mixer(p_l.reshape(bsz, SEQ, -1), p_c.reshape(bsz, CTX_LEN, -1), mlstm_conv_w[0],
                            mlstm_conv_b[0], mlstm_wq[0], mlstm_wk[0], mlstm_wv[0], mlstm_w_gate[0],
                            mlstm_b_gate[0], mlstm_norm_g[0], mlstm_skip[0])
    xl, xc = out_both(mods, o_l, o_c, mlstm_w_out[0], True, in_place=False)

    wd = diff_w_in[0]
    wd = jnp.concatenate([_pair_interleave(wd[:, :D_INNER]), _pair_interleave(wd[:, D_INNER:2 * D_INNER]),
                          wd[:, 2 * D_INNER:]], axis=1)
    mods, p_l, p_c = in_both(1, wd, 1024, rope_cols=((0, 2 * D_INNER),))
    p_l3, p_c3 = p_l.reshape(bsz, SEQ, -1), p_c.reshape(bsz, CTX_LEN, -1)
    lam_init = 0.8 - 0.6 * math.exp(-0.3 * 1)
    lqk = jnp.stack([diff_lq1[0], diff_lk1[0], diff_lq2[0], diff_lk2[0]])
    o_l = _diff_attn(p_l3, p_c3, lqk, diff_subln_g[0], tq=512, lam_init=lam_init)
    o_c = _ctx_attn(p_c3, lqk, diff_subln_g[0], lam_init)
    xl, xc = out_both(mods, o_l, o_c, diff_w_out[0], True)

    mods, p_l, p_c = in_both(2, na_w_in[0], 1024)
    p_l3, p_c3 = p_l.reshape(bsz, SEQ, -1), p_c.reshape(bsz, CTX_LEN, -1)
    o_l = _na_attn(p_l3, p_c3, _na_bias_tiles(na_rpb[0]))
    o_c = _ctx_attn(p_c3)
    xl, xc = out_both(mods, o_l, o_c, na_w_out[0], True)

    kvw = SWA_KV_HEADS * HEAD_DIM
    w = swa_w_in[0]
    wq, wk, wv, wz = (w[:, :D_INNER], w[:, D_INNER:D_INNER + kvw], w[:, D_INNER + kvw:D_INNER + 2 * kvw],
                      w[:, D_INNER + 2 * kvw:])

    def twice(m):
        m = m.reshape(D_MODEL, SWA_KV_HEADS, 1, HEAD_DIM)
        return jnp.broadcast_to(m, (D_MODEL, SWA_KV_HEADS, 2, HEAD_DIM)).reshape(D_MODEL, SWA_KV_HEADS * LANES)

    w3 = jnp.concatenate([_pair_interleave(wq), wz, _pair_interleave(wk, dup=True), twice(wv)], axis=1)
    mods, p_l, p_c = in_both(3, w3, 1024, rope_cols=((0, D_INNER), (SWA_K_COL, SWA_V_COL)))
    sink_rows = jnp.broadcast_to(swa_sink[0][:, None], (SWA_Q_HEADS, LANES))
    o_l = _swa_attn(p_l.reshape(bsz, SEQ, -1), p_c.reshape(bsz, CTX_LEN, -1), sink_rows)
    xl, _ = out_both(mods, o_l, None, swa_w_out[0], False, final=final_g)
    return xl.reshape(bsz, SEQ, D_MODEL)
```

```python
import functools
import math

import numpy as np
import jax
import jax.numpy as jnp
from jax import lax
from jax.experimental import pallas as pl
from jax.experimental.pallas import tpu as pltpu

F32 = jnp.float32
BF16 = jnp.bfloat16

D_MODEL = 1024
SEQ = 2048
CTX_LEN = 256
DEPTH = 4
D_INNER = 2 * D_MODEL
GRID_W = 64
RMS_EPS = 1e-6
ROPE_THETA = 10000.0

MLSTM_HEADS = 8
MLSTM_HEAD_DIM = D_INNER // MLSTM_HEADS
MLSTM_QKV_BLOCK = 4
MLSTM_CONV = 5
SCAN_CHUNK = 256

DIFF_HEADS = 16
DIFF_HPS = 4
HEAD_DIM = 64
NA_HEADS = 32
NA_WIN_ROWS = 8
NA_WIN_COLS = 16
SWA_Q_HEADS = 32
SWA_KV_HEADS = 4
SWA_WINDOW = 128

LANES = 128
MIB = 1024 * 1024
NEG_BIG = -1e30


def _cparams(semantics, vmem_mib):
    return pltpu.CompilerParams(dimension_semantics=semantics, vmem_limit_bytes=vmem_mib * MIB)


def _silu(v):
    h = 0.5 * v
    return h + h * jnp.tanh(h)


def _adaln_kernel(c_ref, w_ref, b_ref, o_ref):
    s = _silu(c_ref[...])
    o_ref[0] = jnp.dot(s, w_ref[0], preferred_element_type=F32,
                       precision=lax.Precision.HIGHEST) + b_ref[0]


def _adaln(cvec, w_ada, b_ada):
    rows = cvec.shape[0]
    return pl.pallas_call(
        _adaln_kernel,
        grid=(DEPTH, 3),
        in_specs=[
            pl.BlockSpec((rows, D_MODEL), lambda i, j: (0, 0)),
            pl.BlockSpec((1, D_MODEL, D_MODEL), lambda i, j: (i, 0, j)),
            pl.BlockSpec((1, 1, D_MODEL), lambda i, j: (i, 0, j)),
        ],
        out_specs=pl.BlockSpec((1, rows, D_MODEL), lambda i, j: (i, 0, j)),
        out_shape=jax.ShapeDtypeStruct((DEPTH, rows, 3 * D_MODEL), F32),
        compiler_params=_cparams(("parallel", "parallel"), 32),
        name="adaln",
    )(cvec, w_ada, b_ada.reshape(DEPTH, 1, 3 * D_MODEL))


ROPE_Q = HEAD_DIM // 4


def _rope_tables():
    inv = np.power(ROPE_THETA, -np.arange(0, 2 * ROPE_Q, 2, dtype=np.float64) / (2 * ROPE_Q))
    t = np.arange(SEQ)
    ang = np.concatenate([(t // GRID_W).astype(np.float64)[:, None] * inv,
                          (t % GRID_W).astype(np.float64)[:, None] * inv], axis=1)
    cos = np.tile(np.cos(ang), (1, 4))
    sin = np.concatenate([-np.tile(np.sin(ang), (1, 2)), np.tile(np.sin(ang), (1, 2))], axis=1)
    return jnp.asarray(cos, F32), jnp.asarray(sin, F32)


def _pair_interleave(w, dup=False):
    d, n = w.shape
    h = n // HEAD_DIM
    w = w.reshape(d, h, 2, 2, ROPE_Q)
    if dup:
        w = jnp.broadcast_to(w[:, :, None], (d, h, 2, 2, 2, ROPE_Q))
    else:
        w = w.reshape(d, h // 2, 2, 2, 2, ROPE_Q)
    return jnp.transpose(w, (0, 1, 4, 2, 3, 5)).reshape(d, -1)


def _rope_chunk(x, cos, sin):
    return x * cos + pltpu.roll(x, LANES // 2, axis=1) * sin


def _inproj_kernel(*refs, tn, rope_plan):
    has_rope = bool(rope_plan)
    if has_rope:
        x_ref, g_ref, sh_ref, sc_ref, w_ref, cos_ref, sin_ref, o_ref, h_scr = refs
    else:
        x_ref, g_ref, sh_ref, sc_ref, w_ref, o_ref, h_scr = refs
    j = pl.program_id(1)

    @pl.when(j == 0)
    def _():
        x = x_ref[...]
        ms = jnp.mean(x * x, axis=-1, keepdims=True)
        y = x * lax.rsqrt(ms + RMS_EPS) * g_ref[...]
        h_scr[...] = (y * (1.0 + sc_ref[0]) + sh_ref[0]).astype(BF16)

    acc = jnp.dot(h_scr[...], w_ref[...], preferred_element_type=F32)

    def store_plain():
        o_ref[...] = acc.astype(BF16)

    def store_rope(flags):
        cos = cos_ref[...]
        sin = sin_ref[...]
        for c, roped in enumerate(flags):
            blk = acc[:, c * LANES:(c + 1) * LANES]
            if roped:
                blk = _rope_chunk(blk, cos, sin)
            o_ref[:, c * LANES:(c + 1) * LANES] = blk.astype(BF16)

    if not has_rope:
        store_plain()
    else:
        any_rope = None
        for flags, tiles in rope_plan:
            hit = functools.reduce(jnp.logical_or, [j == t for t in tiles])
            pl.when(hit)(functools.partial(store_rope, flags))
            any_rope = hit if any_rope is None else jnp.logical_or(any_rope, hit)
        pl.when(jnp.logical_not(any_rope))(store_plain)


def _inproj(x2d, norm_g, mods, w, *, rows_per_mod, mod_row0, tm, tn, rope_cols=(), rope_tabs=None):
    rows, _ = x2d.shape
    n = w.shape[1]
    assert rows % tm == 0 and n % tn == 0 and rows_per_mod % tm == 0
    tiles_per_mod = rows_per_mod // tm
    assert all(lo % LANES == 0 and hi % LANES == 0 for lo, hi in rope_cols)
    plan = {}
    for t in range(n // tn):
        flags = tuple(any(lo <= t * tn + c * LANES < hi for lo, hi in rope_cols) for c in range(tn // LANES))
        if any(flags):
            plan.setdefault(flags, []).append(t)
    rope_plan = tuple((flags, tuple(tiles)) for flags, tiles in plan.items())
    has_rope = bool(rope_plan)
    in_specs = [
        pl.BlockSpec((tm, D_MODEL), lambda i, j: (i, 0)),
        pl.BlockSpec((1, D_MODEL), lambda i, j: (0, 0)),
        pl.BlockSpec((1, 1, D_MODEL), lambda i, j: (mod_row0 + i // tiles_per_mod, 0, 0)),
        pl.BlockSpec((1, 1, D_MODEL), lambda i, j: (mod_row0 + i // tiles_per_mod, 0, 1)),
        pl.BlockSpec((D_MODEL, tn), lambda i, j: (0, j)),
    ]
    args = [x2d, norm_g.reshape(1, D_MODEL), mods, mods, w]
    if has_rope:
        assert SEQ % tm == 0
        seq_tiles = SEQ // tm
        in_specs += [pl.BlockSpec((tm, LANES), lambda i, j: (i % seq_tiles, 0))] * 2
        args += list(rope_tabs)
    vmem = (2 * tm * D_MODEL * 4 + tm * D_MODEL * 2 + 2 * D_MODEL * tn * 2 + 2 * tm * tn * 2
            + tm * tn * 4 + (4 * tm * LANES * 4 if has_rope else 0)) // MIB + 8
    return pl.pallas_call(
        functools.partial(_inproj_kernel, tn=tn, rope_plan=rope_plan),
        grid=(rows // tm, n // tn),
        in_specs=in_specs,
        out_specs=pl.BlockSpec((tm, tn), lambda i, j: (i, j)),
        out_shape=jax.ShapeDtypeStruct((rows, n), BF16),
        scratch_shapes=[pltpu.VMEM((tm, D_MODEL), BF16)],
        compiler_params=_cparams(("parallel", "arbitrary"), vmem),
        name="inproj",
    )(*args)


def _gate(o, z_blk):
    return o * _silu(z_blk.astype(F32))


def _outproj_kernel(o_ref, w_ref, x_ref, gt_ref, *rest, final):
    y = jnp.dot(o_ref[...], w_ref[...], preferred_element_type=F32)
    xn = x_ref[...] + gt_ref[0] * y
    if final:
        fg_ref, out_ref = rest
        ms = jnp.mean(xn * xn, axis=-1, keepdims=True)
        xn = xn * lax.rsqrt(ms + RMS_EPS) * fg_ref[...]
    else:
        (out_ref,) = rest
    out_ref[...] = xn


def _outproj(o2d, w_out, x2d, mods, *, rows_per_mod, mod_row0, tm, final_g=None, in_place=True):
    rows = x2d.shape[0]
    assert rows % tm == 0 and rows_per_mod % tm == 0
    tiles_per_mod = rows_per_mod // tm
    final = final_g is not None
    in_specs = [
        pl.BlockSpec((tm, D_INNER), lambda i: (i, 0)),
        pl.BlockSpec((D_INNER, D_MODEL), lambda i: (0, 0)),
        pl.BlockSpec((tm, D_MODEL), lambda i: (i, 0)),
        pl.BlockSpec((1, 1, D_MODEL), lambda i: (mod_row0 + i // tiles_per_mod, 0, 2)),
    ]
    args = [o2d, w_out, x2d, mods]
    if final:
        in_specs.append(pl.BlockSpec((1, D_MODEL), lambda i: (0, 0)))
        args.append(final_g.reshape(1, D_MODEL))
    vmem = (2 * tm * D_INNER * 2 + 2 * D_INNER * D_MODEL * 2 + 5 * tm * D_MODEL * 4) // MIB + 8
    return pl.pallas_call(
        functools.partial(_outproj_kernel, final=final),
        grid=(rows // tm,),
        in_specs=in_specs,
        out_specs=pl.BlockSpec((tm, D_MODEL), lambda i: (i, 0)),
        out_shape=jax.ShapeDtypeStruct((rows, D_MODEL), F32),
        input_output_aliases={2: 0} if in_place else {},
        compiler_params=_cparams(("parallel",), vmem),
        name="outproj",
    )(*args)


def _half_masks():
    lane = lax.broadcasted_iota(jnp.int32, (1, LANES), 1)
    return lane < HEAD_DIM, lane >= HEAD_DIM


def _nt(a, b):
    return lax.dot_general(a, b, (((1,), (1,)), ((), ())), preferred_element_type=F32)


LOG2E = math.log2(math.e)
QK_SCALE = HEAD_DIM ** -0.5 * LOG2E


def _split_heads(q, interleaved=False):
    lo, hi = _half_masks()
    if interleaved:
        lane = lax.broadcasted_iota(jnp.int32, (1, LANES), 1)
        lo = (lane % HEAD_DIM) < HEAD_DIM // 2
        hi = jnp.logical_not(lo)
    qs = q.astype(F32) * QK_SCALE
    return jnp.concatenate([jnp.where(lo, qs, 0.0), jnp.where(hi, qs, 0.0)], axis=0).astype(BF16)


def _ones_col(rows):
    lane = lax.broadcasted_iota(jnp.int32, (rows, LANES), 1)
    return jnp.where(lane == 0, 1.0, 0.0).astype(BF16)


def _diff_attn_kernel(q_ref, kl_ref, vl_ref, z_ref, kc_ref, vc_ref, lqk_ref, g_ref, o_ref, sa_scr, sb_scr,
                      *, lam_init, tq):
    nq = q_ref.shape[1] // tq
    lqk = lqk_ref[...]
    s1 = jnp.sum(lqk[0:1] * lqk[1:2], axis=-1, keepdims=True)
    s2 = jnp.sum(lqk[2:3] * lqk[3:4], axis=-1, keepdims=True)
    lam = jnp.exp(s1) - jnp.exp(s2) + lam_init
    bufs = (sa_scr, sb_scr)
    units = [(hh, i) for hh in range(DIFF_HPS) for i in range(nq)]

    def logits(n):
        hh, i = units[n]
        hs = slice(hh * LANES, (hh + 1) * LANES)
        s_scr = bufs[n % 2]
        qq = _split_heads(q_ref[0, i * tq:(i + 1) * tq, hs], interleaved=True)
        s_scr[:, :CTX_LEN] = _nt(qq, kc_ref[0, :, hs])
        s_scr[:, CTX_LEN:] = _nt(qq, kl_ref[0, :, hs])

    def finish(n):
        hh, i = units[n]
        hs = slice(hh * LANES, (hh + 1) * LANES)
        s = bufs[n % 2][...]
        m = jnp.max(s, axis=-1, keepdims=True)
        p = jnp.exp2(s - m)
        den = jnp.sum(p, axis=-1, keepdims=True)
        r = 1.0 / den
        pc = p * jnp.concatenate([r[:tq], -lam * r[tq:]], axis=0)
        a = (pc[:tq] + pc[tq:]).astype(BF16)
        o = (jnp.dot(a[:, :CTX_LEN], vc_ref[0, :, hs], preferred_element_type=F32)
             + jnp.dot(a[:, CTX_LEN:], vl_ref[0, :, hs], preferred_element_type=F32))
        ms = jnp.mean(o * o, axis=-1, keepdims=True)
        o = o * lax.rsqrt(ms + RMS_EPS) * g_ref[...] * (1.0 - lam_init)
        o_ref[0, i * tq:(i + 1) * tq, hs] = _gate(o, z_ref[0, i * tq:(i + 1) * tq, hs]).astype(BF16)

    logits(0)
    for n in range(len(units)):
        if n + 1 < len(units):
            logits(n + 1)
        finish(n)


def _diff_attn(proj_lat, proj_ctx, lqk, subln_g, *, tq, lam_init):
    bsz = proj_lat.shape[0]
    w = DIFF_HPS * LANES
    kb, vb, zb = D_INNER // w, 2 * D_INNER // w, 3 * D_INNER // w
    nk = CTX_LEN + SEQ
    score_bytes = 2 * tq * nk * 4
    vmem = (5 * score_bytes) // MIB + 16
    return pl.pallas_call(
        functools.partial(_diff_attn_kernel, lam_init=lam_init, tq=tq),
        grid=(bsz, DIFF_HEADS // DIFF_HPS),
        in_specs=[pl.BlockSpec((1, SEQ, w), lambda b, h: (b, 0, h)),
                  pl.BlockSpec((1, SEQ, w), lambda b, h: (b, 0, kb + h)),
                  pl.BlockSpec((1, SEQ, w), lambda b, h: (b, 0, vb + h)),
                  pl.BlockSpec((1, SEQ, w), lambda b, h: (b, 0, zb + h)),
                  pl.BlockSpec((1, CTX_LEN, w), lambda b, h: (b, 0, kb + h)),
                  pl.BlockSpec((1, CTX_LEN, w), lambda b, h: (b, 0, vb + h)),
                  pl.BlockSpec((4, HEAD_DIM), lambda b, h: (0, 0)),
                  pl.BlockSpec((1, LANES), lambda b, h: (0, 0))],
        out_specs=pl.BlockSpec((1, SEQ, w), lambda b, h: (b, 0, h)),
        out_shape=jax.ShapeDtypeStruct((bsz, SEQ, D_INNER), BF16),
        scratch_shapes=[pltpu.VMEM((2 * tq, nk), F32), pltpu.VMEM((2 * tq, nk), F32)],
        compiler_params=_cparams(("parallel", "parallel"), vmem),
        name="diff_attn",
    )(proj_lat, proj_lat, proj_lat, proj_lat, proj_ctx, proj_ctx, lqk, subln_g.reshape(1, LANES))


CTX_HB = 4


def _ctx_attn_kernel(*refs, diff, lam_init):
    if diff:
        q_ref, k_ref, v_ref, z_ref, lqk_ref, g_ref, o_ref = refs
        lqk = lqk_ref[...]
        lam = (jnp.exp(jnp.sum(lqk[0:1] * lqk[1:2], axis=-1, keepdims=True))
               - jnp.exp(jnp.sum(lqk[2:3] * lqk[3:4], axis=-1, keepdims=True)) + lam_init)
    else:
        q_ref, k_ref, v_ref, z_ref, o_ref = refs
    t = q_ref.shape[1]
    lo, _ = _half_masks()
    for hb in range(CTX_HB):
        sl = slice(hb * LANES, (hb + 1) * LANES)
        qq = _split_heads(q_ref[0, :, sl], interleaved=diff)
        s = _nt(qq, k_ref[0, :, sl])
        p = jnp.exp2(s - jnp.max(s, axis=-1, keepdims=True))
        den = jnp.sum(p, axis=-1, keepdims=True)
        if diff:
            a = (p[:t] - (lam * den[:t] / den[t:]) * p[t:]).astype(BF16)
            o = jnp.dot(a, v_ref[0, :, sl], preferred_element_type=F32) / den[:t]
            ms = jnp.mean(o * o, axis=-1, keepdims=True)
            o = o * lax.rsqrt(ms + RMS_EPS) * g_ref[...] * (1.0 - lam_init)
        else:
            o = jnp.dot(p.astype(BF16), v_ref[0, :, sl], preferred_element_type=F32) / den
            o = jnp.where(lo, o[:t], o[t:])
        o_ref[0, :, sl] = _gate(o, z_ref[0, :, sl]).astype(BF16)


def _ctx_attn(proj_ctx, lqk=None, subln_g=None, lam_init=0.0):
    bsz = proj_ctx.shape[0]
    diff = lqk is not None
    w = CTX_HB * LANES
    kb, vb, zb = D_INNER // w, 2 * D_INNER // w, 3 * D_INNER // w
    in_specs = [pl.BlockSpec((1, CTX_LEN, w), lambda b, h: (b, 0, h)),
                pl.BlockSpec((1, CTX_LEN, w), lambda b, h: (b, 0, kb + h)),
                pl.BlockSpec((1, CTX_LEN, w), lambda b, h: (b, 0, vb + h)),
                pl.BlockSpec((1, CTX_LEN, w), lambda b, h: (b, 0, zb + h))]
    args = [proj_ctx, proj_ctx, proj_ctx, proj_ctx]
    if diff:
        in_specs += [pl.BlockSpec((4, HEAD_DIM), lambda b, h: (0, 0)), pl.BlockSpec((1, LANES), lambda b, h: (0, 0))]
        args += [lqk, subln_g.reshape(1, LANES)]
    return pl.pallas_call(
        functools.partial(_ctx_attn_kernel, diff=diff, lam_init=lam_init),
        grid=(bsz, D_INNER // w),
        in_specs=in_specs,
        out_specs=pl.BlockSpec((1, CTX_LEN, w), lambda b, h: (b, 0, h)),
        out_shape=jax.ShapeDtypeStruct((bsz, CTX_LEN, D_INNER), BF16),
        compiler_params=_cparams(("parallel", "parallel"), 32),
        name="ctx_attn",
    )(*args)


NA_ROWS = SEQ // GRID_W
NA_BLK_ROWS = 4
NA_BLK_Q = NA_BLK_ROWS * GRID_W
NA_NBLK = NA_ROWS // NA_BLK_ROWS
NA_WIN_R = NA_BLK_ROWS + NA_WIN_ROWS
NA_KWIN = NA_WIN_R * GRID_W
NA_CTX_CHUNK = 512
NA_MAX_LANE = LANES + 1


def _na_win_start(i):
    lo, hi = 0, NA_ROWS - NA_WIN_R
    start = i * NA_BLK_ROWS - NA_WIN_ROWS // 2
    return np.clip(start, lo, hi) if isinstance(i, (int, np.integer)) else jnp.clip(start, lo, hi)


NA_NDR = 2 * NA_WIN_ROWS - 1


def _na_bias_plan():
    plan = np.full((3, NA_BLK_ROWS, NA_WIN_R), -1, np.int32)
    for pat, i in enumerate((0, 1, NA_NBLK - 1)):
        ws = int(_na_win_start(i))
        for qr in range(NA_BLK_ROWS):
            r = i * NA_BLK_ROWS + qr
            rs = min(max(r - NA_WIN_ROWS // 2, 0), NA_ROWS - NA_WIN_ROWS)
            for a in range(NA_WIN_R):
                if rs <= ws + a < rs + NA_WIN_ROWS:
                    plan[pat, qr, a] = ws + a - r + NA_WIN_ROWS - 1
    return plan


def _na_bias_tiles(rpb):
    c = np.arange(GRID_W)
    cs = np.clip(c - NA_WIN_COLS // 2, 0, GRID_W - NA_WIN_COLS)
    kc = np.arange(GRID_W)
    col_ok = (kc[None, :] >= cs[:, None]) & (kc[None, :] < cs[:, None] + NA_WIN_COLS)
    dc_idx = np.clip(kc[None, :] - c[:, None], -(NA_WIN_COLS - 1), NA_WIN_COLS - 1) + NA_WIN_COLS - 1
    return jnp.where(col_ok[None, None], rpb[:, :, dc_idx] * LOG2E, NEG_BIG).astype(F32)


def _na_attn_kernel(q_ref, k_ref, v_ref, z_ref, kc_ref, vc_ref, t1_ref, o_ref, oc_scr, bias_ref):
    lo, _ = _half_masks()

    @pl.when(pl.program_id(1) == 0)
    def _():
        plan = _na_bias_plan()
        masked = jnp.full((GRID_W, GRID_W), NEG_BIG, F32)
        for hl in range(2):
            for pat in range(3):
                for qr in range(NA_BLK_ROWS):
                    tiles = [t1_ref[hl, int(d)] if d >= 0 else masked for d in plan[pat, qr]]
                    bias_ref[hl, pat, qr * GRID_W:(qr + 1) * GRID_W, :] = jnp.concatenate(tiles, axis=1)
    kctx = kc_ref[0]
    vctx = jnp.concatenate([vc_ref[0], _ones_col(CTX_LEN)], axis=1)
    lane2 = lax.broadcasted_iota(jnp.int32, (1, 2 * LANES), 1)

    def ctx_chunk(j, carry):
        r0 = pl.multiple_of(jnp.asarray(j, jnp.int32) * NA_CTX_CHUNK, NA_CTX_CHUNK)
        qq = _split_heads(q_ref[0, pl.ds(r0, NA_CTX_CHUNK), :])
        s = _nt(qq, kctx)
        m = jnp.max(s, axis=-1, keepdims=True)
        oc = jnp.dot(jnp.exp2(s - m).astype(BF16), vctx, preferred_element_type=F32)
        oc = jnp.where(lane2 == NA_MAX_LANE, m, oc)
        oc_scr[0, pl.ds(r0, NA_CTX_CHUNK), :] = oc[:NA_CTX_CHUNK]
        oc_scr[1, pl.ds(r0, NA_CTX_CHUNK), :] = oc[NA_CTX_CHUNK:]
        return carry

    lax.fori_loop(0, SEQ // NA_CTX_CHUNK, ctx_chunk, 0, unroll=True)

    def block(i, carry):
        i = jnp.asarray(i, jnp.int32)
        pat = jnp.where(i == 0, 0, jnp.where(i == NA_NBLK - 1, 2, 1))
        t0 = pl.multiple_of(i * NA_BLK_Q, NA_BLK_Q)
        s0 = pl.multiple_of(_na_win_start(i) * GRID_W, NA_BLK_Q)
        qq = _split_heads(q_ref[0, pl.ds(t0, NA_BLK_Q), :])
        kw = k_ref[0, pl.ds(s0, NA_KWIN), :]
        vw = jnp.concatenate([v_ref[0, pl.ds(s0, NA_KWIN), :], _ones_col(NA_KWIN)], axis=1)
        s = _nt(qq, kw) + jnp.concatenate([bias_ref[0, pat], bias_ref[1, pat]], axis=0)
        oc = jnp.concatenate([oc_scr[0, pl.ds(t0, NA_BLK_Q), :], oc_scr[1, pl.ds(t0, NA_BLK_Q), :]], axis=0)
        mc = oc[:, NA_MAX_LANE:NA_MAX_LANE + 1]
        m = jnp.maximum(jnp.max(s, axis=-1, keepdims=True), mc)
        tot = (jnp.dot(jnp.exp2(s - m).astype(BF16), vw, preferred_element_type=F32)
               + jnp.exp2(mc - m) * oc)
        out = tot[:, :LANES] / tot[:, LANES:LANES + 1]
        o = jnp.where(lo, out[:NA_BLK_Q], out[NA_BLK_Q:])
        o_ref[0, pl.ds(t0, NA_BLK_Q), :] = _gate(o, z_ref[0, pl.ds(t0, NA_BLK_Q), :]).astype(BF16)
        return carry

    lax.fori_loop(0, NA_NBLK, block, 0, unroll=True)


def _na_attn(proj_lat, proj_ctx, bias_tiles):
    bsz = proj_lat.shape[0]
    pairs = NA_HEADS // 2
    kb, vb, zb = D_INNER // LANES, 2 * D_INNER // LANES, 3 * D_INNER // LANES
    return pl.pallas_call(
        _na_attn_kernel,
        grid=(pairs, bsz),
        in_specs=[pl.BlockSpec((1, SEQ, LANES), lambda h, b: (b, 0, h)),
                  pl.BlockSpec((1, SEQ, LANES), lambda h, b: (b, 0, kb + h)),
                  pl.BlockSpec((1, SEQ, LANES), lambda h, b: (b, 0, vb + h)),
                  pl.BlockSpec((1, SEQ, LANES), lambda h, b: (b, 0, zb + h)),
                  pl.BlockSpec((1, CTX_LEN, LANES), lambda h, b: (b, 0, kb + h)),
                  pl.BlockSpec((1, CTX_LEN, LANES), lambda h, b: (b, 0, vb + h)),
                  pl.BlockSpec((2, NA_NDR, GRID_W, GRID_W), lambda h, b: (h, 0, 0, 0))],
        out_specs=pl.BlockSpec((1, SEQ, LANES), lambda h, b: (b, 0, h)),
        out_shape=jax.ShapeDtypeStruct((bsz, SEQ, D_INNER), BF16),
        scratch_shapes=[pltpu.VMEM((2, SEQ, 2 * LANES), F32), pltpu.VMEM((2, 3, NA_BLK_Q, NA_KWIN), F32)],
        compiler_params=_cparams(("parallel", "arbitrary"), 48),
        name="na_attn",
    )(proj_lat, proj_lat, proj_lat, proj_lat, proj_ctx, proj_ctx, bias_tiles)


SWA_TQ = 256
SWA_BAND = SWA_TQ + 2 * SWA_WINDOW
SWA_K_COL = 2 * D_INNER
SWA_V_COL = SWA_K_COL + SWA_KV_HEADS * LANES
SWA_N = SWA_V_COL + SWA_KV_HEADS * LANES


SWA_GROUP = SWA_Q_HEADS // SWA_KV_HEADS
SWA_GW = SWA_GROUP * HEAD_DIM


def _swa_attn_kernel(q_ref, k_ref, v_ref, z_ref, kc_ref, vc_ref, sink_ref, o_ref):
    lo, _ = _half_masks()
    kvh = pl.program_id(1)
    n_pair = SWA_GROUP // 2
    kctx = kc_ref[0]
    vctx = jnp.concatenate([vc_ref[0], _ones_col(CTX_LEN)], axis=1)
    sinks = sink_ref[pl.ds(pl.multiple_of(kvh * SWA_GROUP, SWA_GROUP), SWA_GROUP), :][:, 0:1] * LOG2E
    sink = jnp.concatenate([jnp.broadcast_to(sinks[g:g + 1], (SWA_TQ, 1)) for g in range(SWA_GROUP)], axis=0)

    def block(i, carry):
        i = jnp.asarray(i, jnp.int32)
        t0 = pl.multiple_of(i * SWA_TQ, SWA_TQ)
        s0 = pl.multiple_of(jnp.clip(i * SWA_TQ - SWA_WINDOW, 0, SEQ - SWA_BAND), SWA_WINDOW)
        q = q_ref[0, pl.ds(t0, SWA_TQ), :]
        qq = jnp.concatenate([_split_heads(q[:, p * LANES:(p + 1) * LANES], interleaved=True)
                              for p in range(n_pair)], axis=0)
        kw = k_ref[0, pl.ds(s0, SWA_BAND), :]
        vw = jnp.concatenate([v_ref[0, pl.ds(s0, SWA_BAND), :], _ones_col(SWA_BAND)], axis=1)
        tpos = t0 + lax.broadcasted_iota(jnp.int32, (SWA_TQ, SWA_BAND), 0)
        spos = s0 + lax.broadcasted_iota(jnp.int32, (SWA_TQ, SWA_BAND), 1)
        bias = jnp.where(jnp.abs(tpos - spos) <= SWA_WINDOW, 0.0, NEG_BIG)
        s_lat = _nt(qq, kw) + jnp.concatenate([bias] * SWA_GROUP, axis=0)
        s_ctx = _nt(qq, kctx)
        m = jnp.maximum(jnp.maximum(jnp.max(s_lat, axis=-1, keepdims=True),
                                    jnp.max(s_ctx, axis=-1, keepdims=True)), sink)
        tot = (jnp.dot(jnp.exp2(s_lat - m).astype(BF16), vw, preferred_element_type=F32)
               + jnp.dot(jnp.exp2(s_ctx - m).astype(BF16), vctx, preferred_element_type=F32))
        out = tot[:, :LANES] / (tot[:, LANES:LANES + 1] + jnp.exp2(sink - m))
        pairs = [jnp.where(lo, out[(2 * p) * SWA_TQ:(2 * p + 1) * SWA_TQ],
                           out[(2 * p + 1) * SWA_TQ:(2 * p + 2) * SWA_TQ]) for p in range(n_pair)]
        o = jnp.concatenate(pairs, axis=1)
        o_ref[0, pl.ds(t0, SWA_TQ), :] = _gate(o, z_ref[0, pl.ds(t0, SWA_TQ), :]).astype(BF16)
        return carry

    lax.fori_loop(0, SEQ // SWA_TQ, block, 0, unroll=True)


def _swa_attn(proj_lat, proj_ctx, sink_rows):
    bsz = proj_lat.shape[0]
    kb, vb, zb = SWA_K_COL // LANES, SWA_V_COL // LANES, D_INNER // SWA_GW
    return pl.pallas_call(
        _swa_attn_kernel,
        grid=(bsz, SWA_KV_HEADS),
        in_specs=[pl.BlockSpec((1, SEQ, SWA_GW), lambda b, h: (b, 0, h)),
                  pl.BlockSpec((1, SEQ, LANES), lambda b, h: (b, 0, kb + h)),
                  pl.BlockSpec((1, SEQ, LANES), lambda b, h: (b, 0, vb + h)),
                  pl.BlockSpec((1, SEQ, SWA_GW), lambda b, h: (b, 0, zb + h)),
                  pl.BlockSpec((1, CTX_LEN, LANES), lambda b, h: (b, 0, kb + h)),
                  pl.BlockSpec((1, CTX_LEN, LANES), lambda b, h: (b, 0, vb + h)),
                  pl.BlockSpec((SWA_Q_HEADS, LANES), lambda b, h: (0, 0))],
        out_specs=pl.BlockSpec((1, SEQ, SWA_GW), lambda b, h: (b, 0, h)),
        out_shape=jax.ShapeDtypeStruct((bsz, SEQ, D_INNER), BF16),
        compiler_params=_cparams(("parallel", "parallel"), 48),
        name="swa_attn",
    )(proj_lat, proj_lat, proj_lat, proj_lat, proj_ctx, proj_ctx, sink_rows)


GATE_LANES = 128


def _mlstm_prep_kernel(xm_ref, cw_ref, cb_ref, bdq_ref, bdk_ref, bdv_ref, wg_ref, bg_ref,
                       q_ref, kt_ref, v_ref, xc_ref, g_ref):
    h = pl.program_id(1)
    t = xm_ref.shape[1]
    xm_b = xm_ref[0]
    xm = xm_b.astype(F32)

    tpos = lax.broadcasted_iota(jnp.int32, (t, 1), 0)
    conv = jnp.zeros_like(xm) + cb_ref[...]
    for j in range(MLSTM_CONV):
        d = j - MLSTM_CONV // 2
        sh = pltpu.roll(xm, (-d) % t, axis=0) if d else xm
        ok = (tpos + d >= 0) & (tpos + d < t)
        conv = conv + jnp.where(ok, sh, 0.0) * cw_ref[j:j + 1, :]
    xc = _silu(conv)
    xc_b = xc.astype(BF16)
    q = jnp.dot(xc_b, bdq_ref[0], preferred_element_type=F32)
    k = jnp.dot(xc_b, bdk_ref[0], preferred_element_type=F32)
    v = jnp.dot(xm_b, bdv_ref[0], preferred_element_type=F32)
    q_b, k_b, v_b = q.astype(BF16), k.astype(BF16), v.astype(BF16)
    q_ref[0] = q_b
    v_ref[0] = v_b
    xc_ref[0] = xc_b
    ks = k * (MLSTM_HEAD_DIM ** -0.5)
    for c in range(t // SCAN_CHUNK):
        kt_ref[0, 0, c] = ks[c * SCAN_CHUNK:(c + 1) * SCAN_CHUNK, :].T.astype(BF16)
    part = (jnp.dot(q_b, wg_ref[0, 0], preferred_element_type=F32)
            + jnp.dot(k_b, wg_ref[1, 0], preferred_element_type=F32)
            + jnp.dot(v_b, wg_ref[2, 0], preferred_element_type=F32))

    @pl.when(h == 0)
    def _():
        g_ref[0] = part + bg_ref[...]

    @pl.when(h != 0)
    def _():
        g_ref[0] = g_ref[0] + part


def _mlstm_prep(proj3, conv_w, conv_b, bdq, bdk, bdv, wg, bg):
    bsz, t, _ = proj3.shape
    hd = MLSTM_HEAD_DIM
    nc = t // SCAN_CHUNK
    seq_spec = pl.BlockSpec((1, t, hd), lambda b, h: (b, 0, h))
    return pl.pallas_call(
        _mlstm_prep_kernel,
        grid=(bsz, MLSTM_HEADS),
        in_specs=[seq_spec,
                  pl.BlockSpec((MLSTM_CONV, hd), lambda b, h: (0, h)),
                  pl.BlockSpec((1, hd), lambda b, h: (0, h)),
                  pl.BlockSpec((1, hd, hd), lambda b, h: (h, 0, 0)),
                  pl.BlockSpec((1, hd, hd), lambda b, h: (h, 0, 0)),
                  pl.BlockSpec((1, hd, hd), lambda b, h: (h, 0, 0)),
                  pl.BlockSpec((3, 1, hd, GATE_LANES), lambda b, h: (0, h, 0, 0)),
                  pl.BlockSpec((1, GATE_LANES), lambda b, h: (0, 0))],
        out_specs=[seq_spec,
                   pl.BlockSpec((1, 1, nc, hd, SCAN_CHUNK), lambda b, h: (b, h, 0, 0, 0)),
                   seq_spec, seq_spec,
                   pl.BlockSpec((1, t, GATE_LANES), lambda b, h: (b, 0, 0))],
        out_shape=[jax.ShapeDtypeStruct((bsz, t, D_INNER), BF16),
                   jax.ShapeDtypeStruct((bsz, MLSTM_HEADS, nc, hd, SCAN_CHUNK), BF16),
                   jax.ShapeDtypeStruct((bsz, t, D_INNER), BF16),
                   jax.ShapeDtypeStruct((bsz, t, D_INNER), BF16),
                   jax.ShapeDtypeStruct((bsz, t, GATE_LANES), F32)],
        compiler_params=_cparams(("parallel", "arbitrary"), 48),
        name="mlstm_prep",
    )(proj3, conv_w, conv_b.reshape(1, D_INNER), bdq, bdk, bdv, wg, bg)


def _lane_scan(x, op, fill, reverse):
    lane = lax.broadcasted_iota(jnp.int32, x.shape, 1)
    sh = 1
    while sh < LANES:
        if reverse:
            y = jnp.where(lane < LANES - sh, pltpu.roll(x, LANES - sh, axis=1), fill)
        else:
            y = jnp.where(lane >= sh, pltpu.roll(x, sh, axis=1), fill)
        x = op(x, y)
        sh *= 2
    return x


GP_ROWS = 8


def _mlstm_gates_kernel(gc_ref, gl_ref, oc_ref, ol_ref):
    nh = MLSTM_HEADS
    ppc = SCAN_CHUNK // LANES
    gct = gc_ref[0].T
    glt = gl_ref[0].T
    n_c = gc_ref.shape[1] // SCAN_CHUNK
    n_l = gl_ref.shape[1] // SCAN_CHUNK
    zeros = jnp.zeros((nh, SCAN_CHUNK), F32)
    pieces = [(gct, j) for j in range(n_c * ppc)] + [(glt, j) for j in range(n_l * ppc)]
    chunks = [(oc_ref, c) for c in range(n_c)] + [(ol_ref, c) for c in range(n_l)]
    n_all = len(chunks)

    def stack(row0):
        return jnp.concatenate([src[row0:row0 + nh, j * LANES:(j + 1) * LANES] for src, j in pieces], axis=0)

    def piece(x, k):
        return x[k * nh:(k + 1) * nh]

    def earlier(k, rev):
        c = k // ppc
        return range(k + 1, (c + 1) * ppc) if rev else range(c * ppc, k)

    for d in range(2):
        rev = d == 1
        edge = (lambda x: x[:, 0:1]) if rev else (lambda x: x[:, LANES - 1:LANES])
        ig = stack(2 * d * nh)
        fg = stack((2 * d + 1) * nh)
        lf = jnp.minimum(fg, 0.0) - jnp.log1p(jnp.exp(-jnp.abs(fg)))
        bp = _lane_scan(lf, jnp.add, 0.0, rev)
        b = jnp.concatenate([piece(bp, k) + sum(piece(edge(bp), o) for o in earlier(k, rev))
                             for k in range(len(pieces))], axis=0)
        cmp_ = _lane_scan(ig - b, jnp.maximum, -jnp.inf, rev)
        cm = jnp.concatenate([functools.reduce(jnp.maximum, [piece(edge(cmp_), o) for o in earlier(k, rev)],
                                               piece(cmp_, k)) for k in range(len(pieces))], axis=0)
        last = (lambda c: c * ppc) if rev else (lambda c: c * ppc + ppc - 1)
        b_end_c = [piece(edge(b), last(c)) for c in range(n_all)]
        b_end = jnp.concatenate([b_end_c[k // ppc] for k in range(len(pieces))], axis=0)
        g = b_end - b + ig
        gmax_p = jnp.max(g, axis=-1, keepdims=True)
        gmax_c = [functools.reduce(jnp.maximum, [piece(gmax_p, c * ppc + p) for p in range(ppc)])
                  for c in range(n_all)]
        order = list(range(n_all))
        if rev:
            order = list(reversed(range(n_c))) + list(reversed(range(n_c, n_all)))
        m = jnp.zeros((nh, 1), F32)
        m_in, m_out = [None] * n_all, [None] * n_all
        for c in order:
            m_in[c] = m
            m = jnp.maximum(b_end_c[c] + m, gmax_c[c])
            m_out[c] = m
        m_in = jnp.concatenate([m_in[k // ppc] for k in range(len(pieces))], axis=0)
        m_out = jnp.concatenate([m_out[k // ppc] for k in range(len(pieces))], axis=0)
        mx = jnp.maximum(cm, m_in)
        rows = (-mx, b - ig, jnp.exp(m_in - mx), jnp.exp(-(b + mx)), jnp.exp(g - m_out),
                jnp.broadcast_to(jnp.exp(b_end + m_in - m_out), b.shape))
        for k, (dst, c) in enumerate(chunks):
            for r, val in enumerate(rows):
                dst[0, d, c, r] = jnp.concatenate([piece(val, k * ppc + p) for p in range(ppc)], axis=1)
            dst[0, d, c, 6] = zeros
            dst[0, d, c, 7] = zeros


def _mlstm_gates(gates_ctx, gates_lat):
    bsz = gates_lat.shape[0]
    n_c, n_l = CTX_LEN // SCAN_CHUNK, SEQ // SCAN_CHUNK

    def out(n):
        shape = (bsz, 2, n, GP_ROWS, MLSTM_HEADS, SCAN_CHUNK)
        return (pl.BlockSpec((1,) + shape[1:], lambda b: (b, 0, 0, 0, 0, 0)), jax.ShapeDtypeStruct(shape, F32))

    (spec_c, shape_c), (spec_l, shape_l) = out(n_c), out(n_l)
    return pl.pallas_call(
        _mlstm_gates_kernel,
        grid=(bsz,),
        in_specs=[pl.BlockSpec((1, CTX_LEN, GATE_LANES), lambda b: (b, 0, 0)),
                  pl.BlockSpec((1, SEQ, GATE_LANES), lambda b: (b, 0, 0))],
        out_specs=[spec_c, spec_l],
        out_shape=[shape_c, shape_l],
        compiler_params=_cparams(("parallel",), 32),
        name="mlstm_gates",
    )(gates_ctx, gates_lat)


def _mlstm_scan_kernel(qc_ref, ktc_ref, vc_ref, xcc_ref, gpc_ref,
                       ql_ref, ktl_ref, vl_ref, xcl_ref, gpl_ref, ng_ref, sk_ref, zc_ref, zl_ref,
                       oc_ref, ol_ref, hc_scr, hl_scr, st_scr):
    L = SCAN_CHUNK
    hd = MLSTM_HEAD_DIM
    n_c = qc_ref.shape[1] // L
    n_l = ql_ref.shape[1] // L
    lane = lax.broadcasted_iota(jnp.int32, (L, LANES), 1)
    ones_col = jnp.where(lane == 0, 1.0, 0.0).astype(BF16)
    tt = lax.broadcasted_iota(jnp.int32, (L, L), 0)
    ss = lax.broadcasted_iota(jnp.int32, (L, L), 1)
    masks = (ss <= tt, ss >= tt)

    hc_scr[...] = jnp.zeros_like(hc_scr)
    hl_scr[...] = jnp.zeros_like(hl_scr)
    st_scr[...] = jnp.zeros_like(st_scr)

    def chunk(d, q_ref, kt_ref, v_ref, gp_ref, h_scr, c):
        rows = gp_ref[0, d, 0, c]
        cols = rows.T
        a_col, wst, einv = cols[:, 0:1], cols[:, 2:3], cols[:, 3:4]
        c_row, ws_row, decay = rows[1:2, :], rows[4:5, :], rows[5:6, 0:1]
        r0 = pl.multiple_of(c * L, L)
        q = q_ref[0, pl.ds(r0, L), :]
        kt = kt_ref[0, 0, c]
        vx = jnp.concatenate([v_ref[0, pl.ds(r0, L), :], ones_col], axis=1)
        w = jnp.exp(jnp.where(masks[d], a_col - c_row, -jnp.inf))
        s = (jnp.dot(q, kt, preferred_element_type=F32) * w).astype(BF16)
        kw = (kt.astype(F32) * ws_row).astype(BF16)
        both = jnp.dot(jnp.concatenate([s, kw], axis=0), vx, preferred_element_type=F32)
        state = st_scr[d]
        tot = wst * jnp.dot(q, state.astype(BF16), preferred_element_type=F32) + both[:L]
        den = tot[:, hd:hd + 1]
        h = tot[:, :hd] / jnp.maximum(jnp.abs(den), einv)
        h_scr[pl.ds(r0, L), :] = h_scr[pl.ds(r0, L), :] + h
        st_scr[d] = decay * state + both[L:]

    for j in range(n_c):
        chunk(0, qc_ref, ktc_ref, vc_ref, gpc_ref, hc_scr, j)
        chunk(1, qc_ref, ktc_ref, vc_ref, gpc_ref, hc_scr, n_c - 1 - j)

    def lat_step(j, carry):
        chunk(0, ql_ref, ktl_ref, vl_ref, gpl_ref, hl_scr, j)
        chunk(1, ql_ref, ktl_ref, vl_ref, gpl_ref, hl_scr, n_l - 1 - j)
        return carry

    lax.fori_loop(0, n_l, lat_step, 0, unroll=True)

    ng = ng_ref[...]
    sk = sk_ref[...]

    def finish(h_scr, xc_ref, z_ref, o_ref, r0, n):
        h = h_scr[pl.ds(r0, n), :]
        mu = jnp.mean(h, axis=-1, keepdims=True)
        hcen = h - mu
        hn = hcen * lax.rsqrt(jnp.mean(hcen * hcen, axis=-1, keepdims=True) + RMS_EPS)
        o = hn * ng + sk * xc_ref[0, pl.ds(r0, n), :].astype(F32)
        o_ref[0, pl.ds(r0, n), :] = _gate(o, z_ref[0, pl.ds(r0, n), :]).astype(BF16)

    finish(hc_scr, xcc_ref, zc_ref, oc_ref, 0, CTX_LEN)

    def fin_step(i, carry):
        finish(hl_scr, xcl_ref, zl_ref, ol_ref, pl.multiple_of(jnp.asarray(i, jnp.int32) * 256, 256), 256)
        return carry

    lax.fori_loop(0, SEQ // 256, fin_step, 0, unroll=2)


def _mlstm_scan(prep_c, gp_c, prep_l, gp_l, norm_g, skip, proj_c, proj_l):
    bsz = prep_l[0].shape[0]
    hd = MLSTM_HEAD_DIM

    def z_spec(t):
        return pl.BlockSpec((1, t, hd), lambda b, h: (b, 0, MLSTM_HEADS + h))

    def seq_specs(t):
        nc = t // SCAN_CHUNK
        seq = pl.BlockSpec((1, t, hd), lambda b, h: (b, 0, h))
        return [seq,
                pl.BlockSpec((1, 1, nc, hd, SCAN_CHUNK), lambda b, h: (b, h, 0, 0, 0)),
                seq, seq,
                pl.BlockSpec((1, 2, 1, nc, GP_ROWS, SCAN_CHUNK), lambda b, h: (b, 0, h, 0, 0, 0))]

    vec = pl.BlockSpec((1, hd), lambda b, h: (0, h))
    return pl.pallas_call(
        _mlstm_scan_kernel,
        grid=(bsz, MLSTM_HEADS),
        in_specs=seq_specs(CTX_LEN) + seq_specs(SEQ) + [vec, vec, z_spec(CTX_LEN), z_spec(SEQ)],
        out_specs=[pl.BlockSpec((1, CTX_LEN, hd), lambda b, h: (b, 0, h)),
                   pl.BlockSpec((1, SEQ, hd), lambda b, h: (b, 0, h))],
        out_shape=[jax.ShapeDtypeStruct((bsz, CTX_LEN, D_INNER), BF16),
                   jax.ShapeDtypeStruct((bsz, SEQ, D_INNER), BF16)],
        scratch_shapes=[pltpu.VMEM((CTX_LEN, hd), F32), pltpu.VMEM((SEQ, hd), F32),
                        pltpu.VMEM((2, hd, hd + LANES), F32)],
        compiler_params=_cparams(("parallel", "parallel"), 40),
        name="mlstm_scan",
    )(*prep_c, gp_c, *prep_l, gp_l, norm_g.reshape(1, D_INNER), skip.reshape(1, D_INNER), proj_c, proj_l)


def _blockdiag_dense(w):
    per_head = MLSTM_HEAD_DIM // MLSTM_QKV_BLOCK
    wh = w.reshape(MLSTM_HEADS, per_head, MLSTM_QKV_BLOCK, MLSTM_QKV_BLOCK)
    eye = jnp.eye(per_head, dtype=w.dtype)
    dense = wh[:, :, :, None, :] * eye[None, :, None, :, None]
    return dense.reshape(MLSTM_HEADS, MLSTM_HEAD_DIM, MLSTM_HEAD_DIM).astype(BF16)


def _mlstm_mixer(proj_lat, proj_ctx, conv_w, conv_b, wq, wk, wv, w_gate, b_gate, norm_g, skip):
    n_gate = 4 * MLSTM_HEADS
    wg = jnp.pad(w_gate, ((0, 0), (0, GATE_LANES - n_gate))).astype(BF16)
    wg = wg.reshape(3, MLSTM_HEADS, MLSTM_HEAD_DIM, GATE_LANES)
    bg = jnp.pad(b_gate, (0, GATE_LANES - n_gate)).reshape(1, GATE_LANES)
    bds = (_blockdiag_dense(wq), _blockdiag_dense(wk), _blockdiag_dense(wv))
    *prep_c, gates_c = _mlstm_prep(proj_ctx, conv_w, conv_b, *bds, wg, bg)
    *prep_l, gates_l = _mlstm_prep(proj_lat, conv_w, conv_b, *bds, wg, bg)
    gp_c, gp_l = _mlstm_gates(gates_c, gates_l)
    gp_c = jnp.transpose(gp_c, (0, 1, 4, 2, 3, 5))
    gp_l = jnp.transpose(gp_l, (0, 1, 4, 2, 3, 5))
    return _mlstm_scan(prep_c, gp_c, prep_l, gp_l, norm_g, skip, proj_ctx, proj_lat)


def kernel(x, c, ctx, c_ctx, norm_g, w_ada, b_ada, final_g, mlstm_w_in, mlstm_conv_w, mlstm_conv_b, mlstm_wq, mlstm_wk, mlstm_wv, mlstm_w_gate, mlstm_b_gate, mlstm_norm_g, mlstm_skip, mlstm_w_out, diff_w_in, diff_lq1, diff_lk1, diff_lq2, diff_lk2, diff_subln_g, diff_w_out, na_w_in, na_rpb, na_w_out, swa_w_in, swa_sink, swa_w_out):
    bsz = x.shape[0]
    ctx_row = bsz
    mod_rows = -(-(bsz + 1) // 8) * 8
    cvec = jnp.zeros((mod_rows, D_MODEL), F32).at[:bsz].set(c).at[ctx_row].set(c_ctx)
    mods_all = _adaln(cvec, w_ada, b_ada)
    rope_tabs = _rope_tables()

    xl = x.reshape(bsz * SEQ, D_MODEL)
    xc = ctx.reshape(bsz * CTX_LEN, D_MODEL)
    lat = dict(rows_per_mod=SEQ, mod_row0=0)
    con = dict(rows_per_mod=bsz * CTX_LEN, mod_row0=ctx_row)
    tm_ctx = min(1024, bsz * CTX_LEN)

    def in_both(i, w, tn, rope_cols=()):
        mods = mods_all[i].reshape(mod_rows, 1, 3 * D_MODEL)
        wb = w.astype(BF16)
        pl_ = _inproj(xl, norm_g[i], mods, wb, tm=SEQ, tn=tn, rope_cols=rope_cols, rope_tabs=rope_tabs, **lat)
        pc_ = _inproj(xc, norm_g[i], mods, wb, tm=tm_ctx, tn=tn, **con)
        return mods, pl_, pc_

    def out_both(mods, o_lat, o_ctx, w_out, need_ctx, final=None, in_place=True):
        wb = w_out.astype(BF16)
        new_l = _outproj(o_lat.reshape(bsz * SEQ, D_INNER), wb, xl, mods, tm=1024,
                         final_g=final, in_place=in_place, **lat)
        new_c = xc
        if need_ctx:
            new_c = _outproj(o_ctx.reshape(bsz * CTX_LEN, D_INNER), wb, xc, mods, tm=tm_ctx, **con)
        return new_l, new_c

    mods, p_l, p_c = in_both(0, mlstm_w_in[0], 1024)
    o_c, o_l = _mlstm_mixer(p_l.reshape(bsz, SEQ, -1), p_c.reshape(bsz, CTX_LEN, -1), mlstm_conv_w[0],
                            mlstm_conv_b[0], mlstm_wq[0], mlstm_wk[0], mlstm_wv[0], mlstm_w_gate[0],
                            mlstm_b_gate[0], mlstm_norm_g[0], mlstm_skip[0])
    xl, xc = out_both(mods, o_l, o_c, mlstm_w_out[0], True, in_place=False)

    wd = diff_w_in[0]
    wd = jnp.concatenate([_pair_interleave(wd[:, :D_INNER]), _pair_interleave(wd[:, D_INNER:2 * D_INNER]),
                          wd[:, 2 * D_INNER:]], axis=1)
    mods, p_l, p_c = in_both(1, wd, 1024, rope_cols=((0, 2 * D_INNER),))
    p_l3, p_c3 = p_l.reshape(bsz, SEQ, -1), p_c.reshape(bsz, CTX_LEN, -1)
    lam_init = 0.8 - 0.6 * math.exp(-0.3 * 1)
    lqk = jnp.stack([diff_lq1[0], diff_lk1[0], diff_lq2[0], diff_lk2[0]])
    o_l = _diff_attn(p_l3, p_c3, lqk, diff_subln_g[0], tq=512, lam_init=lam_init)
    o_c = _ctx_attn(p_c3, lqk, diff_subln_g[0], lam_init)
    xl, xc = out_both(mods, o_l, o_c, diff_w_out[0], True)

    mods, p_l, p_c = in_both(2, na_w_in[0], 1024)
    p_l3, p_c3 = p_l.reshape(bsz, SEQ, -1), p_c.reshape(bsz, CTX_LEN, -1)
    o_l = _na_attn(p_l3, p_c3, _na_bias_tiles(na_rpb[0]))
    o_c = _ctx_attn(p_c3)
    xl, xc = out_both(mods, o_l, o_c, na_w_out[0], True)

    kvw = SWA_KV_HEADS * HEAD_DIM
    w = swa_w_in[0]
    wq, wk, wv, wz = (w[:, :D_INNER], w[:, D_INNER:D_INNER + kvw], w[:, D_INNER + kvw:D_INNER + 2 * kvw],
                      w[:, D_INNER + 2 * kvw:])

    def twice(m):
        m = m.reshape(D_MODEL, SWA_KV_HEADS, 1, HEAD_DIM)
        return jnp.broadcast_to(m, (D_MODEL, SWA_KV_HEADS, 2, HEAD_DIM)).reshape(D_MODEL, SWA_KV_HEADS * LANES)

    w3 = jnp.concatenate([_pair_interleave(wq), wz, _pair_interleave(wk, dup=True), twice(wv)], axis=1)
    mods, p_l, p_c = in_both(3, w3, 1024, rope_cols=((0, D_INNER), (SWA_K_COL, SWA_V_COL)))
    sink_rows = jnp.broadcast_to(swa_sink[0][:, None], (SWA_Q_HEADS, LANES))
    o_l = _swa_attn(p_l.reshape(bsz, SEQ, -1), p_c.reshape(bsz, CTX_LEN, -1), sink_rows)
    xl, _ = out_both(mods, o_l, None, swa_w_out[0], False, final=final_g)
    return xl.reshape(bsz, SEQ, D_MODEL)
```

```python
import functools
import math

import numpy as np
import jax
import jax.numpy as jnp
from jax import lax
from jax.experimental import pallas as pl
from jax.experimental.pallas import tpu as pltpu

F32 = jnp.float32
BF16 = jnp.bfloat16

D_MODEL = 1024
SEQ = 2048
CTX_LEN = 256
DEPTH = 4
D_INNER = 2 * D_MODEL
GRID_W = 64
RMS_EPS = 1e-6
ROPE_THETA = 10000.0

MLSTM_HEADS = 8
MLSTM_HEAD_DIM = D_INNER // MLSTM_HEADS
MLSTM_QKV_BLOCK = 4
MLSTM_CONV = 5
SCAN_CHUNK = 256

DIFF_HEADS = 16
DIFF_HPS = 2
HEAD_DIM = 64
NA_HEADS = 32
NA_WIN_ROWS = 8
NA_WIN_COLS = 16
SWA_Q_HEADS = 32
SWA_KV_HEADS = 4
SWA_WINDOW = 128

LANES = 128
MIB = 1024 * 1024
NEG_BIG = -1e30


def _cparams(semantics, vmem_mib):
    return pltpu.CompilerParams(dimension_semantics=semantics, vmem_limit_bytes=vmem_mib * MIB)


def _silu(v):
    h = 0.5 * v
    return h + h * jnp.tanh(h)


def _adaln_kernel(c_ref, w_ref, b_ref, o_ref):
    s = _silu(c_ref[...])
    o_ref[0] = jnp.dot(s, w_ref[0], preferred_element_type=F32,
                       precision=lax.Precision.HIGHEST) + b_ref[0]


def _adaln(cvec, w_ada, b_ada):
    rows = cvec.shape[0]
    return pl.pallas_call(
        _adaln_kernel,
        grid=(DEPTH, 3),
        in_specs=[
            pl.BlockSpec((rows, D_MODEL), lambda i, j: (0, 0)),
            pl.BlockSpec((1, D_MODEL, D_MODEL), lambda i, j: (i, 0, j)),
            pl.BlockSpec((1, 1, D_MODEL), lambda i, j: (i, 0, j)),
        ],
        out_specs=pl.BlockSpec((1, rows, D_MODEL), lambda i, j: (i, 0, j)),
        out_shape=jax.ShapeDtypeStruct((DEPTH, rows, 3 * D_MODEL), F32),
        compiler_params=_cparams(("parallel", "parallel"), 32),
        name="adaln",
    )(cvec, w_ada, b_ada.reshape(DEPTH, 1, 3 * D_MODEL))


ROPE_Q = HEAD_DIM // 4


def _rope_tables():
    inv = np.power(ROPE_THETA, -np.arange(0, 2 * ROPE_Q, 2, dtype=np.float64) / (2 * ROPE_Q))
    t = np.arange(SEQ)
    ang = np.concatenate([(t // GRID_W).astype(np.float64)[:, None] * inv,
                          (t % GRID_W).astype(np.float64)[:, None] * inv], axis=1)
    cos = np.tile(np.cos(ang), (1, 4))
    sin = np.concatenate([-np.tile(np.sin(ang), (1, 2)), np.tile(np.sin(ang), (1, 2))], axis=1)
    return jnp.asarray(cos, F32), jnp.asarray(sin, F32)


def _pair_interleave(w, dup=False):
    d, n = w.shape
    h = n // HEAD_DIM
    w = w.reshape(d, h, 2, 2, ROPE_Q)
    if dup:
        w = jnp.broadcast_to(w[:, :, None], (d, h, 2, 2, 2, ROPE_Q))
    else:
        w = w.reshape(d, h // 2, 2, 2, 2, ROPE_Q)
    return jnp.transpose(w, (0, 1, 4, 2, 3, 5)).reshape(d, -1)


def _rope_chunk(x, cos, sin):
    return x * cos + pltpu.roll(x, LANES // 2, axis=1) * sin


def _inproj_kernel(*refs, tn, rope_plan):
    has_rope = bool(rope_plan)
    if has_rope:
        x_ref, g_ref, sh_ref, sc_ref, w_ref, cos_ref, sin_ref, o_ref, h_scr = refs
    else:
        x_ref, g_ref, sh_ref, sc_ref, w_ref, o_ref, h_scr = refs
    j = pl.program_id(1)

    @pl.when(j == 0)
    def _():
        x = x_ref[...]
        ms = jnp.mean(x * x, axis=-1, keepdims=True)
        y = x * lax.rsqrt(ms + RMS_EPS) * g_ref[...]
        h_scr[...] = (y * (1.0 + sc_ref[0]) + sh_ref[0]).astype(BF16)

    acc = jnp.dot(h_scr[...], w_ref[...], preferred_element_type=F32)

    def store_plain():
        o_ref[...] = acc.astype(BF16)

    def store_rope(flags):
        cos = cos_ref[...]
        sin = sin_ref[...]
        for c, roped in enumerate(flags):
            blk = acc[:, c * LANES:(c + 1) * LANES]
            if roped:
                blk = _rope_chunk(blk, cos, sin)
            o_ref[:, c * LANES:(c + 1) * LANES] = blk.astype(BF16)

    if not has_rope:
        store_plain()
    else:
        any_rope = None
        for flags, tiles in rope_plan:
            hit = functools.reduce(jnp.logical_or, [j == t for t in tiles])
            pl.when(hit)(functools.partial(store_rope, flags))
            any_rope = hit if any_rope is None else jnp.logical_or(any_rope, hit)
        pl.when(jnp.logical_not(any_rope))(store_plain)


def _inproj(x2d, norm_g, mods, w, *, rows_per_mod, mod_row0, tm, tn, rope_cols=(), rope_tabs=None):
    rows, _ = x2d.shape
    n = w.shape[1]
    assert rows % tm == 0 and n % tn == 0 and rows_per_mod % tm == 0
    tiles_per_mod = rows_per_mod // tm
    assert all(lo % LANES == 0 and hi % LANES == 0 for lo, hi in rope_cols)
    plan = {}
    for t in range(n // tn):
        flags = tuple(any(lo <= t * tn + c * LANES < hi for lo, hi in rope_cols) for c in range(tn // LANES))
        if any(flags):
            plan.setdefault(flags, []).append(t)
    rope_plan = tuple((flags, tuple(tiles)) for flags, tiles in plan.items())
    has_rope = bool(rope_plan)
    in_specs = [
        pl.BlockSpec((tm, D_MODEL), lambda i, j: (i, 0)),
        pl.BlockSpec((1, D_MODEL), lambda i, j: (0, 0)),
        pl.BlockSpec((1, 1, D_MODEL), lambda i, j: (mod_row0 + i // tiles_per_mod, 0, 0)),
        pl.BlockSpec((1, 1, D_MODEL), lambda i, j: (mod_row0 + i // tiles_per_mod, 0, 1)),
        pl.BlockSpec((D_MODEL, tn), lambda i, j: (0, j)),
    ]
    args = [x2d, norm_g.reshape(1, D_MODEL), mods, mods, w]
    if has_rope:
        assert SEQ % tm == 0
        seq_tiles = SEQ // tm
        in_specs += [pl.BlockSpec((tm, LANES), lambda i, j: (i % seq_tiles, 0))] * 2
        args += list(rope_tabs)
    vmem = (2 * tm * D_MODEL * 4 + tm * D_MODEL * 2 + 2 * D_MODEL * tn * 2 + 2 * tm * tn * 2
            + tm * tn * 4 + (4 * tm * LANES * 4 if has_rope else 0)) // MIB + 8
    return pl.pallas_call(
        functools.partial(_inproj_kernel, tn=tn, rope_plan=rope_plan),
        grid=(rows // tm, n // tn),
        in_specs=in_specs,
        out_specs=pl.BlockSpec((tm, tn), lambda i, j: (i, j)),
        out_shape=jax.ShapeDtypeStruct((rows, n), BF16),
        scratch_shapes=[pltpu.VMEM((tm, D_MODEL), BF16)],
        compiler_params=_cparams(("parallel", "arbitrary"), vmem),
        name="inproj",
    )(*args)


def _gate(o, z_blk):
    return o * _silu(z_blk.astype(F32))


def _outproj_kernel(o_ref, w_ref, x_ref, gt_ref, *rest, final):
    y = jnp.dot(o_ref[...], w_ref[...], preferred_element_type=F32)
    xn = x_ref[...] + gt_ref[0] * y
    if final:
        fg_ref, out_ref = rest
        ms = jnp.mean(xn * xn, axis=-1, keepdims=True)
        xn = xn * lax.rsqrt(ms + RMS_EPS) * fg_ref[...]
    else:
        (out_ref,) = rest
    out_ref[...] = xn


def _outproj(o2d, w_out, x2d, mods, *, rows_per_mod, mod_row0, tm, final_g=None, in_place=True):
    rows = x2d.shape[0]
    assert rows % tm == 0 and rows_per_mod % tm == 0
    tiles_per_mod = rows_per_mod // tm
    final = final_g is not None
    in_specs = [
        pl.BlockSpec((tm, D_INNER), lambda i: (i, 0)),
        pl.BlockSpec((D_INNER, D_MODEL), lambda i: (0, 0)),
        pl.BlockSpec((tm, D_MODEL), lambda i: (i, 0)),
        pl.BlockSpec((1, 1, D_MODEL), lambda i: (mod_row0 + i // tiles_per_mod, 0, 2)),
    ]
    args = [o2d, w_out, x2d, mods]
    if final:
        in_specs.append(pl.BlockSpec((1, D_MODEL), lambda i: (0, 0)))
        args.append(final_g.reshape(1, D_MODEL))
    vmem = (2 * tm * D_INNER * 2 + 2 * D_INNER * D_MODEL * 2 + 5 * tm * D_MODEL * 4) // MIB + 8
    return pl.pallas_call(
        functools.partial(_outproj_kernel, final=final),
        grid=(rows // tm,),
        in_specs=in_specs,
        out_specs=pl.BlockSpec((tm, D_MODEL), lambda i: (i, 0)),
        out_shape=jax.ShapeDtypeStruct((rows, D_MODEL), F32),
        input_output_aliases={2: 0} if in_place else {},
        compiler_params=_cparams(("parallel",), vmem),
        name="outproj",
    )(*args)


def _half_masks():
    lane = lax.broadcasted_iota(jnp.int32, (1, LANES), 1)
    return lane < HEAD_DIM, lane >= HEAD_DIM


def _nt(a, b):
    return lax.dot_general(a, b, (((1,), (1,)), ((), ())), preferred_element_type=F32)


LOG2E = math.log2(math.e)
QK_SCALE = HEAD_DIM ** -0.5 * LOG2E


def _split_heads(q, interleaved=False):
    lo, hi = _half_masks()
    if interleaved:
        lane = lax.broadcasted_iota(jnp.int32, (1, LANES), 1)
        lo = (lane % HEAD_DIM) < HEAD_DIM // 2
        hi = jnp.logical_not(lo)
    qs = q.astype(F32) * QK_SCALE
    return jnp.concatenate([jnp.where(lo, qs, 0.0), jnp.where(hi, qs, 0.0)], axis=0).astype(BF16)


def _ones_col(rows):
    lane = lax.broadcasted_iota(jnp.int32, (rows, LANES), 1)
    return jnp.where(lane == 0, 1.0, 0.0).astype(BF16)


def _diff_attn_kernel(q_ref, kl_ref, vl_ref, z_ref, kc_ref, vc_ref, lqk_ref, g_ref, o_ref, sa_scr, sb_scr,
                      *, lam_init, tq):
    nq = q_ref.shape[1] // tq
    lqk = lqk_ref[...]
    s1 = jnp.sum(lqk[0:1] * lqk[1:2], axis=-1, keepdims=True)
    s2 = jnp.sum(lqk[2:3] * lqk[3:4], axis=-1, keepdims=True)
    lam = jnp.exp(s1) - jnp.exp(s2) + lam_init
    bufs = (sa_scr, sb_scr)
    units = [(hh, i) for hh in range(DIFF_HPS) for i in range(nq)]

    def logits(n):
        hh, i = units[n]
        hs = slice(hh * LANES, (hh + 1) * LANES)
        s_scr = bufs[n % 2]
        qq = _split_heads(q_ref[0, i * tq:(i + 1) * tq, hs], interleaved=True)
        s_scr[:, :CTX_LEN] = _nt(qq, kc_ref[0, :, hs])
        s_scr[:, CTX_LEN:] = _nt(qq, kl_ref[0, :, hs])

    def finish(n):
        hh, i = units[n]
        hs = slice(hh * LANES, (hh + 1) * LANES)
        s = bufs[n % 2][...]
        m = jnp.max(s, axis=-1, keepdims=True)
        p = jnp.exp2(s - m)
        den = jnp.sum(p, axis=-1, keepdims=True)
        r = 1.0 / den
        pc = p * jnp.concatenate([r[:tq], -lam * r[tq:]], axis=0)
        a = (pc[:tq] + pc[tq:]).astype(BF16)
        o = (jnp.dot(a[:, :CTX_LEN], vc_ref[0, :, hs], preferred_element_type=F32)
             + jnp.dot(a[:, CTX_LEN:], vl_ref[0, :, hs], preferred_element_type=F32))
        ms = jnp.mean(o * o, axis=-1, keepdims=True)
        o = o * lax.rsqrt(ms + RMS_EPS) * g_ref[...] * (1.0 - lam_init)
        o_ref[0, i * tq:(i + 1) * tq, hs] = _gate(o, z_ref[0, i * tq:(i + 1) * tq, hs]).astype(BF16)

    logits(0)
    for n in range(len(units)):
        if n + 1 < len(units):
            logits(n + 1)
        finish(n)


def _diff_attn(proj_lat, proj_ctx, lqk, subln_g, *, tq, lam_init):
    bsz = proj_lat.shape[0]
    w = DIFF_HPS * LANES
    kb, vb, zb = D_INNER // w, 2 * D_INNER // w, 3 * D_INNER // w
    nk = CTX_LEN + SEQ
    score_bytes = 2 * tq * nk * 4
    vmem = (5 * score_bytes) // MIB + 16
    return pl.pallas_call(
        functools.partial(_diff_attn_kernel, lam_init=lam_init, tq=tq),
        grid=(bsz, DIFF_HEADS // DIFF_HPS),
        in_specs=[pl.BlockSpec((1, SEQ, w), lambda b, h: (b, 0, h)),
                  pl.BlockSpec((1, SEQ, w), lambda b, h: (b, 0, kb + h)),
                  pl.BlockSpec((1, SEQ, w), lambda b, h: (b, 0, vb + h)),
                  pl.BlockSpec((1, SEQ, w), lambda b, h: (b, 0, zb + h)),
                  pl.BlockSpec((1, CTX_LEN, w), lambda b, h: (b, 0, kb + h)),
                  pl.BlockSpec((1, CTX_LEN, w), lambda b, h: (b, 0, vb + h)),
                  pl.BlockSpec((4, HEAD_DIM), lambda b, h: (0, 0)),
                  pl.BlockSpec((1, LANES), lambda b, h: (0, 0))],
        out_specs=pl.BlockSpec((1, SEQ, w), lambda b, h: (b, 0, h)),
        out_shape=jax.ShapeDtypeStruct((bsz, SEQ, D_INNER), BF16),
        scratch_shapes=[pltpu.VMEM((2 * tq, nk), F32), pltpu.VMEM((2 * tq, nk), F32)],
        compiler_params=_cparams(("parallel", "parallel"), vmem),
        name="diff_attn",
    )(proj_lat, proj_lat, proj_lat, proj_lat, proj_ctx, proj_ctx, lqk, subln_g.reshape(1, LANES))


CTX_HB = 4


def _ctx_attn_kernel(*refs, diff, lam_init):
    if diff:
        q_ref, k_ref, v_ref, z_ref, lqk_ref, g_ref, o_ref = refs
        lqk = lqk_ref[...]
        lam = (jnp.exp(jnp.sum(lqk[0:1] * lqk[1:2], axis=-1, keepdims=True))
               - jnp.exp(jnp.sum(lqk[2:3] * lqk[3:4], axis=-1, keepdims=True)) + lam_init)
    else:
        q_ref, k_ref, v_ref, z_ref, o_ref = refs
    t = q_ref.shape[1]
    lo, _ = _half_masks()
    for hb in range(CTX_HB):
        sl = slice(hb * LANES, (hb + 1) * LANES)
        qq = _split_heads(q_ref[0, :, sl], interleaved=diff)
        s = _nt(qq, k_ref[0, :, sl])
        p = jnp.exp2(s - jnp.max(s, axis=-1, keepdims=True))
        den = jnp.sum(p, axis=-1, keepdims=True)
        if diff:
            a = (p[:t] - (lam * den[:t] / den[t:]) * p[t:]).astype(BF16)
            o = jnp.dot(a, v_ref[0, :, sl], preferred_element_type=F32) / den[:t]
            ms = jnp.mean(o * o, axis=-1, keepdims=True)
            o = o * lax.rsqrt(ms + RMS_EPS) * g_ref[...] * (1.0 - lam_init)
        else:
            o = jnp.dot(p.astype(BF16), v_ref[0, :, sl], preferred_element_type=F32) / den
            o = jnp.where(lo, o[:t], o[t:])
        o_ref[0, :, sl] = _gate(o, z_ref[0, :, sl]).astype(BF16)


def _ctx_attn(proj_ctx, lqk=None, subln_g=None, lam_init=0.0):
    bsz = proj_ctx.shape[0]
    diff = lqk is not None
    w = CTX_HB * LANES
    kb, vb, zb = D_INNER // w, 2 * D_INNER // w, 3 * D_INNER // w
    in_specs = [pl.BlockSpec((1, CTX_LEN, w), lambda b, h: (b, 0, h)),
                pl.BlockSpec((1, CTX_LEN, w), lambda b, h: (b, 0, kb + h)),
                pl.BlockSpec((1, CTX_LEN, w), lambda b, h: (b, 0, vb + h)),
                pl.BlockSpec((1, CTX_LEN, w), lambda b, h: (b, 0, zb + h))]
    args = [proj_ctx, proj_ctx, proj_ctx, proj_ctx]
    if diff:
        in_specs += [pl.BlockSpec((4, HEAD_DIM), lambda b, h: (0, 0)), pl.BlockSpec((1, LANES), lambda b, h: (0, 0))]
        args += [lqk, subln_g.reshape(1, LANES)]
    return pl.pallas_call(
        functools.partial(_ctx_attn_kernel, diff=diff, lam_init=lam_init),
        grid=(bsz, D_INNER // w),
        in_specs=in_specs,
        out_specs=pl.BlockSpec((1, CTX_LEN, w), lambda b, h: (b, 0, h)),
        out_shape=jax.ShapeDtypeStruct((bsz, CTX_LEN, D_INNER), BF16),
        compiler_params=_cparams(("parallel", "parallel"), 32),
        name="ctx_attn",
    )(*args)


NA_ROWS = SEQ // GRID_W
NA_BLK_ROWS = 4
NA_BLK_Q = NA_BLK_ROWS * GRID_W
NA_NBLK = NA_ROWS // NA_BLK_ROWS
NA_WIN_R = NA_BLK_ROWS + NA_WIN_ROWS
NA_KWIN = NA_WIN_R * GRID_W
NA_CTX_CHUNK = 512
NA_MAX_LANE = LANES + 1
NA_PPS = 2


def _na_win_start(i):
    lo, hi = 0, NA_ROWS - NA_WIN_R
    start = i * NA_BLK_ROWS - NA_WIN_ROWS // 2
    return np.clip(start, lo, hi) if isinstance(i, (int, np.integer)) else jnp.clip(start, lo, hi)


NA_NDR = 2 * NA_WIN_ROWS - 1


def _na_bias_plan():
    plan = np.full((3, NA_BLK_ROWS, NA_WIN_R), -1, np.int32)
    for pat, i in enumerate((0, 1, NA_NBLK - 1)):
        ws = int(_na_win_start(i))
        for qr in range(NA_BLK_ROWS):
            r = i * NA_BLK_ROWS + qr
            rs = min(max(r - NA_WIN_ROWS // 2, 0), NA_ROWS - NA_WIN_ROWS)
            for a in range(NA_WIN_R):
                if rs <= ws + a < rs + NA_WIN_ROWS:
                    plan[pat, qr, a] = ws + a - r + NA_WIN_ROWS - 1
    return plan


def _na_bias_tiles(rpb):
    c = np.arange(GRID_W)
    cs = np.clip(c - NA_WIN_COLS // 2, 0, GRID_W - NA_WIN_COLS)
    kc = np.arange(GRID_W)
    col_ok = (kc[None, :] >= cs[:, None]) & (kc[None, :] < cs[:, None] + NA_WIN_COLS)
    dc_idx = np.clip(kc[None, :] - c[:, None], -(NA_WIN_COLS - 1), NA_WIN_COLS - 1) + NA_WIN_COLS - 1
    return jnp.where(col_ok[None, None], rpb[:, :, dc_idx] * LOG2E, NEG_BIG).astype(F32)


def _na_attn_kernel(q_ref, k_ref, v_ref, z_ref, kc_ref, vc_ref, t1_ref, o_ref, oc_scr, bias_ref):
    lo, _ = _half_masks()

    @pl.when(pl.program_id(1) == 0)
    def _():
        plan = _na_bias_plan()
        masked = jnp.full((GRID_W, GRID_W), NEG_BIG, F32)
        for hl in range(2 * NA_PPS):
            for pat in range(3):
                for qr in range(NA_BLK_ROWS):
                    tiles = [t1_ref[hl, int(d)] if d >= 0 else masked for d in plan[pat, qr]]
                    bias_ref[hl, pat, qr * GRID_W:(qr + 1) * GRID_W, :] = jnp.concatenate(tiles, axis=1)
    lane2 = lax.broadcasted_iota(jnp.int32, (1, 2 * LANES), 1)

    for pp in range(NA_PPS):
        ps = slice(pp * LANES, (pp + 1) * LANES)
        kctx = kc_ref[0, :, ps]
        vctx = jnp.concatenate([vc_ref[0, :, ps], _ones_col(CTX_LEN)], axis=1)
        for j in range(SEQ // NA_CTX_CHUNK):
            r0 = j * NA_CTX_CHUNK
            qq = _split_heads(q_ref[0, r0:r0 + NA_CTX_CHUNK, ps])
            s = _nt(qq, kctx)
            m = jnp.max(s, axis=-1, keepdims=True)
            oc = jnp.dot(jnp.exp2(s - m).astype(BF16), vctx, preferred_element_type=F32)
            oc = jnp.where(lane2 == NA_MAX_LANE, m, oc)
            oc_scr[2 * pp, r0:r0 + NA_CTX_CHUNK, :] = oc[:NA_CTX_CHUNK]
            oc_scr[2 * pp + 1, r0:r0 + NA_CTX_CHUNK, :] = oc[NA_CTX_CHUNK:]

    for pp in range(NA_PPS):
        ps = slice(pp * LANES, (pp + 1) * LANES)
        for i in range(NA_NBLK):
            pat = 0 if i == 0 else (2 if i == NA_NBLK - 1 else 1)
            t0 = i * NA_BLK_Q
            s0 = int(_na_win_start(i)) * GRID_W
            qq = _split_heads(q_ref[0, t0:t0 + NA_BLK_Q, ps])
            kw = k_ref[0, s0:s0 + NA_KWIN, ps]
            vw = jnp.concatenate([v_ref[0, s0:s0 + NA_KWIN, ps], _ones_col(NA_KWIN)], axis=1)
            s = _nt(qq, kw) + jnp.concatenate([bias_ref[2 * pp, pat], bias_ref[2 * pp + 1, pat]], axis=0)
            oc = jnp.concatenate([oc_scr[2 * pp, t0:t0 + NA_BLK_Q, :], oc_scr[2 * pp + 1, t0:t0 + NA_BLK_Q, :]],
                                 axis=0)
            mc = oc[:, NA_MAX_LANE:NA_MAX_LANE + 1]
            m = jnp.maximum(jnp.max(s, axis=-1, keepdims=True), mc)
            tot = (jnp.dot(jnp.exp2(s - m).astype(BF16), vw, preferred_element_type=F32)
                   + jnp.exp2(mc - m) * oc)
            out = tot[:, :LANES] / tot[:, LANES:LANES + 1]
            o = jnp.where(lo, out[:NA_BLK_Q], out[NA_BLK_Q:])
            o_ref[0, t0:t0 + NA_BLK_Q, ps] = _gate(o, z_ref[0, t0:t0 + NA_BLK_Q, ps]).astype(BF16)


def _na_attn(proj_lat, proj_ctx, bias_tiles):
    bsz = proj_lat.shape[0]
    w = NA_PPS * LANES
    nh = 2 * NA_PPS
    kb, vb, zb = D_INNER // w, 2 * D_INNER // w, 3 * D_INNER // w
    return pl.pallas_call(
        _na_attn_kernel,
        grid=(NA_HEADS // nh, bsz),
        in_specs=[pl.BlockSpec((1, SEQ, w), lambda h, b: (b, 0, h)),
                  pl.BlockSpec((1, SEQ, w), lambda h, b: (b, 0, kb + h)),
                  pl.BlockSpec((1, SEQ, w), lambda h, b: (b, 0, vb + h)),
                  pl.BlockSpec((1, SEQ, w), lambda h, b: (b, 0, zb + h)),
                  pl.BlockSpec((1, CTX_LEN, w), lambda h, b: (b, 0, kb + h)),
                  pl.BlockSpec((1, CTX_LEN, w), lambda h, b: (b, 0, vb + h)),
                  pl.BlockSpec((nh, NA_NDR, GRID_W, GRID_W), lambda h, b: (h, 0, 0, 0))],
        out_specs=pl.BlockSpec((1, SEQ, w), lambda h, b: (b, 0, h)),
        out_shape=jax.ShapeDtypeStruct((bsz, SEQ, D_INNER), BF16),
        scratch_shapes=[pltpu.VMEM((nh, SEQ, 2 * LANES), F32), pltpu.VMEM((nh, 3, NA_BLK_Q, NA_KWIN), F32)],
        compiler_params=_cparams(("parallel", "arbitrary"), 56),
        name="na_attn",
    )(proj_lat, proj_lat, proj_lat, proj_lat, proj_ctx, proj_ctx, bias_tiles)


SWA_TQ = 256
SWA_BAND = SWA_TQ + 2 * SWA_WINDOW
SWA_K_COL = 2 * D_INNER
SWA_V_COL = SWA_K_COL + SWA_KV_HEADS * LANES
SWA_N = SWA_V_COL + SWA_KV_HEADS * LANES


SWA_GROUP = SWA_Q_HEADS // SWA_KV_HEADS
SWA_GW = SWA_GROUP * HEAD_DIM


def _swa_attn_kernel(q_ref, k_ref, v_ref, z_ref, kc_ref, vc_ref, sink_ref, o_ref):
    lo, _ = _half_masks()
    kvh = pl.program_id(1)
    n_pair = SWA_GROUP // 2
    kctx = kc_ref[0]
    vctx = jnp.concatenate([vc_ref[0], _ones_col(CTX_LEN)], axis=1)
    sinks = sink_ref[pl.ds(pl.multiple_of(kvh * SWA_GROUP, SWA_GROUP), SWA_GROUP), :][:, 0:1] * LOG2E
    sink = jnp.concatenate([jnp.broadcast_to(sinks[g:g + 1], (SWA_TQ, 1)) for g in range(SWA_GROUP)], axis=0)

    def block(i, carry):
        i = jnp.asarray(i, jnp.int32)
        t0 = pl.multiple_of(i * SWA_TQ, SWA_TQ)
        s0 = pl.multiple_of(jnp.clip(i * SWA_TQ - SWA_WINDOW, 0, SEQ - SWA_BAND), SWA_WINDOW)
        q = q_ref[0, pl.ds(t0, SWA_TQ), :]
        qq = jnp.concatenate([_split_heads(q[:, p * LANES:(p + 1) * LANES], interleaved=True)
                              for p in range(n_pair)], axis=0)
        kw = k_ref[0, pl.ds(s0, SWA_BAND), :]
        vw = jnp.concatenate([v_ref[0, pl.ds(s0, SWA_BAND), :], _ones_col(SWA_BAND)], axis=1)
        tpos = t0 + lax.broadcasted_iota(jnp.int32, (SWA_TQ, SWA_BAND), 0)
        spos = s0 + lax.broadcasted_iota(jnp.int32, (SWA_TQ, SWA_BAND), 1)
        bias = jnp.where(jnp.abs(tpos - spos) <= SWA_WINDOW, 0.0, NEG_BIG)
        s_lat = _nt(qq, kw) + jnp.concatenate([bias] * SWA_GROUP, axis=0)
        s_ctx = _nt(qq, kctx)
        m = jnp.maximum(jnp.maximum(jnp.max(s_lat, axis=-1, keepdims=True),
                                    jnp.max(s_ctx, axis=-1, keepdims=True)), sink)
        tot = (jnp.dot(jnp.exp2(s_lat - m).astype(BF16), vw, preferred_element_type=F32)
               + jnp.dot(jnp.exp2(s_ctx - m).astype(BF16), vctx, preferred_element_type=F32))
        out = tot[:, :LANES] / (tot[:, LANES:LANES + 1] + jnp.exp2(sink - m))
        pairs = [jnp.where(lo, out[(2 * p) * SWA_TQ:(2 * p + 1) * SWA_TQ],
                           out[(2 * p + 1) * SWA_TQ:(2 * p + 2) * SWA_TQ]) for p in range(n_pair)]
        o = jnp.concatenate(pairs, axis=1)
        o_ref[0, pl.ds(t0, SWA_TQ), :] = _gate(o, z_ref[0, pl.ds(t0, SWA_TQ), :]).astype(BF16)
        return carry

    lax.fori_loop(0, SEQ // SWA_TQ, block, 0, unroll=True)


def _swa_attn(proj_lat, proj_ctx, sink_rows):
    bsz = proj_lat.shape[0]
    kb, vb, zb = SWA_K_COL // LANES, SWA_V_COL // LANES, D_INNER // SWA_GW
    return pl.pallas_call(
        _swa_attn_kernel,
        grid=(bsz, SWA_KV_HEADS),
        in_specs=[pl.BlockSpec((1, SEQ, SWA_GW), lambda b, h: (b, 0, h)),
                  pl.BlockSpec((1, SEQ, LANES), lambda b, h: (b, 0, kb + h)),
                  pl.BlockSpec((1, SEQ, LANES), lambda b, h: (b, 0, vb + h)),
                  pl.BlockSpec((1, SEQ, SWA_GW), lambda b, h: (b, 0, zb + h)),
                  pl.BlockSpec((1, CTX_LEN, LANES), lambda b, h: (b, 0, kb + h)),
                  pl.BlockSpec((1, CTX_LEN, LANES), lambda b, h: (b, 0, vb + h)),
                  pl.BlockSpec((SWA_Q_HEADS, LANES), lambda b, h: (0, 0))],
        out_specs=pl.BlockSpec((1, SEQ, SWA_GW), lambda b, h: (b, 0, h)),
        out_shape=jax.ShapeDtypeStruct((bsz, SEQ, D_INNER), BF16),
        compiler_params=_cparams(("parallel", "parallel"), 48),
        name="swa_attn",
    )(proj_lat, proj_lat, proj_lat, proj_lat, proj_ctx, proj_ctx, sink_rows)


GATE_LANES = 128


def _mlstm_prep_kernel(xm_ref, cw_ref, cb_ref, bdq_ref, bdk_ref, bdv_ref, wg_ref, bg_ref,
                       q_ref, kt_ref, v_ref, xc_ref, g_ref):
    h = pl.program_id(1)
    t = xm_ref.shape[1]
    xm_b = xm_ref[0]
    xm = xm_b.astype(F32)

    tpos = lax.broadcasted_iota(jnp.int32, (t, 1), 0)
    conv = jnp.zeros_like(xm) + cb_ref[...]
    for j in range(MLSTM_CONV):
        d = j - MLSTM_CONV // 2
        sh = pltpu.roll(xm, (-d) % t, axis=0) if d else xm
        ok = (tpos + d >= 0) & (tpos + d < t)
        conv = conv + jnp.where(ok, sh, 0.0) * cw_ref[j:j + 1, :]
    xc = _silu(conv)
    xc_b = xc.astype(BF16)
    q = jnp.dot(xc_b, bdq_ref[0], preferred_element_type=F32)
    k = jnp.dot(xc_b, bdk_ref[0], preferred_element_type=F32)
    v = jnp.dot(xm_b, bdv_ref[0], preferred_element_type=F32)
    q_b, k_b, v_b = q.astype(BF16), k.astype(BF16), v.astype(BF16)
    q_ref[0] = q_b
    v_ref[0] = v_b
    xc_ref[0] = xc_b
    ks = k * (MLSTM_HEAD_DIM ** -0.5)
    for c in range(t // SCAN_CHUNK):
        kt_ref[0, 0, c] = ks[c * SCAN_CHUNK:(c + 1) * SCAN_CHUNK, :].T.astype(BF16)
    part = (jnp.dot(q_b, wg_ref[0, 0], preferred_element_type=F32)
            + jnp.dot(k_b, wg_ref[1, 0], preferred_element_type=F32)
            + jnp.dot(v_b, wg_ref[2, 0], preferred_element_type=F32))

    @pl.when(h == 0)
    def _():
        g_ref[0] = part + bg_ref[...]

    @pl.when(h != 0)
    def _():
        g_ref[0] = g_ref[0] + part


def _mlstm_prep(proj3, conv_w, conv_b, bdq, bdk, bdv, wg, bg):
    bsz, t, _ = proj3.shape
    hd = MLSTM_HEAD_DIM
    nc = t // SCAN_CHUNK
    seq_spec = pl.BlockSpec((1, t, hd), lambda b, h: (b, 0, h))
    return pl.pallas_call(
        _mlstm_prep_kernel,
        grid=(bsz, MLSTM_HEADS),
        in_specs=[seq_spec,
                  pl.BlockSpec((MLSTM_CONV, hd), lambda b, h: (0, h)),
                  pl.BlockSpec((1, hd), lambda b, h: (0, h)),
                  pl.BlockSpec((1, hd, hd), lambda b, h: (h, 0, 0)),
                  pl.BlockSpec((1, hd, hd), lambda b, h: (h, 0, 0)),
                  pl.BlockSpec((1, hd, hd), lambda b, h: (h, 0, 0)),
                  pl.BlockSpec((3, 1, hd, GATE_LANES), lambda b, h: (0, h, 0, 0)),
                  pl.BlockSpec((1, GATE_LANES), lambda b, h: (0, 0))],
        out_specs=[seq_spec,
                   pl.BlockSpec((1, 1, nc, hd, SCAN_CHUNK), lambda b, h: (b, h, 0, 0, 0)),
                   seq_spec, seq_spec,
                   pl.BlockSpec((1, t, GATE_LANES), lambda b, h: (b, 0, 0))],
        out_shape=[jax.ShapeDtypeStruct((bsz, t, D_INNER), BF16),
                   jax.ShapeDtypeStruct((bsz, MLSTM_HEADS, nc, hd, SCAN_CHUNK), BF16),
                   jax.ShapeDtypeStruct((bsz, t, D_INNER), BF16),
                   jax.ShapeDtypeStruct((bsz, t, D_INNER), BF16),
                   jax.ShapeDtypeStruct((bsz, t, GATE_LANES), F32)],
        compiler_params=_cparams(("parallel", "arbitrary"), 48),
        name="mlstm_prep",
    )(proj3, conv_w, conv_b.reshape(1, D_INNER), bdq, bdk, bdv, wg, bg)


def _lane_scan(x, op, fill, reverse):
    lane = lax.broadcasted_iota(jnp.int32, x.shape, 1)
    sh = 1
    while sh < LANES:
        if reverse:
            y = jnp.where(lane < LANES - sh, pltpu.roll(x, LANES - sh, axis=1), fill)
        else:
            y = jnp.where(lane >= sh, pltpu.roll(x, sh, axis=1), fill)
        x = op(x, y)
        sh *= 2
    return x


GP_ROWS = 8


def _mlstm_gates_kernel(gc_ref, gl_ref, oc_ref, ol_ref):
    nh = MLSTM_HEADS
    ppc = SCAN_CHUNK // LANES
    gct = gc_ref[0].T
    glt = gl_ref[0].T
    n_c = gc_ref.shape[1] // SCAN_CHUNK
    n_l = gl_ref.shape[1] // SCAN_CHUNK
    zeros = jnp.zeros((nh, SCAN_CHUNK), F32)
    pieces = [(gct, j) for j in range(n_c * ppc)] + [(glt, j) for j in range(n_l * ppc)]
    chunks = [(oc_ref, c) for c in range(n_c)] + [(ol_ref, c) for c in range(n_l)]
    n_all = len(chunks)

    def stack(row0):
        return jnp.concatenate([src[row0:row0 + nh, j * LANES:(j + 1) * LANES] for src, j in pieces], axis=0)

    def piece(x, k):
        return x[k * nh:(k + 1) * nh]

    def earlier(k, rev):
        c = k // ppc
        return range(k + 1, (c + 1) * ppc) if rev else range(c * ppc, k)

    for d in range(2):
        rev = d == 1
        edge = (lambda x: x[:, 0:1]) if rev else (lambda x: x[:, LANES - 1:LANES])
        ig = stack(2 * d * nh)
        fg = stack((2 * d + 1) * nh)
        lf = jnp.minimum(fg, 0.0) - jnp.log1p(jnp.exp(-jnp.abs(fg)))
        bp = _lane_scan(lf, jnp.add, 0.0, rev)
        b = jnp.concatenate([piece(bp, k) + sum(piece(edge(bp), o) for o in earlier(k, rev))
                             for k in range(len(pieces))], axis=0)
        cmp_ = _lane_scan(ig - b, jnp.maximum, -jnp.inf, rev)
        cm = jnp.concatenate([functools.reduce(jnp.maximum, [piece(edge(cmp_), o) for o in earlier(k, rev)],
                                               piece(cmp_, k)) for k in range(len(pieces))], axis=0)
        last = (lambda c: c * ppc) if rev else (lambda c: c * ppc + ppc - 1)
        b_end_c = [piece(edge(b), last(c)) for c in range(n_all)]
        b_end = jnp.concatenate([b_end_c[k // ppc] for k in range(len(pieces))], axis=0)
        g = b_end - b + ig
        gmax_p = jnp.max(g, axis=-1, keepdims=True)
        gmax_c = [functools.reduce(jnp.maximum, [piece(gmax_p, c * ppc + p) for p in range(ppc)])
                  for c in range(n_all)]
        order = list(range(n_all))
        if rev:
            order = list(reversed(range(n_c))) + list(reversed(range(n_c, n_all)))
        m = jnp.zeros((nh, 1), F32)
        m_in, m_out = [None] * n_all, [None] * n_all
        for c in order:
            m_in[c] = m
            m = jnp.maximum(b_end_c[c] + m, gmax_c[c])
            m_out[c] = m
        m_in = jnp.concatenate([m_in[k // ppc] for k in range(len(pieces))], axis=0)
        m_out = jnp.concatenate([m_out[k // ppc] for k in range(len(pieces))], axis=0)
        mx = jnp.maximum(cm, m_in)
        rows = (-mx, b - ig, jnp.exp(m_in - mx), jnp.exp(-(b + mx)), jnp.exp(g - m_out),
                jnp.broadcast_to(jnp.exp(b_end + m_in - m_out), b.shape))
        for k, (dst, c) in enumerate(chunks):
            for r, val in enumerate(rows):
                dst[0, d, c, r] = jnp.concatenate([piece(val, k * ppc + p) for p in range(ppc)], axis=1)
            dst[0, d, c, 6] = zeros
            dst[0, d, c, 7] = zeros


def _mlstm_gates(gates_ctx, gates_lat):
    bsz = gates_lat.shape[0]
    n_c, n_l = CTX_LEN // SCAN_CHUNK, SEQ // SCAN_CHUNK

    def out(n):
        shape = (bsz, 2, n, GP_ROWS, MLSTM_HEADS, SCAN_CHUNK)
        return (pl.BlockSpec((1,) + shape[1:], lambda b: (b, 0, 0, 0, 0, 0)), jax.ShapeDtypeStruct(shape, F32))

    (spec_c, shape_c), (spec_l, shape_l) = out(n_c), out(n_l)
    return pl.pallas_call(
        _mlstm_gates_kernel,
        grid=(bsz,),
        in_specs=[pl.BlockSpec((1, CTX_LEN, GATE_LANES), lambda b: (b, 0, 0)),
                  pl.BlockSpec((1, SEQ, GATE_LANES), lambda b: (b, 0, 0))],
        out_specs=[spec_c, spec_l],
        out_shape=[shape_c, shape_l],
        compiler_params=_cparams(("parallel",), 32),
        name="mlstm_gates",
    )(gates_ctx, gates_lat)


def _mlstm_scan_kernel(qc_ref, ktc_ref, vc_ref, xcc_ref, gpc_ref,
                       ql_ref, ktl_ref, vl_ref, xcl_ref, gpl_ref, ng_ref, sk_ref, zc_ref, zl_ref,
                       oc_ref, ol_ref, hc_scr, hl_scr, st_scr):
    L = SCAN_CHUNK
    hd = MLSTM_HEAD_DIM
    n_c = qc_ref.shape[1] // L
    n_l = ql_ref.shape[1] // L
    lane = lax.broadcasted_iota(jnp.int32, (L, LANES), 1)
    ones_col = jnp.where(lane == 0, 1.0, 0.0).astype(BF16)
    tt = lax.broadcasted_iota(jnp.int32, (L, L), 0)
    ss = lax.broadcasted_iota(jnp.int32, (L, L), 1)
    masks = (ss <= tt, ss >= tt)

    hc_scr[...] = jnp.zeros_like(hc_scr)
    hl_scr[...] = jnp.zeros_like(hl_scr)
    st_scr[...] = jnp.zeros_like(st_scr)

    def chunk(d, q_ref, kt_ref, v_ref, gp_ref, h_scr, c):
        rows = gp_ref[0, d, 0, c]
        cols = rows.T
        a_col, wst, einv = cols[:, 0:1], cols[:, 2:3], cols[:, 3:4]
        c_row, ws_row, decay = rows[1:2, :], rows[4:5, :], rows[5:6, 0:1]
        r0 = pl.multiple_of(c * L, L)
        q = q_ref[0, pl.ds(r0, L), :]
        kt = kt_ref[0, 0, c]
        vx = jnp.concatenate([v_ref[0, pl.ds(r0, L), :], ones_col], axis=1)
        w = jnp.exp(jnp.where(masks[d], a_col - c_row, -jnp.inf))
        s = (jnp.dot(q, kt, preferred_element_type=F32) * w).astype(BF16)
        kw = (kt.astype(F32) * ws_row).astype(BF16)
        both = jnp.dot(jnp.concatenate([s, kw], axis=0), vx, preferred_element_type=F32)
        state = st_scr[d]
        tot = wst * jnp.dot(q, state.astype(BF16), preferred_element_type=F32) + both[:L]
        den = tot[:, hd:hd + 1]
        h = tot[:, :hd] / jnp.maximum(jnp.abs(den), einv)
        h_scr[pl.ds(r0, L), :] = h_scr[pl.ds(r0, L), :] + h
        st_scr[d] = decay * state + both[L:]

    for j in range(n_c):
        chunk(0, qc_ref, ktc_ref, vc_ref, gpc_ref, hc_scr, j)
        chunk(1, qc_ref, ktc_ref, vc_ref, gpc_ref, hc_scr, n_c - 1 - j)

    def lat_step(j, carry):
        chunk(0, ql_ref, ktl_ref, vl_ref, gpl_ref, hl_scr, j)
        chunk(1, ql_ref, ktl_ref, vl_ref, gpl_ref, hl_scr, n_l - 1 - j)
        return carry

    lax.fori_loop(0, n_l, lat_step, 0, unroll=True)

    ng = ng_ref[...]
    sk = sk_ref[...]

    def finish(h_scr, xc_ref, z_ref, o_ref, r0, n):
        h = h_scr[pl.ds(r0, n), :]
        mu = jnp.mean(h, axis=-1, keepdims=True)
        hcen = h - mu
        hn = hcen * lax.rsqrt(jnp.mean(hcen * hcen, axis=-1, keepdims=True) + RMS_EPS)
        o = hn * ng + sk * xc_ref[0, pl.ds(r0, n), :].astype(F32)
        o_ref[0, pl.ds(r0, n), :] = _gate(o, z_ref[0, pl.ds(r0, n), :]).astype(BF16)

    finish(hc_scr, xcc_ref, zc_ref, oc_ref, 0, CTX_LEN)

    def fin_step(i, carry):
        finish(hl_scr, xcl_ref, zl_ref, ol_ref, pl.multiple_of(jnp.asarray(i, jnp.int32) * 256, 256), 256)
        return carry

    lax.fori_loop(0, SEQ // 256, fin_step, 0, unroll=2)


def _mlstm_scan(prep_c, gp_c, prep_l, gp_l, norm_g, skip, proj_c, proj_l):
    bsz = prep_l[0].shape[0]
    hd = MLSTM_HEAD_DIM

    def z_spec(t):
        return pl.BlockSpec((1, t, hd), lambda b, h: (b, 0, MLSTM_HEADS + h))

    def seq_specs(t):
        nc = t // SCAN_CHUNK
        seq = pl.BlockSpec((1, t, hd), lambda b, h: (b, 0, h))
        return [seq,
                pl.BlockSpec((1, 1, nc, hd, SCAN_CHUNK), lambda b, h: (b, h, 0, 0, 0)),
                seq, seq,
                pl.BlockSpec((1, 2, 1, nc, GP_ROWS, SCAN_CHUNK), lambda b, h: (b, 0, h, 0, 0, 0))]

    vec = pl.BlockSpec((1, hd), lambda b, h: (0, h))
    return pl.pallas_call(
        _mlstm_scan_kernel,
        grid=(bsz, MLSTM_HEADS),
        in_specs=seq_specs(CTX_LEN) + seq_specs(SEQ) + [vec, vec, z_spec(CTX_LEN), z_spec(SEQ)],
        out_specs=[pl.BlockSpec((1, CTX_LEN, hd), lambda b, h: (b, 0, h)),
                   pl.BlockSpec((1, SEQ, hd), lambda b, h: (b, 0, h))],
        out_shape=[jax.ShapeDtypeStruct((bsz, CTX_LEN, D_INNER), BF16),
                   jax.ShapeDtypeStruct((bsz, SEQ, D_INNER), BF16)],
        scratch_shapes=[pltpu.VMEM((CTX_LEN, hd), F32), pltpu.VMEM((SEQ, hd), F32),
                        pltpu.VMEM((2, hd, hd + LANES), F32)],
        compiler_params=_cparams(("parallel", "parallel"), 40),
        name="mlstm_scan",
    )(*prep_c, gp_c, *prep_l, gp_l, norm_g.reshape(1, D_INNER), skip.reshape(1, D_INNER), proj_c, proj_l)


def _blockdiag_dense(w):
    per_head = MLSTM_HEAD_DIM // MLSTM_QKV_BLOCK
    wh = w.reshape(MLSTM_HEADS, per_head, MLSTM_QKV_BLOCK, MLSTM_QKV_BLOCK)
    eye = jnp.eye(per_head, dtype=w.dtype)
    dense = wh[:, :, :, None, :] * eye[None, :, None, :, None]
    return dense.reshape(MLSTM_HEADS, MLSTM_HEAD_DIM, MLSTM_HEAD_DIM).astype(BF16)


def _mlstm_mixer(proj_lat, proj_ctx, conv_w, conv_b, wq, wk, wv, w_gate, b_gate, norm_g, skip):
    n_gate = 4 * MLSTM_HEADS
    wg = jnp.pad(w_gate, ((0, 0), (0, GATE_LANES - n_gate))).astype(BF16)
    wg = wg.reshape(3, MLSTM_HEADS, MLSTM_HEAD_DIM, GATE_LANES)
    bg = jnp.pad(b_gate, (0, GATE_LANES - n_gate)).reshape(1, GATE_LANES)
    bds = (_blockdiag_dense(wq), _blockdiag_dense(wk), _blockdiag_dense(wv))
    *prep_c, gates_c = _mlstm_prep(proj_ctx, conv_w, conv_b, *bds, wg, bg)
    *prep_l, gates_l = _mlstm_prep(proj_lat, conv_w, conv_b, *bds, wg, bg)
    gp_c, gp_l = _mlstm_gates(gates_c, gates_l)
    gp_c = jnp.transpose(gp_c, (0, 1, 4, 2, 3, 5))
    gp_l = jnp.transpose(gp_l, (0, 1, 4, 2, 3, 5))
    return _mlstm_scan(prep_c, gp_c, prep_l, gp_l, norm_g, skip, proj_ctx, proj_lat)


def kernel(x, c, ctx, c_ctx, norm_g, w_ada, b_ada, final_g, mlstm_w_in, mlstm_conv_w, mlstm_conv_b, mlstm_wq, mlstm_wk, mlstm_wv, mlstm_w_gate, mlstm_b_gate, mlstm_norm_g, mlstm_skip, mlstm_w_out, diff_w_in, diff_lq1, diff_lk1, diff_lq2, diff_lk2, diff_subln_g, diff_w_out, na_w_in, na_rpb, na_w_out, swa_w_in, swa_sink, swa_w_out):
    bsz = x.shape[0]
    ctx_row = bsz
    mod_rows = -(-(bsz + 1) // 8) * 8
    cvec = jnp.zeros((mod_rows, D_MODEL), F32).at[:bsz].set(c).at[ctx_row].set(c_ctx)
    mods_all = _adaln(cvec, w_ada, b_ada)
    rope_tabs = _rope_tables()

    xl = x.reshape(bsz * SEQ, D_MODEL)
    xc = ctx.reshape(bsz * CTX_LEN, D_MODEL)
    lat = dict(rows_per_mod=SEQ, mod_row0=0)
    con = dict(rows_per_mod=bsz * CTX_LEN, mod_row0=ctx_row)
    tm_ctx = min(1024, bsz * CTX_LEN)

    def in_both(i, w, tn, rope_cols=()):
        mods = mods_all[i].reshape(mod_rows, 1, 3 * D_MODEL)
        wb = w.astype(BF16)
        pl_ = _inproj(xl, norm_g[i], mods, wb, tm=SEQ, tn=tn, rope_cols=rope_cols, rope_tabs=rope_tabs, **lat)
        pc_ = _inproj(xc, norm_g[i], mods, wb, tm=tm_ctx, tn=tn, **con)
        return mods, pl_, pc_

    def out_both(mods, o_lat, o_ctx, w_out, need_ctx, final=None, in_place=True):
        wb = w_out.astype(BF16)
        new_l = _outproj(o_lat.reshape(bsz * SEQ, D_INNER), wb, xl, mods, tm=1024,
                         final_g=final, in_place=in_place, **lat)
        new_c = xc
        if need_ctx:
            new_c = _outproj(o_ctx.reshape(bsz * CTX_LEN, D_INNER), wb, xc, mods, tm=tm_ctx, **con)
        return new_l, new_c

    mods, p_l, p_c = in_both(0, mlstm_w_in[0], 1024)
    o_c, o_l = _mlstm_mixer(p_l.reshape(bsz, SEQ, -1), p_c.reshape(bsz, CTX_LEN, -1), mlstm_conv_w[0],
                            mlstm_conv_b[0], mlstm_wq[0], mlstm_wk[0], mlstm_wv[0], mlstm_w_gate[0],
                            mlstm_b_gate[0], mlstm_norm_g[0], mlstm_skip[0])
    xl, xc = out_both(mods, o_l, o_c, mlstm_w_out[0], True, in_place=False)

    wd = diff_w_in[0]
    wd = jnp.concatenate([_pair_interleave(wd[:, :D_INNER]), _pair_interleave(wd[:, D_INNER:2 * D_INNER]),
                          wd[:, 2 * D_INNER:]], axis=1)
    mods, p_l, p_c = in_both(1, wd, 1024, rope_cols=((0, 2 * D_INNER),))
    p_l3, p_c3 = p_l.reshape(bsz, SEQ, -1), p_c.reshape(bsz, CTX_LEN, -1)
    lam_init = 0.8 - 0.6 * math.exp(-0.3 * 1)
    lqk = jnp.stack([diff_lq1[0], diff_lk1[0], diff_lq2[0], diff_lk2[0]])
    o_l = _diff_attn(p_l3, p_c3, lqk, diff_subln_g[0], tq=512, lam_init=lam_init)
    o_c = _ctx_attn(p_c3, lqk, diff_subln_g[0], lam_init)
    xl, xc = out_both(mods, o_l, o_c, diff_w_out[0], True)

    mods, p_l, p_c = in_both(2, na_w_in[0], 1024)
    p_l3, p_c3 = p_l.reshape(bsz, SEQ, -1), p_c.reshape(bsz, CTX_LEN, -1)
    o_l = _na_attn(p_l3, p_c3, _na_bias_tiles(na_rpb[0]))
    o_c = _ctx_attn(p_c3)
    xl, xc = out_both(mods, o_l, o_c, na_w_out[0], True)

    kvw = SWA_KV_HEADS * HEAD_DIM
    w = swa_w_in[0]
    wq, wk, wv, wz = (w[:, :D_INNER], w[:, D_INNER:D_INNER + kvw], w[:, D_INNER + kvw:D_INNER + 2 * kvw],
                      w[:, D_INNER + 2 * kvw:])

    def twice(m):
        m = m.reshape(D_MODEL, SWA_KV_HEADS, 1, HEAD_DIM)
        return jnp.broadcast_to(m, (D_MODEL, SWA_KV_HEADS, 2, HEAD_DIM)).reshape(D_MODEL, SWA_KV_HEADS * LANES)

    w3 = jnp.concatenate([_pair_interleave(wq), wz, _pair_interleave(wk, dup=True), twice(wv)], axis=1)
    mods, p_l, p_c = in_both(3, w3, 1024, rope_cols=((0, D_INNER), (SWA_K_COL, SWA_V_COL)))
    sink_rows = jnp.broadcast_to(swa_sink[0][:, None], (SWA_Q_HEADS, LANES))
    o_l = _swa_attn(p_l.reshape(bsz, SEQ, -1), p_c.reshape(bsz, CTX_LEN, -1), sink_rows)
    xl, _ = out_both(mods, o_l, None, swa_w_out[0], False, final=final_g)
    return xl.reshape(bsz, SEQ, D_MODEL)
```
